```python
import math
import jax, jax.numpy as jnp
from jax import lax
import numpy as np

D_MODEL = 2048
BATCH = 8
SEQ = 8192
DEPTH = 2

MIX_WIDTH = D_MODEL
GW = MIX_WIDTH // 4
GDN_HEAD_DIM = 128
GDN_HEADS = GW // GDN_HEAD_DIM
GDN_CONV = 4
GDN_CHUNK = 64
SSM_HEAD_DIM = 64
SSM_HEADS = GW // SSM_HEAD_DIM
SSM_GROUPS = 2
SSM_STATE = 128
SSM_CONV = 4
SSM_CHUNK = 128
SSM_CONV_DIM = GW + 2 * SSM_GROUPS * SSM_STATE
ATTN_HEAD_DIM = 64
ATTN_Q_HEADS = GW // ATTN_HEAD_DIM
ATTN_KV_HEADS = ATTN_Q_HEADS // 4
WINDOW = 128
ROPE_THETA = 10000.0
SC_CONV = 3
SC_GROUPS = 4
MLP_HIDDEN = 4 * D_MODEL
EPS = 1e-6

IN_WIDTH = (4 * GW + 2 * GDN_HEADS) + (GW + SSM_CONV_DIM + SSM_HEADS) + (GW + 2 * ATTN_KV_HEADS * ATTN_HEAD_DIM) + 3 * GW

kernel_name = 'hymba_style_hybrid_gdn_ssd_swa_shortconv'


def in_proj_split_points():
    sizes = (GW, GW, GW, GW, GDN_HEADS, GDN_HEADS,
             GW, SSM_CONV_DIM, SSM_HEADS,
             GW, ATTN_KV_HEADS * ATTN_HEAD_DIM, ATTN_KV_HEADS * ATTN_HEAD_DIM,
             GW, GW, GW)
    return [int(s) for s in np.cumsum(sizes)[:-1]]


def rms_norm(x, w):
    xf = x.astype(jnp.float32)
    y = xf * lax.rsqrt(jnp.mean(xf * xf, axis=-1, keepdims=True) + EPS)
    return (y * w.astype(jnp.float32)).astype(x.dtype)


def l2_norm(x):
    return x * lax.rsqrt(jnp.sum(x * x, axis=-1, keepdims=True) + EPS)


def causal_dwconv(x, w):
    K, C = w.shape
    return lax.conv_general_dilated(x, w[:, None, :].astype(x.dtype), window_strides=(1,),
                                    padding=[(K - 1, 0)], dimension_numbers=('NWC', 'WIO', 'NWC'),
                                    feature_group_count=C)


def rope_tables(positions):
    inv_freq = ROPE_THETA ** (-jnp.arange(0, ATTN_HEAD_DIM, 2, dtype=jnp.float32) / ATTN_HEAD_DIM)
    ang = positions.astype(jnp.float32)[..., None] * inv_freq
    return jnp.cos(ang), jnp.sin(ang)


def apply_rope(x, cos, sin):
    x1, x2 = jnp.split(x, 2, axis=-1)
    return jnp.concatenate([x1 * cos - x2 * sin, x2 * cos + x1 * sin], axis=-1).astype(x.dtype)


def gated_deltanet(q, k, v, z, b, a, conv_w, a_log, dt_bias, norm_w):
    Bsz, L, _ = q.shape
    H, dk, Cs = GDN_HEADS, GDN_HEAD_DIM, GDN_CHUNK
    Nc = L // Cs
    f32 = jnp.float32
    qkv = jax.nn.silu(causal_dwconv(jnp.concatenate([q, k, v], axis=-1), conv_w))
    q, k, v = [t.reshape(Bsz, L, H, dk).astype(f32) for t in jnp.split(qkv, 3, axis=-1)]
    q = l2_norm(q) * (dk ** -0.5)
    k = l2_norm(k)
    beta = jax.nn.sigmoid(b.astype(f32))
    g = -jnp.exp(a_log.astype(f32)) * jax.nn.softplus(a.astype(f32) + dt_bias.astype(f32))

    def to_chunks(t):
        return t.reshape(Bsz, Nc, Cs, H, -1).transpose(0, 3, 1, 2, 4)

    q, k, v = to_chunks(q), to_chunks(k), to_chunks(v)
    beta = to_chunks(beta[..., None])[..., 0]
    gc = jnp.cumsum(to_chunks(g[..., None])[..., 0], axis=-1)
    tri_incl = jnp.tril(jnp.ones((Cs, Cs), dtype=bool))
    tri_strict = jnp.tril(jnp.ones((Cs, Cs), dtype=bool), -1)
    decay = jnp.exp(jnp.where(tri_incl, gc[..., :, None] - gc[..., None, :], -jnp.inf))
    k_beta = k * beta[..., None]
    m = jnp.where(tri_strict, jnp.einsum('bhncd,bhnsd->bhncs', k_beta, k) * decay, 0.0)
    eye = jnp.eye(Cs, dtype=f32)
    t_inv = lax.linalg.triangular_solve(eye + m, jnp.broadcast_to(eye, m.shape), left_side=True,
                                        lower=True, unit_diagonal=True)
    u = t_inv @ (v * beta[..., None])
    w = t_inv @ (k_beta * jnp.exp(gc)[..., None])
    attn = jnp.where(tri_incl, jnp.einsum('bhncd,bhnsd->bhncs', q, k) * decay, 0.0)
    q_dec = q * jnp.exp(gc)[..., None]
    k_dec = k * jnp.exp(gc[..., -1:] - gc)[..., None]
    chunk_dec = jnp.exp(gc[..., -1])

    def step(S, inp):
        u_c, w_c, qd_c, kd_c, at_c, cd_c = inp
        v_new = u_c - w_c @ S
        o = qd_c @ S + at_c @ v_new
        S = S * cd_c[..., None, None] + jnp.swapaxes(kd_c, -1, -2) @ v_new
        return S, o

    xs = tuple(jnp.moveaxis(t, 2, 0) for t in (u, w, q_dec, k_dec, attn, chunk_dec))
    _, o = lax.scan(step, jnp.zeros((Bsz, H, dk, dk), f32), xs)
    o = o.transpose(1, 0, 3, 2, 4).reshape(Bsz, L, H, dk)
    zf = z.reshape(Bsz, L, H, dk).astype(f32)
    o = rms_norm(o, norm_w) * jax.nn.silu(zf)
    return o.reshape(Bsz, L, GW).astype(z.dtype)


def mamba2_ssd(z, xbc, dt, conv_w, conv_b, a_log, dt_bias, d_skip, norm_w):
    Bsz, L, _ = z.shape
    G, R, P, N, Cs = SSM_GROUPS, SSM_HEADS // SSM_GROUPS, SSM_HEAD_DIM, SSM_STATE, SSM_CHUNK
    Nc = L // Cs
    f32 = jnp.float32
    xbc = jax.nn.silu(causal_dwconv(xbc, conv_w) + conv_b)
    xs, bm, cm = jnp.split(xbc.astype(f32), [GW, GW + G * N], axis=-1)
    xs = xs.reshape(Bsz, Nc, Cs, G, R, P)
    bm = bm.reshape(Bsz, Nc, Cs, G, N)
    cm = cm.reshape(Bsz, Nc, Cs, G, N)
    dt = jax.nn.softplus(dt.astype(f32) + dt_bias.astype(f32)).reshape(Bsz, Nc, Cs, G, R)
    a = -jnp.exp(a_log.astype(f32)).reshape(G, R)
    da_cs = jnp.cumsum((dt * a).transpose(0, 3, 4, 1, 2), axis=-1)
    tri = jnp.tril(jnp.ones((Cs, Cs), dtype=bool))
    lmat = jnp.exp(jnp.where(tri, da_cs[..., :, None] - da_cs[..., None, :], -jnp.inf))
    xdt = xs * dt[..., None]
    cb = jnp.einsum('bclgn,bcsgn->bgcls', cm, bm)
    y_diag = jnp.einsum('bgcls,bgrcls,bcsgrp->bclgrp', cb, lmat, xdt)
    decay_states = jnp.exp(da_cs[..., -1:] - da_cs)
    states = jnp.einsum('bclgn,bgrcl,bclgrp->bcgrpn', bm, decay_states, xdt)
    chunk_dec = jnp.exp(da_cs[..., -1])

    def step(S, inp):
        st, dec = inp
        return S * dec[..., None, None] + st, S

    _, prev = lax.scan(step, jnp.zeros((Bsz, G, R, P, N), f32),
                       (jnp.moveaxis(states, 1, 0), jnp.moveaxis(chunk_dec, 3, 0)))
    prev = jnp.moveaxis(prev, 0, 1)
    y_off = jnp.einsum('bclgn,bcgrpn,bgrcl->bclgrp', cm, prev, jnp.exp(da_cs))
    y = y_diag + y_off + xs * d_skip.astype(f32).reshape(G, R)[:, :, None]
    y = y.reshape(Bsz, L, GW) * jax.nn.silu(z.astype(f32))
    y = rms_norm(y.reshape(Bsz, L, G, GW // G), norm_w.reshape(G, GW // G))
    return y.reshape(Bsz, L, GW).astype(z.dtype)


def band_blocks(t):
    Bsz, L = t.shape[:2]
    cur = t.reshape(Bsz, L // WINDOW, WINDOW, *t.shape[2:])
    prev = jnp.pad(cur[:, :-1], ((0, 0), (1, 0), (0, 0), (0, 0), (0, 0)))
    return jnp.concatenate([prev, cur], axis=2)


def sliding_window_attention(q, k, v, sinks, cos, sin):
    Bsz, L, _ = q.shape
    Hkv, R, hd, W = ATTN_KV_HEADS, ATTN_Q_HEADS // ATTN_KV_HEADS, ATTN_HEAD_DIM, WINDOW
    nb = L // W
    q = apply_rope(q.reshape(Bsz, L, Hkv, R, hd), cos[:, :, None, None], sin[:, :, None, None])
    k = apply_rope(k.reshape(Bsz, L, Hkv, hd), cos[:, :, None], sin[:, :, None])
    v = v.reshape(Bsz, L, Hkv, hd)
    qb = q.reshape(Bsz, nb, W, Hkv, R, hd)
    kb, vb = band_blocks(k), band_blocks(v)
    s = jnp.einsum('bnqgrd,bnkgd->bngrqk', qb, kb, preferred_element_type=jnp.float32) * (hd ** -0.5)
    qi = jnp.arange(W)[:, None]
    kj = jnp.arange(2 * W)[None, :]
    rel = qi + W - kj
    band = (rel >= 0) & (rel < W)
    mask = band[None] & ((jnp.arange(nb)[:, None, None] > 0) | (kj >= W)[None])
    s = jnp.where(mask[None, :, None, None], s, -jnp.inf)
    sink = sinks.astype(jnp.float32).reshape(Hkv, R)[None, None, :, :, None, None]
    mx = jnp.maximum(jnp.max(s, axis=-1, keepdims=True), sink)
    p = jnp.exp(s - mx)
    p = p / (jnp.sum(p, axis=-1, keepdims=True) + jnp.exp(sink - mx))
    o = jnp.einsum('bngrqk,bnkgd->bnqgrd', p.astype(vb.dtype), vb)
    return o.reshape(Bsz, L, Hkv * R * hd)


def short_conv(b, c, h, conv_w):
    return b * causal_dwconv(c * h, conv_w)


def hybrid_mixer(h, cos, sin, w_in, w_out, gdn_conv_w, gdn_a_log, gdn_dt_bias, gdn_norm_w,
                 ssm_conv_w, ssm_conv_b, ssm_a_log, ssm_dt_bias, ssm_d, ssm_norm_w, attn_sinks, sc_conv_w):
    proj = h @ w_in
    (gq, gk, gv, gz, gb, ga, sz, sxbc, sdt, aq, ak, av, cb, cc, ch) = jnp.split(proj, in_proj_split_points(), axis=-1)
    y_a = gated_deltanet(gq, gk, gv, gz, gb, ga, gdn_conv_w, gdn_a_log, gdn_dt_bias, gdn_norm_w)
    y_b = mamba2_ssd(sz, sxbc, sdt, ssm_conv_w, ssm_conv_b, ssm_a_log, ssm_dt_bias, ssm_d, ssm_norm_w)
    y_c = sliding_window_attention(aq, ak, av, attn_sinks, cos, sin)
    y_d = short_conv(cb, cc, ch, sc_conv_w)
    return jnp.concatenate([y_a, y_b, y_c, y_d], axis=-1) @ w_out


def _fwd_setup_inputs(seed: int = 0) -> dict:
    key = jax.random.key(seed)
    ks = jax.random.split(key, 26)
    f32 = jnp.float32

    def nrm(k, shape, s):
        return jax.random.normal(k, shape, f32) * s

    def gain(k, shape):
        return 1.0 + 0.05 * jax.random.normal(k, shape, f32)

    def dt_bias(k, shape):
        dt = jnp.exp(jax.random.uniform(k, shape, f32, math.log(1e-3), math.log(1e-1)))
        return dt + jnp.log(-jnp.expm1(-dt))

    def a_log(k, shape):
        return jnp.log(jax.random.uniform(k, shape, f32, 1.0, 16.0))

    return {
        'x': nrm(ks[0], (BATCH, SEQ, D_MODEL), 1.0),
        'c': nrm(ks[1], (BATCH, D_MODEL), 1.0),
        'positions': jnp.broadcast_to(jnp.arange(SEQ, dtype=jnp.int32), (BATCH, SEQ)),
        'ada_w': nrm(ks[2], (DEPTH, D_MODEL, 6 * D_MODEL), 0.5 * D_MODEL ** -0.5),
        'ada_b': nrm(ks[3], (DEPTH, 6 * D_MODEL), 0.02),
        'norm_pre_mix': gain(ks[4], (DEPTH, D_MODEL)),
        'norm_post_mix': gain(ks[5], (DEPTH, D_MODEL)),
        'norm_pre_mlp': gain(ks[6], (DEPTH, D_MODEL)),
        'norm_post_mlp': gain(ks[7], (DEPTH, D_MODEL)),
        'w_in': nrm(ks[8], (DEPTH, D_MODEL, IN_WIDTH), D_MODEL ** -0.5),
        'w_out': nrm(ks[9], (DEPTH, MIX_WIDTH, D_MODEL), MIX_WIDTH ** -0.5),
        'gdn_conv_w': nrm(ks[10], (DEPTH, GDN_CONV, 3 * GW), GDN_CONV ** -0.5),
        'gdn_a_log': a_log(ks[11], (DEPTH, GDN_HEADS)),
        'gdn_dt_bias': dt_bias(ks[12], (DEPTH, GDN_HEADS)),
        'gdn_norm_w': gain(ks[13], (DEPTH, GDN_HEAD_DIM)),
        'ssm_conv_w': nrm(ks[14], (DEPTH, SSM_CONV, SSM_CONV_DIM), SSM_CONV ** -0.5),
        'ssm_conv_b': nrm(ks[15], (DEPTH, SSM_CONV_DIM), 0.02),
        'ssm_a_log': a_log(ks[16], (DEPTH, SSM_HEADS)),
        'ssm_dt_bias': dt_bias(ks[17], (DEPTH, SSM_HEADS)),
        'ssm_d': gain(ks[18], (DEPTH, SSM_HEADS)),
        'ssm_norm_w': gain(ks[19], (DEPTH, GW)),
        'attn_sinks': nrm(ks[20], (DEPTH, ATTN_Q_HEADS), 0.5),
        'sc_conv_w': nrm(ks[21], (DEPTH, SC_CONV, GW), SC_CONV ** -0.5),
        'w_up': nrm(ks[22], (DEPTH, D_MODEL, MLP_HIDDEN), D_MODEL ** -0.5),
        'w_down': nrm(ks[23], (DEPTH, MLP_HIDDEN, D_MODEL), MLP_HIDDEN ** -0.5),
    }


def _fwd_reference(x, c, positions, ada_w, ada_b, norm_pre_mix, norm_post_mix, norm_pre_mlp, norm_post_mlp,
              w_in, w_out, gdn_conv_w, gdn_a_log, gdn_dt_bias, gdn_norm_w, ssm_conv_w, ssm_conv_b,
              ssm_a_log, ssm_dt_bias, ssm_d, ssm_norm_w, attn_sinks, sc_conv_w, w_up, w_down):
    cos, sin = rope_tables(positions)
    c_act = jax.nn.silu(c)
    for i in range(DEPTH):
        mod = (c_act @ ada_w[i] + ada_b[i])[:, None, :]
        shift_a, scale_a, gate_a, shift_m, scale_m, gate_m = jnp.split(mod, 6, axis=-1)
        h = rms_norm(x, norm_pre_mix[i]) * (1.0 + scale_a) + shift_a
        y = hybrid_mixer(h, cos, sin, w_in[i], w_out[i], gdn_conv_w[i], gdn_a_log[i], gdn_dt_bias[i],
                         gdn_norm_w[i], ssm_conv_w[i], ssm_conv_b[i], ssm_a_log[i], ssm_dt_bias[i],
                         ssm_d[i], ssm_norm_w[i], attn_sinks[i], sc_conv_w[i])
        x = x + gate_a * rms_norm(y, norm_post_mix[i])
        h = rms_norm(x, norm_pre_mlp[i]) * (1.0 + scale_m) + shift_m
        y = jnp.square(jax.nn.relu(h @ w_up[i])) @ w_down[i]
        x = x + gate_m * rms_norm(y, norm_post_mlp[i])
    return x


import jax as _jax
import jax.numpy as _jnp

TWIN_FORMAT = 'train_step'
FWD_PARAMS = ['x', 'c', 'positions', 'ada_w', 'ada_b', 'norm_pre_mix', 'norm_post_mix', 'norm_pre_mlp', 'norm_post_mlp', 'w_in', 'w_out', 'gdn_conv_w', 'gdn_a_log', 'gdn_dt_bias', 'gdn_norm_w', 'ssm_conv_w', 'ssm_conv_b', 'ssm_a_log', 'ssm_dt_bias', 'ssm_d', 'ssm_norm_w', 'attn_sinks', 'sc_conv_w', 'w_up', 'w_down']
TWIN_WEIGHTS = ['ada_w', 'ada_b', 'norm_pre_mix', 'norm_post_mix', 'norm_pre_mlp', 'norm_post_mlp', 'w_in', 'w_out', 'gdn_conv_w', 'gdn_a_log', 'gdn_dt_bias', 'gdn_norm_w', 'ssm_conv_w', 'ssm_conv_b', 'ssm_a_log', 'ssm_dt_bias', 'ssm_d', 'ssm_norm_w', 'attn_sinks', 'sc_conv_w', 'w_up', 'w_down']
TWIN_DIFF_INPUT = 'x'
TWIN_INPUTS = ['x', 'c', 'positions', 'ada_w', 'ada_b', 'norm_pre_mix', 'norm_post_mix', 'norm_pre_mlp', 'norm_post_mlp', 'w_in', 'w_out', 'gdn_conv_w', 'gdn_a_log', 'gdn_dt_bias', 'gdn_norm_w', 'ssm_conv_w', 'ssm_conv_b', 'ssm_a_log', 'ssm_dt_bias', 'ssm_d', 'ssm_norm_w', 'attn_sinks', 'sc_conv_w', 'w_up', 'w_down', 'loss_target', 'm_ada_w', 'm_ada_b', 'm_norm_pre_mix', 'm_norm_post_mix', 'm_norm_pre_mlp', 'm_norm_post_mlp', 'm_w_in', 'm_w_out', 'm_gdn_conv_w', 'm_gdn_a_log', 'm_gdn_dt_bias', 'm_gdn_norm_w', 'm_ssm_conv_w', 'm_ssm_conv_b', 'm_ssm_a_log', 'm_ssm_dt_bias', 'm_ssm_d', 'm_ssm_norm_w', 'm_attn_sinks', 'm_sc_conv_w', 'm_w_up', 'm_w_down', 'v_ada_w', 'v_ada_b', 'v_norm_pre_mix', 'v_norm_post_mix', 'v_norm_pre_mlp', 'v_norm_post_mlp', 'v_w_in', 'v_w_out', 'v_gdn_conv_w', 'v_gdn_a_log', 'v_gdn_dt_bias', 'v_gdn_norm_w', 'v_ssm_conv_w', 'v_ssm_conv_b', 'v_ssm_a_log', 'v_ssm_dt_bias', 'v_ssm_d', 'v_ssm_norm_w', 'v_attn_sinks', 'v_sc_conv_w', 'v_w_up', 'v_w_down']
TWIN_OUTPUTS = ['loss', 'grad_x', 'grad_ada_w', 'grad_ada_b', 'grad_norm_pre_mix', 'grad_norm_post_mix', 'grad_norm_pre_mlp', 'grad_norm_post_mlp', 'grad_w_in', 'grad_w_out', 'grad_gdn_conv_w', 'grad_gdn_a_log', 'grad_gdn_dt_bias', 'grad_gdn_norm_w', 'grad_ssm_conv_w', 'grad_ssm_conv_b', 'grad_ssm_a_log', 'grad_ssm_dt_bias', 'grad_ssm_d', 'grad_ssm_norm_w', 'grad_attn_sinks', 'grad_sc_conv_w', 'grad_w_up', 'grad_w_down', 'delta_ada_w', 'delta_ada_b', 'delta_norm_pre_mix', 'delta_norm_post_mix', 'delta_norm_pre_mlp', 'delta_norm_post_mlp', 'delta_w_in', 'delta_w_out', 'delta_gdn_conv_w', 'delta_gdn_a_log', 'delta_gdn_dt_bias', 'delta_gdn_norm_w', 'delta_ssm_conv_w', 'delta_ssm_conv_b', 'delta_ssm_a_log', 'delta_ssm_dt_bias', 'delta_ssm_d', 'delta_ssm_norm_w', 'delta_attn_sinks', 'delta_sc_conv_w', 'delta_w_up', 'delta_w_down', 'new_m_ada_w', 'new_m_ada_b', 'new_m_norm_pre_mix', 'new_m_norm_post_mix', 'new_m_norm_pre_mlp', 'new_m_norm_post_mlp', 'new_m_w_in', 'new_m_w_out', 'new_m_gdn_conv_w', 'new_m_gdn_a_log', 'new_m_gdn_dt_bias', 'new_m_gdn_norm_w', 'new_m_ssm_conv_w', 'new_m_ssm_conv_b', 'new_m_ssm_a_log', 'new_m_ssm_dt_bias', 'new_m_ssm_d', 'new_m_ssm_norm_w', 'new_m_attn_sinks', 'new_m_sc_conv_w', 'new_m_w_up', 'new_m_w_down', 'new_v_ada_w', 'new_v_ada_b', 'new_v_norm_pre_mix', 'new_v_norm_post_mix', 'new_v_norm_pre_mlp', 'new_v_norm_post_mlp', 'new_v_w_in', 'new_v_w_out', 'new_v_gdn_conv_w', 'new_v_gdn_a_log', 'new_v_gdn_dt_bias', 'new_v_gdn_norm_w', 'new_v_ssm_conv_w', 'new_v_ssm_conv_b', 'new_v_ssm_a_log', 'new_v_ssm_dt_bias', 'new_v_ssm_d', 'new_v_ssm_norm_w', 'new_v_attn_sinks', 'new_v_sc_conv_w', 'new_v_w_up', 'new_v_w_down']
TWIN_LEAF_KINDS = {'loss': 'loss', 'grad_x': 'grad_x', 'grad_ada_w': 'grad_w', 'grad_ada_b': 'grad_w', 'grad_norm_pre_mix': 'grad_w', 'grad_norm_post_mix': 'grad_w', 'grad_norm_pre_mlp': 'grad_w', 'grad_norm_post_mlp': 'grad_w', 'grad_w_in': 'grad_w', 'grad_w_out': 'grad_w', 'grad_gdn_conv_w': 'grad_w', 'grad_gdn_a_log': 'grad_w', 'grad_gdn_dt_bias': 'grad_w', 'grad_gdn_norm_w': 'grad_w', 'grad_ssm_conv_w': 'grad_w', 'grad_ssm_conv_b': 'grad_w', 'grad_ssm_a_log': 'grad_w', 'grad_ssm_dt_bias': 'grad_w', 'grad_ssm_d': 'grad_w', 'grad_ssm_norm_w': 'grad_w', 'grad_attn_sinks': 'grad_w', 'grad_sc_conv_w': 'grad_w', 'grad_w_up': 'grad_w', 'grad_w_down': 'grad_w', 'delta_ada_w': 'delta_w', 'delta_ada_b': 'delta_w', 'delta_norm_pre_mix': 'delta_w', 'delta_norm_post_mix': 'delta_w', 'delta_norm_pre_mlp': 'delta_w', 'delta_norm_post_mlp': 'delta_w', 'delta_w_in': 'delta_w', 'delta_w_out': 'delta_w', 'delta_gdn_conv_w': 'delta_w', 'delta_gdn_a_log': 'delta_w', 'delta_gdn_dt_bias': 'delta_w', 'delta_gdn_norm_w': 'delta_w', 'delta_ssm_conv_w': 'delta_w', 'delta_ssm_conv_b': 'delta_w', 'delta_ssm_a_log': 'delta_w', 'delta_ssm_dt_bias': 'delta_w', 'delta_ssm_d': 'delta_w', 'delta_ssm_norm_w': 'delta_w', 'delta_attn_sinks': 'delta_w', 'delta_sc_conv_w': 'delta_w', 'delta_w_up': 'delta_w', 'delta_w_down': 'delta_w', 'new_m_ada_w': 'new_m', 'new_m_ada_b': 'new_m', 'new_m_norm_pre_mix': 'new_m', 'new_m_norm_post_mix': 'new_m', 'new_m_norm_pre_mlp': 'new_m', 'new_m_norm_post_mlp': 'new_m', 'new_m_w_in': 'new_m', 'new_m_w_out': 'new_m', 'new_m_gdn_conv_w': 'new_m', 'new_m_gdn_a_log': 'new_m', 'new_m_gdn_dt_bias': 'new_m', 'new_m_gdn_norm_w': 'new_m', 'new_m_ssm_conv_w': 'new_m', 'new_m_ssm_conv_b': 'new_m', 'new_m_ssm_a_log': 'new_m', 'new_m_ssm_dt_bias': 'new_m', 'new_m_ssm_d': 'new_m', 'new_m_ssm_norm_w': 'new_m', 'new_m_attn_sinks': 'new_m', 'new_m_sc_conv_w': 'new_m', 'new_m_w_up': 'new_m', 'new_m_w_down': 'new_m', 'new_v_ada_w': 'new_v', 'new_v_ada_b': 'new_v', 'new_v_norm_pre_mix': 'new_v', 'new_v_norm_post_mix': 'new_v', 'new_v_norm_pre_mlp': 'new_v', 'new_v_norm_post_mlp': 'new_v', 'new_v_w_in': 'new_v', 'new_v_w_out': 'new_v', 'new_v_gdn_conv_w': 'new_v', 'new_v_gdn_a_log': 'new_v', 'new_v_gdn_dt_bias': 'new_v', 'new_v_gdn_norm_w': 'new_v', 'new_v_ssm_conv_w': 'new_v', 'new_v_ssm_conv_b': 'new_v', 'new_v_ssm_a_log': 'new_v', 'new_v_ssm_dt_bias': 'new_v', 'new_v_ssm_d': 'new_v', 'new_v_ssm_norm_w': 'new_v', 'new_v_attn_sinks': 'new_v', 'new_v_sc_conv_w': 'new_v', 'new_v_w_up': 'new_v', 'new_v_w_down': 'new_v'}


def _forward(args):
    return _fwd_reference(*[args[k] for k in FWD_PARAMS])


def _output_shape():
    def fwd():
        inp = _fwd_setup_inputs(0)
        return _fwd_reference(*[inp[k] for k in FWD_PARAMS])
    out = _jax.eval_shape(fwd)
    return out.shape, out.dtype

N_MICROBATCH = 1
ADAM_LR = 0.001
ADAM_B1 = 0.9
ADAM_B2 = 0.999
ADAM_EPS = 1e-08
ADAM_WD = 0.01
ADAM_STEP = 10
PER_EXAMPLE_BATCH_AXIS = {'x': 0, 'c': 0, 'positions': 0, 'loss_target': 0}
SHARED_INPUTS = []
_WEIGHT_DTYPES = {'ada_w': _jnp.float32, 'ada_b': _jnp.float32, 'norm_pre_mix': _jnp.float32, 'norm_post_mix': _jnp.float32, 'norm_pre_mlp': _jnp.float32, 'norm_post_mlp': _jnp.float32, 'w_in': _jnp.float32, 'w_out': _jnp.float32, 'gdn_conv_w': _jnp.float32, 'gdn_a_log': _jnp.float32, 'gdn_dt_bias': _jnp.float32, 'gdn_norm_w': _jnp.float32, 'ssm_conv_w': _jnp.float32, 'ssm_conv_b': _jnp.float32, 'ssm_a_log': _jnp.float32, 'ssm_dt_bias': _jnp.float32, 'ssm_d': _jnp.float32, 'ssm_norm_w': _jnp.float32, 'attn_sinks': _jnp.float32, 'sc_conv_w': _jnp.float32, 'w_up': _jnp.float32, 'w_down': _jnp.float32}
MOMENT_SCALE = {'ada_w': 1.254105e+00, 'ada_b': 2.745195e+00, 'norm_pre_mix': 1.099167e-01, 'norm_post_mix': 3.280217e+00, 'norm_pre_mlp': 9.159857e-02, 'norm_post_mlp': 3.371600e+00, 'w_in': 9.452673e-02, 'w_out': 1.694480e-01, 'gdn_conv_w': 7.006472e-02, 'gdn_a_log': 1.917558e-01, 'gdn_dt_bias': 1.913522e-01, 'gdn_norm_w': 3.195776e-01, 'ssm_conv_w': 1.102664e-01, 'ssm_conv_b': 2.604677e-01, 'ssm_a_log': 1.281821e+00, 'ssm_dt_bias': 1.571050e-01, 'ssm_d': 7.005788e-01, 'ssm_norm_w': 2.190532e-01, 'attn_sinks': 1.577488e-02, 'sc_conv_w': 9.916165e-02, 'w_up': 7.396818e-02, 'w_down': 3.598079e-01}


def _to_microbatches(a, axis):
    t = _jnp.moveaxis(a, axis, 0)
    t = t.reshape((N_MICROBATCH, t.shape[0] // N_MICROBATCH) + t.shape[1:])
    return _jnp.moveaxis(t, 1, axis + 1)


def setup_inputs(seed: int = 0) -> dict:
    inp = _fwd_setup_inputs(seed)
    key = _jax.random.fold_in(_jax.random.key(seed), 7919)
    shape, _ = _output_shape()
    out = dict(inp)
    out["loss_target"] = _jax.random.normal(_jax.random.fold_in(key, 0), shape, _jnp.float32)
    for i, name in enumerate(TWIN_WEIGHTS):
        w = inp[name].astype(_jnp.float32)
        if MOMENT_SCALE is None:
            s = _jnp.sqrt(_jnp.mean(_jnp.square(w)) + 1e-30)
        else:
            s = MOMENT_SCALE[name]
        km, kv = _jax.random.split(_jax.random.fold_in(key, i + 1))
        out[name] = w
        out["m_" + name] = s * _jax.random.normal(km, w.shape, _jnp.float32)
        out["v_" + name] = (s * s) * _jax.random.uniform(kv, w.shape, _jnp.float32, 0.5, 1.5)
    if N_MICROBATCH > 1:
        for name, axis in PER_EXAMPLE_BATCH_AXIS.items():
            out[name] = _to_microbatches(out[name], axis)
    return {'x': out['x'], 'c': out['c'], 'positions': out['positions'], 'ada_w': out['ada_w'], 'ada_b': out['ada_b'], 'norm_pre_mix': out['norm_pre_mix'], 'norm_post_mix': out['norm_post_mix'], 'norm_pre_mlp': out['norm_pre_mlp'], 'norm_post_mlp': out['norm_post_mlp'], 'w_in': out['w_in'], 'w_out': out['w_out'], 'gdn_conv_w': out['gdn_conv_w'], 'gdn_a_log': out['gdn_a_log'], 'gdn_dt_bias': out['gdn_dt_bias'], 'gdn_norm_w': out['gdn_norm_w'], 'ssm_conv_w': out['ssm_conv_w'], 'ssm_conv_b': out['ssm_conv_b'], 'ssm_a_log': out['ssm_a_log'], 'ssm_dt_bias': out['ssm_dt_bias'], 'ssm_d': out['ssm_d'], 'ssm_norm_w': out['ssm_norm_w'], 'attn_sinks': out['attn_sinks'], 'sc_conv_w': out['sc_conv_w'], 'w_up': out['w_up'], 'w_down': out['w_down'], 'loss_target': out['loss_target'], 'm_ada_w': out['m_ada_w'], 'm_ada_b': out['m_ada_b'], 'm_norm_pre_mix': out['m_norm_pre_mix'], 'm_norm_post_mix': out['m_norm_post_mix'], 'm_norm_pre_mlp': out['m_norm_pre_mlp'], 'm_norm_post_mlp': out['m_norm_post_mlp'], 'm_w_in': out['m_w_in'], 'm_w_out': out['m_w_out'], 'm_gdn_conv_w': out['m_gdn_conv_w'], 'm_gdn_a_log': out['m_gdn_a_log'], 'm_gdn_dt_bias': out['m_gdn_dt_bias'], 'm_gdn_norm_w': out['m_gdn_norm_w'], 'm_ssm_conv_w': out['m_ssm_conv_w'], 'm_ssm_conv_b': out['m_ssm_conv_b'], 'm_ssm_a_log': out['m_ssm_a_log'], 'm_ssm_dt_bias': out['m_ssm_dt_bias'], 'm_ssm_d': out['m_ssm_d'], 'm_ssm_norm_w': out['m_ssm_norm_w'], 'm_attn_sinks': out['m_attn_sinks'], 'm_sc_conv_w': out['m_sc_conv_w'], 'm_w_up': out['m_w_up'], 'm_w_down': out['m_w_down'], 'v_ada_w': out['v_ada_w'], 'v_ada_b': out['v_ada_b'], 'v_norm_pre_mix': out['v_norm_pre_mix'], 'v_norm_post_mix': out['v_norm_post_mix'], 'v_norm_pre_mlp': out['v_norm_pre_mlp'], 'v_norm_post_mlp': out['v_norm_post_mlp'], 'v_w_in': out['v_w_in'], 'v_w_out': out['v_w_out'], 'v_gdn_conv_w': out['v_gdn_conv_w'], 'v_gdn_a_log': out['v_gdn_a_log'], 'v_gdn_dt_bias': out['v_gdn_dt_bias'], 'v_gdn_norm_w': out['v_gdn_norm_w'], 'v_ssm_conv_w': out['v_ssm_conv_w'], 'v_ssm_conv_b': out['v_ssm_conv_b'], 'v_ssm_a_log': out['v_ssm_a_log'], 'v_ssm_dt_bias': out['v_ssm_dt_bias'], 'v_ssm_d': out['v_ssm_d'], 'v_ssm_norm_w': out['v_ssm_norm_w'], 'v_attn_sinks': out['v_attn_sinks'], 'v_sc_conv_w': out['v_sc_conv_w'], 'v_w_up': out['v_w_up'], 'v_w_down': out['v_w_down']}


def _loss(weights, diff, rest, loss_target):
    with _jax.named_scope("forward"):
        args = {**rest, TWIN_DIFF_INPUT: diff, **{k: w.astype(_WEIGHT_DTYPES[k]) for k, w in weights.items()}}
        y = _forward(args)
    with _jax.named_scope("loss_head"):
        err = _jnp.square(y.astype(_jnp.float32) - loss_target)
        return 0.5 * _jnp.sum(_jnp.mean(err, axis=-1)) if err.ndim else 0.5 * err


def _adamw(w, g, m, v):
    m = ADAM_B1 * m + (1.0 - ADAM_B1) * g
    v = ADAM_B2 * v + (1.0 - ADAM_B2) * _jnp.square(g)
    m_hat = m / (1.0 - ADAM_B1 ** ADAM_STEP)
    v_hat = v / (1.0 - ADAM_B2 ** ADAM_STEP)
    delta = -ADAM_LR * (m_hat / (_jnp.sqrt(v_hat) + ADAM_EPS) + ADAM_WD * w)
    return delta, m, v


def reference(x, c, positions, ada_w, ada_b, norm_pre_mix, norm_post_mix, norm_pre_mlp, norm_post_mlp, w_in, w_out, gdn_conv_w, gdn_a_log, gdn_dt_bias, gdn_norm_w, ssm_conv_w, ssm_conv_b, ssm_a_log, ssm_dt_bias, ssm_d, ssm_norm_w, attn_sinks, sc_conv_w, w_up, w_down, loss_target, m_ada_w, m_ada_b, m_norm_pre_mix, m_norm_post_mix, m_norm_pre_mlp, m_norm_post_mlp, m_w_in, m_w_out, m_gdn_conv_w, m_gdn_a_log, m_gdn_dt_bias, m_gdn_norm_w, m_ssm_conv_w, m_ssm_conv_b, m_ssm_a_log, m_ssm_dt_bias, m_ssm_d, m_ssm_norm_w, m_attn_sinks, m_sc_conv_w, m_w_up, m_w_down, v_ada_w, v_ada_b, v_norm_pre_mix, v_norm_post_mix, v_norm_pre_mlp, v_norm_post_mlp, v_w_in, v_w_out, v_gdn_conv_w, v_gdn_a_log, v_gdn_dt_bias, v_gdn_norm_w, v_ssm_conv_w, v_ssm_conv_b, v_ssm_a_log, v_ssm_dt_bias, v_ssm_d, v_ssm_norm_w, v_attn_sinks, v_sc_conv_w, v_w_up, v_w_down):
    given = dict(x=x, c=c, positions=positions, ada_w=ada_w, ada_b=ada_b, norm_pre_mix=norm_pre_mix, norm_post_mix=norm_post_mix, norm_pre_mlp=norm_pre_mlp, norm_post_mlp=norm_post_mlp, w_in=w_in, w_out=w_out, gdn_conv_w=gdn_conv_w, gdn_a_log=gdn_a_log, gdn_dt_bias=gdn_dt_bias, gdn_norm_w=gdn_norm_w, ssm_conv_w=ssm_conv_w, ssm_conv_b=ssm_conv_b, ssm_a_log=ssm_a_log, ssm_dt_bias=ssm_dt_bias, ssm_d=ssm_d, ssm_norm_w=ssm_norm_w, attn_sinks=attn_sinks, sc_conv_w=sc_conv_w, w_up=w_up, w_down=w_down, loss_target=loss_target, m_ada_w=m_ada_w, m_ada_b=m_ada_b, m_norm_pre_mix=m_norm_pre_mix, m_norm_post_mix=m_norm_post_mix, m_norm_pre_mlp=m_norm_pre_mlp, m_norm_post_mlp=m_norm_post_mlp, m_w_in=m_w_in, m_w_out=m_w_out, m_gdn_conv_w=m_gdn_conv_w, m_gdn_a_log=m_gdn_a_log, m_gdn_dt_bias=m_gdn_dt_bias, m_gdn_norm_w=m_gdn_norm_w, m_ssm_conv_w=m_ssm_conv_w, m_ssm_conv_b=m_ssm_conv_b, m_ssm_a_log=m_ssm_a_log, m_ssm_dt_bias=m_ssm_dt_bias, m_ssm_d=m_ssm_d, m_ssm_norm_w=m_ssm_norm_w, m_attn_sinks=m_attn_sinks, m_sc_conv_w=m_sc_conv_w, m_w_up=m_w_up, m_w_down=m_w_down, v_ada_w=v_ada_w, v_ada_b=v_ada_b, v_norm_pre_mix=v_norm_pre_mix, v_norm_post_mix=v_norm_post_mix, v_norm_pre_mlp=v_norm_pre_mlp, v_norm_post_mlp=v_norm_post_mlp, v_w_in=v_w_in, v_w_out=v_w_out, v_gdn_conv_w=v_gdn_conv_w, v_gdn_a_log=v_gdn_a_log, v_gdn_dt_bias=v_gdn_dt_bias, v_gdn_norm_w=v_gdn_norm_w, v_ssm_conv_w=v_ssm_conv_w, v_ssm_conv_b=v_ssm_conv_b, v_ssm_a_log=v_ssm_a_log, v_ssm_dt_bias=v_ssm_dt_bias, v_ssm_d=v_ssm_d, v_ssm_norm_w=v_ssm_norm_w, v_attn_sinks=v_attn_sinks, v_sc_conv_w=v_sc_conv_w, v_w_up=v_w_up, v_w_down=v_w_down)
    weights = {n: given[n] for n in TWIN_WEIGHTS}
    shared = {n: given[n] for n in SHARED_INPUTS}
    per_example = {n: given[n] for n in ['x', 'c', 'positions']}
    grad_fn = _jax.value_and_grad(_loss, argnums=(0, 1))

    def one_microbatch(ex, loss_target):
        ex = dict(ex)
        diff = ex.pop(TWIN_DIFF_INPUT)
        return grad_fn(weights, diff, {**shared, **ex}, loss_target)

    if N_MICROBATCH == 1:
        loss, (grad_w, grad_x) = one_microbatch(per_example, given["loss_target"])
    else:
        def body(carry, xs):
            loss_sum, grad_sum = carry
            l_k, (gw_k, gx_k) = one_microbatch(xs[0], xs[1])
            with _jax.named_scope("update"):
                return (loss_sum + l_k, _jax.tree.map(_jnp.add, grad_sum, gw_k)), gx_k

        init = (_jnp.zeros((), _jnp.float32), _jax.tree.map(_jnp.zeros_like, weights))
        (loss, grad_w), grad_x = _jax.lax.scan(body, init, (per_example, given["loss_target"]))
    with _jax.named_scope("update"):
        delta_w, new_m, new_v = {}, {}, {}
        for n in TWIN_WEIGHTS:
            delta_w[n], new_m[n], new_v[n] = _adamw(weights[n], grad_w[n], given["m_" + n], given["v_" + n])
    return (loss, grad_x, *[grad_w[n] for n in TWIN_WEIGHTS], *[delta_w[n] for n in TWIN_WEIGHTS],
            *[new_m[n] for n in TWIN_WEIGHTS], *[new_v[n] for n in TWIN_WEIGHTS])
```

```python
import functools
import math

import jax
import jax.numpy as jnp
import numpy as np
from jax import lax
from jax.experimental import pallas as pl
from jax.experimental.pallas import tpu as pltpu

F32 = jnp.float32
BF16 = jnp.bfloat16
HI = lax.Precision.HIGHEST
MESH = pl.DeviceIdType.MESH

N_DEV = 8
D = 2048
GW = 512
MLP = 8192
EPS = 1e-6
IN_W = 5904
SHARD_IN = IN_W // N_DEV
PW = 6272
ROPE_THETA = 10000.0
WINDOW = 128
GDN_CHUNK = 64
SSM_CHUNK = 128
NEG = -1e30

ADAM_LR, ADAM_B1, ADAM_B2, ADAM_EPS, ADAM_WD, ADAM_STEP = 0.001, 0.9, 0.999, 1e-08, 0.01, 10

O_GQ, O_GK, O_GV, O_GZ, O_GB, O_GA = 0, 512, 1024, 1536, 2048, 2052
O_SZ, O_SXBC, O_SDT = 2056, 2568, 3592
O_AQ, O_AK, O_AV = 3600, 4112, 4240
O_CB, O_CC, O_CH = 4368, 4880, 5392
P_QKV, P_GZ, P_SXBC, P_SZ, P_AQ, P_CB, P_AK, P_AV, P_GATE = 0, 1536, 2048, 3072, 3584, 4096, 5632, 5888, 6144
LANE_GB, LANE_GA, LANE_SDT = 0, 4, 8

VMEM_BIG = 56 * 1024 * 1024


def _cparams(sem=None, vmem=None):
    kw = {}
    if sem is not None:
        kw["dimension_semantics"] = sem
    if vmem is not None:
        kw["vmem_limit_bytes"] = vmem
    return pltpu.CompilerParams(**kw)


def _me():
    x, y, c = lax.axis_index("x"), lax.axis_index("y"), lax.axis_index("c")
    return x, y, c, 4 * x + 2 * y + c


def _peer(x, y, c, k):
    px = 1 - x if (k >> 2) & 1 else x
    py = 1 - y if (k >> 1) & 1 else y
    pc = 1 - c if k & 1 else c
    return (px, py, pc), 4 * px + 2 * py + pc


def _ag_small(v, name):
    R, W = v.shape

    def body(x_ref, out_ref, send_sems, recv_sems):
        x, y, c, me = _me()
        out_ref[me] = x_ref[...]
        copies = []
        for k in range(1, N_DEV):
            peer, _ = _peer(x, y, c, k)
            cp = pltpu.make_async_remote_copy(
                src_ref=x_ref, dst_ref=out_ref.at[me], send_sem=send_sems.at[k - 1], recv_sem=recv_sems.at[k - 1],
                device_id=peer, device_id_type=MESH)
            cp.start()
            copies.append(cp)
        for cp in copies:
            cp.wait()

    return pl.pallas_call(
        body, name=name,
        out_shape=jax.ShapeDtypeStruct((N_DEV, R, W), v.dtype),
        in_specs=[pl.BlockSpec(memory_space=pltpu.VMEM)],
        out_specs=pl.BlockSpec(memory_space=pltpu.VMEM),
        scratch_shapes=[pltpu.SemaphoreType.DMA((N_DEV - 1,)), pltpu.SemaphoreType.DMA((N_DEV - 1,))],
    )(v)


def _ag_big(v, name):
    R, W = v.shape

    def body(x_ref, out_ref, send_sems, recv_sems, local_sem):
        x, y, c, me = _me()
        sibling = (x, y, 1 - c)
        chips = [(1 - x, y), (x, 1 - y), (1 - x, 1 - y)]

        def slot(px, py, pc):
            return out_ref.at[4 * px + 2 * py + pc]

        def copy(k, block, to, src=None):
            return pltpu.make_async_remote_copy(
                src_ref=slot(*block) if src is None else src, dst_ref=slot(*block),
                send_sem=send_sems.at[k], recv_sem=recv_sems.at[k], device_id=to, device_id_type=MESH)

        mine = pltpu.make_async_copy(x_ref, slot(x, y, c), local_sem)
        mine.start()
        first = [copy(0, (x, y, c), sibling, src=x_ref)]
        first += [copy(1 + j, (x, y, c), (*chip, c), src=x_ref) for j, chip in enumerate(chips)]
        for cp in first:
            cp.start()
        passed = [copy(4 + j, (*chip, c), sibling) for j, chip in enumerate(chips)]
        for j, chip in enumerate(chips):
            copy(1 + j, (*chip, c), (x, y, c)).wait_recv()
            passed[j].start()
        copy(0, sibling, (x, y, c)).wait_recv()
        for j, chip in enumerate(chips):
            copy(4 + j, (*chip, 1 - c), (x, y, c)).wait_recv()
        for cp in first + passed:
            cp.wait_send()
        mine.wait()

    return pl.pallas_call(
        body, name=name,
        out_shape=jax.ShapeDtypeStruct((N_DEV, R, W), v.dtype),
        in_specs=[pl.BlockSpec(memory_space=pl.ANY)],
        out_specs=pl.BlockSpec(memory_space=pl.ANY),
        scratch_shapes=[pltpu.SemaphoreType.DMA((7,)), pltpu.SemaphoreType.DMA((7,)), pltpu.SemaphoreType.DMA],
    )(v)


def _exchange_big(g, name):
    _, R, W = g.shape

    def body(g_ref, out_ref, send_sems, recv_sems, local_sem):
        x, y, c, me = _me()
        mine = pltpu.make_async_copy(g_ref.at[me], out_ref.at[me], local_sem)
        mine.start()
        copies = []
        for k in range(1, N_DEV):
            peer, pid = _peer(x, y, c, k)
            cp = pltpu.make_async_remote_copy(
                src_ref=g_ref.at[pid], dst_ref=out_ref.at[me], send_sem=send_sems.at[k - 1],
                recv_sem=recv_sems.at[k - 1], device_id=peer, device_id_type=MESH)
            cp.start()
            copies.append(cp)
        for cp in copies:
            cp.wait()
        mine.wait()

    return pl.pallas_call(
        body, name=name,
        out_shape=jax.ShapeDtypeStruct((N_DEV, R, W), g.dtype),
        in_specs=[pl.BlockSpec(memory_space=pl.ANY)],
        out_specs=pl.BlockSpec(memory_space=pl.ANY),
        scratch_shapes=[pltpu.SemaphoreType.DMA((7,)), pltpu.SemaphoreType.DMA((7,)), pltpu.SemaphoreType.DMA],
    )(g)


def _mm(a, b, *, name, ta=False, tb=False, tm=1024, tn=1024, tk=1024, out_dtype=F32, mode="plain", u=None):
    K, M = a.shape if ta else a.shape[::-1]
    N, K2 = b.shape if tb else b.shape[::-1]
    assert K == K2, (a.shape, b.shape)
    tm, tn, tk = min(tm, M), min(tn, N), min(tk, K)
    assert M % tm == 0 and N % tn == 0 and K % tk == 0, (M, N, K, tm, tn, tk)
    nk = K // tk
    a_spec = pl.BlockSpec((tk, tm), lambda i, j, k: (k, i)) if ta else pl.BlockSpec((tm, tk), lambda i, j, k: (i, k))
    b_spec = pl.BlockSpec((tn, tk), lambda i, j, k: (j, k)) if tb else pl.BlockSpec((tk, tn), lambda i, j, k: (k, j))
    o_spec = pl.BlockSpec((tm, tn), lambda i, j, k: (i, j))
    dn = (((0 if ta else 1,), (1 if tb else 0,)), ((), ()))

    def body(*refs):
        a_ref, b_ref = refs[0], refs[1]
        acc = refs[-1]
        k = pl.program_id(2)

        @pl.when(k == 0)
        def _():
            acc[...] = jnp.zeros_like(acc)

        acc[...] += lax.dot_general(a_ref[...].astype(BF16), b_ref[...].astype(BF16), dn, preferred_element_type=F32)

        @pl.when(k == nk - 1)
        def _():
            r = acc[...]
            if mode == "plain":
                refs[2][...] = r.astype(out_dtype)
            elif mode == "relu2":
                rr = jnp.maximum(r, 0.0)
                refs[2][...] = (rr * rr).astype(BF16)
                refs[3][...] = r.astype(BF16)
            else:
                uu = refs[2][...].astype(F32)
                refs[3][...] = (r * (2.0 * jnp.maximum(uu, 0.0))).astype(out_dtype)

    in_specs, args = [a_spec, b_spec], [a, b]
    if mode == "drelu2":
        in_specs.append(o_spec)
        args.append(u)
    if mode == "relu2":
        out_shape = (jax.ShapeDtypeStruct((M, N), BF16), jax.ShapeDtypeStruct((M, N), BF16))
        out_specs = (o_spec, o_spec)
    else:
        out_shape = jax.ShapeDtypeStruct((M, N), out_dtype)
        out_specs = o_spec
    return pl.pallas_call(
        body, name=name, grid=(M // tm, N // tn, nk), in_specs=in_specs, out_specs=out_specs, out_shape=out_shape,
        scratch_shapes=[pltpu.VMEM((tm, tn), F32)],
        compiler_params=_cparams(("parallel", "parallel", "arbitrary"), VMEM_BIG),
    )(*args)


ROW_T = 256


def _rms(x, w):
    return x * lax.rsqrt(jnp.mean(x * x, axis=-1, keepdims=True) + EPS) * w


def _pre_fn(x, w, scale, shift):
    return _rms(x, w) * (1.0 + scale) + shift


def _post_fn(y, w, gate):
    return gate * _rms(y, w)


def _row_spec(T, W):
    return pl.BlockSpec((T, W), lambda i: (i, 0))


def _vec_spec(W):
    return pl.BlockSpec((1, W), lambda i: (0, 0))


def _pre_fwd(x, w, scale, shift, name):
    L = x.shape[0]
    T = min(ROW_T, L)

    def body(x_ref, w_ref, sc_ref, sh_ref, h_ref):
        h_ref[...] = _pre_fn(x_ref[...], w_ref[...], sc_ref[...], sh_ref[...]).astype(BF16)

    return pl.pallas_call(
        body, name=name, grid=(L // T,), in_specs=[_row_spec(T, D), _vec_spec(D), _vec_spec(D), _vec_spec(D)],
        out_specs=_row_spec(T, D), out_shape=jax.ShapeDtypeStruct((L, D), BF16),
        compiler_params=_cparams(("parallel",), VMEM_BIG))(x, w, scale, shift)


def _pre_bwd(x, w, scale, shift, dh, dres, name):
    L = x.shape[0]
    T = min(ROW_T, L)

    def body(x_ref, w_ref, sc_ref, sh_ref, dh_ref, dres_ref, dx_ref, dw_ref, dsc_ref, dsh_ref):
        @pl.when(pl.program_id(0) == 0)
        def _():
            dw_ref[...] = jnp.zeros_like(dw_ref)
            dsc_ref[...] = jnp.zeros_like(dsc_ref)
            dsh_ref[...] = jnp.zeros_like(dsh_ref)

        _, vjp = jax.vjp(_pre_fn, x_ref[...], w_ref[...], sc_ref[...], sh_ref[...])
        dx, dw, dsc, dsh = vjp(dh_ref[...].astype(F32))
        dx_ref[...] = dres_ref[...] + dx
        dw_ref[...] += dw
        dsc_ref[...] += dsc
        dsh_ref[...] += dsh

    vec = jax.ShapeDtypeStruct((1, D), F32)
    return pl.pallas_call(
        body, name=name, grid=(L // T,),
        in_specs=[_row_spec(T, D), _vec_spec(D), _vec_spec(D), _vec_spec(D), _row_spec(T, D), _row_spec(T, D)],
        out_specs=(_row_spec(T, D), _vec_spec(D), _vec_spec(D), _vec_spec(D)),
        out_shape=(jax.ShapeDtypeStruct((L, D), F32), vec, vec, vec),
        compiler_params=_cparams(("arbitrary",), VMEM_BIG))(x, w, scale, shift, dh, dres)


def _post_fwd(x, y, w, gate, name):
    L = x.shape[0]
    T = min(ROW_T, L)

    def body(x_ref, y_ref, w_ref, g_ref, o_ref):
        o_ref[...] = x_ref[...] + _post_fn(y_ref[...], w_ref[...], g_ref[...])

    return pl.pallas_call(
        body, name=name, grid=(L // T,), in_specs=[_row_spec(T, D), _row_spec(T, D), _vec_spec(D), _vec_spec(D)],
        out_specs=_row_spec(T, D), out_shape=jax.ShapeDtypeStruct((L, D), F32),
        compiler_params=_cparams(("parallel",), VMEM_BIG))(x, y, w, gate)


def _post_bwd(y, w, gate, dxn, name):
    L = y.shape[0]
    T = min(ROW_T, L)

    def body(y_ref, w_ref, g_ref, d_ref, dy_ref, dw_ref, dg_ref):
        @pl.when(pl.program_id(0) == 0)
        def _():
            dw_ref[...] = jnp.zeros_like(dw_ref)
            dg_ref[...] = jnp.zeros_like(dg_ref)

        _, vjp = jax.vjp(_post_fn, y_ref[...], w_ref[...], g_ref[...])
        dy, dw, dg = vjp(d_ref[...])
        dy_ref[...] = dy.astype(BF16)
        dw_ref[...] += dw
        dg_ref[...] += dg

    vec = jax.ShapeDtypeStruct((1, D), F32)
    return pl.pallas_call(
        body, name=name, grid=(L // T,), in_specs=[_row_spec(T, D), _vec_spec(D), _vec_spec(D), _row_spec(T, D)],
        out_specs=(_row_spec(T, D), _vec_spec(D), _vec_spec(D)),
        out_shape=(jax.ShapeDtypeStruct((L, D), BF16), vec, vec),
        compiler_params=_cparams(("arbitrary",), VMEM_BIG))(y, w, gate, dxn)


def _loss_head(y, target, name):
    L = y.shape[0]
    T = min(ROW_T, L)

    def body(y_ref, t_ref, s_ref, d_ref):
        @pl.when(pl.program_id(0) == 0)
        def _():
            s_ref[...] = jnp.zeros_like(s_ref)

        e = y_ref[...] - t_ref[...]
        d_ref[...] = e / float(D)
        s_ref[...] += jnp.sum(e * e)

    return pl.pallas_call(
        body, name=name, grid=(L // T,), in_specs=[_row_spec(T, D), _row_spec(T, D)],
        out_specs=(pl.BlockSpec((8, 128), lambda i: (0, 0)), _row_spec(T, D)),
        out_shape=(jax.ShapeDtypeStruct((8, 128), F32), jax.ShapeDtypeStruct((L, D), F32)),
        compiler_params=_cparams(("arbitrary",), VMEM_BIG))(y, target)


CONV_T = 512


def _conv_fwd(x, ci, C, w, b, name):
    L = x.shape[0]
    K = w.shape[0]
    T = min(CONV_T, L)

    def body(xc_ref, xp_ref, w_ref, b_ref, y_ref, buf):
        i = pl.program_id(0)
        buf[0:8, :] = jnp.where(i > 0, xp_ref[...], 0.0)
        buf[8:T + 8, :] = xc_ref[...]
        acc = jnp.zeros((T, C), F32) + b_ref[...]
        for k in range(K):
            acc = acc + w_ref[k:k + 1, :] * buf[pl.ds(8 - (K - 1) + k, T), :]
        y_ref[...] = acc

    return pl.pallas_call(
        body, name=name, grid=(L // T,),
        in_specs=[pl.BlockSpec((T, C), lambda i: (i, ci)),
                  pl.BlockSpec((8, C), lambda i: (jnp.maximum(i * (T // 8) - 1, 0), ci)),
                  pl.BlockSpec((K, C), lambda i: (0, 0)), _vec_spec(C)],
        out_specs=_row_spec(T, C), out_shape=jax.ShapeDtypeStruct((L, C), F32),
        scratch_shapes=[pltpu.VMEM((T + 8, C), F32)],
        compiler_params=_cparams(("parallel",), VMEM_BIG))(x, x, w, b)


def _conv_bwd(dy, x, ci, C, w, name):
    L = x.shape[0]
    K = w.shape[0]
    T = min(CONV_T, L)
    nt = L // T

    def body(dc_ref, dn_ref, xc_ref, xp_ref, w_ref, dx_ref, dw_ref, db_ref, dbuf, xbuf):
        i = pl.program_id(0)

        @pl.when(i == 0)
        def _():
            dw_ref[...] = jnp.zeros_like(dw_ref)
            db_ref[...] = jnp.zeros_like(db_ref)

        dyc = dc_ref[...]
        dbuf[0:T, :] = dyc
        dbuf[T:T + 8, :] = jnp.where(i < nt - 1, dn_ref[...], 0.0)
        xbuf[0:8, :] = jnp.where(i > 0, xp_ref[...], 0.0)
        xbuf[8:T + 8, :] = xc_ref[...]
        dx = jnp.zeros((T, C), F32)
        for k in range(K):
            dx = dx + w_ref[k:k + 1, :] * dbuf[pl.ds(K - 1 - k, T), :]
            dw_ref[k:k + 1, :] += jnp.sum(dyc * xbuf[pl.ds(8 - (K - 1) + k, T), :], axis=0, keepdims=True)
        dx_ref[...] = dx.astype(BF16)
        db_ref[...] += jnp.sum(dyc, axis=0, keepdims=True)

    return pl.pallas_call(
        body, name=name, grid=(nt,),
        in_specs=[_row_spec(T, C),
                  pl.BlockSpec((8, C), lambda i: (jnp.minimum((i + 1) * (T // 8), L // 8 - 1), 0)),
                  pl.BlockSpec((T, C), lambda i: (i, ci)),
                  pl.BlockSpec((8, C), lambda i: (jnp.maximum(i * (T // 8) - 1, 0), ci)),
                  pl.BlockSpec((K, C), lambda i: (0, 0))],
        out_specs=(_row_spec(T, C), pl.BlockSpec((K, C), lambda i: (0, 0)), _vec_spec(C)),
        out_shape=(jax.ShapeDtypeStruct((L, C), BF16), jax.ShapeDtypeStruct((K, C), F32),
                   jax.ShapeDtypeStruct((1, C), F32)),
        scratch_shapes=[pltpu.VMEM((T + 8, C), F32), pltpu.VMEM((T + 8, C), F32)],
        compiler_params=_cparams(("arbitrary",), VMEM_BIG))(dy, dy, x, x, w)


def _sc_fwd(proj, w, name):
    L = proj.shape[0]
    K = w.shape[0]
    C = GW
    T = min(CONV_T, L)
    cb0 = P_CB // C

    def body(b_ref, cc_ref, ch_ref, cp_ref, hp_ref, w_ref, y_ref, buf):
        i = pl.program_id(0)
        buf[0:8, :] = jnp.where(i > 0, cp_ref[...] * hp_ref[...], 0.0)
        buf[8:T + 8, :] = cc_ref[...] * ch_ref[...]
        acc = jnp.zeros((T, C), F32)
        for k in range(K):
            acc = acc + w_ref[k:k + 1, :] * buf[pl.ds(8 - (K - 1) + k, T), :]
        y_ref[...] = (b_ref[...] * acc).astype(BF16)

    def cur(j):
        return pl.BlockSpec((T, C), lambda i: (i, cb0 + j))

    def prev(j):
        return pl.BlockSpec((8, C), lambda i: (jnp.maximum(i * (T // 8) - 1, 0), cb0 + j))

    return pl.pallas_call(
        body, name=name, grid=(L // T,),
        in_specs=[cur(0), cur(1), cur(2), prev(1), prev(2), pl.BlockSpec((K, C), lambda i: (0, 0))],
        out_specs=_row_spec(T, C), out_shape=jax.ShapeDtypeStruct((L, C), BF16),
        scratch_shapes=[pltpu.VMEM((T + 8, C), F32)],
        compiler_params=_cparams(("parallel",), VMEM_BIG))(proj, proj, proj, proj, proj, w)


def _sc_bwd(dycat, proj, w, name):
    L = proj.shape[0]
    K = w.shape[0]
    C = GW
    T = min(CONV_T, L)
    nt = L // T
    cb0 = P_CB // C

    def body(dy_ref, dyn_ref, b_ref, bn_ref, cc_ref, ch_ref, cp_ref, hp_ref, w_ref, d_ref, dw_ref, gbuf, ubuf):
        i = pl.program_id(0)

        @pl.when(i == 0)
        def _():
            dw_ref[...] = jnp.zeros_like(dw_ref)

        dy = dy_ref[...]
        cc, ch = cc_ref[...], ch_ref[...]
        g = dy * b_ref[...]
        gbuf[0:T, :] = g
        gbuf[T:T + 8, :] = jnp.where(i < nt - 1, dyn_ref[...] * bn_ref[...], 0.0)
        ubuf[0:8, :] = jnp.where(i > 0, cp_ref[...] * hp_ref[...], 0.0)
        ubuf[8:T + 8, :] = cc * ch
        conv = jnp.zeros((T, C), F32)
        du = jnp.zeros((T, C), F32)
        for k in range(K):
            us = ubuf[pl.ds(8 - (K - 1) + k, T), :]
            conv = conv + w_ref[k:k + 1, :] * us
            du = du + w_ref[k:k + 1, :] * gbuf[pl.ds(K - 1 - k, T), :]
            dw_ref[k:k + 1, :] += jnp.sum(g * us, axis=0, keepdims=True)
        d_ref[:, 0:C] = (dy * conv).astype(BF16)
        d_ref[:, C:2 * C] = (du * ch).astype(BF16)
        d_ref[:, 2 * C:3 * C] = (du * cc).astype(BF16)

    def cur(j):
        return pl.BlockSpec((T, C), lambda i: (i, cb0 + j))

    def prev(j):
        return pl.BlockSpec((8, C), lambda i: (jnp.maximum(i * (T // 8) - 1, 0), cb0 + j))

    def nxt(col):
        return pl.BlockSpec((8, C), lambda i: (jnp.minimum((i + 1) * (T // 8), L // 8 - 1), col))

    return pl.pallas_call(
        body, name=name, grid=(nt,),
        in_specs=[pl.BlockSpec((T, C), lambda i: (i, 3)), nxt(3), cur(0), nxt(cb0), cur(1), cur(2), prev(1), prev(2),
                  pl.BlockSpec((K, C), lambda i: (0, 0))],
        out_specs=(_row_spec(T, 3 * C), pl.BlockSpec((K, C), lambda i: (0, 0))),
        out_shape=(jax.ShapeDtypeStruct((L, 3 * C), BF16), jax.ShapeDtypeStruct((K, C), F32)),
        scratch_shapes=[pltpu.VMEM((T + 8, C), F32), pltpu.VMEM((T + 8, C), F32)],
        compiler_params=_cparams(("arbitrary",), VMEM_BIG))(dycat, dycat, proj, proj, proj, proj, proj, proj, w)


def _silu(x):
    return x * jax.nn.sigmoid(x)


def _softplus(x):
    return jnp.maximum(x, 0.0) + jnp.log1p(jnp.exp(-jnp.abs(x)))


def _lane_pick(arr, idx):
    lane = lax.broadcasted_iota(jnp.int32, (1, 128), 1)
    return jnp.sum(jnp.where(lane == idx, arr, 0.0), axis=1, keepdims=True)


def _tri(n, strict=False):
    r = lax.broadcasted_iota(jnp.int32, (n, n), 0)
    c = lax.broadcasted_iota(jnp.int32, (n, n), 1)
    return (r > c) if strict else (r >= c)


def _bdot(a, b, dn=(((1,), (0,)), ((), ()))):
    return lax.dot_general(a.astype(BF16), b.astype(BF16), dn, preferred_element_type=F32)


NT = (((1,), (1,)), ((), ()))
TN = (((0,), (0,)), ((), ()))


def _cumsum_mats(col, n):
    tri = _tri(n).astype(F32)
    cb = jnp.broadcast_to(col, (n, n))
    cs_col = jnp.dot(tri, cb, precision=HI, preferred_element_type=F32)
    cs_row = lax.dot_general(cb, tri, (((0,), (1,)), ((), ())), precision=HI, preferred_element_type=F32)
    return cs_col, cs_row


def _gdn_chunk_fn(h, qc, kc, vc, zc, gates, S, alog_row, dtb_row, nw):
    n = GDN_CHUNK
    q, k, v = _silu(qc), _silu(kc), _silu(vc)
    q = q * lax.rsqrt(jnp.sum(q * q, axis=-1, keepdims=True) + EPS) * (128.0 ** -0.5)
    k = k * lax.rsqrt(jnp.sum(k * k, axis=-1, keepdims=True) + EPS)
    beta = jax.nn.sigmoid(_lane_pick(gates, LANE_GB + h))
    a = _lane_pick(gates, LANE_GA + h)
    g = -jnp.exp(_lane_pick(alog_row, LANE_GA + h)) * _softplus(a + _lane_pick(dtb_row, LANE_GA + h))
    gc_col, gc_row = _cumsum_mats(g, n)
    tri_incl, tri_strict = _tri(n), _tri(n, strict=True)
    decay = jnp.exp(jnp.where(tri_incl, gc_col - gc_row, NEG))
    gc = gc_col[:, 0:1]
    g_last = jnp.sum(g, axis=0, keepdims=True)
    kb = k * beta
    m = jnp.where(tri_strict, _bdot(kb, k, NT) * decay, 0.0)
    eye = (lax.broadcasted_iota(jnp.int32, (n, n), 0) == lax.broadcasted_iota(jnp.int32, (n, n), 1)).astype(F32)
    p = -m
    t_inv = eye + p
    for _ in range(5):
        p = jnp.dot(p, p, precision=HI, preferred_element_type=F32)
        t_inv = t_inv + jnp.dot(t_inv, p, precision=HI, preferred_element_type=F32)
    egc = jnp.exp(gc)
    u = _bdot(t_inv, v * beta)
    w = _bdot(t_inv, kb * egc)
    attn = jnp.where(tri_incl, _bdot(q, k, NT) * decay, 0.0)
    v_new = u - _bdot(w, S)
    o = _bdot(q * egc, S) + _bdot(attn, v_new)
    s_new = S * jnp.exp(g_last) + _bdot(k * jnp.exp(g_last - gc), v_new, TN)
    y = _rms(o, nw) * _silu(zc)
    return y, s_new


def _gdn_fwd(qkvc, proj, alog_row, dtb_row, nw, name):
    L = qkvc.shape[0]
    n = GDN_CHUNK
    Nc = L // n

    def body(qkv_ref, z_ref, g_ref, al_ref, db_ref, nw_ref, y_ref, sin_ref, S):
        @pl.when(pl.program_id(0) == 0)
        def _():
            S[...] = jnp.zeros_like(S)

        gates = g_ref[...]
        for h in range(4):
            sl = slice(128 * h, 128 * h + 128)
            sh = S[h]
            sin_ref[0, h] = sh
            y, sn = _gdn_chunk_fn(h, qkv_ref[:, sl], qkv_ref[:, 512 + 128 * h:640 + 128 * h],
                                  qkv_ref[:, 1024 + 128 * h:1152 + 128 * h], z_ref[:, sl], gates, sh,
                                  al_ref[...], db_ref[...], nw_ref[...])
            y_ref[:, sl] = y.astype(BF16)
            S[h] = sn

    return pl.pallas_call(
        body, name=name, grid=(Nc,),
        in_specs=[pl.BlockSpec((n, 1536), lambda i: (i, 0)), pl.BlockSpec((n, 512), lambda i: (i, P_GZ // 512)),
                  pl.BlockSpec((n, 128), lambda i: (i, P_GATE // 128)), _vec_spec(128), _vec_spec(128), _vec_spec(128)],
        out_specs=(pl.BlockSpec((n, 512), lambda i: (i, 0)), pl.BlockSpec((1, 4, 128, 128), lambda i: (i, 0, 0, 0))),
        out_shape=(jax.ShapeDtypeStruct((L, 512), BF16), jax.ShapeDtypeStruct((Nc, 4, 128, 128), F32)),
        scratch_shapes=[pltpu.VMEM((4, 128, 128), F32)],
        compiler_params=_cparams(("arbitrary",), VMEM_BIG))(qkvc, proj, proj, alog_row, dtb_row, nw)


def _gdn_bwd(qkvc, proj, alog_row, dtb_row, nw, s_in, dycat, name):
    L = qkvc.shape[0]
    n = GDN_CHUNK
    Nc = L // n

    def body(qkv_ref, z_ref, g_ref, al_ref, db_ref, nw_ref, sin_ref, dy_ref,
             dqkv_ref, dz_ref, dg_ref, dal_ref, ddb_ref, dnw_ref, dS):
        @pl.when(pl.program_id(0) == 0)
        def _():
            dS[...] = jnp.zeros_like(dS)
            dal_ref[...] = jnp.zeros_like(dal_ref)
            ddb_ref[...] = jnp.zeros_like(ddb_ref)
            dnw_ref[...] = jnp.zeros_like(dnw_ref)

        gates = g_ref[...]
        dgates = jnp.zeros_like(gates)
        for h in range(4):
            sl = slice(128 * h, 128 * h + 128)
            slk = slice(512 + 128 * h, 640 + 128 * h)
            slv = slice(1024 + 128 * h, 1152 + 128 * h)
            _, vjp = jax.vjp(functools.partial(_gdn_chunk_fn, h), qkv_ref[:, sl], qkv_ref[:, slk], qkv_ref[:, slv],
                             z_ref[:, sl], gates, sin_ref[0, h], al_ref[...], db_ref[...], nw_ref[...])
            dq, dk, dv, dz, dgt, ds, dal, ddb, dnw = vjp((dy_ref[:, sl], dS[h]))
            dqkv_ref[:, sl] = dq
            dqkv_ref[:, slk] = dk
            dqkv_ref[:, slv] = dv
            dz_ref[:, sl] = dz.astype(BF16)
            dgates = dgates + dgt
            dS[h] = ds
            dal_ref[...] += dal
            ddb_ref[...] += ddb
            dnw_ref[...] += dnw
        dg_ref[...] = dgates

    rev = lambda i: Nc - 1 - i
    vec = jax.ShapeDtypeStruct((1, 128), F32)
    return pl.pallas_call(
        body, name=name, grid=(Nc,),
        in_specs=[pl.BlockSpec((n, 1536), lambda i: (rev(i), 0)), pl.BlockSpec((n, 512), lambda i: (rev(i), P_GZ // 512)),
                  pl.BlockSpec((n, 128), lambda i: (rev(i), P_GATE // 128)), _vec_spec(128), _vec_spec(128), _vec_spec(128),
                  pl.BlockSpec((1, 4, 128, 128), lambda i: (rev(i), 0, 0, 0)), pl.BlockSpec((n, 512), lambda i: (rev(i), 0))],
        out_specs=(pl.BlockSpec((n, 1536), lambda i: (rev(i), 0)), pl.BlockSpec((n, 512), lambda i: (rev(i), 0)),
                   pl.BlockSpec((n, 128), lambda i: (rev(i), 0)), _vec_spec(128), _vec_spec(128), _vec_spec(128)),
        out_shape=(jax.ShapeDtypeStruct((L, 1536), F32), jax.ShapeDtypeStruct((L, 512), BF16),
                   jax.ShapeDtypeStruct((L, 128), F32), vec, vec, vec),
        scratch_shapes=[pltpu.VMEM((4, 128, 128), F32)],
        compiler_params=_cparams(("arbitrary",), VMEM_BIG))(qkvc, proj, proj, alog_row, dtb_row, nw, s_in, dycat)


def _ssd_chunk_fn(xs, bs, cs, zs, gates, st, alog_row, dtb_row, d_row, nws):
    n = SSM_CHUNK
    tri = _tri(n)
    lane = lax.broadcasted_iota(jnp.int32, (1, 128), 1)
    lo = lane < 64
    xs = [_silu(t) for t in xs]
    bs = [_silu(t) for t in bs]
    cs = [_silu(t) for t in cs]
    cb = [_bdot(cs[g], bs[g], NT) for g in range(2)]
    ys, st_new = [], []
    for p in range(4):
        g = p // 2
        ydiag = jnp.zeros((n, 128), F32)
        cs_lane = jnp.zeros((n, 128), F32)
        dt_lane = jnp.zeros((n, 128), F32)
        tot_lane = jnp.zeros((1, 128), F32)
        dsk_lane = jnp.zeros((1, 128), F32)
        for half in range(2):
            h = 2 * p + half
            hm = lo if half == 0 else jnp.logical_not(lo)
            dt = _softplus(_lane_pick(gates, LANE_SDT + h) + _lane_pick(dtb_row, LANE_SDT + h))
            da = dt * (-jnp.exp(_lane_pick(alog_row, LANE_SDT + h)))
            cs_col, cs_row = _cumsum_mats(da, n)
            lmat = jnp.exp(jnp.where(tri, cs_col - cs_row, NEG))
            xdt_h = jnp.where(hm, xs[p] * dt, 0.0)
            ydiag = ydiag + _bdot(cb[g] * lmat, xdt_h)
            cs_lane = jnp.where(hm, cs_col, cs_lane)
            dt_lane = jnp.where(hm, dt, dt_lane)
            tot_lane = jnp.where(hm, jnp.sum(da, axis=0, keepdims=True), tot_lane)
            dsk_lane = jnp.where(hm, _lane_pick(d_row, LANE_SDT + h), dsk_lane)
        xdt = xs[p] * dt_lane
        st_new.append(st[p] * jnp.exp(tot_lane) + _bdot(bs[g], xdt * jnp.exp(tot_lane - cs_lane), TN))
        yoff = _bdot(cs[g], st[p]) * jnp.exp(cs_lane)
        ys.append((ydiag + yoff + xs[p] * dsk_lane) * _silu(zs[p]))
    out = []
    for g in range(2):
        ms = (jnp.sum(ys[2 * g] ** 2, axis=-1, keepdims=True) + jnp.sum(ys[2 * g + 1] ** 2, axis=-1, keepdims=True)) / 256.0
        rstd = lax.rsqrt(ms + EPS)
        out += [ys[2 * g] * rstd * nws[2 * g], ys[2 * g + 1] * rstd * nws[2 * g + 1]]
    return tuple(out), tuple(st_new)


def _ssd_args(xbc_ref, z_ref, nw_ref):
    xs = [xbc_ref[:, 128 * p:128 * p + 128] for p in range(4)]
    bs = [xbc_ref[:, 512 + 128 * g:640 + 128 * g] for g in range(2)]
    cs = [xbc_ref[:, 768 + 128 * g:896 + 128 * g] for g in range(2)]
    zs = [z_ref[:, 128 * p:128 * p + 128] for p in range(4)]
    nws = [nw_ref[:, 128 * p:128 * p + 128] for p in range(4)]
    return xs, bs, cs, zs, nws


def _ssd_fwd(xbcc, proj, alog_row, dtb_row, d_row, nw, name):
    L = xbcc.shape[0]
    n = SSM_CHUNK
    Nc = L // n

    def body(xbc_ref, z_ref, g_ref, al_ref, db_ref, d_ref, nw_ref, y_ref, sin_ref, S):
        @pl.when(pl.program_id(0) == 0)
        def _():
            S[...] = jnp.zeros_like(S)

        xs, bs, cs, zs, nws = _ssd_args(xbc_ref, z_ref, nw_ref)
        st = [S[p] for p in range(4)]
        sin_ref[0] = S[...]
        ys, sn = _ssd_chunk_fn(xs, bs, cs, zs, g_ref[...], st, al_ref[...], db_ref[...], d_ref[...], nws)
        for p in range(4):
            y_ref[:, 128 * p:128 * p + 128] = ys[p].astype(BF16)
            S[p] = sn[p]

    return pl.pallas_call(
        body, name=name, grid=(Nc,),
        in_specs=[pl.BlockSpec((n, 1024), lambda i: (i, 0)), pl.BlockSpec((n, 512), lambda i: (i, P_SZ // 512)),
                  pl.BlockSpec((n, 128), lambda i: (i, P_GATE // 128)), _vec_spec(128), _vec_spec(128), _vec_spec(128),
                  _vec_spec(512)],
        out_specs=(pl.BlockSpec((n, 512), lambda i: (i, 0)), pl.BlockSpec((1, 4, 128, 128), lambda i: (i, 0, 0, 0))),
        out_shape=(jax.ShapeDtypeStruct((L, 512), BF16), jax.ShapeDtypeStruct((Nc, 4, 128, 128), F32)),
        scratch_shapes=[pltpu.VMEM((4, 128, 128), F32)],
        compiler_params=_cparams(("arbitrary",), VMEM_BIG))(xbcc, proj, proj, alog_row, dtb_row, d_row, nw)


def _ssd_bwd(xbcc, proj, alog_row, dtb_row, d_row, nw, s_in, dycat, name):
    L = xbcc.shape[0]
    n = SSM_CHUNK
    Nc = L // n

    def body(xbc_ref, z_ref, g_ref, al_ref, db_ref, d_ref, nw_ref, sin_ref, dy_ref,
             dxbc_ref, dz_ref, dg_ref, dal_ref, ddb_ref, dd_ref, dnw_ref, dS):
        @pl.when(pl.program_id(0) == 0)
        def _():
            dS[...] = jnp.zeros_like(dS)
            dal_ref[...] = jnp.zeros_like(dal_ref)
            ddb_ref[...] = jnp.zeros_like(ddb_ref)
            dd_ref[...] = jnp.zeros_like(dd_ref)
            dnw_ref[...] = jnp.zeros_like(dnw_ref)

        xs, bs, cs, zs, nws = _ssd_args(xbc_ref, z_ref, nw_ref)
        st = [sin_ref[0, p] for p in range(4)]
        _, vjp = jax.vjp(_ssd_chunk_fn, xs, bs, cs, zs, g_ref[...], st, al_ref[...], db_ref[...], d_ref[...], nws)
        dys = tuple(dy_ref[:, 128 * p:128 * p + 128] for p in range(4))
        dxs, dbs, dcs, dzs, dgt, dst, dal, ddb, dd, dnws = vjp((dys, tuple(dS[p] for p in range(4))))
        for p in range(4):
            dxbc_ref[:, 128 * p:128 * p + 128] = dxs[p]
            dz_ref[:, 128 * p:128 * p + 128] = dzs[p].astype(BF16)
            dS[p] = dst[p]
            dnw_ref[:, 128 * p:128 * p + 128] += dnws[p]
        for g in range(2):
            dxbc_ref[:, 512 + 128 * g:640 + 128 * g] = dbs[g]
            dxbc_ref[:, 768 + 128 * g:896 + 128 * g] = dcs[g]
        dg_ref[...] = dgt
        dal_ref[...] += dal
        ddb_ref[...] += ddb
        dd_ref[...] += dd

    rev = lambda i: Nc - 1 - i
    vec = jax.ShapeDtypeStruct((1, 128), F32)
    return pl.pallas_call(
        body, name=name, grid=(Nc,),
        in_specs=[pl.BlockSpec((n, 1024), lambda i: (rev(i), 0)), pl.BlockSpec((n, 512), lambda i: (rev(i), P_SZ // 512)),
                  pl.BlockSpec((n, 128), lambda i: (rev(i), P_GATE // 128)), _vec_spec(128), _vec_spec(128), _vec_spec(128),
                  _vec_spec(512), pl.BlockSpec((1, 4, 128, 128), lambda i: (rev(i), 0, 0, 0)),
                  pl.BlockSpec((n, 512), lambda i: (rev(i), 1))],
        out_specs=(pl.BlockSpec((n, 1024), lambda i: (rev(i), 0)), pl.BlockSpec((n, 512), lambda i: (rev(i), 0)),
                   pl.BlockSpec((n, 128), lambda i: (rev(i), 0)), _vec_spec(128), _vec_spec(128), _vec_spec(128),
                   _vec_spec(512)),
        out_shape=(jax.ShapeDtypeStruct((L, 1024), F32), jax.ShapeDtypeStruct((L, 512), BF16),
                   jax.ShapeDtypeStruct((L, 128), F32), vec, vec, vec, jax.ShapeDtypeStruct((1, 512), F32)),
        scratch_shapes=[pltpu.VMEM((4, 128, 128), F32)],
        compiler_params=_cparams(("arbitrary",), VMEM_BIG))(xbcc, proj, proj, alog_row, dtb_row, d_row, nw, s_in, dycat)


def _rot(x):
    W = x.shape[1]
    lane = lax.broadcasted_iota(jnp.int32, (1, W), 1)
    first = (lane % 64) < 32
    return jnp.where(first, -pltpu.roll(x, W - 32, 1), pltpu.roll(x, 32, 1))


def _rope(x, cos, sin):
    return x * cos + _rot(x) * sin


def _rope_t(d, cos, sin):
    return d * cos - _rot(d * sin)


def _swa_core_fn(qs, ks, vs, sink_row, mask):
    lane = lax.broadcasted_iota(jnp.int32, (1, 128), 1)
    lo = lane < 64
    outs = []
    for p in range(4):
        g = p // 2
        halves = []
        for half in range(2):
            hm = lo if half == 0 else jnp.logical_not(lo)
            s = _bdot(jnp.where(hm, qs[p], 0.0), ks[g], NT) * 0.125
            s = jnp.where(mask, s, NEG)
            sk = _lane_pick(sink_row, 2 * p + half)
            mx = lax.stop_gradient(jnp.maximum(jnp.max(s, axis=-1, keepdims=True), sk))
            pr = jnp.exp(s - mx)
            pr = pr / (jnp.sum(pr, axis=-1, keepdims=True) + jnp.exp(sk - mx))
            halves.append(_bdot(pr, vs[g]))
        outs.append(jnp.where(lo, halves[0], halves[1]))
    return tuple(outs)


def _swa_prepare(blk, q_ref, kc_ref, kp_ref, vc_ref, vp_ref, cc_ref, sc_ref, cp_ref, sp_ref):
    W = WINDOW
    cos_c, sin_c = cc_ref[...], sc_ref[...]
    cos_b = jnp.concatenate([cp_ref[...], cos_c], axis=0)
    sin_b = jnp.concatenate([sp_ref[...], sin_c], axis=0)
    cos_q, sin_q = jnp.concatenate([cos_c] * 4, axis=1), jnp.concatenate([sin_c] * 4, axis=1)
    cos_k, sin_k = jnp.concatenate([cos_b] * 2, axis=1), jnp.concatenate([sin_b] * 2, axis=1)
    qr = _rope(q_ref[...], cos_q, sin_q)
    kr = _rope(jnp.concatenate([kp_ref[...], kc_ref[...]], axis=0), cos_k, sin_k)
    vb = jnp.concatenate([vp_ref[...], vc_ref[...]], axis=0)
    qi = lax.broadcasted_iota(jnp.int32, (W, 2 * W), 0)
    kj = lax.broadcasted_iota(jnp.int32, (W, 2 * W), 1)
    rel = qi + W - kj
    mask = (rel >= 0) & (rel < W) & ((blk > 0) | (kj >= W))
    qs = [qr[:, 128 * p:128 * p + 128] for p in range(4)]
    ks = [kr[:, 128 * g:128 * g + 128] for g in range(2)]
    vs = [vb[:, 128 * g:128 * g + 128] for g in range(2)]
    return qs, ks, vs, mask, (cos_q, sin_q, cos_k, sin_k)


def _swa_specs(nb, order):
    W = WINDOW
    cur = lambda i: order(i)
    prv = lambda i: jnp.maximum(order(i) - 1, 0)
    return [pl.BlockSpec((W, 512), lambda i: (cur(i), P_AQ // 512)),
            pl.BlockSpec((W, 256), lambda i: (cur(i), P_AK // 256)), pl.BlockSpec((W, 256), lambda i: (prv(i), P_AK // 256)),
            pl.BlockSpec((W, 256), lambda i: (cur(i), P_AV // 256)), pl.BlockSpec((W, 256), lambda i: (prv(i), P_AV // 256)),
            pl.BlockSpec((W, 128), lambda i: (cur(i), 0)), pl.BlockSpec((W, 128), lambda i: (cur(i), 0)),
            pl.BlockSpec((W, 128), lambda i: (prv(i), 0)), pl.BlockSpec((W, 128), lambda i: (prv(i), 0)),
            _vec_spec(128)]


def _swa_fwd(proj, cos, sin, sink_row, name):
    L = proj.shape[0]
    W = WINDOW
    nb = L // W

    def body(q_ref, kc_ref, kp_ref, vc_ref, vp_ref, cc_ref, sc_ref, cp_ref, sp_ref, sk_ref, y_ref):
        blk = pl.program_id(0)
        qs, ks, vs, mask, _ = _swa_prepare(blk, q_ref, kc_ref, kp_ref, vc_ref, vp_ref, cc_ref, sc_ref, cp_ref, sp_ref)
        outs = _swa_core_fn(qs, ks, vs, sk_ref[...], mask)
        for p in range(4):
            y_ref[:, 128 * p:128 * p + 128] = outs[p].astype(BF16)

    return pl.pallas_call(
        body, name=name, grid=(nb,), in_specs=_swa_specs(nb, lambda i: i),
        out_specs=pl.BlockSpec((W, 512), lambda i: (i, 0)), out_shape=jax.ShapeDtypeStruct((L, 512), BF16),
        compiler_params=_cparams(("parallel",), VMEM_BIG))(proj, proj, proj, proj, proj, cos, sin, cos, sin, sink_row)


def _swa_bwd(proj, cos, sin, sink_row, dycat, name):
    L = proj.shape[0]
    W = WINDOW
    nb = L // W
    rev = lambda i: nb - 1 - i

    def body(q_ref, kc_ref, kp_ref, vc_ref, vp_ref, cc_ref, sc_ref, cp_ref, sp_ref, sk_ref, dy_ref,
             dq_ref, dk_ref, dv_ref, dsk_ref, ck, cv):
        i = pl.program_id(0)
        blk = nb - 1 - i

        @pl.when(i == 0)
        def _():
            ck[...] = jnp.zeros_like(ck)
            cv[...] = jnp.zeros_like(cv)
            dsk_ref[...] = jnp.zeros_like(dsk_ref)

        qs, ks, vs, mask, (cos_q, sin_q, cos_k, sin_k) = _swa_prepare(
            blk, q_ref, kc_ref, kp_ref, vc_ref, vp_ref, cc_ref, sc_ref, cp_ref, sp_ref)
        _, vjp = jax.vjp(functools.partial(_swa_core_fn, mask=mask), qs, ks, vs, sk_ref[...])
        dqs, dks, dvs, dsk = vjp(tuple(dy_ref[:, 128 * p:128 * p + 128] for p in range(4)))
        dq_ref[...] = _rope_t(jnp.concatenate(dqs, axis=1), cos_q, sin_q).astype(BF16)
        dkb = _rope_t(jnp.concatenate(dks, axis=1), cos_k, sin_k)
        dvb = jnp.concatenate(dvs, axis=1)
        dk_ref[...] = (dkb[W:2 * W] + ck[...]).astype(BF16)
        dv_ref[...] = (dvb[W:2 * W] + cv[...]).astype(BF16)
        ck[...] = dkb[0:W]
        cv[...] = dvb[0:W]
        dsk_ref[...] += dsk

    return pl.pallas_call(
        body, name=name, grid=(nb,),
        in_specs=_swa_specs(nb, rev) + [pl.BlockSpec((W, 512), lambda i: (rev(i), 2))],
        out_specs=(pl.BlockSpec((W, 512), lambda i: (rev(i), 0)), pl.BlockSpec((W, 256), lambda i: (rev(i), 0)),
                   pl.BlockSpec((W, 256), lambda i: (rev(i), 0)), _vec_spec(128)),
        out_shape=(jax.ShapeDtypeStruct((L, 512), BF16), jax.ShapeDtypeStruct((L, 256), BF16),
                   jax.ShapeDtypeStruct((L, 256), BF16), jax.ShapeDtypeStruct((1, 128), F32)),
        scratch_shapes=[pltpu.VMEM((W, 256), F32), pltpu.VMEM((W, 256), F32)],
        compiler_params=_cparams(("arbitrary",), VMEM_BIG))(
            proj, proj, proj, proj, proj, cos, sin, cos, sin, sink_row, dycat)


def _adamw(w, g, m, v):
    m = ADAM_B1 * m + (1.0 - ADAM_B1) * g
    v = ADAM_B2 * v + (1.0 - ADAM_B2) * (g * g)
    m_hat = m / (1.0 - ADAM_B1 ** ADAM_STEP)
    v_hat = v / (1.0 - ADAM_B2 ** ADAM_STEP)
    delta = -ADAM_LR * (m_hat / (jnp.sqrt(v_hat) + ADAM_EPS) + ADAM_WD * w)
    return delta, m, v


def _ada_fwd(c_all, ada_w, ada_b_sh, name):
    S = ada_w.shape[2]

    def body(c_ref, w_ref, b_ref, o_ref):
        o_ref[0] = _bdot(_silu(c_ref[...]), w_ref[0]) + b_ref[0]

    return pl.pallas_call(
        body, name=name, grid=(2,),
        in_specs=[pl.BlockSpec((N_DEV, D), lambda i: (0, 0)), pl.BlockSpec((1, D, S), lambda i: (i, 0, 0)),
                  pl.BlockSpec((1, 1, S), lambda i: (i, 0, 0))],
        out_specs=pl.BlockSpec((1, N_DEV, S), lambda i: (i, 0, 0)), out_shape=jax.ShapeDtypeStruct((2, N_DEV, S), F32),
        compiler_params=_cparams(("parallel",), VMEM_BIG))(c_all, ada_w, ada_b_sh.reshape(2, 1, S))


def _ada_bwd_adamw(c_all, dmod_sh, w, m, v, name):
    S = w.shape[2]
    T = 256

    def body(c_ref, d_ref, w_ref, m_ref, v_ref, g_ref, dl_ref, nm_ref, nv_ref):
        g = _bdot(_silu(c_ref[...]), d_ref[0], TN)
        dl, nm, nv = _adamw(w_ref[0], g, m_ref[0], v_ref[0])
        g_ref[0], dl_ref[0], nm_ref[0], nv_ref[0] = g, dl, nm, nv

    blk = pl.BlockSpec((1, T, S), lambda i, j: (i, j, 0))
    sds = jax.ShapeDtypeStruct(w.shape, F32)
    return pl.pallas_call(
        body, name=name, grid=(2, D // T),
        in_specs=[pl.BlockSpec((N_DEV, T), lambda i, j: (0, j)), pl.BlockSpec((1, N_DEV, S), lambda i, j: (i, 0, 0)),
                  blk, blk, blk],
        out_specs=(blk, blk, blk, blk), out_shape=(sds, sds, sds, sds),
        compiler_params=_cparams(("parallel", "parallel"), VMEM_BIG))(c_all, dmod_sh, w, m, v)


def _sum8(r, name):
    _, R, W = r.shape
    T = R
    for cand in (2496, 2048, 1024, 512, 256, 128, 64, 32, 16, 8):
        if R % cand == 0:
            T = cand
            break

    def body(r_ref, o_ref):
        acc = r_ref[0].astype(F32)
        for d in range(1, N_DEV):
            acc = acc + r_ref[d].astype(F32)
        o_ref[...] = acc

    return pl.pallas_call(
        body, name=name, grid=(R // T,), in_specs=[pl.BlockSpec((N_DEV, T, W), lambda i: (0, i, 0))],
        out_specs=pl.BlockSpec((T, W), lambda i: (i, 0)), out_shape=jax.ShapeDtypeStruct((R, W), F32),
        compiler_params=_cparams(("parallel",), VMEM_BIG))(r)


def _adamw_call(w, g, m, v, name):
    R, W = w.shape
    T = R
    for cand in (1024, 512, 256, 128, 64, 32, 16, 8):
        if R % cand == 0 and R > cand and cand * W * 4 <= 2 * 1024 * 1024:
            T = cand
            break

    def body(w_ref, g_ref, m_ref, v_ref, dl_ref, nm_ref, nv_ref):
        dl_ref[...], nm_ref[...], nv_ref[...] = _adamw(w_ref[...], g_ref[...], m_ref[...], v_ref[...])

    blk = pl.BlockSpec((T, W), lambda i: (i, 0))
    sds = jax.ShapeDtypeStruct((R, W), F32)
    return pl.pallas_call(
        body, name=name, grid=(R // T,), in_specs=[blk, blk, blk, blk], out_specs=(blk, blk, blk),
        out_shape=(sds, sds, sds), compiler_params=_cparams(("parallel",), VMEM_BIG))(w, g, m, v)


def _permute_w_in(w):
    z = lambda n: jnp.zeros((w.shape[0], n), w.dtype)
    s = lambda o, n: w[:, o:o + n]
    kd = [s(O_AK, 64), s(O_AK, 64), s(O_AK + 64, 64), s(O_AK + 64, 64)]
    vd = [s(O_AV, 64), s(O_AV, 64), s(O_AV + 64, 64), s(O_AV + 64, 64)]
    return jnp.concatenate(
        [s(O_GQ, 1536), s(O_GZ, 512), s(O_SXBC, 1024), s(O_SZ, 512), s(O_AQ, 512), s(O_CB, 1536)] + kd + vd
        + [s(O_GB, 4), s(O_GA, 4), s(O_SDT, 8), z(112)], axis=1)


def _unpermute_dw_in(g):
    s = lambda o, n: g[:, o:o + n]
    dk = jnp.concatenate([s(P_AK, 64) + s(P_AK + 64, 64), s(P_AK + 128, 64) + s(P_AK + 192, 64)], axis=1)
    dv = jnp.concatenate([s(P_AV, 64) + s(P_AV + 64, 64), s(P_AV + 128, 64) + s(P_AV + 192, 64)], axis=1)
    return jnp.concatenate(
        [s(P_QKV, 1536), s(P_GZ, 512), s(P_GATE, 8), s(P_SZ, 512), s(P_SXBC, 1024), s(P_GATE + 8, 8), s(P_AQ, 512),
         dk, dv, s(P_CB, 1536)], axis=1)


def _row128(vals, lane0):
    n = vals.shape[0]
    return jnp.concatenate([jnp.zeros((lane0,), F32), vals, jnp.zeros((128 - lane0 - n,), F32)]).reshape(1, 128)


def _pad_rows128(flat):
    n = flat.shape[0]
    pad = (-n) % 1024
    return jnp.concatenate([flat, jnp.zeros((pad,), flat.dtype)]).reshape(-1, 128)


def kernel(x, c, positions, ada_w, ada_b, norm_pre_mix, norm_post_mix, norm_pre_mlp, norm_post_mlp, w_in, w_out, gdn_conv_w, gdn_a_log, gdn_dt_bias, gdn_norm_w, ssm_conv_w, ssm_conv_b, ssm_a_log, ssm_dt_bias, ssm_d, ssm_norm_w, attn_sinks, sc_conv_w, w_up, w_down, loss_target, m_ada_w, m_ada_b, m_norm_pre_mix, m_norm_post_mix, m_norm_pre_mlp, m_norm_post_mlp, m_w_in, m_w_out, m_gdn_conv_w, m_gdn_a_log, m_gdn_dt_bias, m_gdn_norm_w, m_ssm_conv_w, m_ssm_conv_b, m_ssm_a_log, m_ssm_dt_bias, m_ssm_d, m_ssm_norm_w, m_attn_sinks, m_sc_conv_w, m_w_up, m_w_down, v_ada_w, v_ada_b, v_norm_pre_mix, v_norm_post_mix, v_norm_pre_mlp, v_norm_post_mlp, v_w_in, v_w_out, v_gdn_conv_w, v_gdn_a_log, v_gdn_dt_bias, v_gdn_norm_w, v_ssm_conv_w, v_ssm_conv_b, v_ssm_a_log, v_ssm_dt_bias, v_ssm_d, v_ssm_norm_w, v_attn_sinks, v_sc_conv_w, v_w_up, v_w_down):
    L = x.shape[1]
    me = 4 * lax.axis_index("x") + 2 * lax.axis_index("y") + lax.axis_index("c")
    x0 = x[0]
    target = loss_target[0]
    S_ADA = ada_w.shape[2]

    small = jnp.concatenate([c.reshape(-1), gdn_conv_w.reshape(-1), ssm_conv_w.reshape(-1), sc_conv_w.reshape(-1)])
    n_small = small.shape[0]
    g_small = _ag_small(_pad_rows128(small), "ag_c_conv").reshape(N_DEV, -1)[:, :n_small]
    c_all = g_small[:, :D]
    o = D
    gdn_cw = g_small[:, o:o + 2 * 4 * 192].reshape(N_DEV, 2, 4, 192).transpose(1, 2, 0, 3).reshape(2, 4, 1536)
    o += 2 * 4 * 192
    ssm_cw = g_small[:, o:o + 2 * 4 * 128].reshape(N_DEV, 2, 4, 128).transpose(1, 2, 0, 3).reshape(2, 4, 1024)
    o += 2 * 4 * 128
    sc_cw = g_small[:, o:o + 2 * 3 * 64].reshape(N_DEV, 2, 3, 64).transpose(1, 2, 0, 3).reshape(2, 3, 512)

    ada_b_sh = lax.dynamic_slice(ada_b, (0, me * S_ADA), (2, S_ADA))
    mod_sh = _ada_fwd(c_all, ada_w, ada_b_sh, "ada_fwd")
    mod_all = _ag_small(mod_sh.reshape(-1, 128), "ag_mod").reshape(N_DEV, 2, N_DEV, S_ADA)
    mod_me = lax.dynamic_index_in_dim(mod_all, me, axis=2, keepdims=False)
    mod_me = mod_me.transpose(1, 0, 2).reshape(2, 6 * D)

    parts = []
    for i in range(2):
        parts += [w_in[i].reshape(-1), w_out[i].reshape(-1), w_up[i].reshape(-1), w_down[i].reshape(-1)]
    pack = jnp.concatenate([p.astype(BF16) for p in parts]).reshape(-1, 128)
    allw = _ag_big(pack, "ag_weights").reshape(N_DEV, -1)
    win_p, wout_f, wup_f, wdown_f = [], [], [], []
    o = 0
    for i in range(2):
        n = D * SHARD_IN
        t = allw[:, o:o + n].reshape(N_DEV, D, SHARD_IN).transpose(1, 0, 2).reshape(D, IN_W)
        win_p.append(_permute_w_in(t))
        o += n
        n = (D // N_DEV) * D
        wout_f.append(allw[:, o:o + n].reshape(D, D))
        o += n
        n = D * (MLP // N_DEV)
        wup_f.append(allw[:, o:o + n].reshape(N_DEV, D, MLP // N_DEV).transpose(1, 0, 2).reshape(D, MLP))
        o += n
        n = (MLP // N_DEV) * D
        wdown_f.append(allw[:, o:o + n].reshape(MLP, D))
        o += n

    inv_freq = ROPE_THETA ** (-jnp.arange(0, 64, 2, dtype=F32) / 64)
    ang = positions[0].astype(F32)[:, None] * inv_freq
    cos = jnp.tile(jnp.cos(ang), (1, 4))
    sin = jnp.tile(jnp.sin(ang), (1, 4))

    vec = lambda a: a.reshape(1, -1)

    saved = []
    xc = x0
    for i in range(2):
        sh_a, sc_a, gt_a, sh_m, sc_m, gt_m = [vec(t) for t in jnp.split(mod_me[i], 6)]
        gal, gdb = _row128(gdn_a_log[i], LANE_GA), _row128(gdn_dt_bias[i], LANE_GA)
        sal, sdb, sdd = (_row128(ssm_a_log[i], LANE_SDT), _row128(ssm_dt_bias[i], LANE_SDT), _row128(ssm_d[i], LANE_SDT))
        gnw, snw, skr = vec(gdn_norm_w[i]), vec(ssm_norm_w[i]), _row128(attn_sinks[i], 0)
        zero_b = jnp.zeros((1, 1536), F32)

        h = _pre_fwd(xc, vec(norm_pre_mix[i]), sc_a, sh_a, f"pre_mix_fwd{i}")
        proj = _mm(h, win_p[i], name=f"in_proj{i}", tn=896, tk=2048)
        qkvc = _conv_fwd(proj, 0, 1536, gdn_cw[i], zero_b, f"gdn_conv_fwd{i}")
        ya, gdn_s = _gdn_fwd(qkvc, proj, gal, gdb, gnw, f"gdn_fwd{i}")
        xbcc = _conv_fwd(proj, P_SXBC // 1024, 1024, ssm_cw[i], vec(ssm_conv_b[i]), f"ssm_conv_fwd{i}")
        yb, ssd_s = _ssd_fwd(xbcc, proj, sal, sdb, sdd, snw, f"ssd_fwd{i}")
        yc = _swa_fwd(proj, cos, sin, skr, f"swa_fwd{i}")
        yd = _sc_fwd(proj, sc_cw[i], f"sc_fwd{i}")
        ycat = jnp.concatenate([ya, yb, yc, yd], axis=1)
        out = _mm(ycat, wout_f[i], name=f"out_proj{i}", tk=2048)
        x1 = _post_fwd(xc, out, vec(norm_post_mix[i]), gt_a, f"post_mix_fwd{i}")
        h2 = _pre_fwd(x1, vec(norm_pre_mlp[i]), sc_m, sh_m, f"pre_mlp_fwd{i}")
        act, u = _mm(h2, wup_f[i], name=f"up_proj{i}", tk=2048, mode="relu2")
        y2 = _mm(act, wdown_f[i], name=f"down_proj{i}", tk=2048)
        x2 = _post_fwd(x1, y2, vec(norm_post_mlp[i]), gt_m, f"post_mlp_fwd{i}")
        saved.append(dict(x0=xc, h=h, proj=proj, qkvc=qkvc, gdn_s=gdn_s, xbcc=xbcc, ssd_s=ssd_s, ycat=ycat, out=out,
                          x1=x1, h2=h2, act=act, u=u, y2=y2, mods=(sh_a, sc_a, gt_a, sh_m, sc_m, gt_m),
                          rows=(gal, gdb, sal, sdb, sdd, gnw, snw, skr)))
        xc = x2

    sq, dx = _loss_head(xc, target, "loss_head")
    loss = lax.psum(0.5 * sq[0, 0] / float(D), ("x", "y", "c"))

    gw = [dict() for _ in range(2)]
    for i in (1, 0):
        sv = saved[i]
        sh_a, sc_a, gt_a, sh_m, sc_m, gt_m = sv["mods"]
        gal, gdb, sal, sdb, sdd, gnw, snw, skr = sv["rows"]
        proj = sv["proj"]
        dy2, g_npostmlp, d_gt_m = _post_bwd(sv["y2"], vec(norm_post_mlp[i]), gt_m, dx, f"post_mlp_bwd{i}")
        du = _mm(dy2, wdown_f[i], name=f"down_proj_dx{i}", tb=True, tk=2048, out_dtype=BF16, mode="drelu2", u=sv["u"])
        g_wdown = _mm(sv["act"], dy2, name=f"down_proj_dw{i}", ta=True)
        dh2 = _mm(du, wup_f[i], name=f"up_proj_dx{i}", tb=True, tk=2048, out_dtype=BF16)
        g_wup = _mm(sv["h2"], du, name=f"up_proj_dw{i}", ta=True)
        dx1, g_npremlp, d_sc_m, d_sh_m = _pre_bwd(sv["x1"], vec(norm_pre_mlp[i]), sc_m, sh_m, dh2, dx, f"pre_mlp_bwd{i}")
        dout, g_npostmix, d_gt_a = _post_bwd(sv["out"], vec(norm_post_mix[i]), gt_a, dx1, f"post_mix_bwd{i}")
        dycat = _mm(dout, wout_f[i], name=f"out_proj_dx{i}", tb=True, tk=2048)
        g_wout = _mm(sv["ycat"], dout, name=f"out_proj_dw{i}", ta=True)

        dqkvc, dgz, dgate_g, d_gal, d_gdb, d_gnw = _gdn_bwd(sv["qkvc"], proj, gal, gdb, gnw, sv["gdn_s"], dycat, f"gdn_bwd{i}")
        dqkv, g_gdn_cw, _ = _conv_bwd(dqkvc, proj, 0, 1536, gdn_cw[i], f"gdn_conv_bwd{i}")
        dxbcc, dsz, dgate_s, d_sal, d_sdb, d_sdd, d_snw = _ssd_bwd(
            sv["xbcc"], proj, sal, sdb, sdd, snw, sv["ssd_s"], dycat, f"ssd_bwd{i}")
        dsxbc, g_ssm_cw, g_ssm_cb = _conv_bwd(dxbcc, proj, P_SXBC // 1024, 1024, ssm_cw[i], f"ssm_conv_bwd{i}")
        daq, dak, dav, d_skr = _swa_bwd(proj, cos, sin, skr, dycat, f"swa_bwd{i}")
        dsc, g_sc_cw = _sc_bwd(dycat, proj, sc_cw[i], f"sc_bwd{i}")
        dgate = (dgate_g + dgate_s).astype(BF16)
        dproj = jnp.concatenate([dqkv, dgz, dsxbc, dsz, daq, dsc, dak, dav, dgate], axis=1)
        dh = _mm(dproj, win_p[i], name=f"in_proj_dx{i}", tb=True, tk=896, out_dtype=BF16)
        g_winp = _mm(sv["h"], dproj, name=f"in_proj_dw{i}", ta=True, tn=896)
        dx, g_npremix, d_sc_a, d_sh_a = _pre_bwd(sv["x0"], vec(norm_pre_mix[i]), sc_a, sh_a, dh, dx1, f"pre_mix_bwd{i}")

        gw[i] = dict(
            w_in=_unpermute_dw_in(g_winp), w_out=g_wout, w_up=g_wup, w_down=g_wdown,
            dmod=jnp.concatenate([d_sh_a, d_sc_a, d_gt_a, d_sh_m, d_sc_m, d_gt_m], axis=1).reshape(-1),
            norm_pre_mix=g_npremix.reshape(-1), norm_post_mix=g_npostmix.reshape(-1),
            norm_pre_mlp=g_npremlp.reshape(-1), norm_post_mlp=g_npostmlp.reshape(-1),
            gdn_conv_w=g_gdn_cw.reshape(-1), gdn_a_log=d_gal[0, LANE_GA:LANE_GA + 4], gdn_dt_bias=d_gdb[0, LANE_GA:LANE_GA + 4],
            gdn_norm_w=d_gnw.reshape(-1), ssm_conv_w=g_ssm_cw.reshape(-1), ssm_conv_b=g_ssm_cb.reshape(-1),
            ssm_a_log=d_sal[0, LANE_SDT:LANE_SDT + 8], ssm_dt_bias=d_sdb[0, LANE_SDT:LANE_SDT + 8],
            ssm_d=d_sdd[0, LANE_SDT:LANE_SDT + 8], ssm_norm_w=d_snw.reshape(-1), attn_sinks=d_skr[0, 0:8],
            sc_conv_w=g_sc_cw.reshape(-1))
    grad_x = dx[None]

    gparts = []
    for i in range(2):
        gparts += [gw[i]["w_in"].reshape(D, N_DEV, SHARD_IN).transpose(1, 0, 2).reshape(N_DEV, -1),
                   gw[i]["w_out"].reshape(N_DEV, -1),
                   gw[i]["w_up"].reshape(D, N_DEV, MLP // N_DEV).transpose(1, 0, 2).reshape(N_DEV, -1),
                   gw[i]["w_down"].reshape(N_DEV, -1)]
    gpack = jnp.concatenate(gparts, axis=1).reshape(N_DEV, -1, 128)
    gsum = _sum8(_exchange_big(gpack, "grad_exchange"), "grad_sum").reshape(-1)
    big = {}
    o = 0
    for i in range(2):
        for name, shape in (("w_in", (D, SHARD_IN)), ("w_out", (D // N_DEV, D)), ("w_up", (D, MLP // N_DEV)),
                            ("w_down", (MLP // N_DEV, D))):
            n = shape[0] * shape[1]
            big.setdefault(name, []).append(gsum[o:o + n].reshape(shape))
            o += n
    grads, deltas, new_m, new_v = {}, {}, {}, {}
    for name, w, m, v in (("w_in", w_in, m_w_in, v_w_in), ("w_out", w_out, m_w_out, v_w_out),
                          ("w_up", w_up, m_w_up, v_w_up), ("w_down", w_down, m_w_down, v_w_down)):
        g = jnp.stack(big[name])
        W = w.shape[2]
        dl, nm, nv = _adamw_call(w.reshape(-1, W), g.reshape(-1, W), m.reshape(-1, W), v.reshape(-1, W), f"adamw_{name}")
        grads[name], deltas[name], new_m[name], new_v[name] = g, dl.reshape(w.shape), nm.reshape(w.shape), nv.reshape(w.shape)

    small_names = ["dmod", "norm_pre_mix", "norm_post_mix", "norm_pre_mlp", "norm_post_mlp", "gdn_conv_w", "gdn_a_log",
                   "gdn_dt_bias", "gdn_norm_w", "ssm_conv_w", "ssm_conv_b", "ssm_a_log", "ssm_dt_bias", "ssm_d",
                   "ssm_norm_w", "attn_sinks", "sc_conv_w"]
    flat = jnp.concatenate([gw[i][nm_] for nm_ in small_names for i in range(2)])
    n_flat = flat.shape[0]
    g_all = _ag_small(_pad_rows128(flat), "ag_small_grads")
    tot = _sum8(g_all, "small_grad_sum").reshape(-1)[:n_flat]
    dmod_all = g_all.reshape(N_DEV, -1)[:, :2 * 6 * D].reshape(N_DEV, 2, 6 * D)
    sm = {}
    o = 0
    for nm_ in small_names:
        n = gw[0][nm_].shape[0]
        sm[nm_] = jnp.stack([tot[o:o + n], tot[o + n:o + 2 * n]])
        o += 2 * n

    dmod_sh = lax.dynamic_slice(dmod_all.transpose(1, 0, 2), (0, 0, me * S_ADA), (2, N_DEV, S_ADA))
    grads["ada_w"], deltas["ada_w"], new_m["ada_w"], new_v["ada_w"] = _ada_bwd_adamw(
        c_all, dmod_sh, ada_w, m_ada_w, v_ada_w, "ada_bwd_adamw")

    def shard_cols(full, K, C):
        return lax.dynamic_slice(full.reshape(2, K, C), (0, 0, me * (C // N_DEV)), (2, K, C // N_DEV))

    small_g = dict(
        ada_b=sm["dmod"], norm_pre_mix=sm["norm_pre_mix"], norm_post_mix=sm["norm_post_mix"],
        norm_pre_mlp=sm["norm_pre_mlp"], norm_post_mlp=sm["norm_post_mlp"],
        gdn_conv_w=shard_cols(sm["gdn_conv_w"], 4, 1536), gdn_a_log=sm["gdn_a_log"], gdn_dt_bias=sm["gdn_dt_bias"],
        gdn_norm_w=sm["gdn_norm_w"], ssm_conv_w=shard_cols(sm["ssm_conv_w"], 4, 1024), ssm_conv_b=sm["ssm_conv_b"],
        ssm_a_log=sm["ssm_a_log"], ssm_dt_bias=sm["ssm_dt_bias"], ssm_d=sm["ssm_d"], ssm_norm_w=sm["ssm_norm_w"],
        attn_sinks=sm["attn_sinks"], sc_conv_w=shard_cols(sm["sc_conv_w"], 3, 512))
    small_w = dict(
        ada_b=(ada_b, m_ada_b, v_ada_b), norm_pre_mix=(norm_pre_mix, m_norm_pre_mix, v_norm_pre_mix),
        norm_post_mix=(norm_post_mix, m_norm_post_mix, v_norm_post_mix),
        norm_pre_mlp=(norm_pre_mlp, m_norm_pre_mlp, v_norm_pre_mlp),
        norm_post_mlp=(norm_post_mlp, m_norm_post_mlp, v_norm_post_mlp),
        gdn_conv_w=(gdn_conv_w, m_gdn_conv_w, v_gdn_conv_w), gdn_a_log=(gdn_a_log, m_gdn_a_log, v_gdn_a_log),
        gdn_dt_bias=(gdn_dt_bias, m_gdn_dt_bias, v_gdn_dt_bias), gdn_norm_w=(gdn_norm_w, m_gdn_norm_w, v_gdn_norm_w),
        ssm_conv_w=(ssm_conv_w, m_ssm_conv_w, v_ssm_conv_w), ssm_conv_b=(ssm_conv_b, m_ssm_conv_b, v_ssm_conv_b),
        ssm_a_log=(ssm_a_log, m_ssm_a_log, v_ssm_a_log), ssm_dt_bias=(ssm_dt_bias, m_ssm_dt_bias, v_ssm_dt_bias),
        ssm_d=(ssm_d, m_ssm_d, v_ssm_d), ssm_norm_w=(ssm_norm_w, m_ssm_norm_w, v_ssm_norm_w),
        attn_sinks=(attn_sinks, m_attn_sinks, v_attn_sinks), sc_conv_w=(sc_conv_w, m_sc_conv_w, v_sc_conv_w))
    names = list(small_w)
    sizes = [int(np.prod(small_w[n_][0].shape)) for n_ in names]
    pk = lambda j: _pad_rows128(jnp.concatenate([small_w[n_][j].reshape(-1) for n_ in names]))
    gk = _pad_rows128(jnp.concatenate([small_g[n_].reshape(-1) for n_ in names]))
    dl, nm, nv = _adamw_call(pk(0), gk, pk(1), pk(2), "adamw_small")
    dl, nm, nv = dl.reshape(-1), nm.reshape(-1), nv.reshape(-1)
    o = 0
    for n_, sz in zip(names, sizes):
        shp = small_w[n_][0].shape
        grads[n_] = small_g[n_].reshape(shp)
        deltas[n_], new_m[n_], new_v[n_] = dl[o:o + sz].reshape(shp), nm[o:o + sz].reshape(shp), nv[o:o + sz].reshape(shp)
        o += sz

    order = ["ada_w", "ada_b", "norm_pre_mix", "norm_post_mix", "norm_pre_mlp", "norm_post_mlp", "w_in", "w_out",
             "gdn_conv_w", "gdn_a_log", "gdn_dt_bias", "gdn_norm_w", "ssm_conv_w", "ssm_conv_b", "ssm_a_log",
             "ssm_dt_bias", "ssm_d", "ssm_norm_w", "attn_sinks", "sc_conv_w", "w_up", "w_down"]
    return (loss, grad_x, *[grads[n_] for n_ in order], *[deltas[n_] for n_ in order],
            *[new_m[n_] for n_ in order], *[new_v[n_] for n_ in order])
```

```python
import functools
import math

import jax
import jax.numpy as jnp
import numpy as np
from jax import lax
from jax.experimental import pallas as pl
from jax.experimental.pallas import tpu as pltpu

F32 = jnp.float32
BF16 = jnp.bfloat16
HI = lax.Precision.HIGHEST
MESH = pl.DeviceIdType.MESH

N_DEV = 8
D = 2048
GW = 512
MLP = 8192
EPS = 1e-6
IN_W = 5904
SHARD_IN = IN_W // N_DEV
SHARD_PAD = 752
PW = 6272
ROPE_THETA = 10000.0
WINDOW = 128
GDN_CHUNK = 64
SSM_CHUNK = 128
NEG = -1e30

ADAM_LR, ADAM_B1, ADAM_B2, ADAM_EPS, ADAM_WD, ADAM_STEP = 0.001, 0.9, 0.999, 1e-08, 0.01, 10

O_GQ, O_GK, O_GV, O_GZ, O_GB, O_GA = 0, 512, 1024, 1536, 2048, 2052
O_SZ, O_SXBC, O_SDT = 2056, 2568, 3592
O_AQ, O_AK, O_AV = 3600, 4112, 4240
O_CB, O_CC, O_CH = 4368, 4880, 5392
P_QKV, P_GZ, P_SXBC, P_SZ, P_AQ, P_CB, P_AK, P_AV, P_GATE = 0, 1536, 2048, 3072, 3584, 4096, 5632, 5888, 6144
LANE_GB, LANE_GA, LANE_SDT = 0, 4, 8

VMEM_BIG = 56 * 1024 * 1024


def _cparams(sem=None, vmem=None):
    kw = {}
    if sem is not None:
        kw["dimension_semantics"] = sem
    if vmem is not None:
        kw["vmem_limit_bytes"] = vmem
    return pltpu.CompilerParams(**kw)


def _me():
    x, y, c = lax.axis_index("x"), lax.axis_index("y"), lax.axis_index("c")
    return x, y, c, 4 * x + 2 * y + c


def _peer(x, y, c, k):
    px = 1 - x if (k >> 2) & 1 else x
    py = 1 - y if (k >> 1) & 1 else y
    pc = 1 - c if k & 1 else c
    return (px, py, pc), 4 * px + 2 * py + pc


def _ag_small(v, name):
    R, W = v.shape

    def body(x_ref, out_ref, send_sems, recv_sems):
        x, y, c, me = _me()
        out_ref[me] = x_ref[...]
        copies = []
        for k in range(1, N_DEV):
            peer, _ = _peer(x, y, c, k)
            cp = pltpu.make_async_remote_copy(
                src_ref=x_ref, dst_ref=out_ref.at[me], send_sem=send_sems.at[k - 1], recv_sem=recv_sems.at[k - 1],
                device_id=peer, device_id_type=MESH)
            cp.start()
            copies.append(cp)
        for cp in copies:
            cp.wait()

    return pl.pallas_call(
        body, name=name,
        out_shape=jax.ShapeDtypeStruct((N_DEV, R, W), v.dtype),
        in_specs=[pl.BlockSpec(memory_space=pltpu.VMEM)],
        out_specs=pl.BlockSpec(memory_space=pltpu.VMEM),
        scratch_shapes=[pltpu.SemaphoreType.DMA((N_DEV - 1,)), pltpu.SemaphoreType.DMA((N_DEV - 1,))],
    )(v)


def _ag_multi(arrs, name):
    A = len(arrs)

    def body(*refs):
        x_refs, out_refs = refs[:A], refs[A:2 * A]
        send_sems, recv_sems, local_sems = refs[2 * A:]
        x, y, c, me = _me()
        sibling = (x, y, 1 - c)
        chips = [(1 - x, y), (x, 1 - y), (1 - x, 1 - y)]

        def copy(a, k, block, to, src=None):
            slot = out_refs[a].at[4 * block[0] + 2 * block[1] + block[2]]
            return pltpu.make_async_remote_copy(
                src_ref=slot if src is None else src, dst_ref=slot, send_sem=send_sems.at[7 * a + k],
                recv_sem=recv_sems.at[7 * a + k], device_id=to, device_id_type=MESH)

        mine = [pltpu.make_async_copy(x_refs[a], out_refs[a].at[me], local_sems.at[a]) for a in range(A)]
        for cp in mine:
            cp.start()
        first = []
        for a in range(A):
            first.append(copy(a, 0, (x, y, c), sibling, src=x_refs[a]))
            first += [copy(a, 1 + j, (x, y, c), (*chip, c), src=x_refs[a]) for j, chip in enumerate(chips)]
        for cp in first:
            cp.start()
        passed = []
        for j, chip in enumerate(chips):
            for a in range(A):
                copy(a, 1 + j, (*chip, c), (x, y, c)).wait_recv()
                cp = copy(a, 4 + j, (*chip, c), sibling)
                cp.start()
                passed.append(cp)
        for a in range(A):
            copy(a, 0, sibling, (x, y, c)).wait_recv()
            for j, chip in enumerate(chips):
                copy(a, 4 + j, (*chip, 1 - c), (x, y, c)).wait_recv()
        for cp in first + passed:
            cp.wait_send()
        for cp in mine:
            cp.wait()

    hbm = pl.BlockSpec(memory_space=pl.ANY)
    return pl.pallas_call(
        body, name=name,
        out_shape=tuple(jax.ShapeDtypeStruct((N_DEV,) + v.shape, v.dtype) for v in arrs),
        in_specs=[hbm] * A, out_specs=tuple([hbm] * A),
        scratch_shapes=[pltpu.SemaphoreType.DMA((7 * A,)), pltpu.SemaphoreType.DMA((7 * A,)),
                        pltpu.SemaphoreType.DMA((A,))],
    )(*arrs)


def _exchange_multi(groups, name):
    flat = [(w, i, g) for w, layers in enumerate(groups) for i, g in enumerate(layers)]
    A, G = len(flat), len(groups)

    def body(*refs):
        g_refs, out_refs = refs[:A], refs[A:A + G]
        send_sems, recv_sems, local_sems = refs[A + G:]
        x, y, c, me = _me()
        mine = [pltpu.make_async_copy(g_refs[a].at[me], out_refs[w].at[me, i], local_sems.at[a])
                for a, (w, i, _) in enumerate(flat)]
        for cp in mine:
            cp.start()
        copies = []
        for k in range(1, N_DEV):
            peer, pid = _peer(x, y, c, k)
            for a, (w, i, _) in enumerate(flat):
                cp = pltpu.make_async_remote_copy(
                    src_ref=g_refs[a].at[pid], dst_ref=out_refs[w].at[me, i], send_sem=send_sems.at[7 * a + k - 1],
                    recv_sem=recv_sems.at[7 * a + k - 1], device_id=peer, device_id_type=MESH)
                cp.start()
                copies.append(cp)
        for cp in copies:
            cp.wait()
        for cp in mine:
            cp.wait()

    hbm = pl.BlockSpec(memory_space=pl.ANY)
    return pl.pallas_call(
        body, name=name,
        out_shape=tuple(jax.ShapeDtypeStruct((N_DEV, len(layers)) + layers[0].shape[1:], layers[0].dtype)
                        for layers in groups),
        in_specs=[hbm] * A, out_specs=tuple([hbm] * G),
        scratch_shapes=[pltpu.SemaphoreType.DMA((7 * A,)), pltpu.SemaphoreType.DMA((7 * A,)),
                        pltpu.SemaphoreType.DMA((A,))],
    )(*[g for _, _, g in flat])


def _mm(a, b, *, name, ta=False, tb=False, tm=1024, tn=1024, tk=1024, out_dtype=F32, mode="plain", u=None,
        shards=None):
    K, M = a.shape if ta else a.shape[::-1]
    if shards == "b":
        N, K2 = (b.shape[1], b.shape[0] * b.shape[2]) if tb else (b.shape[0] * b.shape[2], b.shape[1])
    else:
        N, K2 = b.shape if tb else b.shape[::-1]
    assert K == K2, (a.shape, b.shape)
    tm, tn, tk = min(tm, M), min(tn, N), min(tk, K)
    assert M % tm == 0 and N % tn == 0 and K % tk == 0, (M, N, K, tm, tn, tk)
    nk = K // tk
    a_spec = pl.BlockSpec((tk, tm), lambda i, j, k: (k, i)) if ta else pl.BlockSpec((tm, tk), lambda i, j, k: (i, k))
    if shards == "b" and tb:
        assert tk == b.shape[2]
        b_spec = pl.BlockSpec((None, tn, tk), lambda i, j, k: (k, j, 0))
    elif shards == "b":
        assert tn == b.shape[2]
        b_spec = pl.BlockSpec((None, tk, tn), lambda i, j, k: (j, k, 0))
    else:
        b_spec = pl.BlockSpec((tn, tk), lambda i, j, k: (j, k)) if tb else pl.BlockSpec((tk, tn), lambda i, j, k: (k, j))
    if shards == "o":
        o_spec = pl.BlockSpec((None, tm, tn), lambda i, j, k: (j, i, 0))
        o_shape = (N // tn, M, tn)
    else:
        o_spec = pl.BlockSpec((tm, tn), lambda i, j, k: (i, j))
        o_shape = (M, N)
    dn = (((0 if ta else 1,), (1 if tb else 0,)), ((), ()))

    def body(*refs):
        a_ref, b_ref = refs[0], refs[1]
        acc = refs[-1]
        k = pl.program_id(2)

        @pl.when(k == 0)
        def _():
            acc[...] = jnp.zeros_like(acc)

        acc[...] += lax.dot_general(a_ref[...].astype(BF16), b_ref[...].astype(BF16), dn, preferred_element_type=F32)

        @pl.when(k == nk - 1)
        def _():
            r = acc[...]
            if mode == "plain":
                refs[2][...] = r.astype(out_dtype)
            elif mode == "relu2":
                rr = jnp.maximum(r, 0.0)
                refs[2][...] = (rr * rr).astype(BF16)
                refs[3][...] = r.astype(BF16)
            else:
                uu = refs[2][...].astype(F32)
                refs[3][...] = (r * (2.0 * jnp.maximum(uu, 0.0))).astype(out_dtype)

    in_specs, args = [a_spec, b_spec], [a, b]
    if mode == "drelu2":
        in_specs.append(o_spec)
        args.append(u)
    if mode == "relu2":
        out_shape = (jax.ShapeDtypeStruct(o_shape, BF16), jax.ShapeDtypeStruct(o_shape, BF16))
        out_specs = (o_spec, o_spec)
    else:
        out_shape = jax.ShapeDtypeStruct(o_shape, out_dtype)
        out_specs = o_spec
    return pl.pallas_call(
        body, name=name, grid=(M // tm, N // tn, nk), in_specs=in_specs, out_specs=out_specs, out_shape=out_shape,
        scratch_shapes=[pltpu.VMEM((tm, tn), F32)],
        compiler_params=_cparams(("parallel", "parallel", "arbitrary"), VMEM_BIG),
    )(*args)


ROW_T = 256


def _rms(x, w):
    return x * lax.rsqrt(jnp.mean(x * x, axis=-1, keepdims=True) + EPS) * w


def _pre_fn(x, w, scale, shift):
    return _rms(x, w) * (1.0 + scale) + shift


def _post_fn(y, w, gate):
    return gate * _rms(y, w)


def _row_spec(T, W):
    return pl.BlockSpec((T, W), lambda i: (i, 0))


def _vec_spec(W):
    return pl.BlockSpec((1, W), lambda i: (0, 0))


def _pre_fwd(x, w, scale, shift, name):
    L = x.shape[0]
    T = min(ROW_T, L)

    def body(x_ref, w_ref, sc_ref, sh_ref, h_ref):
        h_ref[...] = _pre_fn(x_ref[...], w_ref[...], sc_ref[...], sh_ref[...]).astype(BF16)

    return pl.pallas_call(
        body, name=name, grid=(L // T,), in_specs=[_row_spec(T, D), _vec_spec(D), _vec_spec(D), _vec_spec(D)],
        out_specs=_row_spec(T, D), out_shape=jax.ShapeDtypeStruct((L, D), BF16),
        compiler_params=_cparams(("parallel",), VMEM_BIG))(x, w, scale, shift)


def _pre_bwd(x, w, scale, shift, dh, dres, name):
    L = x.shape[0]
    T = min(ROW_T, L)

    def body(x_ref, w_ref, sc_ref, sh_ref, dh_ref, dres_ref, dx_ref, dw_ref, dsc_ref, dsh_ref):
        @pl.when(pl.program_id(0) == 0)
        def _():
            dw_ref[...] = jnp.zeros_like(dw_ref)
            dsc_ref[...] = jnp.zeros_like(dsc_ref)
            dsh_ref[...] = jnp.zeros_like(dsh_ref)

        _, vjp = jax.vjp(_pre_fn, x_ref[...], w_ref[...], sc_ref[...], sh_ref[...])
        dx, dw, dsc, dsh = vjp(dh_ref[...].astype(F32))
        dx_ref[...] = dres_ref[...] + dx
        dw_ref[...] += dw
        dsc_ref[...] += dsc
        dsh_ref[...] += dsh

    vec = jax.ShapeDtypeStruct((1, D), F32)
    return pl.pallas_call(
        body, name=name, grid=(L // T,),
        in_specs=[_row_spec(T, D), _vec_spec(D), _vec_spec(D), _vec_spec(D), _row_spec(T, D), _row_spec(T, D)],
        out_specs=(_row_spec(T, D), _vec_spec(D), _vec_spec(D), _vec_spec(D)),
        out_shape=(jax.ShapeDtypeStruct((L, D), F32), vec, vec, vec),
        compiler_params=_cparams(("arbitrary",), VMEM_BIG))(x, w, scale, shift, dh, dres)


def _post_fwd(x, y, w, gate, name):
    L = x.shape[0]
    T = min(ROW_T, L)

    def body(x_ref, y_ref, w_ref, g_ref, o_ref):
        o_ref[...] = x_ref[...] + _post_fn(y_ref[...], w_ref[...], g_ref[...])

    return pl.pallas_call(
        body, name=name, grid=(L // T,), in_specs=[_row_spec(T, D), _row_spec(T, D), _vec_spec(D), _vec_spec(D)],
        out_specs=_row_spec(T, D), out_shape=jax.ShapeDtypeStruct((L, D), F32),
        compiler_params=_cparams(("parallel",), VMEM_BIG))(x, y, w, gate)


def _post_bwd(y, w, gate, dxn, name):
    L = y.shape[0]
    T = min(ROW_T, L)

    def body(y_ref, w_ref, g_ref, d_ref, dy_ref, dw_ref, dg_ref):
        @pl.when(pl.program_id(0) == 0)
        def _():
            dw_ref[...] = jnp.zeros_like(dw_ref)
            dg_ref[...] = jnp.zeros_like(dg_ref)

        _, vjp = jax.vjp(_post_fn, y_ref[...], w_ref[...], g_ref[...])
        dy, dw, dg = vjp(d_ref[...])
        dy_ref[...] = dy.astype(BF16)
        dw_ref[...] += dw
        dg_ref[...] += dg

    vec = jax.ShapeDtypeStruct((1, D), F32)
    return pl.pallas_call(
        body, name=name, grid=(L // T,), in_specs=[_row_spec(T, D), _vec_spec(D), _vec_spec(D), _row_spec(T, D)],
        out_specs=(_row_spec(T, D), _vec_spec(D), _vec_spec(D)),
        out_shape=(jax.ShapeDtypeStruct((L, D), BF16), vec, vec),
        compiler_params=_cparams(("arbitrary",), VMEM_BIG))(y, w, gate, dxn)


def _loss_head(y, target, name):
    L = y.shape[0]
    T = min(ROW_T, L)

    def body(y_ref, t_ref, s_ref, d_ref):
        @pl.when(pl.program_id(0) == 0)
        def _():
            s_ref[...] = jnp.zeros_like(s_ref)

        e = y_ref[...] - t_ref[...]
        d_ref[...] = e / float(D)
        s_ref[...] += jnp.sum(e * e)

    return pl.pallas_call(
        body, name=name, grid=(L // T,), in_specs=[_row_spec(T, D), _row_spec(T, D)],
        out_specs=(pl.BlockSpec((8, 128), lambda i: (0, 0)), _row_spec(T, D)),
        out_shape=(jax.ShapeDtypeStruct((8, 128), F32), jax.ShapeDtypeStruct((L, D), F32)),
        compiler_params=_cparams(("arbitrary",), VMEM_BIG))(y, target)


CONV_T = 512


def _conv_fwd(x, ci, C, w, b, name):
    L = x.shape[0]
    K = w.shape[0]
    T = min(CONV_T, L)

    def body(xc_ref, xp_ref, w_ref, b_ref, y_ref, buf):
        i = pl.program_id(0)
        buf[0:8, :] = jnp.where(i > 0, xp_ref[...], 0.0)
        buf[8:T + 8, :] = xc_ref[...]
        acc = jnp.zeros((T, C), F32) + b_ref[...]
        for k in range(K):
            acc = acc + w_ref[k:k + 1, :] * buf[pl.ds(8 - (K - 1) + k, T), :]
        y_ref[...] = acc

    return pl.pallas_call(
        body, name=name, grid=(L // T,),
        in_specs=[pl.BlockSpec((T, C), lambda i: (i, ci)),
                  pl.BlockSpec((8, C), lambda i: (jnp.maximum(i * (T // 8) - 1, 0), ci)),
                  pl.BlockSpec((K, C), lambda i: (0, 0)), _vec_spec(C)],
        out_specs=_row_spec(T, C), out_shape=jax.ShapeDtypeStruct((L, C), F32),
        scratch_shapes=[pltpu.VMEM((T + 8, C), F32)],
        compiler_params=_cparams(("parallel",), VMEM_BIG))(x, x, w, b)


def _conv_bwd(dy, x, ci, C, w, name):
    L = x.shape[0]
    K = w.shape[0]
    T = min(CONV_T, L)
    nt = L // T

    def body(dc_ref, dn_ref, xc_ref, xp_ref, w_ref, dx_ref, dw_ref, db_ref, dbuf, xbuf):
        i = pl.program_id(0)

        @pl.when(i == 0)
        def _():
            dw_ref[...] = jnp.zeros_like(dw_ref)
            db_ref[...] = jnp.zeros_like(db_ref)

        dyc = dc_ref[...]
        dbuf[0:T, :] = dyc
        dbuf[T:T + 8, :] = jnp.where(i < nt - 1, dn_ref[...], 0.0)
        xbuf[0:8, :] = jnp.where(i > 0, xp_ref[...], 0.0)
        xbuf[8:T + 8, :] = xc_ref[...]
        dx = jnp.zeros((T, C), F32)
        for k in range(K):
            dx = dx + w_ref[k:k + 1, :] * dbuf[pl.ds(K - 1 - k, T), :]
            dw_ref[k:k + 1, :] += jnp.sum(dyc * xbuf[pl.ds(8 - (K - 1) + k, T), :], axis=0, keepdims=True)
        dx_ref[...] = dx.astype(BF16)
        db_ref[...] += jnp.sum(dyc, axis=0, keepdims=True)

    return pl.pallas_call(
        body, name=name, grid=(nt,),
        in_specs=[_row_spec(T, C),
                  pl.BlockSpec((8, C), lambda i: (jnp.minimum((i + 1) * (T // 8), L // 8 - 1), 0)),
                  pl.BlockSpec((T, C), lambda i: (i, ci)),
                  pl.BlockSpec((8, C), lambda i: (jnp.maximum(i * (T // 8) - 1, 0), ci)),
                  pl.BlockSpec((K, C), lambda i: (0, 0))],
        out_specs=(_row_spec(T, C), pl.BlockSpec((K, C), lambda i: (0, 0)), _vec_spec(C)),
        out_shape=(jax.ShapeDtypeStruct((L, C), BF16), jax.ShapeDtypeStruct((K, C), F32),
                   jax.ShapeDtypeStruct((1, C), F32)),
        scratch_shapes=[pltpu.VMEM((T + 8, C), F32), pltpu.VMEM((T + 8, C), F32)],
        compiler_params=_cparams(("arbitrary",), VMEM_BIG))(dy, dy, x, x, w)


def _sc_fwd(proj, w, name):
    L = proj.shape[0]
    K = w.shape[0]
    C = GW
    T = min(CONV_T, L)
    cb0 = P_CB // C

    def body(b_ref, cc_ref, ch_ref, cp_ref, hp_ref, w_ref, y_ref, buf):
        i = pl.program_id(0)
        buf[0:8, :] = jnp.where(i > 0, cp_ref[...] * hp_ref[...], 0.0)
        buf[8:T + 8, :] = cc_ref[...] * ch_ref[...]
        acc = jnp.zeros((T, C), F32)
        for k in range(K):
            acc = acc + w_ref[k:k + 1, :] * buf[pl.ds(8 - (K - 1) + k, T), :]
        y_ref[...] = (b_ref[...] * acc).astype(BF16)

    def cur(j):
        return pl.BlockSpec((T, C), lambda i: (i, cb0 + j))

    def prev(j):
        return pl.BlockSpec((8, C), lambda i: (jnp.maximum(i * (T // 8) - 1, 0), cb0 + j))

    return pl.pallas_call(
        body, name=name, grid=(L // T,),
        in_specs=[cur(0), cur(1), cur(2), prev(1), prev(2), pl.BlockSpec((K, C), lambda i: (0, 0))],
        out_specs=_row_spec(T, C), out_shape=jax.ShapeDtypeStruct((L, C), BF16),
        scratch_shapes=[pltpu.VMEM((T + 8, C), F32)],
        compiler_params=_cparams(("parallel",), VMEM_BIG))(proj, proj, proj, proj, proj, w)


def _sc_bwd(dycat, proj, w, name):
    L = proj.shape[0]
    K = w.shape[0]
    C = GW
    T = min(CONV_T, L)
    nt = L // T
    cb0 = P_CB // C

    def body(dy_ref, dyn_ref, b_ref, bn_ref, cc_ref, ch_ref, cp_ref, hp_ref, w_ref, d_ref, dw_ref, gbuf, ubuf):
        i = pl.program_id(0)

        @pl.when(i == 0)
        def _():
            dw_ref[...] = jnp.zeros_like(dw_ref)

        dy = dy_ref[...]
        cc, ch = cc_ref[...], ch_ref[...]
        g = dy * b_ref[...]
        gbuf[0:T, :] = g
        gbuf[T:T + 8, :] = jnp.where(i < nt - 1, dyn_ref[...] * bn_ref[...], 0.0)
        ubuf[0:8, :] = jnp.where(i > 0, cp_ref[...] * hp_ref[...], 0.0)
        ubuf[8:T + 8, :] = cc * ch
        conv = jnp.zeros((T, C), F32)
        du = jnp.zeros((T, C), F32)
        for k in range(K):
            us = ubuf[pl.ds(8 - (K - 1) + k, T), :]
            conv = conv + w_ref[k:k + 1, :] * us
            du = du + w_ref[k:k + 1, :] * gbuf[pl.ds(K - 1 - k, T), :]
            dw_ref[k:k + 1, :] += jnp.sum(g * us, axis=0, keepdims=True)
        d_ref[:, 0:C] = (dy * conv).astype(BF16)
        d_ref[:, C:2 * C] = (du * ch).astype(BF16)
        d_ref[:, 2 * C:3 * C] = (du * cc).astype(BF16)

    def cur(j):
        return pl.BlockSpec((T, C), lambda i: (i, cb0 + j))

    def prev(j):
        return pl.BlockSpec((8, C), lambda i: (jnp.maximum(i * (T // 8) - 1, 0), cb0 + j))

    def nxt(col):
        return pl.BlockSpec((8, C), lambda i: (jnp.minimum((i + 1) * (T // 8), L // 8 - 1), col))

    return pl.pallas_call(
        body, name=name, grid=(nt,),
        in_specs=[pl.BlockSpec((T, C), lambda i: (i, 3)), nxt(3), cur(0), nxt(cb0), cur(1), cur(2), prev(1), prev(2),
                  pl.BlockSpec((K, C), lambda i: (0, 0))],
        out_specs=(_row_spec(T, 3 * C), pl.BlockSpec((K, C), lambda i: (0, 0))),
        out_shape=(jax.ShapeDtypeStruct((L, 3 * C), BF16), jax.ShapeDtypeStruct((K, C), F32)),
        scratch_shapes=[pltpu.VMEM((T + 8, C), F32), pltpu.VMEM((T + 8, C), F32)],
        compiler_params=_cparams(("arbitrary",), VMEM_BIG))(dycat, dycat, proj, proj, proj, proj, proj, proj, w)


def _silu(x):
    return x * jax.nn.sigmoid(x)


def _softplus(x):
    return jnp.maximum(x, 0.0) + jnp.log1p(jnp.exp(-jnp.abs(x)))


def _lane_pick(arr, idx):
    lane = lax.broadcasted_iota(jnp.int32, (1, 128), 1)
    return jnp.sum(jnp.where(lane == idx, arr, 0.0), axis=1, keepdims=True)


def _tri(n, strict=False):
    r = lax.broadcasted_iota(jnp.int32, (n, n), 0)
    c = lax.broadcasted_iota(jnp.int32, (n, n), 1)
    return (r > c) if strict else (r >= c)


def _bdot(a, b, dn=(((1,), (0,)), ((), ()))):
    return lax.dot_general(a.astype(BF16), b.astype(BF16), dn, preferred_element_type=F32)


NT = (((1,), (1,)), ((), ()))
TN = (((0,), (0,)), ((), ()))


def _split3(x):
    hi = x.astype(BF16)
    r = x - hi.astype(F32)
    mid = r.astype(BF16)
    lo = (r - mid.astype(F32)).astype(BF16)
    return hi, mid, lo


def _mask_dot(pieces, mask, dn, mask_first):
    def one(p):
        ops = (mask, p) if mask_first else (p, mask)
        return lax.dot_general(ops[0], ops[1], dn, preferred_element_type=F32)
    hi, mid, lo = pieces
    return (one(lo) + one(mid)) + one(hi)


def _tri_apply(x, upper):
    n = x.shape[0]
    r = lax.broadcasted_iota(jnp.int32, (n, n), 0)
    c = lax.broadcasted_iota(jnp.int32, (n, n), 1)
    mask = ((r <= c) if upper else (r >= c)).astype(BF16)
    return _mask_dot(_split3(x), mask, (((1,), (0,)), ((), ())), True)


@jax.custom_vjp
def _cumsum_col(cb):
    return _tri_apply(cb, False)


_cumsum_col.defvjp(lambda cb: (_tri_apply(cb, False), None), lambda _, d: (_tri_apply(d, True),))


def _cumsum_row_fwd(cb):
    tri = _tri(cb.shape[0]).astype(BF16)
    return _mask_dot(_split3(cb), tri, (((0,), (1,)), ((), ())), False)


def _cumsum_row_bwd(_, d):
    tri = _tri(d.shape[1]).astype(BF16)
    return (_mask_dot(_split3(d), tri, (((0,), (1,)), ((), ())), True),)


@jax.custom_vjp
def _cumsum_row(cb):
    return _cumsum_row_fwd(cb)


_cumsum_row.defvjp(lambda cb: (_cumsum_row_fwd(cb), None), _cumsum_row_bwd)


def _cumsum_mats(col, n):
    cb = jnp.broadcast_to(col, (n, n))
    return _cumsum_col(cb), _cumsum_row(cb)


def _dot3(a, b):
    ah, bh = a.astype(BF16), b.astype(BF16)
    al, bl = (a - ah.astype(F32)).astype(BF16), (b - bh.astype(F32)).astype(BF16)
    d = lambda p, q: jnp.dot(p, q, preferred_element_type=F32)
    return (d(al, bh) + d(ah, bl)) + d(ah, bh)


def _unit_lower_inv_impl(m):
    n = m.shape[0]
    eye = (lax.broadcasted_iota(jnp.int32, (n, n), 0) == lax.broadcasted_iota(jnp.int32, (n, n), 1)).astype(F32)
    p = -m
    t = eye + p
    for _ in range(int(math.log2(n)) - 1):
        p = _dot3(p, p)
        t = t + _dot3(t, p)
    return t


@jax.custom_vjp
def _unit_lower_inv(m):
    return _unit_lower_inv_impl(m)


def _unit_lower_inv_fwd(m):
    t = _unit_lower_inv_impl(m)
    return t, t


def _unit_lower_inv_bwd(t, d):
    return (-_bdot(_bdot(t, d, TN), t, NT),)


_unit_lower_inv.defvjp(_unit_lower_inv_fwd, _unit_lower_inv_bwd)


def _gdn_chunk_fn(h, qc, kc, vc, zc, gates, S, alog_row, dtb_row, nw):
    n = GDN_CHUNK
    q, k, v = _silu(qc), _silu(kc), _silu(vc)
    q = q * lax.rsqrt(jnp.sum(q * q, axis=-1, keepdims=True) + EPS) * (128.0 ** -0.5)
    k = k * lax.rsqrt(jnp.sum(k * k, axis=-1, keepdims=True) + EPS)
    beta = jax.nn.sigmoid(_lane_pick(gates, LANE_GB + h))
    a = _lane_pick(gates, LANE_GA + h)
    g = -jnp.exp(_lane_pick(alog_row, LANE_GA + h)) * _softplus(a + _lane_pick(dtb_row, LANE_GA + h))
    gc_col, gc_row = _cumsum_mats(g, n)
    tri_incl, tri_strict = _tri(n), _tri(n, strict=True)
    decay = jnp.exp(jnp.where(tri_incl, gc_col - gc_row, NEG))
    gc = gc_col[:, 0:1]
    g_last = jnp.sum(g, axis=0, keepdims=True)
    kb = k * beta
    m = jnp.where(tri_strict, _bdot(kb, k, NT) * decay, 0.0)
    t_inv = _unit_lower_inv(m)
    egc = jnp.exp(gc)
    u = _bdot(t_inv, v * beta)
    w = _bdot(t_inv, kb * egc)
    attn = jnp.where(tri_incl, _bdot(q, k, NT) * decay, 0.0)
    v_new = u - _bdot(w, S)
    o = _bdot(q * egc, S) + _bdot(attn, v_new)
    s_new = S * jnp.exp(g_last) + _bdot(k * jnp.exp(g_last - gc), v_new, TN)
    y = _rms(o, nw) * _silu(zc)
    return y, s_new


def _gdn_fwd(qkvc, proj, alog_row, dtb_row, nw, name):
    L = qkvc.shape[0]
    n = GDN_CHUNK
    Nc = L // n

    def body(qkv_ref, z_ref, g_ref, al_ref, db_ref, nw_ref, y_ref, sin_ref, S):
        @pl.when(pl.program_id(0) == 0)
        def _():
            S[...] = jnp.zeros_like(S)

        gates = g_ref[...]
        for h in range(4):
            sl = slice(128 * h, 128 * h + 128)
            sh = S[h]
            sin_ref[0, h] = sh
            y, sn = _gdn_chunk_fn(h, qkv_ref[:, sl], qkv_ref[:, 512 + 128 * h:640 + 128 * h],
                                  qkv_ref[:, 1024 + 128 * h:1152 + 128 * h], z_ref[:, sl], gates, sh,
                                  al_ref[...], db_ref[...], nw_ref[...])
            y_ref[:, sl] = y.astype(BF16)
            S[h] = sn

    return pl.pallas_call(
        body, name=name, grid=(Nc,),
        in_specs=[pl.BlockSpec((n, 1536), lambda i: (i, 0)), pl.BlockSpec((n, 512), lambda i: (i, P_GZ // 512)),
                  pl.BlockSpec((n, 128), lambda i: (i, P_GATE // 128)), _vec_spec(128), _vec_spec(128), _vec_spec(128)],
        out_specs=(pl.BlockSpec((n, 512), lambda i: (i, 0)), pl.BlockSpec((1, 4, 128, 128), lambda i: (i, 0, 0, 0))),
        out_shape=(jax.ShapeDtypeStruct((L, 512), BF16), jax.ShapeDtypeStruct((Nc, 4, 128, 128), F32)),
        scratch_shapes=[pltpu.VMEM((4, 128, 128), F32)],
        compiler_params=_cparams(("arbitrary",), VMEM_BIG))(qkvc, proj, proj, alog_row, dtb_row, nw)


def _gdn_bwd(qkvc, proj, alog_row, dtb_row, nw, s_in, dycat, name):
    L = qkvc.shape[0]
    n = GDN_CHUNK
    Nc = L // n

    def body(qkv_ref, z_ref, g_ref, al_ref, db_ref, nw_ref, sin_ref, dy_ref,
             dqkv_ref, dz_ref, dg_ref, dal_ref, ddb_ref, dnw_ref, dS):
        @pl.when(pl.program_id(0) == 0)
        def _():
            dS[...] = jnp.zeros_like(dS)
            dal_ref[...] = jnp.zeros_like(dal_ref)
            ddb_ref[...] = jnp.zeros_like(ddb_ref)
            dnw_ref[...] = jnp.zeros_like(dnw_ref)

        gates = g_ref[...]
        dgates = jnp.zeros_like(gates)
        for h in range(4):
            sl = slice(128 * h, 128 * h + 128)
            slk = slice(512 + 128 * h, 640 + 128 * h)
            slv = slice(1024 + 128 * h, 1152 + 128 * h)
            _, vjp = jax.vjp(functools.partial(_gdn_chunk_fn, h), qkv_ref[:, sl], qkv_ref[:, slk], qkv_ref[:, slv],
                             z_ref[:, sl], gates, sin_ref[0, h], al_ref[...], db_ref[...], nw_ref[...])
            dq, dk, dv, dz, dgt, ds, dal, ddb, dnw = vjp((dy_ref[:, sl], dS[h]))
            dqkv_ref[:, sl] = dq
            dqkv_ref[:, slk] = dk
            dqkv_ref[:, slv] = dv
            dz_ref[:, sl] = dz.astype(BF16)
            dgates = dgates + dgt
            dS[h] = ds
            dal_ref[...] += dal
            ddb_ref[...] += ddb
            dnw_ref[...] += dnw
        dg_ref[...] = dgates

    rev = lambda i: Nc - 1 - i
    vec = jax.ShapeDtypeStruct((1, 128), F32)
    return pl.pallas_call(
        body, name=name, grid=(Nc,),
        in_specs=[pl.BlockSpec((n, 1536), lambda i: (rev(i), 0)), pl.BlockSpec((n, 512), lambda i: (rev(i), P_GZ // 512)),
                  pl.BlockSpec((n, 128), lambda i: (rev(i), P_GATE // 128)), _vec_spec(128), _vec_spec(128), _vec_spec(128),
                  pl.BlockSpec((1, 4, 128, 128), lambda i: (rev(i), 0, 0, 0)), pl.BlockSpec((n, 512), lambda i: (rev(i), 0))],
        out_specs=(pl.BlockSpec((n, 1536), lambda i: (rev(i), 0)), pl.BlockSpec((n, 512), lambda i: (rev(i), 0)),
                   pl.BlockSpec((n, 128), lambda i: (rev(i), 0)), _vec_spec(128), _vec_spec(128), _vec_spec(128)),
        out_shape=(jax.ShapeDtypeStruct((L, 1536), F32), jax.ShapeDtypeStruct((L, 512), BF16),
                   jax.ShapeDtypeStruct((L, 128), F32), vec, vec, vec),
        scratch_shapes=[pltpu.VMEM((4, 128, 128), F32)],
        compiler_params=_cparams(("arbitrary",), VMEM_BIG))(qkvc, proj, proj, alog_row, dtb_row, nw, s_in, dycat)


def _ssd_chunk_fn(xs, bs, cs, zs, gates, st, alog_row, dtb_row, d_row, nws):
    n = SSM_CHUNK
    tri = _tri(n)
    lane = lax.broadcasted_iota(jnp.int32, (1, 128), 1)
    lo = lane < 64
    xs = [_silu(t) for t in xs]
    bs = [_silu(t) for t in bs]
    cs = [_silu(t) for t in cs]
    cb = [_bdot(cs[g], bs[g], NT) for g in range(2)]
    ys, st_new = [], []
    for p in range(4):
        g = p // 2
        ydiag = jnp.zeros((n, 128), F32)
        cs_lane = jnp.zeros((n, 128), F32)
        dt_lane = jnp.zeros((n, 128), F32)
        tot_lane = jnp.zeros((1, 128), F32)
        dsk_lane = jnp.zeros((1, 128), F32)
        for half in range(2):
            h = 2 * p + half
            hm = lo if half == 0 else jnp.logical_not(lo)
            dt = _softplus(_lane_pick(gates, LANE_SDT + h) + _lane_pick(dtb_row, LANE_SDT + h))
            da = dt * (-jnp.exp(_lane_pick(alog_row, LANE_SDT + h)))
            cs_col, cs_row = _cumsum_mats(da, n)
            lmat = jnp.exp(jnp.where(tri, cs_col - cs_row, NEG))
            xdt_h = jnp.where(hm, xs[p] * dt, 0.0)
            ydiag = ydiag + _bdot(cb[g] * lmat, xdt_h)
            cs_lane = jnp.where(hm, cs_col, cs_lane)
            dt_lane = jnp.where(hm, dt, dt_lane)
            tot_lane = jnp.where(hm, jnp.sum(da, axis=0, keepdims=True), tot_lane)
            dsk_lane = jnp.where(hm, _lane_pick(d_row, LANE_SDT + h), dsk_lane)
        xdt = xs[p] * dt_lane
        st_new.append(st[p] * jnp.exp(tot_lane) + _bdot(bs[g], xdt * jnp.exp(tot_lane - cs_lane), TN))
        yoff = _bdot(cs[g], st[p]) * jnp.exp(cs_lane)
        ys.append((ydiag + yoff + xs[p] * dsk_lane) * _silu(zs[p]))
    out = []
    for g in range(2):
        ms = (jnp.sum(ys[2 * g] ** 2, axis=-1, keepdims=True) + jnp.sum(ys[2 * g + 1] ** 2, axis=-1, keepdims=True)) / 256.0
        rstd = lax.rsqrt(ms + EPS)
        out += [ys[2 * g] * rstd * nws[2 * g], ys[2 * g + 1] * rstd * nws[2 * g + 1]]
    return tuple(out), tuple(st_new)


def _ssd_args(xbc_ref, z_ref, nw_ref):
    xs = [xbc_ref[:, 128 * p:128 * p + 128] for p in range(4)]
    bs = [xbc_ref[:, 512 + 128 * g:640 + 128 * g] for g in range(2)]
    cs = [xbc_ref[:, 768 + 128 * g:896 + 128 * g] for g in range(2)]
    zs = [z_ref[:, 128 * p:128 * p + 128] for p in range(4)]
    nws = [nw_ref[:, 128 * p:128 * p + 128] for p in range(4)]
    return xs, bs, cs, zs, nws


def _ssd_fwd(xbcc, proj, alog_row, dtb_row, d_row, nw, name):
    L = xbcc.shape[0]
    n = SSM_CHUNK
    Nc = L // n

    def body(xbc_ref, z_ref, g_ref, al_ref, db_ref, d_ref, nw_ref, y_ref, sin_ref, S):
        @pl.when(pl.program_id(0) == 0)
        def _():
            S[...] = jnp.zeros_like(S)

        xs, bs, cs, zs, nws = _ssd_args(xbc_ref, z_ref, nw_ref)
        st = [S[p] for p in range(4)]
        sin_ref[0] = S[...]
        ys, sn = _ssd_chunk_fn(xs, bs, cs, zs, g_ref[...], st, al_ref[...], db_ref[...], d_ref[...], nws)
        for p in range(4):
            y_ref[:, 128 * p:128 * p + 128] = ys[p].astype(BF16)
            S[p] = sn[p]

    return pl.pallas_call(
        body, name=name, grid=(Nc,),
        in_specs=[pl.BlockSpec((n, 1024), lambda i: (i, 0)), pl.BlockSpec((n, 512), lambda i: (i, P_SZ // 512)),
                  pl.BlockSpec((n, 128), lambda i: (i, P_GATE // 128)), _vec_spec(128), _vec_spec(128), _vec_spec(128),
                  _vec_spec(512)],
        out_specs=(pl.BlockSpec((n, 512), lambda i: (i, 0)), pl.BlockSpec((1, 4, 128, 128), lambda i: (i, 0, 0, 0))),
        out_shape=(jax.ShapeDtypeStruct((L, 512), BF16), jax.ShapeDtypeStruct((Nc, 4, 128, 128), F32)),
        scratch_shapes=[pltpu.VMEM((4, 128, 128), F32)],
        compiler_params=_cparams(("arbitrary",), VMEM_BIG))(xbcc, proj, proj, alog_row, dtb_row, d_row, nw)


def _ssd_bwd(xbcc, proj, alog_row, dtb_row, d_row, nw, s_in, dycat, name):
    L = xbcc.shape[0]
    n = SSM_CHUNK
    Nc = L // n

    def body(xbc_ref, z_ref, g_ref, al_ref, db_ref, d_ref, nw_ref, sin_ref, dy_ref,
             dxbc_ref, dz_ref, dg_ref, dal_ref, ddb_ref, dd_ref, dnw_ref, dS):
        @pl.when(pl.program_id(0) == 0)
        def _():
            dS[...] = jnp.zeros_like(dS)
            dal_ref[...] = jnp.zeros_like(dal_ref)
            ddb_ref[...] = jnp.zeros_like(ddb_ref)
            dd_ref[...] = jnp.zeros_like(dd_ref)
            dnw_ref[...] = jnp.zeros_like(dnw_ref)

        xs, bs, cs, zs, nws = _ssd_args(xbc_ref, z_ref, nw_ref)
        st = [sin_ref[0, p] for p in range(4)]
        _, vjp = jax.vjp(_ssd_chunk_fn, xs, bs, cs, zs, g_ref[...], st, al_ref[...], db_ref[...], d_ref[...], nws)
        dys = tuple(dy_ref[:, 128 * p:128 * p + 128] for p in range(4))
        dxs, dbs, dcs, dzs, dgt, dst, dal, ddb, dd, dnws = vjp((dys, tuple(dS[p] for p in range(4))))
        for p in range(4):
            dxbc_ref[:, 128 * p:128 * p + 128] = dxs[p]
            dz_ref[:, 128 * p:128 * p + 128] = dzs[p].astype(BF16)
            dS[p] = dst[p]
            dnw_ref[:, 128 * p:128 * p + 128] += dnws[p]
        for g in range(2):
            dxbc_ref[:, 512 + 128 * g:640 + 128 * g] = dbs[g]
            dxbc_ref[:, 768 + 128 * g:896 + 128 * g] = dcs[g]
        dg_ref[...] = dgt
        dal_ref[...] += dal
        ddb_ref[...] += ddb
        dd_ref[...] += dd

    rev = lambda i: Nc - 1 - i
    vec = jax.ShapeDtypeStruct((1, 128), F32)
    return pl.pallas_call(
        body, name=name, grid=(Nc,),
        in_specs=[pl.BlockSpec((n, 1024), lambda i: (rev(i), 0)), pl.BlockSpec((n, 512), lambda i: (rev(i), P_SZ // 512)),
                  pl.BlockSpec((n, 128), lambda i: (rev(i), P_GATE // 128)), _vec_spec(128), _vec_spec(128), _vec_spec(128),
                  _vec_spec(512), pl.BlockSpec((1, 4, 128, 128), lambda i: (rev(i), 0, 0, 0)),
                  pl.BlockSpec((n, 512), lambda i: (rev(i), 1))],
        out_specs=(pl.BlockSpec((n, 1024), lambda i: (rev(i), 0)), pl.BlockSpec((n, 512), lambda i: (rev(i), 0)),
                   pl.BlockSpec((n, 128), lambda i: (rev(i), 0)), _vec_spec(128), _vec_spec(128), _vec_spec(128),
                   _vec_spec(512)),
        out_shape=(jax.ShapeDtypeStruct((L, 1024), F32), jax.ShapeDtypeStruct((L, 512), BF16),
                   jax.ShapeDtypeStruct((L, 128), F32), vec, vec, vec, jax.ShapeDtypeStruct((1, 512), F32)),
        scratch_shapes=[pltpu.VMEM((4, 128, 128), F32)],
        compiler_params=_cparams(("arbitrary",), VMEM_BIG))(xbcc, proj, proj, alog_row, dtb_row, d_row, nw, s_in, dycat)


def _rot(x):
    W = x.shape[1]
    lane = lax.broadcasted_iota(jnp.int32, (1, W), 1)
    first = (lane % 64) < 32
    return jnp.where(first, -pltpu.roll(x, W - 32, 1), pltpu.roll(x, 32, 1))


def _rope(x, cos, sin):
    return x * cos + _rot(x) * sin


def _rope_t(d, cos, sin):
    return d * cos - _rot(d * sin)


def _swa_core_fn(qs, ks, vs, sink_row, mask):
    lane = lax.broadcasted_iota(jnp.int32, (1, 128), 1)
    lo = lane < 64
    outs = []
    for p in range(4):
        g = p // 2
        halves = []
        for half in range(2):
            hm = lo if half == 0 else jnp.logical_not(lo)
            s = _bdot(jnp.where(hm, qs[p], 0.0), ks[g], NT) * 0.125
            s = jnp.where(mask, s, NEG)
            sk = _lane_pick(sink_row, 2 * p + half)
            mx = lax.stop_gradient(jnp.maximum(jnp.max(s, axis=-1, keepdims=True), sk))
            pr = jnp.exp(s - mx)
            pr = pr / (jnp.sum(pr, axis=-1, keepdims=True) + jnp.exp(sk - mx))
            halves.append(_bdot(pr, vs[g]))
        outs.append(jnp.where(lo, halves[0], halves[1]))
    return tuple(outs)


def _swa_prepare(blk, q_ref, kc_ref, kp_ref, vc_ref, vp_ref, cc_ref, sc_ref, cp_ref, sp_ref):
    W = WINDOW
    cos_c, sin_c = cc_ref[...], sc_ref[...]
    cos_b = jnp.concatenate([cp_ref[...], cos_c], axis=0)
    sin_b = jnp.concatenate([sp_ref[...], sin_c], axis=0)
    cos_q, sin_q = jnp.concatenate([cos_c] * 4, axis=1), jnp.concatenate([sin_c] * 4, axis=1)
    cos_k, sin_k = jnp.concatenate([cos_b] * 2, axis=1), jnp.concatenate([sin_b] * 2, axis=1)
    qr = _rope(q_ref[...], cos_q, sin_q)
    kr = _rope(jnp.concatenate([kp_ref[...], kc_ref[...]], axis=0), cos_k, sin_k)
    vb = jnp.concatenate([vp_ref[...], vc_ref[...]], axis=0)
    qi = lax.broadcasted_iota(jnp.int32, (W, 2 * W), 0)
    kj = lax.broadcasted_iota(jnp.int32, (W, 2 * W), 1)
    rel = qi + W - kj
    mask = (rel >= 0) & (rel < W) & ((blk > 0) | (kj >= W))
    qs = [qr[:, 128 * p:128 * p + 128] for p in range(4)]
    ks = [kr[:, 128 * g:128 * g + 128] for g in range(2)]
    vs = [vb[:, 128 * g:128 * g + 128] for g in range(2)]
    return qs, ks, vs, mask, (cos_q, sin_q, cos_k, sin_k)


def _swa_specs(nb, order):
    W = WINDOW
    cur = lambda i: order(i)
    prv = lambda i: jnp.maximum(order(i) - 1, 0)
    return [pl.BlockSpec((W, 512), lambda i: (cur(i), P_AQ // 512)),
            pl.BlockSpec((W, 256), lambda i: (cur(i), P_AK // 256)), pl.BlockSpec((W, 256), lambda i: (prv(i), P_AK // 256)),
            pl.BlockSpec((W, 256), lambda i: (cur(i), P_AV // 256)), pl.BlockSpec((W, 256), lambda i: (prv(i), P_AV // 256)),
            pl.BlockSpec((W, 128), lambda i: (cur(i), 0)), pl.BlockSpec((W, 128), lambda i: (cur(i), 0)),
            pl.BlockSpec((W, 128), lambda i: (prv(i), 0)), pl.BlockSpec((W, 128), lambda i: (prv(i), 0)),
            _vec_spec(128)]


def _swa_fwd(proj, cos, sin, sink_row, name):
    L = proj.shape[0]
    W = WINDOW
    nb = L // W

    def body(q_ref, kc_ref, kp_ref, vc_ref, vp_ref, cc_ref, sc_ref, cp_ref, sp_ref, sk_ref, y_ref):
        blk = pl.program_id(0)
        qs, ks, vs, mask, _ = _swa_prepare(blk, q_ref, kc_ref, kp_ref, vc_ref, vp_ref, cc_ref, sc_ref, cp_ref, sp_ref)
        outs = _swa_core_fn(qs, ks, vs, sk_ref[...], mask)
        for p in range(4):
            y_ref[:, 128 * p:128 * p + 128] = outs[p].astype(BF16)

    return pl.pallas_call(
        body, name=name, grid=(nb,), in_specs=_swa_specs(nb, lambda i: i),
        out_specs=pl.BlockSpec((W, 512), lambda i: (i, 0)), out_shape=jax.ShapeDtypeStruct((L, 512), BF16),
        compiler_params=_cparams(("parallel",), VMEM_BIG))(proj, proj, proj, proj, proj, cos, sin, cos, sin, sink_row)


def _swa_bwd(proj, cos, sin, sink_row, dycat, name):
    L = proj.shape[0]
    W = WINDOW
    nb = L // W
    rev = lambda i: nb - 1 - i

    def body(q_ref, kc_ref, kp_ref, vc_ref, vp_ref, cc_ref, sc_ref, cp_ref, sp_ref, sk_ref, dy_ref,
             dq_ref, dk_ref, dv_ref, dsk_ref, ck, cv):
        i = pl.program_id(0)
        blk = nb - 1 - i

        @pl.when(i == 0)
        def _():
            ck[...] = jnp.zeros_like(ck)
            cv[...] = jnp.zeros_like(cv)
            dsk_ref[...] = jnp.zeros_like(dsk_ref)

        qs, ks, vs, mask, (cos_q, sin_q, cos_k, sin_k) = _swa_prepare(
            blk, q_ref, kc_ref, kp_ref, vc_ref, vp_ref, cc_ref, sc_ref, cp_ref, sp_ref)
        _, vjp = jax.vjp(functools.partial(_swa_core_fn, mask=mask), qs, ks, vs, sk_ref[...])
        dqs, dks, dvs, dsk = vjp(tuple(dy_ref[:, 128 * p:128 * p + 128] for p in range(4)))
        dq_ref[...] = _rope_t(jnp.concatenate(dqs, axis=1), cos_q, sin_q).astype(BF16)
        dkb = _rope_t(jnp.concatenate(dks, axis=1), cos_k, sin_k)
        dvb = jnp.concatenate(dvs, axis=1)
        dk_ref[...] = (dkb[W:2 * W] + ck[...]).astype(BF16)
        dv_ref[...] = (dvb[W:2 * W] + cv[...]).astype(BF16)
        ck[...] = dkb[0:W]
        cv[...] = dvb[0:W]
        dsk_ref[...] += dsk

    return pl.pallas_call(
        body, name=name, grid=(nb,),
        in_specs=_swa_specs(nb, rev) + [pl.BlockSpec((W, 512), lambda i: (rev(i), 2))],
        out_specs=(pl.BlockSpec((W, 512), lambda i: (rev(i), 0)), pl.BlockSpec((W, 256), lambda i: (rev(i), 0)),
                   pl.BlockSpec((W, 256), lambda i: (rev(i), 0)), _vec_spec(128)),
        out_shape=(jax.ShapeDtypeStruct((L, 512), BF16), jax.ShapeDtypeStruct((L, 256), BF16),
                   jax.ShapeDtypeStruct((L, 256), BF16), jax.ShapeDtypeStruct((1, 128), F32)),
        scratch_shapes=[pltpu.VMEM((W, 256), F32), pltpu.VMEM((W, 256), F32)],
        compiler_params=_cparams(("arbitrary",), VMEM_BIG))(
            proj, proj, proj, proj, proj, cos, sin, cos, sin, sink_row, dycat)


def _adamw(w, g, m, v):
    m = ADAM_B1 * m + (1.0 - ADAM_B1) * g
    v = ADAM_B2 * v + (1.0 - ADAM_B2) * (g * g)
    m_hat = m / (1.0 - ADAM_B1 ** ADAM_STEP)
    v_hat = v / (1.0 - ADAM_B2 ** ADAM_STEP)
    delta = -ADAM_LR * (m_hat / (jnp.sqrt(v_hat) + ADAM_EPS) + ADAM_WD * w)
    return delta, m, v


def _ada_fwd(c_all, ada_w, ada_b_sh, name):
    S = ada_w.shape[2]

    def body(c_ref, w_ref, b_ref, o_ref):
        o_ref[0] = _bdot(_silu(c_ref[...]), w_ref[0]) + b_ref[0]

    return pl.pallas_call(
        body, name=name, grid=(2,),
        in_specs=[pl.BlockSpec((N_DEV, D), lambda i: (0, 0)), pl.BlockSpec((1, D, S), lambda i: (i, 0, 0)),
                  pl.BlockSpec((1, 1, S), lambda i: (i, 0, 0))],
        out_specs=pl.BlockSpec((1, N_DEV, S), lambda i: (i, 0, 0)), out_shape=jax.ShapeDtypeStruct((2, N_DEV, S), F32),
        compiler_params=_cparams(("parallel",), VMEM_BIG))(c_all, ada_w, ada_b_sh.reshape(2, 1, S))


def _ada_bwd_adamw(c_all, dmod_sh, w, m, v, name):
    S = w.shape[2]
    T = 256

    def body(c_ref, d_ref, w_ref, m_ref, v_ref, g_ref, dl_ref, nm_ref, nv_ref):
        g = _bdot(_silu(c_ref[...]), d_ref[0], TN)
        dl, nm, nv = _adamw(w_ref[0], g, m_ref[0], v_ref[0])
        g_ref[0], dl_ref[0], nm_ref[0], nv_ref[0] = g, dl, nm, nv

    blk = pl.BlockSpec((1, T, S), lambda i, j: (i, j, 0))
    sds = jax.ShapeDtypeStruct(w.shape, F32)
    return pl.pallas_call(
        body, name=name, grid=(2, D // T),
        in_specs=[pl.BlockSpec((N_DEV, T), lambda i, j: (0, j)), pl.BlockSpec((1, N_DEV, S), lambda i, j: (i, 0, 0)),
                  blk, blk, blk],
        out_specs=(blk, blk, blk, blk), out_shape=(sds, sds, sds, sds),
        compiler_params=_cparams(("parallel", "parallel"), VMEM_BIG))(c_all, dmod_sh, w, m, v)


def _sum8(r, name):
    _, R, W = r.shape
    T = R
    for cand in (2496, 2048, 1024, 512, 256, 128, 64, 32, 16, 8):
        if R % cand == 0:
            T = cand
            break

    def body(r_ref, o_ref):
        acc = r_ref[0].astype(F32)
        for d in range(1, N_DEV):
            acc = acc + r_ref[d].astype(F32)
        o_ref[...] = acc

    return pl.pallas_call(
        body, name=name, grid=(R // T,), in_specs=[pl.BlockSpec((N_DEV, T, W), lambda i: (0, i, 0))],
        out_specs=pl.BlockSpec((T, W), lambda i: (i, 0)), out_shape=jax.ShapeDtypeStruct((R, W), F32),
        compiler_params=_cparams(("parallel",), VMEM_BIG))(r)


def _adamw_call(w, g, m, v, name):
    R, W = w.shape
    T = R
    for cand in (1024, 512, 256, 128, 64, 32, 16, 8):
        if R % cand == 0 and R > cand and cand * W * 4 <= 2 * 1024 * 1024:
            T = cand
            break

    def body(w_ref, g_ref, m_ref, v_ref, dl_ref, nm_ref, nv_ref):
        dl_ref[...], nm_ref[...], nv_ref[...] = _adamw(w_ref[...], g_ref[...], m_ref[...], v_ref[...])

    blk = pl.BlockSpec((T, W), lambda i: (i, 0))
    sds = jax.ShapeDtypeStruct((R, W), F32)
    return pl.pallas_call(
        body, name=name, grid=(R // T,), in_specs=[blk, blk, blk, blk], out_specs=(blk, blk, blk),
        out_shape=(sds, sds, sds), compiler_params=_cparams(("parallel",), VMEM_BIG))(w, g, m, v)


def _sum8_adamw(r, w, m, v, name):
    _, _, R, W = r.shape
    T = (256 * 1024) // W
    assert R % T == 0, (R, W, T)

    def body(r_ref, w_ref, m_ref, v_ref, g_ref, dl_ref, nm_ref, nv_ref):
        g = r_ref[0].astype(F32)
        for d in range(1, N_DEV):
            g = g + r_ref[d].astype(F32)
        g_ref[...] = g
        dl_ref[...], nm_ref[...], nv_ref[...] = _adamw(w_ref[...], g, m_ref[...], v_ref[...])

    blk = pl.BlockSpec((None, T, W), lambda i, j: (i, j, 0))
    sds = jax.ShapeDtypeStruct((2, R, W), F32)
    return pl.pallas_call(
        body, name=name, grid=(2, R // T),
        in_specs=[pl.BlockSpec((N_DEV, None, T, W), lambda i, j: (0, i, j, 0)), blk, blk, blk],
        out_specs=(blk, blk, blk, blk), out_shape=(sds, sds, sds, sds),
        compiler_params=_cparams(("parallel", "parallel"), VMEM_BIG))(r, w, m, v)


def _permute_rows(wt):
    z = lambda n: jnp.zeros((n, wt.shape[1]), wt.dtype)
    s = lambda o, n: wt[o:o + n]
    kd = [s(O_AK, 64), s(O_AK, 64), s(O_AK + 64, 64), s(O_AK + 64, 64)]
    vd = [s(O_AV, 64), s(O_AV, 64), s(O_AV + 64, 64), s(O_AV + 64, 64)]
    return jnp.concatenate(
        [s(O_GQ, 1536), s(O_GZ, 512), s(O_SXBC, 1024), s(O_SZ, 512), s(O_AQ, 512), s(O_CB, 1536)] + kd + vd
        + [s(O_GB, 4), s(O_GA, 4), s(O_SDT, 8), z(112)], axis=0)


def _unpermute_rows(g):
    s = lambda o, n: g[o:o + n]
    add = lambda o: (s(o, 64).astype(F32) + s(o + 64, 64).astype(F32)).astype(g.dtype)
    dk = [add(P_AK), add(P_AK + 128)]
    dv = [add(P_AV), add(P_AV + 128)]
    return jnp.concatenate(
        [s(P_QKV, 1536), s(P_GZ, 512), s(P_GATE, 8), s(P_SZ, 512), s(P_SXBC, 1024), s(P_GATE + 8, 8), s(P_AQ, 512)]
        + dk + dv + [s(P_CB, 1536)], axis=0)


def _row128(vals, lane0):
    n = vals.shape[0]
    return jnp.concatenate([jnp.zeros((lane0,), F32), vals, jnp.zeros((128 - lane0 - n,), F32)]).reshape(1, 128)


def _pad_rows128(flat):
    n = flat.shape[0]
    pad = (-n) % 1024
    return jnp.concatenate([flat, jnp.zeros((pad,), flat.dtype)]).reshape(-1, 128)


def kernel(x, c, positions, ada_w, ada_b, norm_pre_mix, norm_post_mix, norm_pre_mlp, norm_post_mlp, w_in, w_out, gdn_conv_w, gdn_a_log, gdn_dt_bias, gdn_norm_w, ssm_conv_w, ssm_conv_b, ssm_a_log, ssm_dt_bias, ssm_d, ssm_norm_w, attn_sinks, sc_conv_w, w_up, w_down, loss_target, m_ada_w, m_ada_b, m_norm_pre_mix, m_norm_post_mix, m_norm_pre_mlp, m_norm_post_mlp, m_w_in, m_w_out, m_gdn_conv_w, m_gdn_a_log, m_gdn_dt_bias, m_gdn_norm_w, m_ssm_conv_w, m_ssm_conv_b, m_ssm_a_log, m_ssm_dt_bias, m_ssm_d, m_ssm_norm_w, m_attn_sinks, m_sc_conv_w, m_w_up, m_w_down, v_ada_w, v_ada_b, v_norm_pre_mix, v_norm_post_mix, v_norm_pre_mlp, v_norm_post_mlp, v_w_in, v_w_out, v_gdn_conv_w, v_gdn_a_log, v_gdn_dt_bias, v_gdn_norm_w, v_ssm_conv_w, v_ssm_conv_b, v_ssm_a_log, v_ssm_dt_bias, v_ssm_d, v_ssm_norm_w, v_attn_sinks, v_sc_conv_w, v_w_up, v_w_down):
    L = x.shape[1]
    me = 4 * lax.axis_index("x") + 2 * lax.axis_index("y") + lax.axis_index("c")
    x0 = x[0]
    target = loss_target[0]
    S_ADA = ada_w.shape[2]

    small = jnp.concatenate([c.reshape(-1), gdn_conv_w.reshape(-1), ssm_conv_w.reshape(-1), sc_conv_w.reshape(-1)])
    n_small = small.shape[0]
    g_small = _ag_small(_pad_rows128(small), "ag_c_conv").reshape(N_DEV, -1)[:, :n_small]
    c_all = g_small[:, :D]
    o = D
    gdn_cw = g_small[:, o:o + 2 * 4 * 192].reshape(N_DEV, 2, 4, 192).transpose(1, 2, 0, 3).reshape(2, 4, 1536)
    o += 2 * 4 * 192
    ssm_cw = g_small[:, o:o + 2 * 4 * 128].reshape(N_DEV, 2, 4, 128).transpose(1, 2, 0, 3).reshape(2, 4, 1024)
    o += 2 * 4 * 128
    sc_cw = g_small[:, o:o + 2 * 3 * 64].reshape(N_DEV, 2, 3, 64).transpose(1, 2, 0, 3).reshape(2, 3, 512)

    ada_b_sh = lax.dynamic_slice(ada_b, (0, me * S_ADA), (2, S_ADA))
    mod_sh = _ada_fwd(c_all, ada_w, ada_b_sh, "ada_fwd")
    mod_all = _ag_small(mod_sh.reshape(-1, 128), "ag_mod").reshape(N_DEV, 2, N_DEV, S_ADA)
    mod_me = lax.dynamic_index_in_dim(mod_all, me, axis=2, keepdims=False)
    mod_me = mod_me.transpose(1, 0, 2).reshape(2, 6 * D)

    win_t, wout_f, wup_g, wdown_f = [], [], [], []
    for i in range(2):
        wt = jnp.pad(w_in[i].T.astype(BF16), ((0, SHARD_PAD - SHARD_IN), (0, 0)))
        parts = [wt, w_out[i].astype(BF16), w_up[i].astype(BF16), w_down[i].astype(BF16)]
        g_in, g_out, g_up, g_down = _ag_multi(parts, f"ag_weights{i}")
        win_t.append(_permute_rows(g_in[:, :SHARD_IN].reshape(IN_W, D)))
        wout_f.append(g_out.reshape(D, D))
        wup_g.append(g_up)
        wdown_f.append(g_down.reshape(MLP, D))

    inv_freq = ROPE_THETA ** (-jnp.arange(0, 64, 2, dtype=F32) / 64)
    ang = positions[0].astype(F32)[:, None] * inv_freq
    cos = jnp.tile(jnp.cos(ang), (1, 4))
    sin = jnp.tile(jnp.sin(ang), (1, 4))

    vec = lambda a: a.reshape(1, -1)

    saved = []
    xc = x0
    for i in range(2):
        sh_a, sc_a, gt_a, sh_m, sc_m, gt_m = [vec(t) for t in jnp.split(mod_me[i], 6)]
        gal, gdb = _row128(gdn_a_log[i], LANE_GA), _row128(gdn_dt_bias[i], LANE_GA)
        sal, sdb, sdd = (_row128(ssm_a_log[i], LANE_SDT), _row128(ssm_dt_bias[i], LANE_SDT), _row128(ssm_d[i], LANE_SDT))
        gnw, snw, skr = vec(gdn_norm_w[i]), vec(ssm_norm_w[i]), _row128(attn_sinks[i], 0)
        zero_b = jnp.zeros((1, 1536), F32)

        h = _pre_fwd(xc, vec(norm_pre_mix[i]), sc_a, sh_a, f"pre_mix_fwd{i}")
        proj = _mm(h, win_t[i], name=f"in_proj{i}", tb=True, tn=896, tk=2048)
        qkvc = _conv_fwd(proj, 0, 1536, gdn_cw[i], zero_b, f"gdn_conv_fwd{i}")
        ya, gdn_s = _gdn_fwd(qkvc, proj, gal, gdb, gnw, f"gdn_fwd{i}")
        xbcc = _conv_fwd(proj, P_SXBC // 1024, 1024, ssm_cw[i], vec(ssm_conv_b[i]), f"ssm_conv_fwd{i}")
        yb, ssd_s = _ssd_fwd(xbcc, proj, sal, sdb, sdd, snw, f"ssd_fwd{i}")
        yc = _swa_fwd(proj, cos, sin, skr, f"swa_fwd{i}")
        yd = _sc_fwd(proj, sc_cw[i], f"sc_fwd{i}")
        ycat = jnp.concatenate([ya, yb, yc, yd], axis=1)
        out = _mm(ycat, wout_f[i], name=f"out_proj{i}", tk=2048)
        x1 = _post_fwd(xc, out, vec(norm_post_mix[i]), gt_a, f"post_mix_fwd{i}")
        h2 = _pre_fwd(x1, vec(norm_pre_mlp[i]), sc_m, sh_m, f"pre_mlp_fwd{i}")
        act, u = _mm(h2, wup_g[i], name=f"up_proj{i}", tk=2048, mode="relu2", shards="b")
        y2 = _mm(act, wdown_f[i], name=f"down_proj{i}", tk=2048)
        x2 = _post_fwd(x1, y2, vec(norm_post_mlp[i]), gt_m, f"post_mlp_fwd{i}")
        saved.append(dict(x0=xc, h=h, proj=proj, qkvc=qkvc, gdn_s=gdn_s, xbcc=xbcc, ssd_s=ssd_s, ycat=ycat, out=out,
                          x1=x1, h2=h2, act=act, u=u, y2=y2, mods=(sh_a, sc_a, gt_a, sh_m, sc_m, gt_m),
                          rows=(gal, gdb, sal, sdb, sdd, gnw, snw, skr)))
        xc = x2

    sq, dx = _loss_head(xc, target, "loss_head")
    loss = lax.psum(0.5 * sq[0, 0] / float(D), ("x", "y", "c"))

    gw = [dict() for _ in range(2)]
    for i in (1, 0):
        sv = saved[i]
        sh_a, sc_a, gt_a, sh_m, sc_m, gt_m = sv["mods"]
        gal, gdb, sal, sdb, sdd, gnw, snw, skr = sv["rows"]
        proj = sv["proj"]
        dy2, g_npostmlp, d_gt_m = _post_bwd(sv["y2"], vec(norm_post_mlp[i]), gt_m, dx, f"post_mlp_bwd{i}")
        du = _mm(dy2, wdown_f[i], name=f"down_proj_dx{i}", tb=True, tk=2048, out_dtype=BF16, mode="drelu2", u=sv["u"])
        g_wdown = _mm(sv["act"], dy2, name=f"down_proj_dw{i}", ta=True, out_dtype=BF16)
        dh2 = _mm(du, wup_g[i], name=f"up_proj_dx{i}", tb=True, tk=1024, out_dtype=BF16, shards="b")
        g_wup = _mm(sv["h2"], du, name=f"up_proj_dw{i}", ta=True, out_dtype=BF16, shards="o")
        dx1, g_npremlp, d_sc_m, d_sh_m = _pre_bwd(sv["x1"], vec(norm_pre_mlp[i]), sc_m, sh_m, dh2, dx, f"pre_mlp_bwd{i}")
        dout, g_npostmix, d_gt_a = _post_bwd(sv["out"], vec(norm_post_mix[i]), gt_a, dx1, f"post_mix_bwd{i}")
        dycat = _mm(dout, wout_f[i], name=f"out_proj_dx{i}", tb=True, tk=2048)
        g_wout = _mm(sv["ycat"], dout, name=f"out_proj_dw{i}", ta=True, out_dtype=BF16)

        dqkvc, dgz, dgate_g, d_gal, d_gdb, d_gnw = _gdn_bwd(sv["qkvc"], proj, gal, gdb, gnw, sv["gdn_s"], dycat, f"gdn_bwd{i}")
        dqkv, g_gdn_cw, _ = _conv_bwd(dqkvc, proj, 0, 1536, gdn_cw[i], f"gdn_conv_bwd{i}")
        dxbcc, dsz, dgate_s, d_sal, d_sdb, d_sdd, d_snw = _ssd_bwd(
            sv["xbcc"], proj, sal, sdb, sdd, snw, sv["ssd_s"], dycat, f"ssd_bwd{i}")
        dsxbc, g_ssm_cw, g_ssm_cb = _conv_bwd(dxbcc, proj, P_SXBC // 1024, 1024, ssm_cw[i], f"ssm_conv_bwd{i}")
        daq, dak, dav, d_skr = _swa_bwd(proj, cos, sin, skr, dycat, f"swa_bwd{i}")
        dsc, g_sc_cw = _sc_bwd(dycat, proj, sc_cw[i], f"sc_bwd{i}")
        dgate = (dgate_g + dgate_s).astype(BF16)
        dproj = jnp.concatenate([dqkv, dgz, dsxbc, dsz, daq, dsc, dak, dav, dgate], axis=1)
        dh = _mm(dproj, win_t[i], name=f"in_proj_dx{i}", tk=896, out_dtype=BF16)
        g_wint = _mm(dproj, sv["h"], name=f"in_proj_dw{i}", ta=True, tm=896, out_dtype=BF16)
        g_wint = jnp.pad(_unpermute_rows(g_wint).reshape(N_DEV, SHARD_IN, D), ((0, 0), (0, SHARD_PAD - SHARD_IN), (0, 0)))
        dx, g_npremix, d_sc_a, d_sh_a = _pre_bwd(sv["x0"], vec(norm_pre_mix[i]), sc_a, sh_a, dh, dx1, f"pre_mix_bwd{i}")

        gw[i] = dict(
            w_in=g_wint, w_out=g_wout.reshape(N_DEV, D // N_DEV, D), w_up=g_wup,
            w_down=g_wdown.reshape(N_DEV, MLP // N_DEV, D),
            dmod=jnp.concatenate([d_sh_a, d_sc_a, d_gt_a, d_sh_m, d_sc_m, d_gt_m], axis=1).reshape(-1),
            norm_pre_mix=g_npremix.reshape(-1), norm_post_mix=g_npostmix.reshape(-1),
            norm_pre_mlp=g_npremlp.reshape(-1), norm_post_mlp=g_npostmlp.reshape(-1),
            gdn_conv_w=g_gdn_cw.reshape(-1), gdn_a_log=d_gal[0, LANE_GA:LANE_GA + 4], gdn_dt_bias=d_gdb[0, LANE_GA:LANE_GA + 4],
            gdn_norm_w=d_gnw.reshape(-1), ssm_conv_w=g_ssm_cw.reshape(-1), ssm_conv_b=g_ssm_cb.reshape(-1),
            ssm_a_log=d_sal[0, LANE_SDT:LANE_SDT + 8], ssm_dt_bias=d_sdb[0, LANE_SDT:LANE_SDT + 8],
            ssm_d=d_sdd[0, LANE_SDT:LANE_SDT + 8], ssm_norm_w=d_snw.reshape(-1), attn_sinks=d_skr[0, 0:8],
            sc_conv_w=g_sc_cw.reshape(-1))
    grad_x = dx[None]

    big_names = ("w_in", "w_out", "w_up", "w_down")
    r_in, r_out, r_up, r_down = _exchange_multi([[gw[0][n_], gw[1][n_]] for n_ in big_names], "grad_exchange")
    grads, deltas, new_m, new_v = {}, {}, {}, {}
    for n_, r, w, m, v in (("w_out", r_out, w_out, m_w_out, v_w_out), ("w_up", r_up, w_up, m_w_up, v_w_up),
                           ("w_down", r_down, w_down, m_w_down, v_w_down)):
        grads[n_], deltas[n_], new_m[n_], new_v[n_] = _sum8_adamw(r, w, m, v, f"sum_adamw_{n_}")
    g_int = _sum8(r_in.reshape(N_DEV, 2 * SHARD_PAD, D), "grad_sum_w_in").reshape(2, SHARD_PAD, D)
    g_in = g_int[:, :SHARD_IN].transpose(0, 2, 1)
    dl, nm, nv = _adamw_call(w_in.reshape(-1, SHARD_IN), g_in.reshape(-1, SHARD_IN), m_w_in.reshape(-1, SHARD_IN),
                             v_w_in.reshape(-1, SHARD_IN), "adamw_w_in")
    grads["w_in"], deltas["w_in"] = g_in, dl.reshape(w_in.shape)
    new_m["w_in"], new_v["w_in"] = nm.reshape(w_in.shape), nv.reshape(w_in.shape)

    small_names = ["dmod", "norm_pre_mix", "norm_post_mix", "norm_pre_mlp", "norm_post_mlp", "gdn_conv_w", "gdn_a_log",
                   "gdn_dt_bias", "gdn_norm_w", "ssm_conv_w", "ssm_conv_b", "ssm_a_log", "ssm_dt_bias", "ssm_d",
                   "ssm_norm_w", "attn_sinks", "sc_conv_w"]
    flat = jnp.concatenate([gw[i][nm_] for nm_ in small_names for i in range(2)])
    n_flat = flat.shape[0]
    g_all = _ag_small(_pad_rows128(flat), "ag_small_grads")
    tot = _sum8(g_all, "small_grad_sum").reshape(-1)[:n_flat]
    dmod_all = g_all.reshape(N_DEV, -1)[:, :2 * 6 * D].reshape(N_DEV, 2, 6 * D)
    sm = {}
    o = 0
    for nm_ in small_names:
        n = gw[0][nm_].shape[0]
        sm[nm_] = jnp.stack([tot[o:o + n], tot[o + n:o + 2 * n]])
        o += 2 * n

    dmod_sh = lax.dynamic_slice(dmod_all.transpose(1, 0, 2), (0, 0, me * S_ADA), (2, N_DEV, S_ADA))
    grads["ada_w"], deltas["ada_w"], new_m["ada_w"], new_v["ada_w"] = _ada_bwd_adamw(
        c_all, dmod_sh, ada_w, m_ada_w, v_ada_w, "ada_bwd_adamw")

    def shard_cols(full, K, C):
        return lax.dynamic_slice(full.reshape(2, K, C), (0, 0, me * (C // N_DEV)), (2, K, C // N_DEV))

    small_g = dict(
        ada_b=sm["dmod"], norm_pre_mix=sm["norm_pre_mix"], norm_post_mix=sm["norm_post_mix"],
        norm_pre_mlp=sm["norm_pre_mlp"], norm_post_mlp=sm["norm_post_mlp"],
        gdn_conv_w=shard_cols(sm["gdn_conv_w"], 4, 1536), gdn_a_log=sm["gdn_a_log"], gdn_dt_bias=sm["gdn_dt_bias"],
        gdn_norm_w=sm["gdn_norm_w"], ssm_conv_w=shard_cols(sm["ssm_conv_w"], 4, 1024), ssm_conv_b=sm["ssm_conv_b"],
        ssm_a_log=sm["ssm_a_log"], ssm_dt_bias=sm["ssm_dt_bias"], ssm_d=sm["ssm_d"], ssm_norm_w=sm["ssm_norm_w"],
        attn_sinks=sm["attn_sinks"], sc_conv_w=shard_cols(sm["sc_conv_w"], 3, 512))
    small_w = dict(
        ada_b=(ada_b, m_ada_b, v_ada_b), norm_pre_mix=(norm_pre_mix, m_norm_pre_mix, v_norm_pre_mix),
        norm_post_mix=(norm_post_mix, m_norm_post_mix, v_norm_post_mix),
        norm_pre_mlp=(norm_pre_mlp, m_norm_pre_mlp, v_norm_pre_mlp),
        norm_post_mlp=(norm_post_mlp, m_norm_post_mlp, v_norm_post_mlp),
        gdn_conv_w=(gdn_conv_w, m_gdn_conv_w, v_gdn_conv_w), gdn_a_log=(gdn_a_log, m_gdn_a_log, v_gdn_a_log),
        gdn_dt_bias=(gdn_dt_bias, m_gdn_dt_bias, v_gdn_dt_bias), gdn_norm_w=(gdn_norm_w, m_gdn_norm_w, v_gdn_norm_w),
        ssm_conv_w=(ssm_conv_w, m_ssm_conv_w, v_ssm_conv_w), ssm_conv_b=(ssm_conv_b, m_ssm_conv_b, v_ssm_conv_b),
        ssm_a_log=(ssm_a_log, m_ssm_a_log, v_ssm_a_log), ssm_dt_bias=(ssm_dt_bias, m_ssm_dt_bias, v_ssm_dt_bias),
        ssm_d=(ssm_d, m_ssm_d, v_ssm_d), ssm_norm_w=(ssm_norm_w, m_ssm_norm_w, v_ssm_norm_w),
        attn_sinks=(attn_sinks, m_attn_sinks, v_attn_sinks), sc_conv_w=(sc_conv_w, m_sc_conv_w, v_sc_conv_w))
    names = list(small_w)
    sizes = [int(np.prod(small_w[n_][0].shape)) for n_ in names]
    pk = lambda j: _pad_rows128(jnp.concatenate([small_w[n_][j].reshape(-1) for n_ in names]))
    gk = _pad_rows128(jnp.concatenate([small_g[n_].reshape(-1) for n_ in names]))
    dl, nm, nv = _adamw_call(pk(0), gk, pk(1), pk(2), "adamw_small")
    dl, nm, nv = dl.reshape(-1), nm.reshape(-1), nv.reshape(-1)
    o = 0
    for n_, sz in zip(names, sizes):
        shp = small_w[n_][0].shape
        grads[n_] = small_g[n_].reshape(shp)
        deltas[n_], new_m[n_], new_v[n_] = dl[o:o + sz].reshape(shp), nm[o:o + sz].reshape(shp), nv[o:o + sz].reshape(shp)
        o += sz

    order = ["ada_w", "ada_b", "norm_pre_mix", "norm_post_mix", "norm_pre_mlp", "norm_post_mlp", "w_in", "w_out",
             "gdn_conv_w", "gdn_a_log", "gdn_dt_bias", "gdn_norm_w", "ssm_conv_w", "ssm_conv_b", "ssm_a_log",
             "ssm_dt_bias", "ssm_d", "ssm_norm_w", "attn_sinks", "sc_conv_w", "w_up", "w_down"]
    return (loss, grad_x, *[grads[n_] for n_ in order], *[deltas[n_] for n_ in order],
            *[new_m[n_] for n_ in order], *[new_v[n_] for n_ in order])
```

```python
import functools
import math

import jax
import jax.numpy as jnp
import numpy as np
from jax import lax
from jax.experimental import pallas as pl
from jax.experimental.pallas import tpu as pltpu

F32 = jnp.float32
BF16 = jnp.bfloat16
MESH = pl.DeviceIdType.MESH

N_DEV = 8
D = 2048
GW = 512
MLP = 8192
EPS = 1e-6
IN_W = 5904
SHARD_IN = IN_W // N_DEV
SHARD_PAD = 752
PW = 6272
ROPE_THETA = 10000.0
WINDOW = 128
GDN_CHUNK = 64
SSM_CHUNK = 128
NEG = -1e30

ADAM_LR, ADAM_B1, ADAM_B2, ADAM_EPS, ADAM_WD, ADAM_STEP = 0.001, 0.9, 0.999, 1e-08, 0.01, 10

O_GQ, O_GK, O_GV, O_GZ, O_GB, O_GA = 0, 512, 1024, 1536, 2048, 2052
O_SZ, O_SXBC, O_SDT = 2056, 2568, 3592
O_AQ, O_AK, O_AV = 3600, 4112, 4240
O_CB, O_CC, O_CH = 4368, 4880, 5392
P_QKV, P_GZ, P_SXBC, P_SZ, P_AQ, P_CB, P_AK, P_AV, P_GATE = 0, 1536, 2048, 3072, 3584, 4096, 5632, 5888, 6144
LANE_GB, LANE_GA, LANE_SDT = 0, 4, 8

VMEM_BIG = 56 * 1024 * 1024


def _cparams(sem=None, vmem=None):
    kw = {}
    if sem is not None:
        kw["dimension_semantics"] = sem
    if vmem is not None:
        kw["vmem_limit_bytes"] = vmem
    return pltpu.CompilerParams(**kw)


def _me():
    x, y, c = lax.axis_index("x"), lax.axis_index("y"), lax.axis_index("c")
    return x, y, c, 4 * x + 2 * y + c


def _peer(x, y, c, k):
    px = 1 - x if (k >> 2) & 1 else x
    py = 1 - y if (k >> 1) & 1 else y
    pc = 1 - c if k & 1 else c
    return (px, py, pc), 4 * px + 2 * py + pc


def _ag_small(v, name):
    R, W = v.shape

    def body(x_ref, out_ref, send_sems, recv_sems):
        x, y, c, me = _me()
        out_ref[me] = x_ref[...]
        copies = []
        for k in range(1, N_DEV):
            peer, _ = _peer(x, y, c, k)
            cp = pltpu.make_async_remote_copy(
                src_ref=x_ref, dst_ref=out_ref.at[me], send_sem=send_sems.at[k - 1], recv_sem=recv_sems.at[k - 1],
                device_id=peer, device_id_type=MESH)
            cp.start()
            copies.append(cp)
        for cp in copies:
            cp.wait()

    return pl.pallas_call(
        body, name=name,
        out_shape=jax.ShapeDtypeStruct((N_DEV, R, W), v.dtype),
        in_specs=[pl.BlockSpec(memory_space=pltpu.VMEM)],
        out_specs=pl.BlockSpec(memory_space=pltpu.VMEM),
        scratch_shapes=[pltpu.SemaphoreType.DMA((N_DEV - 1,)), pltpu.SemaphoreType.DMA((N_DEV - 1,))],
    )(v)


def _ag_multi(arrs, name):
    cargo = _AgCargo(arrs)
    A = cargo.n_in

    def body(*refs):
        start, mid, end = cargo.phases(refs[:A], refs[A:A + cargo.n_out], refs[A + cargo.n_out:])
        start()
        mid()
        end()

    hbm = pl.BlockSpec(memory_space=pl.ANY)
    return pl.pallas_call(
        body, name=name, out_shape=tuple(cargo.out_shapes), in_specs=[hbm] * A, out_specs=tuple([hbm] * cargo.n_out),
        scratch_shapes=cargo.scratch)(*cargo.arrays)


class _AgCargo:
    def __init__(self, arrs):
        A = len(arrs)
        self.arrays = list(arrs)
        self.n_in = self.n_out = A
        self.out_shapes = [jax.ShapeDtypeStruct((N_DEV,) + v.shape, v.dtype) for v in arrs]
        self.scratch = [pltpu.SemaphoreType.DMA((7 * A,)), pltpu.SemaphoreType.DMA((7 * A,)),
                        pltpu.SemaphoreType.DMA((A,))]

    def phases(self, x_refs, out_refs, sems):
        A = self.n_in
        send_sems, recv_sems, local_sems = sems

        def ctx():
            x, y, c, me = _me()
            return x, y, c, me, (x, y, 1 - c), [(1 - x, y), (x, 1 - y), (1 - x, 1 - y)]

        def copy(a, k, block, to, src=None):
            slot = out_refs[a].at[4 * block[0] + 2 * block[1] + block[2]]
            return pltpu.make_async_remote_copy(
                src_ref=slot if src is None else src, dst_ref=slot, send_sem=send_sems.at[7 * a + k],
                recv_sem=recv_sems.at[7 * a + k], device_id=to, device_id_type=MESH)

        def start():
            x, y, c, me, sibling, chips = ctx()
            for a in range(A):
                pltpu.make_async_copy(x_refs[a], out_refs[a].at[me], local_sems.at[a]).start()
            for a in range(A):
                copy(a, 0, (x, y, c), sibling, src=x_refs[a]).start()
                for j, chip in enumerate(chips):
                    copy(a, 1 + j, (x, y, c), (*chip, c), src=x_refs[a]).start()

        def mid():
            x, y, c, me, sibling, chips = ctx()
            for j, chip in enumerate(chips):
                for a in range(A):
                    copy(a, 1 + j, (*chip, c), (x, y, c)).wait_recv()
                    copy(a, 4 + j, (*chip, c), sibling).start()

        def end():
            x, y, c, me, sibling, chips = ctx()
            for a in range(A):
                copy(a, 0, sibling, (x, y, c)).wait_recv()
                for j, chip in enumerate(chips):
                    copy(a, 4 + j, (*chip, 1 - c), (x, y, c)).wait_recv()
            for a in range(A):
                copy(a, 0, (x, y, c), sibling, src=x_refs[a]).wait_send()
                for j, chip in enumerate(chips):
                    copy(a, 1 + j, (x, y, c), (*chip, c), src=x_refs[a]).wait_send()
                    copy(a, 4 + j, (*chip, c), sibling).wait_send()
                pltpu.make_async_copy(x_refs[a], out_refs[a].at[me], local_sems.at[a]).wait()

        return start, mid, end


class _ExCargo:
    def __init__(self, groups):
        self.flat = [(w, i) for w, layers in enumerate(groups) for i, _ in enumerate(layers)]
        self.arrays = [g for layers in groups for g in layers]
        A = len(self.arrays)
        self.n_in, self.n_out = A, len(groups)
        self.out_shapes = [jax.ShapeDtypeStruct((N_DEV, len(layers)) + layers[0].shape[1:], layers[0].dtype)
                           for layers in groups]
        self.scratch = [pltpu.SemaphoreType.DMA((7 * A,)), pltpu.SemaphoreType.DMA((7 * A,)),
                        pltpu.SemaphoreType.DMA((A,))]

    def phases(self, g_refs, out_refs, sems):
        send_sems, recv_sems, local_sems = sems

        def copies():
            x, y, c, me = _me()
            local = [pltpu.make_async_copy(g_refs[a].at[me], out_refs[w].at[me, i], local_sems.at[a])
                     for a, (w, i) in enumerate(self.flat)]
            remote = []
            for k in range(1, N_DEV):
                peer, pid = _peer(x, y, c, k)
                for a, (w, i) in enumerate(self.flat):
                    remote.append(pltpu.make_async_remote_copy(
                        src_ref=g_refs[a].at[pid], dst_ref=out_refs[w].at[me, i], send_sem=send_sems.at[7 * a + k - 1],
                        recv_sem=recv_sems.at[7 * a + k - 1], device_id=peer, device_id_type=MESH))
            return local, remote

        def start():
            local, remote = copies()
            for cp in local + remote:
                cp.start()

        def end():
            local, remote = copies()
            for cp in remote + local:
                cp.wait()

        return start, (lambda: None), end


def _exchange_multi(groups, name):
    cargo = _ExCargo(groups)
    A = cargo.n_in

    def body(*refs):
        start, _, end = cargo.phases(refs[:A], refs[A:A + cargo.n_out], refs[A + cargo.n_out:])
        start()
        end()

    hbm = pl.BlockSpec(memory_space=pl.ANY)
    return pl.pallas_call(
        body, name=name, out_shape=tuple(cargo.out_shapes), in_specs=[hbm] * A, out_specs=tuple([hbm] * cargo.n_out),
        scratch_shapes=cargo.scratch)(*cargo.arrays)


def _mm(a, b, *, name, ta=False, tb=False, tm=1024, tn=1024, tk=1024, out_dtype=F32, mode="plain", u=None,
        shards=None, cargo=None):
    K, M = a.shape if ta else a.shape[::-1]
    if shards == "b":
        N, K2 = (b.shape[1], b.shape[0] * b.shape[2]) if tb else (b.shape[0] * b.shape[2], b.shape[1])
    else:
        N, K2 = b.shape if tb else b.shape[::-1]
    assert K == K2, (a.shape, b.shape)
    tm, tn, tk = min(tm, M), min(tn, N), min(tk, K)
    assert M % tm == 0 and N % tn == 0 and K % tk == 0, (M, N, K, tm, tn, tk)
    nk = K // tk
    a_spec = pl.BlockSpec((tk, tm), lambda i, j, k: (k, i)) if ta else pl.BlockSpec((tm, tk), lambda i, j, k: (i, k))
    if shards == "b" and tb:
        assert tk == b.shape[2]
        b_spec = pl.BlockSpec((None, tn, tk), lambda i, j, k: (k, j, 0))
    elif shards == "b":
        assert tn == b.shape[2]
        b_spec = pl.BlockSpec((None, tk, tn), lambda i, j, k: (j, k, 0))
    else:
        b_spec = pl.BlockSpec((tn, tk), lambda i, j, k: (j, k)) if tb else pl.BlockSpec((tk, tn), lambda i, j, k: (k, j))
    if shards == "o":
        o_spec = pl.BlockSpec((None, tm, tn), lambda i, j, k: (j, i, 0))
        o_shape = (N // tn, M, tn)
    else:
        o_spec = pl.BlockSpec((tm, tn), lambda i, j, k: (i, j))
        o_shape = (M, N)
    dn = (((0 if ta else 1,), (1 if tb else 0,)), ((), ()))
    ni_, nj = M // tm, N // tn
    nsteps = ni_ * nj * nk
    n_in = 3 if mode == "drelu2" else 2
    n_out = 2 if mode == "relu2" else 1
    c_arr, c_ispec, c_oshape, c_ospec = _cargo_io(cargo)
    ni, no = len(c_arr), len(c_oshape)

    def body(*refs):
        a_ref, b_ref = refs[0], refs[1]
        u_ref = refs[2] if mode == "drelu2" else None
        outs = refs[n_in + ni:n_in + ni + n_out]
        acc = refs[n_in + ni + n_out + no]
        k = pl.program_id(2)
        step = (pl.program_id(0) * nj + pl.program_id(1)) * nk + k
        drain = _carry(cargo, (refs[n_in:n_in + ni], refs[n_in + ni + n_out:n_in + ni + n_out + no],
                               refs[n_in + ni + n_out + no + 1:]), step, nsteps)

        @pl.when(k == 0)
        def _():
            acc[...] = jnp.zeros_like(acc)

        acc[...] += lax.dot_general(a_ref[...].astype(BF16), b_ref[...].astype(BF16), dn, preferred_element_type=F32)

        @pl.when(k == nk - 1)
        def _():
            r = acc[...]
            if mode == "plain":
                outs[0][...] = r.astype(out_dtype)
            elif mode == "relu2":
                rr = jnp.maximum(r, 0.0)
                outs[0][...] = (rr * rr).astype(BF16)
                outs[1][...] = r.astype(BF16)
            else:
                outs[0][...] = (r * (2.0 * jnp.maximum(u_ref[...].astype(F32), 0.0))).astype(out_dtype)

        drain()

    in_specs, args = [a_spec, b_spec], [a, b]
    if mode == "drelu2":
        in_specs.append(o_spec)
        args.append(u)
    if mode == "relu2":
        out_shape = [jax.ShapeDtypeStruct(o_shape, BF16), jax.ShapeDtypeStruct(o_shape, BF16)]
        out_specs = [o_spec, o_spec]
    else:
        out_shape = [jax.ShapeDtypeStruct(o_shape, out_dtype)]
        out_specs = [o_spec]
    res = pl.pallas_call(
        body, name=name, grid=(ni_, nj, nk), in_specs=in_specs + c_ispec, out_specs=tuple(out_specs + c_ospec),
        out_shape=tuple(out_shape + c_oshape),
        scratch_shapes=[pltpu.VMEM((tm, tn), F32)] + (cargo.scratch if cargo else []),
        compiler_params=_cparams(("arbitrary",) * 3 if cargo else ("parallel", "parallel", "arbitrary"), VMEM_BIG),
    )(*args, *c_arr)
    return res[0] if len(res) == 1 else res


ROW_T = 256


def _rms(x, w):
    return x * lax.rsqrt(jnp.mean(x * x, axis=-1, keepdims=True) + EPS) * w


def _pre_fn(x, w, scale, shift):
    return _rms(x, w) * (1.0 + scale) + shift


def _post_fn(y, w, gate):
    return gate * _rms(y, w)


def _row_spec(T, W):
    return pl.BlockSpec((T, W), lambda i: (i, 0))


def _vec_spec(W):
    return pl.BlockSpec((1, W), lambda i: (0, 0))


def _pre_fwd(x, w, scale, shift, name):
    L = x.shape[0]
    T = min(ROW_T, L)

    def body(x_ref, w_ref, sc_ref, sh_ref, h_ref):
        h_ref[...] = _pre_fn(x_ref[...], w_ref[...], sc_ref[...], sh_ref[...]).astype(BF16)

    return pl.pallas_call(
        body, name=name, grid=(L // T,), in_specs=[_row_spec(T, D), _vec_spec(D), _vec_spec(D), _vec_spec(D)],
        out_specs=_row_spec(T, D), out_shape=jax.ShapeDtypeStruct((L, D), BF16),
        compiler_params=_cparams(("parallel",), VMEM_BIG))(x, w, scale, shift)


def _pre_bwd(x, w, scale, shift, dh, dres, name):
    L = x.shape[0]
    T = min(ROW_T, L)

    def body(x_ref, w_ref, sc_ref, sh_ref, dh_ref, dres_ref, dx_ref, dw_ref, dsc_ref, dsh_ref):
        @pl.when(pl.program_id(0) == 0)
        def _():
            dw_ref[...] = jnp.zeros_like(dw_ref)
            dsc_ref[...] = jnp.zeros_like(dsc_ref)
            dsh_ref[...] = jnp.zeros_like(dsh_ref)

        _, vjp = jax.vjp(_pre_fn, x_ref[...], w_ref[...], sc_ref[...], sh_ref[...])
        dx, dw, dsc, dsh = vjp(dh_ref[...].astype(F32))
        dx_ref[...] = dres_ref[...] + dx
        dw_ref[...] += dw
        dsc_ref[...] += dsc
        dsh_ref[...] += dsh

    vec = jax.ShapeDtypeStruct((1, D), F32)
    return pl.pallas_call(
        body, name=name, grid=(L // T,),
        in_specs=[_row_spec(T, D), _vec_spec(D), _vec_spec(D), _vec_spec(D), _row_spec(T, D), _row_spec(T, D)],
        out_specs=(_row_spec(T, D), _vec_spec(D), _vec_spec(D), _vec_spec(D)),
        out_shape=(jax.ShapeDtypeStruct((L, D), F32), vec, vec, vec),
        compiler_params=_cparams(("arbitrary",), VMEM_BIG))(x, w, scale, shift, dh, dres)


def _post_fwd(x, y, w, gate, name):
    L = x.shape[0]
    T = min(ROW_T, L)

    def body(x_ref, y_ref, w_ref, g_ref, o_ref):
        o_ref[...] = x_ref[...] + _post_fn(y_ref[...], w_ref[...], g_ref[...])

    return pl.pallas_call(
        body, name=name, grid=(L // T,), in_specs=[_row_spec(T, D), _row_spec(T, D), _vec_spec(D), _vec_spec(D)],
        out_specs=_row_spec(T, D), out_shape=jax.ShapeDtypeStruct((L, D), F32),
        compiler_params=_cparams(("parallel",), VMEM_BIG))(x, y, w, gate)


def _post_bwd(y, w, gate, dxn, name):
    L = y.shape[0]
    T = min(ROW_T, L)

    def body(y_ref, w_ref, g_ref, d_ref, dy_ref, dw_ref, dg_ref):
        @pl.when(pl.program_id(0) == 0)
        def _():
            dw_ref[...] = jnp.zeros_like(dw_ref)
            dg_ref[...] = jnp.zeros_like(dg_ref)

        _, vjp = jax.vjp(_post_fn, y_ref[...], w_ref[...], g_ref[...])
        dy, dw, dg = vjp(d_ref[...])
        dy_ref[...] = dy.astype(BF16)
        dw_ref[...] += dw
        dg_ref[...] += dg

    vec = jax.ShapeDtypeStruct((1, D), F32)
    return pl.pallas_call(
        body, name=name, grid=(L // T,), in_specs=[_row_spec(T, D), _vec_spec(D), _vec_spec(D), _row_spec(T, D)],
        out_specs=(_row_spec(T, D), _vec_spec(D), _vec_spec(D)),
        out_shape=(jax.ShapeDtypeStruct((L, D), BF16), vec, vec),
        compiler_params=_cparams(("arbitrary",), VMEM_BIG))(y, w, gate, dxn)


def _loss_head(y, target, name):
    L = y.shape[0]
    T = min(ROW_T, L)

    def body(y_ref, t_ref, s_ref, d_ref):
        @pl.when(pl.program_id(0) == 0)
        def _():
            s_ref[...] = jnp.zeros_like(s_ref)

        e = y_ref[...] - t_ref[...]
        d_ref[...] = e / float(D)
        s_ref[...] += jnp.sum(e * e)

    return pl.pallas_call(
        body, name=name, grid=(L // T,), in_specs=[_row_spec(T, D), _row_spec(T, D)],
        out_specs=(pl.BlockSpec((8, 128), lambda i: (0, 0)), _row_spec(T, D)),
        out_shape=(jax.ShapeDtypeStruct((8, 128), F32), jax.ShapeDtypeStruct((L, D), F32)),
        compiler_params=_cparams(("arbitrary",), VMEM_BIG))(y, target)


CONV_T = 512


def _conv_fwd(x, ci, C, w, b, name):
    L = x.shape[0]
    K = w.shape[0]
    T = min(CONV_T, L)

    def body(xc_ref, xp_ref, w_ref, b_ref, y_ref, buf):
        i = pl.program_id(0)
        buf[0:8, :] = jnp.where(i > 0, xp_ref[...], 0.0)
        buf[8:T + 8, :] = xc_ref[...]
        acc = jnp.zeros((T, C), F32) + b_ref[...]
        for k in range(K):
            acc = acc + w_ref[k:k + 1, :] * buf[pl.ds(8 - (K - 1) + k, T), :]
        y_ref[...] = acc

    return pl.pallas_call(
        body, name=name, grid=(L // T,),
        in_specs=[pl.BlockSpec((T, C), lambda i: (i, ci)),
                  pl.BlockSpec((8, C), lambda i: (jnp.maximum(i * (T // 8) - 1, 0), ci)),
                  pl.BlockSpec((K, C), lambda i: (0, 0)), _vec_spec(C)],
        out_specs=_row_spec(T, C), out_shape=jax.ShapeDtypeStruct((L, C), F32),
        scratch_shapes=[pltpu.VMEM((T + 8, C), F32)],
        compiler_params=_cparams(("parallel",), VMEM_BIG))(x, x, w, b)


def _conv_bwd(dy, x, ci, C, w, name):
    L = x.shape[0]
    K = w.shape[0]
    T = min(CONV_T, L)
    nt = L // T

    def body(dc_ref, dn_ref, xc_ref, xp_ref, w_ref, dx_ref, dw_ref, db_ref, dbuf, xbuf):
        i = pl.program_id(0)

        @pl.when(i == 0)
        def _():
            dw_ref[...] = jnp.zeros_like(dw_ref)
            db_ref[...] = jnp.zeros_like(db_ref)

        dyc = dc_ref[...]
        dbuf[0:T, :] = dyc
        dbuf[T:T + 8, :] = jnp.where(i < nt - 1, dn_ref[...], 0.0)
        xbuf[0:8, :] = jnp.where(i > 0, xp_ref[...], 0.0)
        xbuf[8:T + 8, :] = xc_ref[...]
        dx = jnp.zeros((T, C), F32)
        for k in range(K):
            dx = dx + w_ref[k:k + 1, :] * dbuf[pl.ds(K - 1 - k, T), :]
            dw_ref[k:k + 1, :] += jnp.sum(dyc * xbuf[pl.ds(8 - (K - 1) + k, T), :], axis=0, keepdims=True)
        dx_ref[...] = dx.astype(BF16)
        db_ref[...] += jnp.sum(dyc, axis=0, keepdims=True)

    return pl.pallas_call(
        body, name=name, grid=(nt,),
        in_specs=[_row_spec(T, C),
                  pl.BlockSpec((8, C), lambda i: (jnp.minimum((i + 1) * (T // 8), L // 8 - 1), 0)),
                  pl.BlockSpec((T, C), lambda i: (i, ci)),
                  pl.BlockSpec((8, C), lambda i: (jnp.maximum(i * (T // 8) - 1, 0), ci)),
                  pl.BlockSpec((K, C), lambda i: (0, 0))],
        out_specs=(_row_spec(T, C), pl.BlockSpec((K, C), lambda i: (0, 0)), _vec_spec(C)),
        out_shape=(jax.ShapeDtypeStruct((L, C), BF16), jax.ShapeDtypeStruct((K, C), F32),
                   jax.ShapeDtypeStruct((1, C), F32)),
        scratch_shapes=[pltpu.VMEM((T + 8, C), F32), pltpu.VMEM((T + 8, C), F32)],
        compiler_params=_cparams(("arbitrary",), VMEM_BIG))(dy, dy, x, x, w)


def _sc_fwd(proj, w, name):
    L = proj.shape[0]
    K = w.shape[0]
    C = GW
    T = min(CONV_T, L)
    cb0 = P_CB // C

    def body(b_ref, cc_ref, ch_ref, cp_ref, hp_ref, w_ref, y_ref, buf):
        i = pl.program_id(0)
        buf[0:8, :] = jnp.where(i > 0, cp_ref[...] * hp_ref[...], 0.0)
        buf[8:T + 8, :] = cc_ref[...] * ch_ref[...]
        acc = jnp.zeros((T, C), F32)
        for k in range(K):
            acc = acc + w_ref[k:k + 1, :] * buf[pl.ds(8 - (K - 1) + k, T), :]
        y_ref[...] = (b_ref[...] * acc).astype(BF16)

    def cur(j):
        return pl.BlockSpec((T, C), lambda i: (i, cb0 + j))

    def prev(j):
        return pl.BlockSpec((8, C), lambda i: (jnp.maximum(i * (T // 8) - 1, 0), cb0 + j))

    return pl.pallas_call(
        body, name=name, grid=(L // T,),
        in_specs=[cur(0), cur(1), cur(2), prev(1), prev(2), pl.BlockSpec((K, C), lambda i: (0, 0))],
        out_specs=_row_spec(T, C), out_shape=jax.ShapeDtypeStruct((L, C), BF16),
        scratch_shapes=[pltpu.VMEM((T + 8, C), F32)],
        compiler_params=_cparams(("parallel",), VMEM_BIG))(proj, proj, proj, proj, proj, w)


def _sc_bwd(dycat, proj, w, name):
    L = proj.shape[0]
    K = w.shape[0]
    C = GW
    T = min(CONV_T, L)
    nt = L // T
    cb0 = P_CB // C

    def body(dy_ref, dyn_ref, b_ref, bn_ref, cc_ref, ch_ref, cp_ref, hp_ref, w_ref, d_ref, dw_ref, gbuf, ubuf):
        i = pl.program_id(0)

        @pl.when(i == 0)
        def _():
            dw_ref[...] = jnp.zeros_like(dw_ref)

        dy = dy_ref[...]
        cc, ch = cc_ref[...], ch_ref[...]
        g = dy * b_ref[...]
        gbuf[0:T, :] = g
        gbuf[T:T + 8, :] = jnp.where(i < nt - 1, dyn_ref[...] * bn_ref[...], 0.0)
        ubuf[0:8, :] = jnp.where(i > 0, cp_ref[...] * hp_ref[...], 0.0)
        ubuf[8:T + 8, :] = cc * ch
        conv = jnp.zeros((T, C), F32)
        du = jnp.zeros((T, C), F32)
        for k in range(K):
            us = ubuf[pl.ds(8 - (K - 1) + k, T), :]
            conv = conv + w_ref[k:k + 1, :] * us
            du = du + w_ref[k:k + 1, :] * gbuf[pl.ds(K - 1 - k, T), :]
            dw_ref[k:k + 1, :] += jnp.sum(g * us, axis=0, keepdims=True)
        d_ref[:, 0:C] = (dy * conv).astype(BF16)
        d_ref[:, C:2 * C] = (du * ch).astype(BF16)
        d_ref[:, 2 * C:3 * C] = (du * cc).astype(BF16)

    def cur(j):
        return pl.BlockSpec((T, C), lambda i: (i, cb0 + j))

    def prev(j):
        return pl.BlockSpec((8, C), lambda i: (jnp.maximum(i * (T // 8) - 1, 0), cb0 + j))

    def nxt(col):
        return pl.BlockSpec((8, C), lambda i: (jnp.minimum((i + 1) * (T // 8), L // 8 - 1), col))

    return pl.pallas_call(
        body, name=name, grid=(nt,),
        in_specs=[pl.BlockSpec((T, C), lambda i: (i, 3)), nxt(3), cur(0), nxt(cb0), cur(1), cur(2), prev(1), prev(2),
                  pl.BlockSpec((K, C), lambda i: (0, 0))],
        out_specs=(_row_spec(T, 3 * C), pl.BlockSpec((K, C), lambda i: (0, 0))),
        out_shape=(jax.ShapeDtypeStruct((L, 3 * C), BF16), jax.ShapeDtypeStruct((K, C), F32)),
        scratch_shapes=[pltpu.VMEM((T + 8, C), F32), pltpu.VMEM((T + 8, C), F32)],
        compiler_params=_cparams(("arbitrary",), VMEM_BIG))(dycat, dycat, proj, proj, proj, proj, proj, proj, w)


def _silu(x):
    return x * jax.nn.sigmoid(x)


def _softplus(x):
    return jnp.maximum(x, 0.0) + jnp.log1p(jnp.exp(-jnp.abs(x)))


def _lane_pick(arr, idx):
    lane = lax.broadcasted_iota(jnp.int32, (1, 128), 1)
    return jnp.sum(jnp.where(lane == idx, arr, 0.0), axis=1, keepdims=True)


def _tri(n, strict=False):
    r = lax.broadcasted_iota(jnp.int32, (n, n), 0)
    c = lax.broadcasted_iota(jnp.int32, (n, n), 1)
    return (r > c) if strict else (r >= c)


def _bdot(a, b, dn=(((1,), (0,)), ((), ()))):
    return lax.dot_general(a.astype(BF16), b.astype(BF16), dn, preferred_element_type=F32)


NT = (((1,), (1,)), ((), ()))
TN = (((0,), (0,)), ((), ()))


def _split3(x):
    hi = x.astype(BF16)
    r = x - hi.astype(F32)
    mid = r.astype(BF16)
    lo = (r - mid.astype(F32)).astype(BF16)
    return hi, mid, lo


def _mask_dot(pieces, mask, dn, mask_first):
    def one(p):
        ops = (mask, p) if mask_first else (p, mask)
        return lax.dot_general(ops[0], ops[1], dn, preferred_element_type=F32)
    hi, mid, lo = pieces
    return (one(lo) + one(mid)) + one(hi)


def _tri_apply(x, upper):
    n = x.shape[0]
    r = lax.broadcasted_iota(jnp.int32, (n, n), 0)
    c = lax.broadcasted_iota(jnp.int32, (n, n), 1)
    mask = ((r <= c) if upper else (r >= c)).astype(BF16)
    return _mask_dot(_split3(x), mask, (((1,), (0,)), ((), ())), True)


@jax.custom_vjp
def _cumsum_col(cb):
    return _tri_apply(cb, False)


_cumsum_col.defvjp(lambda cb: (_tri_apply(cb, False), None), lambda _, d: (_tri_apply(d, True),))


def _cumsum_row_fwd(cb):
    tri = _tri(cb.shape[0]).astype(BF16)
    return _mask_dot(_split3(cb), tri, (((0,), (1,)), ((), ())), False)


def _cumsum_row_bwd(_, d):
    tri = _tri(d.shape[1]).astype(BF16)
    return (_mask_dot(_split3(d), tri, (((0,), (1,)), ((), ())), True),)


@jax.custom_vjp
def _cumsum_row(cb):
    return _cumsum_row_fwd(cb)


_cumsum_row.defvjp(lambda cb: (_cumsum_row_fwd(cb), None), _cumsum_row_bwd)


def _cumsum_mats(col, n):
    cb = jnp.broadcast_to(col, (n, n))
    return _cumsum_col(cb), _cumsum_row(cb)


def _dot3(a, b):
    ah, bh = a.astype(BF16), b.astype(BF16)
    al, bl = (a - ah.astype(F32)).astype(BF16), (b - bh.astype(F32)).astype(BF16)
    d = lambda p, q: jnp.dot(p, q, preferred_element_type=F32)
    return (d(al, bh) + d(ah, bl)) + d(ah, bh)


def _unit_lower_inv_impl(ms):
    n = ms[0].shape[0]
    eye = (lax.broadcasted_iota(jnp.int32, (n, n), 0) == lax.broadcasted_iota(jnp.int32, (n, n), 1)).astype(F32)
    ps = [-m for m in ms]
    ts = [eye + p for p in ps]
    for _ in range(int(math.log2(n)) - 1):
        ps = [_dot3(p, p) for p in ps]
        ts = [t + _dot3(t, p) for t, p in zip(ts, ps)]
    return tuple(ts)


@jax.custom_vjp
def _unit_lower_inv(ms):
    return _unit_lower_inv_impl(ms)


def _unit_lower_inv_fwd(ms):
    ts = _unit_lower_inv_impl(ms)
    return ts, ts


def _unit_lower_inv_bwd(ts, ds):
    xs = [_bdot(t, d, TN) for t, d in zip(ts, ds)]
    return (tuple(-_bdot(x, t, NT) for x, t in zip(xs, ts)),)


_unit_lower_inv.defvjp(_unit_lower_inv_fwd, _unit_lower_inv_bwd)


GDN_SUB = 2
GDN_H = 4


def _gdn_step_fn(qs, ks, vs, zs, gs, S, alog_row, dtb_row, nw):
    n = GDN_CHUNK
    lanes = [(h, c) for h in range(GDN_H) for c in range(GDN_SUB)]
    z3 = lambda f, a, b, c_: [f(x, y, z) for x, y, z in zip(a, b, c_)]
    q = [_silu(qs[h][c]) for h, c in lanes]
    k = [_silu(ks[h][c]) for h, c in lanes]
    v = [_silu(vs[h][c]) for h, c in lanes]
    q = [t * lax.rsqrt(jnp.sum(t * t, axis=-1, keepdims=True) + EPS) * (128.0 ** -0.5) for t in q]
    k = [t * lax.rsqrt(jnp.sum(t * t, axis=-1, keepdims=True) + EPS) for t in k]
    beta = [jax.nn.sigmoid(_lane_pick(gs[c], LANE_GB + h)) for h, c in lanes]
    rate = [-jnp.exp(_lane_pick(alog_row, LANE_GA + h)) for h in range(GDN_H)]
    bias = [_lane_pick(dtb_row, LANE_GA + h) for h in range(GDN_H)]
    g = [rate[h] * _softplus(_lane_pick(gs[c], LANE_GA + h) + bias[h]) for h, c in lanes]
    cb = [jnp.broadcast_to(t, (n, n)) for t in g]
    gc_col = [_cumsum_col(t) for t in cb]
    gc_row = [_cumsum_row(t) for t in cb]
    tri_incl, tri_strict = _tri(n), _tri(n, strict=True)
    decay = [jnp.exp(jnp.where(tri_incl, a - b, NEG)) for a, b in zip(gc_col, gc_row)]
    gc = [t[:, 0:1] for t in gc_col]
    g_last = [jnp.sum(t, axis=0, keepdims=True) for t in g]
    kb = [a * b for a, b in zip(k, beta)]
    m = z3(lambda a, b, d: jnp.where(tri_strict, _bdot(a, b, NT) * d, 0.0), kb, k, decay)
    t_inv = _unit_lower_inv(tuple(m))
    egc = [jnp.exp(t) for t in gc]
    u = z3(lambda t, a, b: _bdot(t, a * b), t_inv, v, beta)
    w = z3(lambda t, a, e: _bdot(t, a * e), t_inv, kb, egc)
    attn = z3(lambda a, b, d: jnp.where(tri_incl, _bdot(a, b, NT) * d, 0.0), q, k, decay)
    qd = [a * e for a, e in zip(q, egc)]
    kd = z3(lambda a, gl, c_: a * jnp.exp(gl - c_), k, g_last, gc)
    egl = [jnp.exp(t) for t in g_last]
    zg = [_silu(zs[h][c]) for h, c in lanes]
    S = list(S)
    ys = [[None] * GDN_SUB for _ in range(GDN_H)]
    for c in range(GDN_SUB):
        ii = [h * GDN_SUB + c for h in range(GDN_H)]
        v_new = [u[i] - _bdot(w[i], S[h]) for h, i in enumerate(ii)]
        o = [_bdot(qd[i], S[h]) + _bdot(attn[i], v_new[h]) for h, i in enumerate(ii)]
        S = [S[h] * egl[i] + _bdot(kd[i], v_new[h], TN) for h, i in enumerate(ii)]
        for h, i in enumerate(ii):
            ys[h][c] = _rms(o[h], nw) * zg[i]
    return tuple(tuple(y) for y in ys), tuple(S)


def _gdn_step_args(qkv_ref, z_ref, g_ref):
    n = GDN_CHUNK
    rows = [slice(n * c, n * c + n) for c in range(GDN_SUB)]
    col = lambda ref, o: tuple(tuple(ref[r, o + 128 * h:o + 128 * h + 128] for r in rows) for h in range(GDN_H))
    return rows, (col(qkv_ref, 0), col(qkv_ref, 512), col(qkv_ref, 1024), col(z_ref, 0), tuple(g_ref[r, :] for r in rows))


def _carry(cargo, refs, step, nsteps):
    if cargo is None:
        return lambda: None
    start, mid, end = cargo.phases(*refs)
    pl.when(step == 0)(start)
    pl.when(step == (3 * nsteps) // 4)(mid)
    return lambda: pl.when(step == nsteps - 1)(end)


def _cargo_io(cargo):
    if cargo is None:
        return [], [], [], []
    hbm = pl.BlockSpec(memory_space=pl.ANY)
    return list(cargo.arrays), [hbm] * cargo.n_in, list(cargo.out_shapes), [hbm] * cargo.n_out


def _gdn_fwd(qkvc, proj, alog_row, dtb_row, nw, name, cargo=None):
    L = qkvc.shape[0]
    n = GDN_CHUNK * GDN_SUB
    Nc = L // n
    c_arr, c_ispec, c_oshape, c_ospec = _cargo_io(cargo)
    ni, no = len(c_arr), len(c_oshape)

    def body(*refs):
        qkv_ref, z_ref, g_ref, al_ref, db_ref, nw_ref = refs[:6]
        y_ref, sin_ref = refs[6 + ni:8 + ni]
        S = refs[8 + ni + no]
        step = pl.program_id(0)
        drain = _carry(cargo, (refs[6:6 + ni], refs[8 + ni:8 + ni + no], refs[9 + ni + no:]), step, Nc)

        @pl.when(step == 0)
        def _():
            S[...] = jnp.zeros_like(S)

        rows, args = _gdn_step_args(qkv_ref, z_ref, g_ref)
        states = tuple(S[h] for h in range(GDN_H))
        ys, sn = _gdn_step_fn(*args, states, al_ref[...], db_ref[...], nw_ref[...])
        for h in range(GDN_H):
            sin_ref[0, h] = states[h]
            for c in range(GDN_SUB):
                y_ref[rows[c], 128 * h:128 * h + 128] = ys[h][c].astype(BF16)
            S[h] = sn[h]
        drain()

    return pl.pallas_call(
        body, name=name, grid=(Nc,),
        in_specs=[pl.BlockSpec((n, 1536), lambda i: (i, 0)), pl.BlockSpec((n, 512), lambda i: (i, P_GZ // 512)),
                  pl.BlockSpec((n, 128), lambda i: (i, P_GATE // 128)), _vec_spec(128), _vec_spec(128), _vec_spec(128)]
        + c_ispec,
        out_specs=tuple([pl.BlockSpec((n, 512), lambda i: (i, 0)), pl.BlockSpec((1, 4, 128, 128), lambda i: (i, 0, 0, 0))]
                        + c_ospec),
        out_shape=tuple([jax.ShapeDtypeStruct((L, 512), BF16), jax.ShapeDtypeStruct((Nc, 4, 128, 128), F32)] + c_oshape),
        scratch_shapes=[pltpu.VMEM((4, 128, 128), F32)] + (cargo.scratch if cargo else []),
        compiler_params=_cparams(("arbitrary",), VMEM_BIG))(qkvc, proj, proj, alog_row, dtb_row, nw, *c_arr)


def _gdn_bwd(qkvc, proj, alog_row, dtb_row, nw, s_in, dycat, name, cargo=None):
    L = qkvc.shape[0]
    n = GDN_CHUNK * GDN_SUB
    Nc = L // n
    c_arr, c_ispec, c_oshape, c_ospec = _cargo_io(cargo)
    ni, no = len(c_arr), len(c_oshape)

    def body(*refs):
        qkv_ref, z_ref, g_ref, al_ref, db_ref, nw_ref, sin_ref, dy_ref = refs[:8]
        dqkv_ref, dz_ref, dg_ref, dal_ref, ddb_ref, dnw_ref = refs[8 + ni:14 + ni]
        dS = refs[14 + ni + no]
        step = pl.program_id(0)
        drain = _carry(cargo, (refs[8:8 + ni], refs[14 + ni:14 + ni + no], refs[15 + ni + no:]), step, Nc)

        @pl.when(step == 0)
        def _():
            dS[...] = jnp.zeros_like(dS)
            dal_ref[...] = jnp.zeros_like(dal_ref)
            ddb_ref[...] = jnp.zeros_like(ddb_ref)
            dnw_ref[...] = jnp.zeros_like(dnw_ref)

        rows, args = _gdn_step_args(qkv_ref, z_ref, g_ref)
        s_in = tuple(sin_ref[0, h] for h in range(GDN_H))
        _, vjp = jax.vjp(_gdn_step_fn, *args, s_in, al_ref[...], db_ref[...], nw_ref[...])
        dys = tuple(tuple(dy_ref[r, 128 * h:128 * h + 128] for r in rows) for h in range(GDN_H))
        dq, dk, dv, dz, dgt, ds, dal, ddb, dnw = vjp((dys, tuple(dS[h] for h in range(GDN_H))))
        for h in range(GDN_H):
            for c in range(GDN_SUB):
                dqkv_ref[rows[c], 128 * h:128 * h + 128] = dq[h][c]
                dqkv_ref[rows[c], 512 + 128 * h:640 + 128 * h] = dk[h][c]
                dqkv_ref[rows[c], 1024 + 128 * h:1152 + 128 * h] = dv[h][c]
                dz_ref[rows[c], 128 * h:128 * h + 128] = dz[h][c].astype(BF16)
            dS[h] = ds[h]
        for c in range(GDN_SUB):
            dg_ref[rows[c], :] = dgt[c]
        dal_ref[...] += dal
        ddb_ref[...] += ddb
        dnw_ref[...] += dnw
        drain()

    rev = lambda i: Nc - 1 - i
    vec = jax.ShapeDtypeStruct((1, 128), F32)
    return pl.pallas_call(
        body, name=name, grid=(Nc,),
        in_specs=[pl.BlockSpec((n, 1536), lambda i: (rev(i), 0)), pl.BlockSpec((n, 512), lambda i: (rev(i), P_GZ // 512)),
                  pl.BlockSpec((n, 128), lambda i: (rev(i), P_GATE // 128)), _vec_spec(128), _vec_spec(128), _vec_spec(128),
                  pl.BlockSpec((1, 4, 128, 128), lambda i: (rev(i), 0, 0, 0)), pl.BlockSpec((n, 512), lambda i: (rev(i), 0))]
        + c_ispec,
        out_specs=tuple([pl.BlockSpec((n, 1536), lambda i: (rev(i), 0)), pl.BlockSpec((n, 512), lambda i: (rev(i), 0)),
                         pl.BlockSpec((n, 128), lambda i: (rev(i), 0)), _vec_spec(128), _vec_spec(128), _vec_spec(128)]
                        + c_ospec),
        out_shape=tuple([jax.ShapeDtypeStruct((L, 1536), F32), jax.ShapeDtypeStruct((L, 512), BF16),
                         jax.ShapeDtypeStruct((L, 128), F32), vec, vec, vec] + c_oshape),
        scratch_shapes=[pltpu.VMEM((4, 128, 128), F32)] + (cargo.scratch if cargo else []),
        compiler_params=_cparams(("arbitrary",), VMEM_BIG))(qkvc, proj, proj, alog_row, dtb_row, nw, s_in, dycat, *c_arr)


def _ssd_chunk_fn(xs, bs, cs, zs, gates, st, alog_row, dtb_row, d_row, nws):
    n = SSM_CHUNK
    tri = _tri(n)
    lane = lax.broadcasted_iota(jnp.int32, (1, 128), 1)
    lo = lane < 64
    xs = [_silu(t) for t in xs]
    bs = [_silu(t) for t in bs]
    cs = [_silu(t) for t in cs]
    cb = [_bdot(cs[g], bs[g], NT) for g in range(2)]
    ys, st_new = [], []
    for p in range(4):
        g = p // 2
        ydiag = jnp.zeros((n, 128), F32)
        cs_lane = jnp.zeros((n, 128), F32)
        dt_lane = jnp.zeros((n, 128), F32)
        tot_lane = jnp.zeros((1, 128), F32)
        dsk_lane = jnp.zeros((1, 128), F32)
        for half in range(2):
            h = 2 * p + half
            hm = lo if half == 0 else jnp.logical_not(lo)
            dt = _softplus(_lane_pick(gates, LANE_SDT + h) + _lane_pick(dtb_row, LANE_SDT + h))
            da = dt * (-jnp.exp(_lane_pick(alog_row, LANE_SDT + h)))
            cs_col, cs_row = _cumsum_mats(da, n)
            lmat = jnp.exp(jnp.where(tri, cs_col - cs_row, NEG))
            xdt_h = jnp.where(hm, xs[p] * dt, 0.0)
            ydiag = ydiag + _bdot(cb[g] * lmat, xdt_h)
            cs_lane = jnp.where(hm, cs_col, cs_lane)
            dt_lane = jnp.where(hm, dt, dt_lane)
            tot_lane = jnp.where(hm, jnp.sum(da, axis=0, keepdims=True), tot_lane)
            dsk_lane = jnp.where(hm, _lane_pick(d_row, LANE_SDT + h), dsk_lane)
        xdt = xs[p] * dt_lane
        st_new.append(st[p] * jnp.exp(tot_lane) + _bdot(bs[g], xdt * jnp.exp(tot_lane - cs_lane), TN))
        yoff = _bdot(cs[g], st[p]) * jnp.exp(cs_lane)
        ys.append((ydiag + yoff + xs[p] * dsk_lane) * _silu(zs[p]))
    out = []
    for g in range(2):
        ms = (jnp.sum(ys[2 * g] ** 2, axis=-1, keepdims=True) + jnp.sum(ys[2 * g + 1] ** 2, axis=-1, keepdims=True)) / 256.0
        rstd = lax.rsqrt(ms + EPS)
        out += [ys[2 * g] * rstd * nws[2 * g], ys[2 * g + 1] * rstd * nws[2 * g + 1]]
    return tuple(out), tuple(st_new)


def _ssd_args(xbc_ref, z_ref, nw_ref):
    xs = [xbc_ref[:, 128 * p:128 * p + 128] for p in range(4)]
    bs = [xbc_ref[:, 512 + 128 * g:640 + 128 * g] for g in range(2)]
    cs = [xbc_ref[:, 768 + 128 * g:896 + 128 * g] for g in range(2)]
    zs = [z_ref[:, 128 * p:128 * p + 128] for p in range(4)]
    nws = [nw_ref[:, 128 * p:128 * p + 128] for p in range(4)]
    return xs, bs, cs, zs, nws


def _ssd_fwd(xbcc, proj, alog_row, dtb_row, d_row, nw, name):
    L = xbcc.shape[0]
    n = SSM_CHUNK
    Nc = L // n

    def body(xbc_ref, z_ref, g_ref, al_ref, db_ref, d_ref, nw_ref, y_ref, sin_ref, S):
        @pl.when(pl.program_id(0) == 0)
        def _():
            S[...] = jnp.zeros_like(S)

        xs, bs, cs, zs, nws = _ssd_args(xbc_ref, z_ref, nw_ref)
        st = [S[p] for p in range(4)]
        sin_ref[0] = S[...]
        ys, sn = _ssd_chunk_fn(xs, bs, cs, zs, g_ref[...], st, al_ref[...], db_ref[...], d_ref[...], nws)
        for p in range(4):
            y_ref[:, 128 * p:128 * p + 128] = ys[p].astype(BF16)
            S[p] = sn[p]

    return pl.pallas_call(
        body, name=name, grid=(Nc,),
        in_specs=[pl.BlockSpec((n, 1024), lambda i: (i, 0)), pl.BlockSpec((n, 512), lambda i: (i, P_SZ // 512)),
                  pl.BlockSpec((n, 128), lambda i: (i, P_GATE // 128)), _vec_spec(128), _vec_spec(128), _vec_spec(128),
                  _vec_spec(512)],
        out_specs=(pl.BlockSpec((n, 512), lambda i: (i, 0)), pl.BlockSpec((1, 4, 128, 128), lambda i: (i, 0, 0, 0))),
        out_shape=(jax.ShapeDtypeStruct((L, 512), BF16), jax.ShapeDtypeStruct((Nc, 4, 128, 128), F32)),
        scratch_shapes=[pltpu.VMEM((4, 128, 128), F32)],
        compiler_params=_cparams(("arbitrary",), VMEM_BIG))(xbcc, proj, proj, alog_row, dtb_row, d_row, nw)


def _ssd_bwd(xbcc, proj, alog_row, dtb_row, d_row, nw, s_in, dycat, name):
    L = xbcc.shape[0]
    n = SSM_CHUNK
    Nc = L // n

    def body(xbc_ref, z_ref, g_ref, al_ref, db_ref, d_ref, nw_ref, sin_ref, dy_ref,
             dxbc_ref, dz_ref, dg_ref, dal_ref, ddb_ref, dd_ref, dnw_ref, dS):
        @pl.when(pl.program_id(0) == 0)
        def _():
            dS[...] = jnp.zeros_like(dS)
            dal_ref[...] = jnp.zeros_like(dal_ref)
            ddb_ref[...] = jnp.zeros_like(ddb_ref)
            dd_ref[...] = jnp.zeros_like(dd_ref)
            dnw_ref[...] = jnp.zeros_like(dnw_ref)

        xs, bs, cs, zs, nws = _ssd_args(xbc_ref, z_ref, nw_ref)
        st = [sin_ref[0, p] for p in range(4)]
        _, vjp = jax.vjp(_ssd_chunk_fn, xs, bs, cs, zs, g_ref[...], st, al_ref[...], db_ref[...], d_ref[...], nws)
        dys = tuple(dy_ref[:, 128 * p:128 * p + 128] for p in range(4))
        dxs, dbs, dcs, dzs, dgt, dst, dal, ddb, dd, dnws = vjp((dys, tuple(dS[p] for p in range(4))))
        for p in range(4):
            dxbc_ref[:, 128 * p:128 * p + 128] = dxs[p]
            dz_ref[:, 128 * p:128 * p + 128] = dzs[p].astype(BF16)
            dS[p] = dst[p]
            dnw_ref[:, 128 * p:128 * p + 128] += dnws[p]
        for g in range(2):
            dxbc_ref[:, 512 + 128 * g:640 + 128 * g] = dbs[g]
            dxbc_ref[:, 768 + 128 * g:896 + 128 * g] = dcs[g]
        dg_ref[...] = dgt
        dal_ref[...] += dal
        ddb_ref[...] += ddb
        dd_ref[...] += dd

    rev = lambda i: Nc - 1 - i
    vec = jax.ShapeDtypeStruct((1, 128), F32)
    return pl.pallas_call(
        body, name=name, grid=(Nc,),
        in_specs=[pl.BlockSpec((n, 1024), lambda i: (rev(i), 0)), pl.BlockSpec((n, 512), lambda i: (rev(i), P_SZ // 512)),
                  pl.BlockSpec((n, 128), lambda i: (rev(i), P_GATE // 128)), _vec_spec(128), _vec_spec(128), _vec_spec(128),
                  _vec_spec(512), pl.BlockSpec((1, 4, 128, 128), lambda i: (rev(i), 0, 0, 0)),
                  pl.BlockSpec((n, 512), lambda i: (rev(i), 1))],
        out_specs=(pl.BlockSpec((n, 1024), lambda i: (rev(i), 0)), pl.BlockSpec((n, 512), lambda i: (rev(i), 0)),
                   pl.BlockSpec((n, 128), lambda i: (rev(i), 0)), _vec_spec(128), _vec_spec(128), _vec_spec(128),
                   _vec_spec(512)),
        out_shape=(jax.ShapeDtypeStruct((L, 1024), F32), jax.ShapeDtypeStruct((L, 512), BF16),
                   jax.ShapeDtypeStruct((L, 128), F32), vec, vec, vec, jax.ShapeDtypeStruct((1, 512), F32)),
        scratch_shapes=[pltpu.VMEM((4, 128, 128), F32)],
        compiler_params=_cparams(("arbitrary",), VMEM_BIG))(xbcc, proj, proj, alog_row, dtb_row, d_row, nw, s_in, dycat)


def _rot(x):
    W = x.shape[1]
    lane = lax.broadcasted_iota(jnp.int32, (1, W), 1)
    first = (lane % 64) < 32
    return jnp.where(first, -pltpu.roll(x, W - 32, 1), pltpu.roll(x, 32, 1))


def _rope(x, cos, sin):
    return x * cos + _rot(x) * sin


def _rope_t(d, cos, sin):
    return d * cos - _rot(d * sin)


def _swa_core_fn(qs, ks, vs, sink_row, mask):
    lane = lax.broadcasted_iota(jnp.int32, (1, 128), 1)
    lo = lane < 64
    outs = []
    for p in range(4):
        g = p // 2
        halves = []
        for half in range(2):
            hm = lo if half == 0 else jnp.logical_not(lo)
            s = _bdot(jnp.where(hm, qs[p], 0.0), ks[g], NT) * 0.125
            s = jnp.where(mask, s, NEG)
            sk = _lane_pick(sink_row, 2 * p + half)
            mx = lax.stop_gradient(jnp.maximum(jnp.max(s, axis=-1, keepdims=True), sk))
            pr = jnp.exp(s - mx)
            pr = pr / (jnp.sum(pr, axis=-1, keepdims=True) + jnp.exp(sk - mx))
            halves.append(_bdot(pr, vs[g]))
        outs.append(jnp.where(lo, halves[0], halves[1]))
    return tuple(outs)


def _swa_prepare(blk, q_ref, kc_ref, kp_ref, vc_ref, vp_ref, cc_ref, sc_ref, cp_ref, sp_ref):
    W = WINDOW
    cos_c, sin_c = cc_ref[...], sc_ref[...]
    cos_b = jnp.concatenate([cp_ref[...], cos_c], axis=0)
    sin_b = jnp.concatenate([sp_ref[...], sin_c], axis=0)
    cos_q, sin_q = jnp.concatenate([cos_c] * 4, axis=1), jnp.concatenate([sin_c] * 4, axis=1)
    cos_k, sin_k = jnp.concatenate([cos_b] * 2, axis=1), jnp.concatenate([sin_b] * 2, axis=1)
    qr = _rope(q_ref[...], cos_q, sin_q)
    kr = _rope(jnp.concatenate([kp_ref[...], kc_ref[...]], axis=0), cos_k, sin_k)
    vb = jnp.concatenate([vp_ref[...], vc_ref[...]], axis=0)
    qi = lax.broadcasted_iota(jnp.int32, (W, 2 * W), 0)
    kj = lax.broadcasted_iota(jnp.int32, (W, 2 * W), 1)
    rel = qi + W - kj
    mask = (rel >= 0) & (rel < W) & ((blk > 0) | (kj >= W))
    qs = [qr[:, 128 * p:128 * p + 128] for p in range(4)]
    ks = [kr[:, 128 * g:128 * g + 128] for g in range(2)]
    vs = [vb[:, 128 * g:128 * g + 128] for g in range(2)]
    return qs, ks, vs, mask, (cos_q, sin_q, cos_k, sin_k)


def _swa_specs(nb, order):
    W = WINDOW
    cur = lambda i: order(i)
    prv = lambda i: jnp.maximum(order(i) - 1, 0)
    return [pl.BlockSpec((W, 512), lambda i: (cur(i), P_AQ // 512)),
            pl.BlockSpec((W, 256), lambda i: (cur(i), P_AK // 256)), pl.BlockSpec((W, 256), lambda i: (prv(i), P_AK // 256)),
            pl.BlockSpec((W, 256), lambda i: (cur(i), P_AV // 256)), pl.BlockSpec((W, 256), lambda i: (prv(i), P_AV // 256)),
            pl.BlockSpec((W, 128), lambda i: (cur(i), 0)), pl.BlockSpec((W, 128), lambda i: (cur(i), 0)),
            pl.BlockSpec((W, 128), lambda i: (prv(i), 0)), pl.BlockSpec((W, 128), lambda i: (prv(i), 0)),
            _vec_spec(128)]


def _swa_fwd(proj, cos, sin, sink_row, name):
    L = proj.shape[0]
    W = WINDOW
    nb = L // W

    def body(q_ref, kc_ref, kp_ref, vc_ref, vp_ref, cc_ref, sc_ref, cp_ref, sp_ref, sk_ref, y_ref):
        blk = pl.program_id(0)
        qs, ks, vs, mask, _ = _swa_prepare(blk, q_ref, kc_ref, kp_ref, vc_ref, vp_ref, cc_ref, sc_ref, cp_ref, sp_ref)
        outs = _swa_core_fn(qs, ks, vs, sk_ref[...], mask)
        for p in range(4):
            y_ref[:, 128 * p:128 * p + 128] = outs[p].astype(BF16)

    return pl.pallas_call(
        body, name=name, grid=(nb,), in_specs=_swa_specs(nb, lambda i: i),
        out_specs=pl.BlockSpec((W, 512), lambda i: (i, 0)), out_shape=jax.ShapeDtypeStruct((L, 512), BF16),
        compiler_params=_cparams(("parallel",), VMEM_BIG))(proj, proj, proj, proj, proj, cos, sin, cos, sin, sink_row)


def _swa_bwd(proj, cos, sin, sink_row, dycat, name):
    L = proj.shape[0]
    W = WINDOW
    nb = L // W
    rev = lambda i: nb - 1 - i

    def body(q_ref, kc_ref, kp_ref, vc_ref, vp_ref, cc_ref, sc_ref, cp_ref, sp_ref, sk_ref, dy_ref,
             dq_ref, dk_ref, dv_ref, dsk_ref, ck, cv):
        i = pl.program_id(0)
        blk = nb - 1 - i

        @pl.when(i == 0)
        def _():
            ck[...] = jnp.zeros_like(ck)
            cv[...] = jnp.zeros_like(cv)
            dsk_ref[...] = jnp.zeros_like(dsk_ref)

        qs, ks, vs, mask, (cos_q, sin_q, cos_k, sin_k) = _swa_prepare(
            blk, q_ref, kc_ref, kp_ref, vc_ref, vp_ref, cc_ref, sc_ref, cp_ref, sp_ref)
        _, vjp = jax.vjp(functools.partial(_swa_core_fn, mask=mask), qs, ks, vs, sk_ref[...])
        dqs, dks, dvs, dsk = vjp(tuple(dy_ref[:, 128 * p:128 * p + 128] for p in range(4)))
        dq_ref[...] = _rope_t(jnp.concatenate(dqs, axis=1), cos_q, sin_q).astype(BF16)
        dkb = _rope_t(jnp.concatenate(dks, axis=1), cos_k, sin_k)
        dvb = jnp.concatenate(dvs, axis=1)
        dk_ref[...] = (dkb[W:2 * W] + ck[...]).astype(BF16)
        dv_ref[...] = (dvb[W:2 * W] + cv[...]).astype(BF16)
        ck[...] = dkb[0:W]
        cv[...] = dvb[0:W]
        dsk_ref[...] += dsk

    return pl.pallas_call(
        body, name=name, grid=(nb,),
        in_specs=_swa_specs(nb, rev) + [pl.BlockSpec((W, 512), lambda i: (rev(i), 2))],
        out_specs=(pl.BlockSpec((W, 512), lambda i: (rev(i), 0)), pl.BlockSpec((W, 256), lambda i: (rev(i), 0)),
                   pl.BlockSpec((W, 256), lambda i: (rev(i), 0)), _vec_spec(128)),
        out_shape=(jax.ShapeDtypeStruct((L, 512), BF16), jax.ShapeDtypeStruct((L, 256), BF16),
                   jax.ShapeDtypeStruct((L, 256), BF16), jax.ShapeDtypeStruct((1, 128), F32)),
        scratch_shapes=[pltpu.VMEM((W, 256), F32), pltpu.VMEM((W, 256), F32)],
        compiler_params=_cparams(("arbitrary",), VMEM_BIG))(
            proj, proj, proj, proj, proj, cos, sin, cos, sin, sink_row, dycat)


def _adamw(w, g, m, v):
    m = ADAM_B1 * m + (1.0 - ADAM_B1) * g
    v = ADAM_B2 * v + (1.0 - ADAM_B2) * (g * g)
    m_hat = m / (1.0 - ADAM_B1 ** ADAM_STEP)
    v_hat = v / (1.0 - ADAM_B2 ** ADAM_STEP)
    delta = -ADAM_LR * (m_hat / (jnp.sqrt(v_hat) + ADAM_EPS) + ADAM_WD * w)
    return delta, m, v


def _ada_fwd(c_all, ada_w, ada_b_sh, name):
    S = ada_w.shape[2]

    def body(c_ref, w_ref, b_ref, o_ref):
        o_ref[0] = _bdot(_silu(c_ref[...]), w_ref[0]) + b_ref[0]

    return pl.pallas_call(
        body, name=name, grid=(2,),
        in_specs=[pl.BlockSpec((N_DEV, D), lambda i: (0, 0)), pl.BlockSpec((1, D, S), lambda i: (i, 0, 0)),
                  pl.BlockSpec((1, 1, S), lambda i: (i, 0, 0))],
        out_specs=pl.BlockSpec((1, N_DEV, S), lambda i: (i, 0, 0)), out_shape=jax.ShapeDtypeStruct((2, N_DEV, S), F32),
        compiler_params=_cparams(("parallel",), VMEM_BIG))(c_all, ada_w, ada_b_sh.reshape(2, 1, S))


def _ada_bwd_adamw(c_all, dmod_sh, w, m, v, name):
    S = w.shape[2]
    T = 256

    def body(c_ref, d_ref, w_ref, m_ref, v_ref, g_ref, dl_ref, nm_ref, nv_ref):
        g = _bdot(_silu(c_ref[...]), d_ref[0], TN)
        dl, nm, nv = _adamw(w_ref[0], g, m_ref[0], v_ref[0])
        g_ref[0], dl_ref[0], nm_ref[0], nv_ref[0] = g, dl, nm, nv

    blk = pl.BlockSpec((1, T, S), lambda i, j: (i, j, 0))
    sds = jax.ShapeDtypeStruct(w.shape, F32)
    return pl.pallas_call(
        body, name=name, grid=(2, D // T),
        in_specs=[pl.BlockSpec((N_DEV, T), lambda i, j: (0, j)), pl.BlockSpec((1, N_DEV, S), lambda i, j: (i, 0, 0)),
                  blk, blk, blk],
        out_specs=(blk, blk, blk, blk), out_shape=(sds, sds, sds, sds),
        compiler_params=_cparams(("parallel", "parallel"), VMEM_BIG))(c_all, dmod_sh, w, m, v)


def _sum8(r, name):
    _, R, W = r.shape
    T = R
    for cand in (2496, 2048, 1024, 512, 256, 128, 64, 32, 16, 8):
        if R % cand == 0:
            T = cand
            break

    def body(r_ref, o_ref):
        acc = r_ref[0].astype(F32)
        for d in range(1, N_DEV):
            acc = acc + r_ref[d].astype(F32)
        o_ref[...] = acc

    return pl.pallas_call(
        body, name=name, grid=(R // T,), in_specs=[pl.BlockSpec((N_DEV, T, W), lambda i: (0, i, 0))],
        out_specs=pl.BlockSpec((T, W), lambda i: (i, 0)), out_shape=jax.ShapeDtypeStruct((R, W), F32),
        compiler_params=_cparams(("parallel",), VMEM_BIG))(r)


def _adamw_call(w, g, m, v, name):
    R, W = w.shape
    T = R
    for cand in (1024, 512, 256, 128, 64, 32, 16, 8):
        if R % cand == 0 and R > cand and cand * W * 4 <= 2 * 1024 * 1024:
            T = cand
            break

    def body(w_ref, g_ref, m_ref, v_ref, dl_ref, nm_ref, nv_ref):
        dl_ref[...], nm_ref[...], nv_ref[...] = _adamw(w_ref[...], g_ref[...], m_ref[...], v_ref[...])

    blk = pl.BlockSpec((T, W), lambda i: (i, 0))
    sds = jax.ShapeDtypeStruct((R, W), F32)
    return pl.pallas_call(
        body, name=name, grid=(R // T,), in_specs=[blk, blk, blk, blk], out_specs=(blk, blk, blk),
        out_shape=(sds, sds, sds), compiler_params=_cparams(("parallel",), VMEM_BIG))(w, g, m, v)


def _sum8_adamw(r0, r1, w, m, v, name):
    _, _, R, W = r0.shape
    T = (256 * 1024) // W
    assert R % T == 0, (R, W, T)
    nj = R // T

    def body(r0_ref, r1_ref, w_ref, m_ref, v_ref, g_ref, dl_ref, nm_ref, nv_ref):
        def run(r_ref):
            g = r_ref[0].astype(F32)
            for d in range(1, N_DEV):
                g = g + r_ref[d].astype(F32)
            g_ref[...] = g
            dl_ref[...], nm_ref[...], nv_ref[...] = _adamw(w_ref[...], g, m_ref[...], v_ref[...])

        pl.when(pl.program_id(0) == 0)(lambda: run(r0_ref))
        pl.when(pl.program_id(0) == 1)(lambda: run(r1_ref))

    r0_spec = pl.BlockSpec((N_DEV, None, T, W), lambda i, j: (0, 0, jnp.where(i == 0, j, nj - 1), 0))
    r1_spec = pl.BlockSpec((N_DEV, None, T, W), lambda i, j: (0, 0, jnp.where(i == 1, j, 0), 0))
    blk = pl.BlockSpec((None, T, W), lambda i, j: (i, j, 0))
    sds = jax.ShapeDtypeStruct((2, R, W), F32)
    return pl.pallas_call(
        body, name=name, grid=(2, nj), in_specs=[r0_spec, r1_spec, blk, blk, blk],
        out_specs=(blk, blk, blk, blk), out_shape=(sds, sds, sds, sds),
        compiler_params=_cparams(("arbitrary", "arbitrary"), VMEM_BIG))(r0, r1, w, m, v)


def _permute_rows(wt):
    z = lambda n: jnp.zeros((n, wt.shape[1]), wt.dtype)
    s = lambda o, n: wt[o:o + n]
    kd = [s(O_AK, 64), s(O_AK, 64), s(O_AK + 64, 64), s(O_AK + 64, 64)]
    vd = [s(O_AV, 64), s(O_AV, 64), s(O_AV + 64, 64), s(O_AV + 64, 64)]
    return jnp.concatenate(
        [s(O_GQ, 1536), s(O_GZ, 512), s(O_SXBC, 1024), s(O_SZ, 512), s(O_AQ, 512), s(O_CB, 1536)] + kd + vd
        + [s(O_GB, 4), s(O_GA, 4), s(O_SDT, 8), z(112)], axis=0)


def _unpermute_rows(g):
    s = lambda o, n: g[o:o + n]
    add = lambda o: (s(o, 64).astype(F32) + s(o + 64, 64).astype(F32)).astype(g.dtype)
    dk = [add(P_AK), add(P_AK + 128)]
    dv = [add(P_AV), add(P_AV + 128)]
    return jnp.concatenate(
        [s(P_QKV, 1536), s(P_GZ, 512), s(P_GATE, 8), s(P_SZ, 512), s(P_SXBC, 1024), s(P_GATE + 8, 8), s(P_AQ, 512)]
        + dk + dv + [s(P_CB, 1536)], axis=0)


def _row128(vals, lane0):
    n = vals.shape[0]
    return jnp.concatenate([jnp.zeros((lane0,), F32), vals, jnp.zeros((128 - lane0 - n,), F32)]).reshape(1, 128)


def _pad_rows128(flat):
    n = flat.shape[0]
    pad = (-n) % 1024
    return jnp.concatenate([flat, jnp.zeros((pad,), flat.dtype)]).reshape(-1, 128)


def kernel(x, c, positions, ada_w, ada_b, norm_pre_mix, norm_post_mix, norm_pre_mlp, norm_post_mlp, w_in, w_out, gdn_conv_w, gdn_a_log, gdn_dt_bias, gdn_norm_w, ssm_conv_w, ssm_conv_b, ssm_a_log, ssm_dt_bias, ssm_d, ssm_norm_w, attn_sinks, sc_conv_w, w_up, w_down, loss_target, m_ada_w, m_ada_b, m_norm_pre_mix, m_norm_post_mix, m_norm_pre_mlp, m_norm_post_mlp, m_w_in, m_w_out, m_gdn_conv_w, m_gdn_a_log, m_gdn_dt_bias, m_gdn_norm_w, m_ssm_conv_w, m_ssm_conv_b, m_ssm_a_log, m_ssm_dt_bias, m_ssm_d, m_ssm_norm_w, m_attn_sinks, m_sc_conv_w, m_w_up, m_w_down, v_ada_w, v_ada_b, v_norm_pre_mix, v_norm_post_mix, v_norm_pre_mlp, v_norm_post_mlp, v_w_in, v_w_out, v_gdn_conv_w, v_gdn_a_log, v_gdn_dt_bias, v_gdn_norm_w, v_ssm_conv_w, v_ssm_conv_b, v_ssm_a_log, v_ssm_dt_bias, v_ssm_d, v_ssm_norm_w, v_attn_sinks, v_sc_conv_w, v_w_up, v_w_down):
    L = x.shape[1]
    me = 4 * lax.axis_index("x") + 2 * lax.axis_index("y") + lax.axis_index("c")
    x0 = x[0]
    target = loss_target[0]
    S_ADA = ada_w.shape[2]

    small = jnp.concatenate([c.reshape(-1), gdn_conv_w.reshape(-1), ssm_conv_w.reshape(-1), sc_conv_w.reshape(-1)])
    n_small = small.shape[0]
    g_small = _ag_small(_pad_rows128(small), "ag_c_conv").reshape(N_DEV, -1)[:, :n_small]
    c_all = g_small[:, :D]
    o = D
    gdn_cw = g_small[:, o:o + 2 * 4 * 192].reshape(N_DEV, 2, 4, 192).transpose(1, 2, 0, 3).reshape(2, 4, 1536)
    o += 2 * 4 * 192
    ssm_cw = g_small[:, o:o + 2 * 4 * 128].reshape(N_DEV, 2, 4, 128).transpose(1, 2, 0, 3).reshape(2, 4, 1024)
    o += 2 * 4 * 128
    sc_cw = g_small[:, o:o + 2 * 3 * 64].reshape(N_DEV, 2, 3, 64).transpose(1, 2, 0, 3).reshape(2, 3, 512)

    ada_b_sh = lax.dynamic_slice(ada_b, (0, me * S_ADA), (2, S_ADA))
    mod_sh = _ada_fwd(c_all, ada_w, ada_b_sh, "ada_fwd")
    mod_all = _ag_small(mod_sh.reshape(-1, 128), "ag_mod").reshape(N_DEV, 2, N_DEV, S_ADA)
    mod_me = lax.dynamic_index_in_dim(mod_all, me, axis=2, keepdims=False)
    mod_me = mod_me.transpose(1, 0, 2).reshape(2, 6 * D)

    wt_sh = [jnp.pad(w_in[i].T.astype(BF16), ((0, SHARD_PAD - SHARD_IN), (0, 0))) for i in range(2)]
    wo_sh, wu_sh, wd_sh = ([w[i].astype(BF16) for i in range(2)] for w in (w_out, w_up, w_down))
    full_w_in = lambda g: _permute_rows(g[:, :SHARD_IN].reshape(IN_W, D))
    win_t = [full_w_in(_ag_multi([wt_sh[0]], "ag_w_in0")[0]), None]
    wout_f, wup_g, wdown_f = [None, None], [None, None], [None, None]

    inv_freq = ROPE_THETA ** (-jnp.arange(0, 64, 2, dtype=F32) / 64)
    ang = positions[0].astype(F32)[:, None] * inv_freq
    cos = jnp.tile(jnp.cos(ang), (1, 4))
    sin = jnp.tile(jnp.sin(ang), (1, 4))

    vec = lambda a: a.reshape(1, -1)

    saved = []
    xc = x0
    for i in range(2):
        sh_a, sc_a, gt_a, sh_m, sc_m, gt_m = [vec(t) for t in jnp.split(mod_me[i], 6)]
        gal, gdb = _row128(gdn_a_log[i], LANE_GA), _row128(gdn_dt_bias[i], LANE_GA)
        sal, sdb, sdd = (_row128(ssm_a_log[i], LANE_SDT), _row128(ssm_dt_bias[i], LANE_SDT), _row128(ssm_d[i], LANE_SDT))
        gnw, snw, skr = vec(gdn_norm_w[i]), vec(ssm_norm_w[i]), _row128(attn_sinks[i], 0)
        zero_b = jnp.zeros((1, 1536), F32)

        h = _pre_fwd(xc, vec(norm_pre_mix[i]), sc_a, sh_a, f"pre_mix_fwd{i}")
        riders = [wo_sh[0], wu_sh[0]] if i == 0 else [wd_sh[1]]
        proj, *got = _mm(h, win_t[i], name=f"in_proj{i}", tb=True, tn=896, tk=2048, cargo=_AgCargo(riders))
        if i == 0:
            wout_f[0], wup_g[0] = got[0].reshape(D, D), got[1]
        else:
            wdown_f[1] = got[0].reshape(MLP, D)
        qkvc = _conv_fwd(proj, 0, 1536, gdn_cw[i], zero_b, f"gdn_conv_fwd{i}")
        ya, gdn_s = _gdn_fwd(qkvc, proj, gal, gdb, gnw, f"gdn_fwd{i}")
        xbcc = _conv_fwd(proj, P_SXBC // 1024, 1024, ssm_cw[i], vec(ssm_conv_b[i]), f"ssm_conv_fwd{i}")
        yb, ssd_s = _ssd_fwd(xbcc, proj, sal, sdb, sdd, snw, f"ssd_fwd{i}")
        yc = _swa_fwd(proj, cos, sin, skr, f"swa_fwd{i}")
        yd = _sc_fwd(proj, sc_cw[i], f"sc_fwd{i}")
        ycat = jnp.concatenate([ya, yb, yc, yd], axis=1)
        out = _mm(ycat, wout_f[i], name=f"out_proj{i}", tk=2048)
        x1 = _post_fwd(xc, out, vec(norm_post_mix[i]), gt_a, f"post_mix_fwd{i}")
        h2 = _pre_fwd(x1, vec(norm_pre_mlp[i]), sc_m, sh_m, f"pre_mlp_fwd{i}")
        if i == 0:
            act, u, got_d, got_o = _mm(h2, wup_g[0], name="up_proj0", tk=2048, mode="relu2", shards="b",
                                       cargo=_AgCargo([wd_sh[0], wo_sh[1]]))
            wdown_f[0], wout_f[1] = got_d.reshape(MLP, D), got_o.reshape(D, D)
            y2, got_i, wup_g[1] = _mm(act, wdown_f[0], name="down_proj0", tk=2048, cargo=_AgCargo([wt_sh[1], wu_sh[1]]))
            win_t[1] = full_w_in(got_i)
        else:
            act, u = _mm(h2, wup_g[1], name="up_proj1", tk=2048, mode="relu2", shards="b")
            y2 = _mm(act, wdown_f[1], name="down_proj1", tk=2048)
        x2 = _post_fwd(x1, y2, vec(norm_post_mlp[i]), gt_m, f"post_mlp_fwd{i}")
        saved.append(dict(x0=xc, h=h, proj=proj, qkvc=qkvc, gdn_s=gdn_s, xbcc=xbcc, ssd_s=ssd_s, ycat=ycat, out=out,
                          x1=x1, h2=h2, act=act, u=u, y2=y2, mods=(sh_a, sc_a, gt_a, sh_m, sc_m, gt_m),
                          rows=(gal, gdb, sal, sdb, sdd, gnw, snw, skr)))
        xc = x2

    sq, dx = _loss_head(xc, target, "loss_head")
    loss = lax.psum(0.5 * sq[0, 0] / float(D), ("x", "y", "c"))

    gw = [dict() for _ in range(2)]
    r_in, r_out, r_up, r_down = [None, None], [None, None], [None, None], [None, None]
    for i in (1, 0):
        sv = saved[i]
        sh_a, sc_a, gt_a, sh_m, sc_m, gt_m = sv["mods"]
        gal, gdb, sal, sdb, sdd, gnw, snw, skr = sv["rows"]
        proj = sv["proj"]
        dy2, g_npostmlp, d_gt_m = _post_bwd(sv["y2"], vec(norm_post_mlp[i]), gt_m, dx, f"post_mlp_bwd{i}")
        if i == 0:
            du, r_in[1] = _mm(dy2, wdown_f[0], name="down_proj_dx0", tb=True, tk=2048, out_dtype=BF16, mode="drelu2",
                              u=sv["u"], cargo=_ExCargo([[gw[1]["w_in"]]]))
        else:
            du = _mm(dy2, wdown_f[1], name="down_proj_dx1", tb=True, tk=2048, out_dtype=BF16, mode="drelu2", u=sv["u"])
        g_wdown = _mm(sv["act"], dy2, name=f"down_proj_dw{i}", ta=True, out_dtype=BF16).reshape(N_DEV, MLP // N_DEV, D)
        dh2, r_down[i] = _mm(du, wup_g[i], name=f"up_proj_dx{i}", tb=True, tk=1024, out_dtype=BF16, shards="b",
                             cargo=_ExCargo([[g_wdown]]))
        g_wup = _mm(sv["h2"], du, name=f"up_proj_dw{i}", ta=True, out_dtype=BF16, shards="o")
        dx1, g_npremlp, d_sc_m, d_sh_m = _pre_bwd(sv["x1"], vec(norm_pre_mlp[i]), sc_m, sh_m, dh2, dx, f"pre_mlp_bwd{i}")
        dout, g_npostmix, d_gt_a = _post_bwd(sv["out"], vec(norm_post_mix[i]), gt_a, dx1, f"post_mix_bwd{i}")
        dycat = _mm(dout, wout_f[i], name=f"out_proj_dx{i}", tb=True, tk=2048)
        g_wout = _mm(sv["ycat"], dout, name=f"out_proj_dw{i}", ta=True, out_dtype=BF16).reshape(N_DEV, D // N_DEV, D)

        dqkvc, dgz, dgate_g, d_gal, d_gdb, d_gnw, r_up[i] = _gdn_bwd(
            sv["qkvc"], proj, gal, gdb, gnw, sv["gdn_s"], dycat, f"gdn_bwd{i}", cargo=_ExCargo([[g_wup]]))
        dqkv, g_gdn_cw, _ = _conv_bwd(dqkvc, proj, 0, 1536, gdn_cw[i], f"gdn_conv_bwd{i}")
        dxbcc, dsz, dgate_s, d_sal, d_sdb, d_sdd, d_snw = _ssd_bwd(
            sv["xbcc"], proj, sal, sdb, sdd, snw, sv["ssd_s"], dycat, f"ssd_bwd{i}")
        dsxbc, g_ssm_cw, g_ssm_cb = _conv_bwd(dxbcc, proj, P_SXBC // 1024, 1024, ssm_cw[i], f"ssm_conv_bwd{i}")
        daq, dak, dav, d_skr = _swa_bwd(proj, cos, sin, skr, dycat, f"swa_bwd{i}")
        dsc, g_sc_cw = _sc_bwd(dycat, proj, sc_cw[i], f"sc_bwd{i}")
        dgate = (dgate_g + dgate_s).astype(BF16)
        dproj = jnp.concatenate([dqkv, dgz, dsxbc, dsz, daq, dsc, dak, dav, dgate], axis=1)
        dh, r_out[i] = _mm(dproj, win_t[i], name=f"in_proj_dx{i}", tk=896, out_dtype=BF16, cargo=_ExCargo([[g_wout]]))
        g_wint = _mm(dproj, sv["h"], name=f"in_proj_dw{i}", ta=True, tm=896, out_dtype=BF16)
        g_wint = jnp.pad(_unpermute_rows(g_wint).reshape(N_DEV, SHARD_IN, D), ((0, 0), (0, SHARD_PAD - SHARD_IN), (0, 0)))
        dx, g_npremix, d_sc_a, d_sh_a = _pre_bwd(sv["x0"], vec(norm_pre_mix[i]), sc_a, sh_a, dh, dx1, f"pre_mix_bwd{i}")

        gw[i] = dict(
            w_in=g_wint, w_out=g_wout,
            dmod=jnp.concatenate([d_sh_a, d_sc_a, d_gt_a, d_sh_m, d_sc_m, d_gt_m], axis=1).reshape(-1),
            norm_pre_mix=g_npremix.reshape(-1), norm_post_mix=g_npostmix.reshape(-1),
            norm_pre_mlp=g_npremlp.reshape(-1), norm_post_mlp=g_npostmlp.reshape(-1),
            gdn_conv_w=g_gdn_cw.reshape(-1), gdn_a_log=d_gal[0, LANE_GA:LANE_GA + 4], gdn_dt_bias=d_gdb[0, LANE_GA:LANE_GA + 4],
            gdn_norm_w=d_gnw.reshape(-1), ssm_conv_w=g_ssm_cw.reshape(-1), ssm_conv_b=g_ssm_cb.reshape(-1),
            ssm_a_log=d_sal[0, LANE_SDT:LANE_SDT + 8], ssm_dt_bias=d_sdb[0, LANE_SDT:LANE_SDT + 8],
            ssm_d=d_sdd[0, LANE_SDT:LANE_SDT + 8], ssm_norm_w=d_snw.reshape(-1), attn_sinks=d_skr[0, 0:8],
            sc_conv_w=g_sc_cw.reshape(-1))
    grad_x = dx[None]

    (r_in[0],) = _exchange_multi([[gw[0]["w_in"]]], "grad_exchange_w_in0")
    grads, deltas, new_m, new_v = {}, {}, {}, {}
    for n_, r, w, m, v in (("w_out", r_out, w_out, m_w_out, v_w_out), ("w_up", r_up, w_up, m_w_up, v_w_up),
                           ("w_down", r_down, w_down, m_w_down, v_w_down)):
        grads[n_], deltas[n_], new_m[n_], new_v[n_] = _sum8_adamw(r[0], r[1], w, m, v, f"sum_adamw_{n_}")
    g_int = jnp.stack([_sum8(r_in[i].reshape(N_DEV, SHARD_PAD, D), f"grad_sum_w_in{i}") for i in range(2)])
    g_in = g_int[:, :SHARD_IN].transpose(0, 2, 1)
    dl, nm, nv = _adamw_call(w_in.reshape(-1, SHARD_IN), g_in.reshape(-1, SHARD_IN), m_w_in.reshape(-1, SHARD_IN),
                             v_w_in.reshape(-1, SHARD_IN), "adamw_w_in")
    grads["w_in"], deltas["w_in"] = g_in, dl.reshape(w_in.shape)
    new_m["w_in"], new_v["w_in"] = nm.reshape(w_in.shape), nv.reshape(w_in.shape)

    small_names = ["dmod", "norm_pre_mix", "norm_post_mix", "norm_pre_mlp", "norm_post_mlp", "gdn_conv_w", "gdn_a_log",
                   "gdn_dt_bias", "gdn_norm_w", "ssm_conv_w", "ssm_conv_b", "ssm_a_log", "ssm_dt_bias", "ssm_d",
                   "ssm_norm_w", "attn_sinks", "sc_conv_w"]
    flat = jnp.concatenate([gw[i][nm_] for nm_ in small_names for i in range(2)])
    n_flat = flat.shape[0]
    g_all = _ag_small(_pad_rows128(flat), "ag_small_grads")
    tot = _sum8(g_all, "small_grad_sum").reshape(-1)[:n_flat]
    dmod_all = g_all.reshape(N_DEV, -1)[:, :2 * 6 * D].reshape(N_DEV, 2, 6 * D)
    sm = {}
    o = 0
    for nm_ in small_names:
        n = gw[0][nm_].shape[0]
        sm[nm_] = jnp.stack([tot[o:o + n], tot[o + n:o + 2 * n]])
        o += 2 * n

    dmod_sh = lax.dynamic_slice(dmod_all.transpose(1, 0, 2), (0, 0, me * S_ADA), (2, N_DEV, S_ADA))
    grads["ada_w"], deltas["ada_w"], new_m["ada_w"], new_v["ada_w"] = _ada_bwd_adamw(
        c_all, dmod_sh, ada_w, m_ada_w, v_ada_w, "ada_bwd_adamw")

    def shard_cols(full, K, C):
        return lax.dynamic_slice(full.reshape(2, K, C), (0, 0, me * (C // N_DEV)), (2, K, C // N_DEV))

    small_g = dict(
        ada_b=sm["dmod"], norm_pre_mix=sm["norm_pre_mix"], norm_post_mix=sm["norm_post_mix"],
        norm_pre_mlp=sm["norm_pre_mlp"], norm_post_mlp=sm["norm_post_mlp"],
        gdn_conv_w=shard_cols(sm["gdn_conv_w"], 4, 1536), gdn_a_log=sm["gdn_a_log"], gdn_dt_bias=sm["gdn_dt_bias"],
        gdn_norm_w=sm["gdn_norm_w"], ssm_conv_w=shard_cols(sm["ssm_conv_w"], 4, 1024), ssm_conv_b=sm["ssm_conv_b"],
        ssm_a_log=sm["ssm_a_log"], ssm_dt_bias=sm["ssm_dt_bias"], ssm_d=sm["ssm_d"], ssm_norm_w=sm["ssm_norm_w"],
        attn_sinks=sm["attn_sinks"], sc_conv_w=shard_cols(sm["sc_conv_w"], 3, 512))
    small_w = dict(
        ada_b=(ada_b, m_ada_b, v_ada_b), norm_pre_mix=(norm_pre_mix, m_norm_pre_mix, v_norm_pre_mix),
        norm_post_mix=(norm_post_mix, m_norm_post_mix, v_norm_post_mix),
        norm_pre_mlp=(norm_pre_mlp, m_norm_pre_mlp, v_norm_pre_mlp),
        norm_post_mlp=(norm_post_mlp, m_norm_post_mlp, v_norm_post_mlp),
        gdn_conv_w=(gdn_conv_w, m_gdn_conv_w, v_gdn_conv_w), gdn_a_log=(gdn_a_log, m_gdn_a_log, v_gdn_a_log),
        gdn_dt_bias=(gdn_dt_bias, m_gdn_dt_bias, v_gdn_dt_bias), gdn_norm_w=(gdn_norm_w, m_gdn_norm_w, v_gdn_norm_w),
        ssm_conv_w=(ssm_conv_w, m_ssm_conv_w, v_ssm_conv_w), ssm_conv_b=(ssm_conv_b, m_ssm_conv_b, v_ssm_conv_b),
        ssm_a_log=(ssm_a_log, m_ssm_a_log, v_ssm_a_log), ssm_dt_bias=(ssm_dt_bias, m_ssm_dt_bias, v_ssm_dt_bias),
        ssm_d=(ssm_d, m_ssm_d, v_ssm_d), ssm_norm_w=(ssm_norm_w, m_ssm_norm_w, v_ssm_norm_w),
        attn_sinks=(attn_sinks, m_attn_sinks, v_attn_sinks), sc_conv_w=(sc_conv_w, m_sc_conv_w, v_sc_conv_w))
    names = list(small_w)
    sizes = [int(np.prod(small_w[n_][0].shape)) for n_ in names]
    pk = lambda j: _pad_rows128(jnp.concatenate([small_w[n_][j].reshape(-1) for n_ in names]))
    gk = _pad_rows128(jnp.concatenate([small_g[n_].reshape(-1) for n_ in names]))
    dl, nm, nv = _adamw_call(pk(0), gk, pk(1), pk(2), "adamw_small")
    dl, nm, nv = dl.reshape(-1), nm.reshape(-1), nv.reshape(-1)
    o = 0
    for n_, sz in zip(names, sizes):
        shp = small_w[n_][0].shape
        grads[n_] = small_g[n_].reshape(shp)
        deltas[n_], new_m[n_], new_v[n_] = dl[o:o + sz].reshape(shp), nm[o:o + sz].reshape(shp), nv[o:o + sz].reshape(shp)
        o += sz

    order = ["ada_w", "ada_b", "norm_pre_mix", "norm_post_mix", "norm_pre_mlp", "norm_post_mlp", "w_in", "w_out",
             "gdn_conv_w", "gdn_a_log", "gdn_dt_bias", "gdn_norm_w", "ssm_conv_w", "ssm_conv_b", "ssm_a_log",
             "ssm_dt_bias", "ssm_d", "ssm_norm_w", "attn_sinks", "sc_conv_w", "w_up", "w_down"]
    return (loss, grad_x, *[grads[n_] for n_ in order], *[deltas[n_] for n_ in order],
            *[new_m[n_] for n_ in order], *[new_v[n_] for n_ in order])
```

```python
import functools
import math

import jax
import jax.numpy as jnp
import numpy as np
from jax import lax
from jax.experimental import pallas as pl
from jax.experimental.pallas import tpu as pltpu

F32 = jnp.float32
BF16 = jnp.bfloat16
MESH = pl.DeviceIdType.MESH

N_DEV = 8
D = 2048
GW = 512
MLP = 8192
EPS = 1e-6
IN_W = 5904
SHARD_IN = IN_W // N_DEV
SHARD_PAD = 752
PW = 6272
ROPE_THETA = 10000.0
WINDOW = 128
GDN_CHUNK = 64
SSM_CHUNK = 128
NEG = -1e30

ADAM_LR, ADAM_B1, ADAM_B2, ADAM_EPS, ADAM_WD, ADAM_STEP = 0.001, 0.9, 0.999, 1e-08, 0.01, 10

O_GQ, O_GK, O_GV, O_GZ, O_GB, O_GA = 0, 512, 1024, 1536, 2048, 2052
O_SZ, O_SXBC, O_SDT = 2056, 2568, 3592
O_AQ, O_AK, O_AV = 3600, 4112, 4240
O_CB, O_CC, O_CH = 4368, 4880, 5392
P_QKV, P_GZ, P_SXBC, P_SZ, P_AQ, P_CB, P_AK, P_AV, P_GATE = 0, 1536, 2048, 3072, 3584, 4096, 5632, 5888, 6144
LANE_GB, LANE_GA, LANE_SDT = 0, 4, 8

VMEM_BIG = 56 * 1024 * 1024


def _cparams(sem=None, vmem=None):
    kw = {}
    if sem is not None:
        kw["dimension_semantics"] = sem
    if vmem is not None:
        kw["vmem_limit_bytes"] = vmem
    return pltpu.CompilerParams(**kw)


def _me():
    x, y, c = lax.axis_index("x"), lax.axis_index("y"), lax.axis_index("c")
    return x, y, c, 4 * x + 2 * y + c


def _peer(x, y, c, k):
    px = 1 - x if (k >> 2) & 1 else x
    py = 1 - y if (k >> 1) & 1 else y
    pc = 1 - c if k & 1 else c
    return (px, py, pc), 4 * px + 2 * py + pc


def _ag_small(v, name):
    R, W = v.shape

    def body(x_ref, out_ref, send_sems, recv_sems):
        x, y, c, me = _me()
        out_ref[me] = x_ref[...]
        copies = []
        for k in range(1, N_DEV):
            peer, _ = _peer(x, y, c, k)
            cp = pltpu.make_async_remote_copy(
                src_ref=x_ref, dst_ref=out_ref.at[me], send_sem=send_sems.at[k - 1], recv_sem=recv_sems.at[k - 1],
                device_id=peer, device_id_type=MESH)
            cp.start()
            copies.append(cp)
        for cp in copies:
            cp.wait()

    return pl.pallas_call(
        body, name=name,
        out_shape=jax.ShapeDtypeStruct((N_DEV, R, W), v.dtype),
        in_specs=[pl.BlockSpec(memory_space=pltpu.VMEM)],
        out_specs=pl.BlockSpec(memory_space=pltpu.VMEM),
        scratch_shapes=[pltpu.SemaphoreType.DMA((N_DEV - 1,)), pltpu.SemaphoreType.DMA((N_DEV - 1,))],
    )(v)


def _ag_multi(arrs, name):
    cargo = _AgCargo(arrs)
    A = cargo.n_in

    def body(*refs):
        start, mid, end = cargo.phases(refs[:A], refs[A:A + cargo.n_out], refs[A + cargo.n_out:])
        start()
        mid()
        end()

    hbm = pl.BlockSpec(memory_space=pl.ANY)
    return pl.pallas_call(
        body, name=name, out_shape=tuple(cargo.out_shapes), in_specs=[hbm] * A, out_specs=tuple([hbm] * cargo.n_out),
        scratch_shapes=cargo.scratch)(*cargo.arrays)


class _AgCargo:
    def __init__(self, arrs):
        A = len(arrs)
        self.arrays = list(arrs)
        self.n_in = self.n_out = A
        self.out_shapes = [jax.ShapeDtypeStruct((N_DEV,) + v.shape, v.dtype) for v in arrs]
        self.scratch = [pltpu.SemaphoreType.DMA((7 * A,)), pltpu.SemaphoreType.DMA((7 * A,)),
                        pltpu.SemaphoreType.DMA((A,))]

    def phases(self, x_refs, out_refs, sems):
        A = self.n_in
        send_sems, recv_sems, local_sems = sems

        def ctx():
            x, y, c, me = _me()
            return x, y, c, me, (x, y, 1 - c), [(1 - x, y), (x, 1 - y), (1 - x, 1 - y)]

        def copy(a, k, block, to, src=None):
            slot = out_refs[a].at[4 * block[0] + 2 * block[1] + block[2]]
            return pltpu.make_async_remote_copy(
                src_ref=slot if src is None else src, dst_ref=slot, send_sem=send_sems.at[7 * a + k],
                recv_sem=recv_sems.at[7 * a + k], device_id=to, device_id_type=MESH)

        def start():
            x, y, c, me, sibling, chips = ctx()
            for a in range(A):
                pltpu.make_async_copy(x_refs[a], out_refs[a].at[me], local_sems.at[a]).start()
            for a in range(A):
                copy(a, 0, (x, y, c), sibling, src=x_refs[a]).start()
                for j, chip in enumerate(chips):
                    copy(a, 1 + j, (x, y, c), (*chip, c), src=x_refs[a]).start()

        def mid():
            x, y, c, me, sibling, chips = ctx()
            for j, chip in enumerate(chips):
                for a in range(A):
                    copy(a, 1 + j, (*chip, c), (x, y, c)).wait_recv()
                    copy(a, 4 + j, (*chip, c), sibling).start()

        def end():
            x, y, c, me, sibling, chips = ctx()
            for a in range(A):
                copy(a, 0, sibling, (x, y, c)).wait_recv()
                for j, chip in enumerate(chips):
                    copy(a, 4 + j, (*chip, 1 - c), (x, y, c)).wait_recv()
            for a in range(A):
                copy(a, 0, (x, y, c), sibling, src=x_refs[a]).wait_send()
                for j, chip in enumerate(chips):
                    copy(a, 1 + j, (x, y, c), (*chip, c), src=x_refs[a]).wait_send()
                    copy(a, 4 + j, (*chip, c), sibling).wait_send()
                pltpu.make_async_copy(x_refs[a], out_refs[a].at[me], local_sems.at[a]).wait()

        return start, mid, end


class _ExCargo:
    def __init__(self, groups):
        self.flat = [(w, i) for w, layers in enumerate(groups) for i, _ in enumerate(layers)]
        self.arrays = [g for layers in groups for g in layers]
        A = len(self.arrays)
        self.n_in, self.n_out = A, len(groups)
        self.out_shapes = [jax.ShapeDtypeStruct((N_DEV, len(layers)) + layers[0].shape[1:], layers[0].dtype)
                           for layers in groups]
        self.scratch = [pltpu.SemaphoreType.DMA((7 * A,)), pltpu.SemaphoreType.DMA((7 * A,)),
                        pltpu.SemaphoreType.DMA((A,))]

    def phases(self, g_refs, out_refs, sems):
        send_sems, recv_sems, local_sems = sems

        def copies():
            x, y, c, me = _me()
            local = [pltpu.make_async_copy(g_refs[a].at[me], out_refs[w].at[me, i], local_sems.at[a])
                     for a, (w, i) in enumerate(self.flat)]
            remote = []
            for k in range(1, N_DEV):
                peer, pid = _peer(x, y, c, k)
                for a, (w, i) in enumerate(self.flat):
                    remote.append(pltpu.make_async_remote_copy(
                        src_ref=g_refs[a].at[pid], dst_ref=out_refs[w].at[me, i], send_sem=send_sems.at[7 * a + k - 1],
                        recv_sem=recv_sems.at[7 * a + k - 1], device_id=peer, device_id_type=MESH))
            return local, remote

        def start():
            local, remote = copies()
            for cp in local + remote:
                cp.start()

        def end():
            local, remote = copies()
            for cp in remote + local:
                cp.wait()

        return start, (lambda: None), end


def _mm(a, b, *, name, ta=False, tb=False, tm=1024, tn=1024, tk=1024, out_dtype=F32, mode="plain", u=None,
        shards=None, cargo=None):
    K, M = a.shape if ta else a.shape[::-1]
    if shards == "b":
        N, K2 = (b.shape[1], b.shape[0] * b.shape[2]) if tb else (b.shape[0] * b.shape[2], b.shape[1])
    else:
        N, K2 = b.shape if tb else b.shape[::-1]
    assert K == K2, (a.shape, b.shape)
    tm, tn, tk = min(tm, M), min(tn, N), min(tk, K)
    assert M % tm == 0 and N % tn == 0 and K % tk == 0, (M, N, K, tm, tn, tk)
    nk = K // tk
    a_spec = pl.BlockSpec((tk, tm), lambda i, j, k: (k, i)) if ta else pl.BlockSpec((tm, tk), lambda i, j, k: (i, k))
    if shards == "b" and tb:
        assert tk == b.shape[2]
        b_spec = pl.BlockSpec((None, tn, tk), lambda i, j, k: (k, j, 0))
    elif shards == "b":
        assert tn == b.shape[2]
        b_spec = pl.BlockSpec((None, tk, tn), lambda i, j, k: (j, k, 0))
    else:
        b_spec = pl.BlockSpec((tn, tk), lambda i, j, k: (j, k)) if tb else pl.BlockSpec((tk, tn), lambda i, j, k: (k, j))
    if shards == "o":
        o_spec = pl.BlockSpec((None, tm, tn), lambda i, j, k: (j, i, 0))
        o_shape = (N // tn, M, tn)
    else:
        o_spec = pl.BlockSpec((tm, tn), lambda i, j, k: (i, j))
        o_shape = (M, N)
    dn = (((0 if ta else 1,), (1 if tb else 0,)), ((), ()))
    ni_, nj = M // tm, N // tn
    nsteps = ni_ * nj * nk
    n_in = 3 if mode == "drelu2" else 2
    n_out = 2 if mode == "relu2" else 1
    c_arr, c_ispec, c_oshape, c_ospec = _cargo_io(cargo)
    ni, no = len(c_arr), len(c_oshape)

    def body(*refs):
        a_ref, b_ref = refs[0], refs[1]
        u_ref = refs[2] if mode == "drelu2" else None
        outs = refs[n_in + ni:n_in + ni + n_out]
        acc = refs[n_in + ni + n_out + no]
        k = pl.program_id(2)
        step = (pl.program_id(0) * nj + pl.program_id(1)) * nk + k
        drain = _carry(cargo, (refs[n_in:n_in + ni], refs[n_in + ni + n_out:n_in + ni + n_out + no],
                               refs[n_in + ni + n_out + no + 1:]), step, nsteps)

        @pl.when(k == 0)
        def _():
            acc[...] = jnp.zeros_like(acc)

        acc[...] += lax.dot_general(a_ref[...].astype(BF16), b_ref[...].astype(BF16), dn, preferred_element_type=F32)

        @pl.when(k == nk - 1)
        def _():
            r = acc[...]
            if mode == "plain":
                outs[0][...] = r.astype(out_dtype)
            elif mode == "relu2":
                rr = jnp.maximum(r, 0.0)
                outs[0][...] = (rr * rr).astype(BF16)
                outs[1][...] = r.astype(BF16)
            else:
                outs[0][...] = (r * (2.0 * jnp.maximum(u_ref[...].astype(F32), 0.0))).astype(out_dtype)

        drain()

    in_specs, args = [a_spec, b_spec], [a, b]
    if mode == "drelu2":
        in_specs.append(o_spec)
        args.append(u)
    if mode == "relu2":
        out_shape = [jax.ShapeDtypeStruct(o_shape, BF16), jax.ShapeDtypeStruct(o_shape, BF16)]
        out_specs = [o_spec, o_spec]
    else:
        out_shape = [jax.ShapeDtypeStruct(o_shape, out_dtype)]
        out_specs = [o_spec]
    res = pl.pallas_call(
        body, name=name, grid=(ni_, nj, nk), in_specs=in_specs + c_ispec, out_specs=tuple(out_specs + c_ospec),
        out_shape=tuple(out_shape + c_oshape),
        scratch_shapes=[pltpu.VMEM((tm, tn), F32)] + (cargo.scratch if cargo else []),
        compiler_params=_cparams(("arbitrary",) * 3 if cargo else ("parallel", "parallel", "arbitrary"), VMEM_BIG),
    )(*args, *c_arr)
    return res[0] if len(res) == 1 else res


ROW_T = 256


def _rms(x, w):
    return x * lax.rsqrt(jnp.mean(x * x, axis=-1, keepdims=True) + EPS) * w


def _pre_fn(x, w, scale, shift):
    return _rms(x, w) * (1.0 + scale) + shift


def _post_fn(y, w, gate):
    return gate * _rms(y, w)


def _row_spec(T, W):
    return pl.BlockSpec((T, W), lambda i: (i, 0))


def _vec_spec(W):
    return pl.BlockSpec((1, W), lambda i: (0, 0))


def _pre_fwd(x, w, scale, shift, name):
    L = x.shape[0]
    T = min(ROW_T, L)

    def body(x_ref, w_ref, sc_ref, sh_ref, h_ref):
        h_ref[...] = _pre_fn(x_ref[...], w_ref[...], sc_ref[...], sh_ref[...]).astype(BF16)

    return pl.pallas_call(
        body, name=name, grid=(L // T,), in_specs=[_row_spec(T, D), _vec_spec(D), _vec_spec(D), _vec_spec(D)],
        out_specs=_row_spec(T, D), out_shape=jax.ShapeDtypeStruct((L, D), BF16),
        compiler_params=_cparams(("parallel",), VMEM_BIG))(x, w, scale, shift)


def _pre_bwd(x, w, scale, shift, dh, dres, name):
    L = x.shape[0]
    T = min(ROW_T, L)

    def body(x_ref, w_ref, sc_ref, sh_ref, dh_ref, dres_ref, dx_ref, dw_ref, dsc_ref, dsh_ref):
        @pl.when(pl.program_id(0) == 0)
        def _():
            dw_ref[...] = jnp.zeros_like(dw_ref)
            dsc_ref[...] = jnp.zeros_like(dsc_ref)
            dsh_ref[...] = jnp.zeros_like(dsh_ref)

        _, vjp = jax.vjp(_pre_fn, x_ref[...], w_ref[...], sc_ref[...], sh_ref[...])
        dx, dw, dsc, dsh = vjp(dh_ref[...].astype(F32))
        dx_ref[...] = dres_ref[...] + dx
        dw_ref[...] += dw
        dsc_ref[...] += dsc
        dsh_ref[...] += dsh

    vec = jax.ShapeDtypeStruct((1, D), F32)
    return pl.pallas_call(
        body, name=name, grid=(L // T,),
        in_specs=[_row_spec(T, D), _vec_spec(D), _vec_spec(D), _vec_spec(D), _row_spec(T, D), _row_spec(T, D)],
        out_specs=(_row_spec(T, D), _vec_spec(D), _vec_spec(D), _vec_spec(D)),
        out_shape=(jax.ShapeDtypeStruct((L, D), F32), vec, vec, vec),
        compiler_params=_cparams(("arbitrary",), VMEM_BIG))(x, w, scale, shift, dh, dres)


def _post_fwd(x, y, w, gate, name):
    L = x.shape[0]
    T = min(ROW_T, L)

    def body(x_ref, y_ref, w_ref, g_ref, o_ref):
        o_ref[...] = x_ref[...] + _post_fn(y_ref[...], w_ref[...], g_ref[...])

    return pl.pallas_call(
        body, name=name, grid=(L // T,), in_specs=[_row_spec(T, D), _row_spec(T, D), _vec_spec(D), _vec_spec(D)],
        out_specs=_row_spec(T, D), out_shape=jax.ShapeDtypeStruct((L, D), F32),
        compiler_params=_cparams(("parallel",), VMEM_BIG))(x, y, w, gate)


def _post_bwd(y, w, gate, dxn, name):
    L = y.shape[0]
    T = min(ROW_T, L)

    def body(y_ref, w_ref, g_ref, d_ref, dy_ref, dw_ref, dg_ref):
        @pl.when(pl.program_id(0) == 0)
        def _():
            dw_ref[...] = jnp.zeros_like(dw_ref)
            dg_ref[...] = jnp.zeros_like(dg_ref)

        _, vjp = jax.vjp(_post_fn, y_ref[...], w_ref[...], g_ref[...])
        dy, dw, dg = vjp(d_ref[...])
        dy_ref[...] = dy.astype(BF16)
        dw_ref[...] += dw
        dg_ref[...] += dg

    vec = jax.ShapeDtypeStruct((1, D), F32)
    return pl.pallas_call(
        body, name=name, grid=(L // T,), in_specs=[_row_spec(T, D), _vec_spec(D), _vec_spec(D), _row_spec(T, D)],
        out_specs=(_row_spec(T, D), _vec_spec(D), _vec_spec(D)),
        out_shape=(jax.ShapeDtypeStruct((L, D), BF16), vec, vec),
        compiler_params=_cparams(("arbitrary",), VMEM_BIG))(y, w, gate, dxn)


def _post_pre_fwd(x, y, w_post, gate, w_pre, scale, shift, name):
    L = x.shape[0]
    T = min(ROW_T, L)

    def body(x_ref, y_ref, wp_ref, g_ref, w_ref, sc_ref, sh_ref, o_ref, h_ref):
        xn = x_ref[...] + _post_fn(y_ref[...], wp_ref[...], g_ref[...])
        o_ref[...] = xn
        h_ref[...] = _pre_fn(xn, w_ref[...], sc_ref[...], sh_ref[...]).astype(BF16)

    return pl.pallas_call(
        body, name=name, grid=(L // T,),
        in_specs=[_row_spec(T, D), _row_spec(T, D)] + [_vec_spec(D)] * 5,
        out_specs=(_row_spec(T, D), _row_spec(T, D)),
        out_shape=(jax.ShapeDtypeStruct((L, D), F32), jax.ShapeDtypeStruct((L, D), BF16)),
        compiler_params=_cparams(("parallel",), VMEM_BIG))(x, y, w_post, gate, w_pre, scale, shift)


def _pre_post_bwd(x, w_pre, scale, shift, dh, dres, y, w_post, gate, name):
    L = x.shape[0]
    T = min(ROW_T, L)

    def body(x_ref, w_ref, sc_ref, sh_ref, dh_ref, dres_ref, y_ref, wp_ref, g_ref,
             dx_ref, dy_ref, dw_ref, dsc_ref, dsh_ref, dwp_ref, dg_ref):
        @pl.when(pl.program_id(0) == 0)
        def _():
            for r in (dw_ref, dsc_ref, dsh_ref, dwp_ref, dg_ref):
                r[...] = jnp.zeros_like(r)

        _, vjp = jax.vjp(_pre_fn, x_ref[...], w_ref[...], sc_ref[...], sh_ref[...])
        dx, dw, dsc, dsh = vjp(dh_ref[...].astype(F32))
        dx = dres_ref[...] + dx
        dx_ref[...] = dx
        _, vjp2 = jax.vjp(_post_fn, y_ref[...], wp_ref[...], g_ref[...])
        dy, dwp, dg = vjp2(dx)
        dy_ref[...] = dy.astype(BF16)
        dw_ref[...] += dw
        dsc_ref[...] += dsc
        dsh_ref[...] += dsh
        dwp_ref[...] += dwp
        dg_ref[...] += dg

    vec = jax.ShapeDtypeStruct((1, D), F32)
    v = _vec_spec(D)
    r = _row_spec(T, D)
    return pl.pallas_call(
        body, name=name, grid=(L // T,), in_specs=[r, v, v, v, r, r, r, v, v],
        out_specs=(r, r, v, v, v, v, v),
        out_shape=(jax.ShapeDtypeStruct((L, D), F32), jax.ShapeDtypeStruct((L, D), BF16), vec, vec, vec, vec, vec),
        compiler_params=_cparams(("arbitrary",), VMEM_BIG))(x, w_pre, scale, shift, dh, dres, y, w_post, gate)


def _loss_head(y, target, name):
    L = y.shape[0]
    T = min(ROW_T, L)

    def body(y_ref, t_ref, s_ref, d_ref):
        @pl.when(pl.program_id(0) == 0)
        def _():
            s_ref[...] = jnp.zeros_like(s_ref)

        e = y_ref[...] - t_ref[...]
        d_ref[...] = e / float(D)
        s_ref[...] += jnp.sum(e * e)

    return pl.pallas_call(
        body, name=name, grid=(L // T,), in_specs=[_row_spec(T, D), _row_spec(T, D)],
        out_specs=(pl.BlockSpec((8, 128), lambda i: (0, 0)), _row_spec(T, D)),
        out_shape=(jax.ShapeDtypeStruct((8, 128), F32), jax.ShapeDtypeStruct((L, D), F32)),
        compiler_params=_cparams(("arbitrary",), VMEM_BIG))(y, target)


CONV_T = 512


def _conv_fwd(x, ci, C, w, b, name):
    L = x.shape[0]
    K = w.shape[0]
    T = min(CONV_T, L)

    def body(xc_ref, xp_ref, w_ref, b_ref, y_ref, buf):
        i = pl.program_id(0)
        buf[0:8, :] = jnp.where(i > 0, xp_ref[...], 0.0)
        buf[8:T + 8, :] = xc_ref[...]
        acc = jnp.zeros((T, C), F32) + b_ref[...]
        for k in range(K):
            acc = acc + w_ref[k:k + 1, :] * buf[pl.ds(8 - (K - 1) + k, T), :]
        y_ref[...] = acc

    return pl.pallas_call(
        body, name=name, grid=(L // T,),
        in_specs=[pl.BlockSpec((T, C), lambda i: (i, ci)),
                  pl.BlockSpec((8, C), lambda i: (jnp.maximum(i * (T // 8) - 1, 0), ci)),
                  pl.BlockSpec((K, C), lambda i: (0, 0)), _vec_spec(C)],
        out_specs=_row_spec(T, C), out_shape=jax.ShapeDtypeStruct((L, C), F32),
        scratch_shapes=[pltpu.VMEM((T + 8, C), F32)],
        compiler_params=_cparams(("parallel",), VMEM_BIG))(x, x, w, b)


def _conv_bwd(dy, x, ci, C, w, name):
    L = x.shape[0]
    K = w.shape[0]
    T = min(CONV_T, L)
    nt = L // T

    def body(dc_ref, dn_ref, xc_ref, xp_ref, w_ref, dx_ref, dw_ref, db_ref, dbuf, xbuf):
        i = pl.program_id(0)

        @pl.when(i == 0)
        def _():
            dw_ref[...] = jnp.zeros_like(dw_ref)
            db_ref[...] = jnp.zeros_like(db_ref)

        dyc = dc_ref[...]
        dbuf[0:T, :] = dyc
        dbuf[T:T + 8, :] = jnp.where(i < nt - 1, dn_ref[...], 0.0)
        xbuf[0:8, :] = jnp.where(i > 0, xp_ref[...], 0.0)
        xbuf[8:T + 8, :] = xc_ref[...]
        dx = jnp.zeros((T, C), F32)
        for k in range(K):
            dx = dx + w_ref[k:k + 1, :] * dbuf[pl.ds(K - 1 - k, T), :]
            dw_ref[k:k + 1, :] += jnp.sum(dyc * xbuf[pl.ds(8 - (K - 1) + k, T), :], axis=0, keepdims=True)
        dx_ref[...] = dx.astype(BF16)
        db_ref[...] += jnp.sum(dyc, axis=0, keepdims=True)

    return pl.pallas_call(
        body, name=name, grid=(nt,),
        in_specs=[_row_spec(T, C),
                  pl.BlockSpec((8, C), lambda i: (jnp.minimum((i + 1) * (T // 8), L // 8 - 1), 0)),
                  pl.BlockSpec((T, C), lambda i: (i, ci)),
                  pl.BlockSpec((8, C), lambda i: (jnp.maximum(i * (T // 8) - 1, 0), ci)),
                  pl.BlockSpec((K, C), lambda i: (0, 0))],
        out_specs=(_row_spec(T, C), pl.BlockSpec((K, C), lambda i: (0, 0)), _vec_spec(C)),
        out_shape=(jax.ShapeDtypeStruct((L, C), BF16), jax.ShapeDtypeStruct((K, C), F32),
                   jax.ShapeDtypeStruct((1, C), F32)),
        scratch_shapes=[pltpu.VMEM((T + 8, C), F32), pltpu.VMEM((T + 8, C), F32)],
        compiler_params=_cparams(("arbitrary",), VMEM_BIG))(dy, dy, x, x, w)


def _sc_fwd(proj, w, name):
    L = proj.shape[0]
    K = w.shape[0]
    C = GW
    T = min(CONV_T, L)
    cb0 = P_CB // C

    def body(b_ref, cc_ref, ch_ref, cp_ref, hp_ref, w_ref, y_ref, buf):
        i = pl.program_id(0)
        buf[0:8, :] = jnp.where(i > 0, cp_ref[...] * hp_ref[...], 0.0)
        buf[8:T + 8, :] = cc_ref[...] * ch_ref[...]
        acc = jnp.zeros((T, C), F32)
        for k in range(K):
            acc = acc + w_ref[k:k + 1, :] * buf[pl.ds(8 - (K - 1) + k, T), :]
        y_ref[...] = (b_ref[...] * acc).astype(BF16)

    def cur(j):
        return pl.BlockSpec((T, C), lambda i: (i, cb0 + j))

    def prev(j):
        return pl.BlockSpec((8, C), lambda i: (jnp.maximum(i * (T // 8) - 1, 0), cb0 + j))

    return pl.pallas_call(
        body, name=name, grid=(L // T,),
        in_specs=[cur(0), cur(1), cur(2), prev(1), prev(2), pl.BlockSpec((K, C), lambda i: (0, 0))],
        out_specs=_row_spec(T, C), out_shape=jax.ShapeDtypeStruct((L, C), BF16),
        scratch_shapes=[pltpu.VMEM((T + 8, C), F32)],
        compiler_params=_cparams(("parallel",), VMEM_BIG))(proj, proj, proj, proj, proj, w)


def _sc_bwd(dycat, proj, w, name):
    L = proj.shape[0]
    K = w.shape[0]
    C = GW
    T = min(CONV_T, L)
    nt = L // T
    cb0 = P_CB // C

    def body(dy_ref, dyn_ref, b_ref, bn_ref, cc_ref, ch_ref, cp_ref, hp_ref, w_ref, d_ref, dw_ref, gbuf, ubuf):
        i = pl.program_id(0)

        @pl.when(i == 0)
        def _():
            dw_ref[...] = jnp.zeros_like(dw_ref)

        dy = dy_ref[...]
        cc, ch = cc_ref[...], ch_ref[...]
        g = dy * b_ref[...]
        gbuf[0:T, :] = g
        gbuf[T:T + 8, :] = jnp.where(i < nt - 1, dyn_ref[...] * bn_ref[...], 0.0)
        ubuf[0:8, :] = jnp.where(i > 0, cp_ref[...] * hp_ref[...], 0.0)
        ubuf[8:T + 8, :] = cc * ch
        conv = jnp.zeros((T, C), F32)
        du = jnp.zeros((T, C), F32)
        for k in range(K):
            us = ubuf[pl.ds(8 - (K - 1) + k, T), :]
            conv = conv + w_ref[k:k + 1, :] * us
            du = du + w_ref[k:k + 1, :] * gbuf[pl.ds(K - 1 - k, T), :]
            dw_ref[k:k + 1, :] += jnp.sum(g * us, axis=0, keepdims=True)
        d_ref[:, 0:C] = (dy * conv).astype(BF16)
        d_ref[:, C:2 * C] = (du * ch).astype(BF16)
        d_ref[:, 2 * C:3 * C] = (du * cc).astype(BF16)

    def cur(j):
        return pl.BlockSpec((T, C), lambda i: (i, cb0 + j))

    def prev(j):
        return pl.BlockSpec((8, C), lambda i: (jnp.maximum(i * (T // 8) - 1, 0), cb0 + j))

    def nxt(col):
        return pl.BlockSpec((8, C), lambda i: (jnp.minimum((i + 1) * (T // 8), L // 8 - 1), col))

    return pl.pallas_call(
        body, name=name, grid=(nt,),
        in_specs=[pl.BlockSpec((T, C), lambda i: (i, 3)), nxt(3), cur(0), nxt(cb0), cur(1), cur(2), prev(1), prev(2),
                  pl.BlockSpec((K, C), lambda i: (0, 0))],
        out_specs=(_row_spec(T, 3 * C), pl.BlockSpec((K, C), lambda i: (0, 0))),
        out_shape=(jax.ShapeDtypeStruct((L, 3 * C), BF16), jax.ShapeDtypeStruct((K, C), F32)),
        scratch_shapes=[pltpu.VMEM((T + 8, C), F32), pltpu.VMEM((T + 8, C), F32)],
        compiler_params=_cparams(("arbitrary",), VMEM_BIG))(dycat, dycat, proj, proj, proj, proj, proj, proj, w)


def _silu(x):
    return x * jax.nn.sigmoid(x)


def _softplus(x):
    return jnp.maximum(x, 0.0) + jnp.log1p(jnp.exp(-jnp.abs(x)))


def _lane_pick(arr, idx):
    lane = lax.broadcasted_iota(jnp.int32, (1, 128), 1)
    return jnp.sum(jnp.where(lane == idx, arr, 0.0), axis=1, keepdims=True)


def _tri(n, strict=False):
    r = lax.broadcasted_iota(jnp.int32, (n, n), 0)
    c = lax.broadcasted_iota(jnp.int32, (n, n), 1)
    return (r > c) if strict else (r >= c)


def _bdot(a, b, dn=(((1,), (0,)), ((), ()))):
    return lax.dot_general(a.astype(BF16), b.astype(BF16), dn, preferred_element_type=F32)


NT = (((1,), (1,)), ((), ()))
TN = (((0,), (0,)), ((), ()))


def _split3(x):
    hi = x.astype(BF16)
    r = x - hi.astype(F32)
    mid = r.astype(BF16)
    lo = (r - mid.astype(F32)).astype(BF16)
    return hi, mid, lo


def _mask_dot(pieces, mask, dn, mask_first):
    def one(p):
        ops = (mask, p) if mask_first else (p, mask)
        return lax.dot_general(ops[0], ops[1], dn, preferred_element_type=F32)
    hi, mid, lo = pieces
    return (one(lo) + one(mid)) + one(hi)


def _tri_apply(x, upper):
    n = x.shape[0]
    r = lax.broadcasted_iota(jnp.int32, (n, n), 0)
    c = lax.broadcasted_iota(jnp.int32, (n, n), 1)
    mask = ((r <= c) if upper else (r >= c)).astype(BF16)
    return _mask_dot(_split3(x), mask, (((1,), (0,)), ((), ())), True)


@jax.custom_vjp
def _cumsum_col(cb):
    return _tri_apply(cb, False)


_cumsum_col.defvjp(lambda cb: (_tri_apply(cb, False), None), lambda _, d: (_tri_apply(d, True),))


def _cumsum_row_fwd(cb):
    tri = _tri(cb.shape[0]).astype(BF16)
    return _mask_dot(_split3(cb), tri, (((0,), (1,)), ((), ())), False)


def _cumsum_row_bwd(_, d):
    tri = _tri(d.shape[1]).astype(BF16)
    return (_mask_dot(_split3(d), tri, (((0,), (1,)), ((), ())), True),)


@jax.custom_vjp
def _cumsum_row(cb):
    return _cumsum_row_fwd(cb)


_cumsum_row.defvjp(lambda cb: (_cumsum_row_fwd(cb), None), _cumsum_row_bwd)


def _dot3(a, b):
    ah, bh = a.astype(BF16), b.astype(BF16)
    al, bl = (a - ah.astype(F32)).astype(BF16), (b - bh.astype(F32)).astype(BF16)
    d = lambda p, q: jnp.dot(p, q, preferred_element_type=F32)
    return (d(al, bh) + d(ah, bl)) + d(ah, bh)


def _unit_lower_inv_impl(ms):
    n = ms[0].shape[0]
    eye = (lax.broadcasted_iota(jnp.int32, (n, n), 0) == lax.broadcasted_iota(jnp.int32, (n, n), 1)).astype(F32)
    ps = [-m for m in ms]
    ts = [eye + p for p in ps]
    for _ in range(int(math.log2(n)) - 1):
        ps = [_dot3(p, p) for p in ps]
        ts = [t + _dot3(t, p) for t, p in zip(ts, ps)]
    return tuple(ts)


@jax.custom_vjp
def _unit_lower_inv(ms):
    return _unit_lower_inv_impl(ms)


def _unit_lower_inv_fwd(ms):
    ts = _unit_lower_inv_impl(ms)
    return ts, ts


def _unit_lower_inv_bwd(ts, ds):
    xs = [_bdot(t, d, TN) for t, d in zip(ts, ds)]
    return (tuple(-_bdot(x, t, NT) for x, t in zip(xs, ts)),)


_unit_lower_inv.defvjp(_unit_lower_inv_fwd, _unit_lower_inv_bwd)


GDN_SUB = 2
GDN_H = 4


def _gdn_step_fn(qs, ks, vs, zs, gs, S, alog_row, dtb_row, nw):
    n = GDN_CHUNK
    lanes = [(h, c) for h in range(GDN_H) for c in range(GDN_SUB)]
    z3 = lambda f, a, b, c_: [f(x, y, z) for x, y, z in zip(a, b, c_)]
    q = [_silu(qs[h][c]) for h, c in lanes]
    k = [_silu(ks[h][c]) for h, c in lanes]
    v = [_silu(vs[h][c]) for h, c in lanes]
    q = [t * lax.rsqrt(jnp.sum(t * t, axis=-1, keepdims=True) + EPS) * (128.0 ** -0.5) for t in q]
    k = [t * lax.rsqrt(jnp.sum(t * t, axis=-1, keepdims=True) + EPS) for t in k]
    beta = [jax.nn.sigmoid(_lane_pick(gs[c], LANE_GB + h)) for h, c in lanes]
    rate = [-jnp.exp(_lane_pick(alog_row, LANE_GA + h)) for h in range(GDN_H)]
    bias = [_lane_pick(dtb_row, LANE_GA + h) for h in range(GDN_H)]
    g = [rate[h] * _softplus(_lane_pick(gs[c], LANE_GA + h) + bias[h]) for h, c in lanes]
    cb = [jnp.broadcast_to(t, (n, n)) for t in g]
    gc_col = [_cumsum_col(t) for t in cb]
    gc_row = [_cumsum_row(t) for t in cb]
    tri_incl, tri_strict = _tri(n), _tri(n, strict=True)
    decay = [jnp.exp(jnp.where(tri_incl, a - b, NEG)) for a, b in zip(gc_col, gc_row)]
    gc = [t[:, 0:1] for t in gc_col]
    g_last = [jnp.sum(t, axis=0, keepdims=True) for t in g]
    kb = [a * b for a, b in zip(k, beta)]
    m = z3(lambda a, b, d: jnp.where(tri_strict, _bdot(a, b, NT) * d, 0.0), kb, k, decay)
    t_inv = _unit_lower_inv(tuple(m))
    egc = [jnp.exp(t) for t in gc]
    u = z3(lambda t, a, b: _bdot(t, a * b), t_inv, v, beta)
    w = z3(lambda t, a, e: _bdot(t, a * e), t_inv, kb, egc)
    attn = z3(lambda a, b, d: jnp.where(tri_incl, _bdot(a, b, NT) * d, 0.0), q, k, decay)
    qd = [a * e for a, e in zip(q, egc)]
    kd = z3(lambda a, gl, c_: a * jnp.exp(gl - c_), k, g_last, gc)
    egl = [jnp.exp(t) for t in g_last]
    zg = [_silu(zs[h][c]) for h, c in lanes]
    S = list(S)
    ys = [[None] * GDN_SUB for _ in range(GDN_H)]
    for c in range(GDN_SUB):
        ii = [h * GDN_SUB + c for h in range(GDN_H)]
        v_new = [u[i] - _bdot(w[i], S[h]) for h, i in enumerate(ii)]
        o = [_bdot(qd[i], S[h]) + _bdot(attn[i], v_new[h]) for h, i in enumerate(ii)]
        S = [S[h] * egl[i] + _bdot(kd[i], v_new[h], TN) for h, i in enumerate(ii)]
        for h, i in enumerate(ii):
            ys[h][c] = _rms(o[h], nw) * zg[i]
    return tuple(tuple(y) for y in ys), tuple(S)


def _gdn_step_args(qkv_ref, z_ref, g_ref):
    n = GDN_CHUNK
    rows = [slice(n * c, n * c + n) for c in range(GDN_SUB)]
    col = lambda ref, o: tuple(tuple(ref[r, o + 128 * h:o + 128 * h + 128] for r in rows) for h in range(GDN_H))
    return rows, (col(qkv_ref, 0), col(qkv_ref, 512), col(qkv_ref, 1024), col(z_ref, 0), tuple(g_ref[r, :] for r in rows))


def _carry(cargo, refs, step, nsteps):
    if cargo is None:
        return lambda: None
    start, mid, end = cargo.phases(*refs)
    pl.when(step == 0)(start)
    pl.when(step == (3 * nsteps) // 4)(mid)
    return lambda: pl.when(step == nsteps - 1)(end)


def _cargo_io(cargo):
    if cargo is None:
        return [], [], [], []
    hbm = pl.BlockSpec(memory_space=pl.ANY)
    return list(cargo.arrays), [hbm] * cargo.n_in, list(cargo.out_shapes), [hbm] * cargo.n_out


def _gdn_fwd(qkvc, proj, alog_row, dtb_row, nw, name, cargo=None):
    L = qkvc.shape[0]
    n = GDN_CHUNK * GDN_SUB
    Nc = L // n
    c_arr, c_ispec, c_oshape, c_ospec = _cargo_io(cargo)
    ni, no = len(c_arr), len(c_oshape)

    def body(*refs):
        qkv_ref, z_ref, g_ref, al_ref, db_ref, nw_ref = refs[:6]
        y_ref, sin_ref = refs[6 + ni:8 + ni]
        S = refs[8 + ni + no]
        step = pl.program_id(0)
        drain = _carry(cargo, (refs[6:6 + ni], refs[8 + ni:8 + ni + no], refs[9 + ni + no:]), step, Nc)

        @pl.when(step == 0)
        def _():
            S[...] = jnp.zeros_like(S)

        rows, args = _gdn_step_args(qkv_ref, z_ref, g_ref)
        states = tuple(S[h] for h in range(GDN_H))
        ys, sn = _gdn_step_fn(*args, states, al_ref[...], db_ref[...], nw_ref[...])
        for h in range(GDN_H):
            sin_ref[0, h] = states[h]
            for c in range(GDN_SUB):
                y_ref[rows[c], 128 * h:128 * h + 128] = ys[h][c].astype(BF16)
            S[h] = sn[h]
        drain()

    return pl.pallas_call(
        body, name=name, grid=(Nc,),
        in_specs=[pl.BlockSpec((n, 1536), lambda i: (i, 0)), pl.BlockSpec((n, 512), lambda i: (i, P_GZ // 512)),
                  pl.BlockSpec((n, 128), lambda i: (i, P_GATE // 128)), _vec_spec(128), _vec_spec(128), _vec_spec(128)]
        + c_ispec,
        out_specs=tuple([pl.BlockSpec((n, 512), lambda i: (i, 0)), pl.BlockSpec((1, 4, 128, 128), lambda i: (i, 0, 0, 0))]
                        + c_ospec),
        out_shape=tuple([jax.ShapeDtypeStruct((L, 512), BF16), jax.ShapeDtypeStruct((Nc, 4, 128, 128), F32)] + c_oshape),
        scratch_shapes=[pltpu.VMEM((4, 128, 128), F32)] + (cargo.scratch if cargo else []),
        compiler_params=_cparams(("arbitrary",), VMEM_BIG))(qkvc, proj, proj, alog_row, dtb_row, nw, *c_arr)


def _gdn_bwd(qkvc, proj, alog_row, dtb_row, nw, s_in, dycat, name, cargo=None):
    L = qkvc.shape[0]
    n = GDN_CHUNK * GDN_SUB
    Nc = L // n
    c_arr, c_ispec, c_oshape, c_ospec = _cargo_io(cargo)
    ni, no = len(c_arr), len(c_oshape)

    def body(*refs):
        qkv_ref, z_ref, g_ref, al_ref, db_ref, nw_ref, sin_ref, dy_ref = refs[:8]
        dqkv_ref, dz_ref, dg_ref, dal_ref, ddb_ref, dnw_ref = refs[8 + ni:14 + ni]
        dS = refs[14 + ni + no]
        step = pl.program_id(0)
        drain = _carry(cargo, (refs[8:8 + ni], refs[14 + ni:14 + ni + no], refs[15 + ni + no:]), step, Nc)

        @pl.when(step == 0)
        def _():
            dS[...] = jnp.zeros_like(dS)
            dal_ref[...] = jnp.zeros_like(dal_ref)
            ddb_ref[...] = jnp.zeros_like(ddb_ref)
            dnw_ref[...] = jnp.zeros_like(dnw_ref)

        rows, args = _gdn_step_args(qkv_ref, z_ref, g_ref)
        s_in = tuple(sin_ref[0, h] for h in range(GDN_H))
        _, vjp = jax.vjp(_gdn_step_fn, *args, s_in, al_ref[...], db_ref[...], nw_ref[...])
        dys = tuple(tuple(dy_ref[r, 128 * h:128 * h + 128] for r in rows) for h in range(GDN_H))
        dq, dk, dv, dz, dgt, ds, dal, ddb, dnw = vjp((dys, tuple(dS[h] for h in range(GDN_H))))
        for h in range(GDN_H):
            for c in range(GDN_SUB):
                dqkv_ref[rows[c], 128 * h:128 * h + 128] = dq[h][c]
                dqkv_ref[rows[c], 512 + 128 * h:640 + 128 * h] = dk[h][c]
                dqkv_ref[rows[c], 1024 + 128 * h:1152 + 128 * h] = dv[h][c]
                dz_ref[rows[c], 128 * h:128 * h + 128] = dz[h][c].astype(BF16)
            dS[h] = ds[h]
        for c in range(GDN_SUB):
            dg_ref[rows[c], :] = dgt[c]
        dal_ref[...] += dal
        ddb_ref[...] += ddb
        dnw_ref[...] += dnw
        drain()

    rev = lambda i: Nc - 1 - i
    vec = jax.ShapeDtypeStruct((1, 128), F32)
    return pl.pallas_call(
        body, name=name, grid=(Nc,),
        in_specs=[pl.BlockSpec((n, 1536), lambda i: (rev(i), 0)), pl.BlockSpec((n, 512), lambda i: (rev(i), P_GZ // 512)),
                  pl.BlockSpec((n, 128), lambda i: (rev(i), P_GATE // 128)), _vec_spec(128), _vec_spec(128), _vec_spec(128),
                  pl.BlockSpec((1, 4, 128, 128), lambda i: (rev(i), 0, 0, 0)), pl.BlockSpec((n, 512), lambda i: (rev(i), 0))]
        + c_ispec,
        out_specs=tuple([pl.BlockSpec((n, 1536), lambda i: (rev(i), 0)), pl.BlockSpec((n, 512), lambda i: (rev(i), 0)),
                         pl.BlockSpec((n, 128), lambda i: (rev(i), 0)), _vec_spec(128), _vec_spec(128), _vec_spec(128)]
                        + c_ospec),
        out_shape=tuple([jax.ShapeDtypeStruct((L, 1536), F32), jax.ShapeDtypeStruct((L, 512), BF16),
                         jax.ShapeDtypeStruct((L, 128), F32), vec, vec, vec] + c_oshape),
        scratch_shapes=[pltpu.VMEM((4, 128, 128), F32)] + (cargo.scratch if cargo else []),
        compiler_params=_cparams(("arbitrary",), VMEM_BIG))(qkvc, proj, proj, alog_row, dtb_row, nw, s_in, dycat, *c_arr)


def _ssd_chunk_fn(xs, bs, cs, zs, gates, st, alog_row, dtb_row, d_row, nws):
    n = SSM_CHUNK
    tri = _tri(n)
    lane = lax.broadcasted_iota(jnp.int32, (1, 128), 1)
    lo = lane < 64
    xs = [_silu(t) for t in xs]
    bs = [_silu(t) for t in bs]
    cs = [_silu(t) for t in cs]
    cb = [_bdot(cs[g], bs[g], NT) for g in range(2)]
    H = range(8)
    P = range(4)
    dt = [_softplus(_lane_pick(gates, LANE_SDT + h) + _lane_pick(dtb_row, LANE_SDT + h)) for h in H]
    da = [dt[h] * (-jnp.exp(_lane_pick(alog_row, LANE_SDT + h))) for h in H]
    dab = [jnp.broadcast_to(t, (n, n)) for t in da]
    cs_col = [_cumsum_col(t) for t in dab]
    cs_row = [_cumsum_row(t) for t in dab]
    lmat = [jnp.exp(jnp.where(tri, a - b, NEG)) for a, b in zip(cs_col, cs_row)]
    xdt_h = [jnp.where(lo if h % 2 == 0 else jnp.logical_not(lo), xs[h // 2] * dt[h], 0.0) for h in H]
    yd = [_bdot(cb[h // 4] * lmat[h], xdt_h[h]) for h in H]
    tot = [jnp.sum(t, axis=0, keepdims=True) for t in da]
    dsk = [_lane_pick(d_row, LANE_SDT + h) for h in H]
    cs_lane = [jnp.where(lo, cs_col[2 * p], cs_col[2 * p + 1]) for p in P]
    dt_lane = [jnp.where(lo, dt[2 * p], dt[2 * p + 1]) for p in P]
    tot_lane = [jnp.where(lo, tot[2 * p], tot[2 * p + 1]) for p in P]
    dsk_lane = [jnp.where(lo, dsk[2 * p], dsk[2 * p + 1]) for p in P]
    xdt = [xs[p] * dt_lane[p] for p in P]
    upd = [_bdot(bs[p // 2], xdt[p] * jnp.exp(tot_lane[p] - cs_lane[p]), TN) for p in P]
    st_new = [st[p] * jnp.exp(tot_lane[p]) + upd[p] for p in P]
    yoff = [_bdot(cs[p // 2], st[p]) * jnp.exp(cs_lane[p]) for p in P]
    zg = [_silu(zs[p]) for p in P]
    ys = [(yd[2 * p] + yd[2 * p + 1] + yoff[p] + xs[p] * dsk_lane[p]) * zg[p] for p in P]
    out = []
    for g in range(2):
        ms = (jnp.sum(ys[2 * g] ** 2, axis=-1, keepdims=True) + jnp.sum(ys[2 * g + 1] ** 2, axis=-1, keepdims=True)) / 256.0
        rstd = lax.rsqrt(ms + EPS)
        out += [ys[2 * g] * rstd * nws[2 * g], ys[2 * g + 1] * rstd * nws[2 * g + 1]]
    return tuple(out), tuple(st_new)


def _ssd_args(xbc_ref, z_ref, nw_ref):
    xs = [xbc_ref[:, 128 * p:128 * p + 128] for p in range(4)]
    bs = [xbc_ref[:, 512 + 128 * g:640 + 128 * g] for g in range(2)]
    cs = [xbc_ref[:, 768 + 128 * g:896 + 128 * g] for g in range(2)]
    zs = [z_ref[:, 128 * p:128 * p + 128] for p in range(4)]
    nws = [nw_ref[:, 128 * p:128 * p + 128] for p in range(4)]
    return xs, bs, cs, zs, nws


def _ssd_fwd(xbcc, proj, alog_row, dtb_row, d_row, nw, name):
    L = xbcc.shape[0]
    n = SSM_CHUNK
    Nc = L // n

    def body(xbc_ref, z_ref, g_ref, al_ref, db_ref, d_ref, nw_ref, y_ref, sin_ref, S):
        @pl.when(pl.program_id(0) == 0)
        def _():
            S[...] = jnp.zeros_like(S)

        xs, bs, cs, zs, nws = _ssd_args(xbc_ref, z_ref, nw_ref)
        st = [S[p] for p in range(4)]
        sin_ref[0] = S[...]
        ys, sn = _ssd_chunk_fn(xs, bs, cs, zs, g_ref[...], st, al_ref[...], db_ref[...], d_ref[...], nws)
        for p in range(4):
            y_ref[:, 128 * p:128 * p + 128] = ys[p].astype(BF16)
            S[p] = sn[p]

    return pl.pallas_call(
        body, name=name, grid=(Nc,),
        in_specs=[pl.BlockSpec((n, 1024), lambda i: (i, 0)), pl.BlockSpec((n, 512), lambda i: (i, P_SZ // 512)),
                  pl.BlockSpec((n, 128), lambda i: (i, P_GATE // 128)), _vec_spec(128), _vec_spec(128), _vec_spec(128),
                  _vec_spec(512)],
        out_specs=(pl.BlockSpec((n, 512), lambda i: (i, 0)), pl.BlockSpec((1, 4, 128, 128), lambda i: (i, 0, 0, 0))),
        out_shape=(jax.ShapeDtypeStruct((L, 512), BF16), jax.ShapeDtypeStruct((Nc, 4, 128, 128), F32)),
        scratch_shapes=[pltpu.VMEM((4, 128, 128), F32)],
        compiler_params=_cparams(("arbitrary",), VMEM_BIG))(xbcc, proj, proj, alog_row, dtb_row, d_row, nw)


def _ssd_bwd(xbcc, proj, alog_row, dtb_row, d_row, nw, s_in, dycat, name):
    L = xbcc.shape[0]
    n = SSM_CHUNK
    Nc = L // n

    def body(xbc_ref, z_ref, g_ref, al_ref, db_ref, d_ref, nw_ref, sin_ref, dy_ref,
             dxbc_ref, dz_ref, dg_ref, dal_ref, ddb_ref, dd_ref, dnw_ref, dS):
        @pl.when(pl.program_id(0) == 0)
        def _():
            dS[...] = jnp.zeros_like(dS)
            dal_ref[...] = jnp.zeros_like(dal_ref)
            ddb_ref[...] = jnp.zeros_like(ddb_ref)
            dd_ref[...] = jnp.zeros_like(dd_ref)
            dnw_ref[...] = jnp.zeros_like(dnw_ref)

        xs, bs, cs, zs, nws = _ssd_args(xbc_ref, z_ref, nw_ref)
        st = [sin_ref[0, p] for p in range(4)]
        _, vjp = jax.vjp(_ssd_chunk_fn, xs, bs, cs, zs, g_ref[...], st, al_ref[...], db_ref[...], d_ref[...], nws)
        dys = tuple(dy_ref[:, 128 * p:128 * p + 128] for p in range(4))
        dxs, dbs, dcs, dzs, dgt, dst, dal, ddb, dd, dnws = vjp((dys, tuple(dS[p] for p in range(4))))
        for p in range(4):
            dxbc_ref[:, 128 * p:128 * p + 128] = dxs[p]
            dz_ref[:, 128 * p:128 * p + 128] = dzs[p].astype(BF16)
            dS[p] = dst[p]
            dnw_ref[:, 128 * p:128 * p + 128] += dnws[p]
        for g in range(2):
            dxbc_ref[:, 512 + 128 * g:640 + 128 * g] = dbs[g]
            dxbc_ref[:, 768 + 128 * g:896 + 128 * g] = dcs[g]
        dg_ref[...] = dgt
        dal_ref[...] += dal
        ddb_ref[...] += ddb
        dd_ref[...] += dd

    rev = lambda i: Nc - 1 - i
    vec = jax.ShapeDtypeStruct((1, 128), F32)
    return pl.pallas_call(
        body, name=name, grid=(Nc,),
        in_specs=[pl.BlockSpec((n, 1024), lambda i: (rev(i), 0)), pl.BlockSpec((n, 512), lambda i: (rev(i), P_SZ // 512)),
                  pl.BlockSpec((n, 128), lambda i: (rev(i), P_GATE // 128)), _vec_spec(128), _vec_spec(128), _vec_spec(128),
                  _vec_spec(512), pl.BlockSpec((1, 4, 128, 128), lambda i: (rev(i), 0, 0, 0)),
                  pl.BlockSpec((n, 512), lambda i: (rev(i), 1))],
        out_specs=(pl.BlockSpec((n, 1024), lambda i: (rev(i), 0)), pl.BlockSpec((n, 512), lambda i: (rev(i), 0)),
                   pl.BlockSpec((n, 128), lambda i: (rev(i), 0)), _vec_spec(128), _vec_spec(128), _vec_spec(128),
                   _vec_spec(512)),
        out_shape=(jax.ShapeDtypeStruct((L, 1024), F32), jax.ShapeDtypeStruct((L, 512), BF16),
                   jax.ShapeDtypeStruct((L, 128), F32), vec, vec, vec, jax.ShapeDtypeStruct((1, 512), F32)),
        scratch_shapes=[pltpu.VMEM((4, 128, 128), F32)],
        compiler_params=_cparams(("arbitrary",), VMEM_BIG))(xbcc, proj, proj, alog_row, dtb_row, d_row, nw, s_in, dycat)


def _rot(x):
    W = x.shape[1]
    lane = lax.broadcasted_iota(jnp.int32, (1, W), 1)
    first = (lane % 64) < 32
    return jnp.where(first, -pltpu.roll(x, W - 32, 1), pltpu.roll(x, 32, 1))


def _rope(x, cos, sin):
    return x * cos + _rot(x) * sin


def _rope_t(d, cos, sin):
    return d * cos - _rot(d * sin)


def _swa_core_fn(qs, ks, vs, sink_row, mask):
    lane = lax.broadcasted_iota(jnp.int32, (1, 128), 1)
    lo = lane < 64
    H = range(8)
    qm = [jnp.where(lo if h % 2 == 0 else jnp.logical_not(lo), qs[h // 2], 0.0) for h in H]
    s = [jnp.where(mask, _bdot(qm[h], ks[h // 4], NT) * 0.125, NEG) for h in H]
    sk = [_lane_pick(sink_row, h) for h in H]
    mx = [lax.stop_gradient(jnp.maximum(jnp.max(s[h], axis=-1, keepdims=True), sk[h])) for h in H]
    pr = [jnp.exp(s[h] - mx[h]) for h in H]
    den = [jnp.sum(pr[h], axis=-1, keepdims=True) + jnp.exp(sk[h] - mx[h]) for h in H]
    pr = [pr[h] / den[h] for h in H]
    ov = [_bdot(pr[h], vs[h // 4]) for h in H]
    return tuple(jnp.where(lo, ov[2 * p], ov[2 * p + 1]) for p in range(4))


def _swa_prepare(blk, q_ref, kc_ref, kp_ref, vc_ref, vp_ref, cc_ref, sc_ref, cp_ref, sp_ref):
    W = WINDOW
    cos_c, sin_c = cc_ref[...], sc_ref[...]
    cos_b = jnp.concatenate([cp_ref[...], cos_c], axis=0)
    sin_b = jnp.concatenate([sp_ref[...], sin_c], axis=0)
    cos_q, sin_q = jnp.concatenate([cos_c] * 4, axis=1), jnp.concatenate([sin_c] * 4, axis=1)
    cos_k, sin_k = jnp.concatenate([cos_b] * 2, axis=1), jnp.concatenate([sin_b] * 2, axis=1)
    qr = _rope(q_ref[...], cos_q, sin_q)
    kr = _rope(jnp.concatenate([kp_ref[...], kc_ref[...]], axis=0), cos_k, sin_k)
    vb = jnp.concatenate([vp_ref[...], vc_ref[...]], axis=0)
    qi = lax.broadcasted_iota(jnp.int32, (W, 2 * W), 0)
    kj = lax.broadcasted_iota(jnp.int32, (W, 2 * W), 1)
    rel = qi + W - kj
    mask = (rel >= 0) & (rel < W) & ((blk > 0) | (kj >= W))
    qs = [qr[:, 128 * p:128 * p + 128] for p in range(4)]
    ks = [kr[:, 128 * g:128 * g + 128] for g in range(2)]
    vs = [vb[:, 128 * g:128 * g + 128] for g in range(2)]
    return qs, ks, vs, mask, (cos_q, sin_q, cos_k, sin_k)


def _swa_specs(nb, order):
    W = WINDOW
    cur = lambda i: order(i)
    prv = lambda i: jnp.maximum(order(i) - 1, 0)
    return [pl.BlockSpec((W, 512), lambda i: (cur(i), P_AQ // 512)),
            pl.BlockSpec((W, 256), lambda i: (cur(i), P_AK // 256)), pl.BlockSpec((W, 256), lambda i: (prv(i), P_AK // 256)),
            pl.BlockSpec((W, 256), lambda i: (cur(i), P_AV // 256)), pl.BlockSpec((W, 256), lambda i: (prv(i), P_AV // 256)),
            pl.BlockSpec((W, 128), lambda i: (cur(i), 0)), pl.BlockSpec((W, 128), lambda i: (cur(i), 0)),
            pl.BlockSpec((W, 128), lambda i: (prv(i), 0)), pl.BlockSpec((W, 128), lambda i: (prv(i), 0)),
            _vec_spec(128)]


def _swa_fwd(proj, cos, sin, sink_row, name):
    L = proj.shape[0]
    W = WINDOW
    nb = L // W

    def body(q_ref, kc_ref, kp_ref, vc_ref, vp_ref, cc_ref, sc_ref, cp_ref, sp_ref, sk_ref, y_ref):
        blk = pl.program_id(0)
        qs, ks, vs, mask, _ = _swa_prepare(blk, q_ref, kc_ref, kp_ref, vc_ref, vp_ref, cc_ref, sc_ref, cp_ref, sp_ref)
        outs = _swa_core_fn(qs, ks, vs, sk_ref[...], mask)
        for p in range(4):
            y_ref[:, 128 * p:128 * p + 128] = outs[p].astype(BF16)

    return pl.pallas_call(
        body, name=name, grid=(nb,), in_specs=_swa_specs(nb, lambda i: i),
        out_specs=pl.BlockSpec((W, 512), lambda i: (i, 0)), out_shape=jax.ShapeDtypeStruct((L, 512), BF16),
        compiler_params=_cparams(("parallel",), VMEM_BIG))(proj, proj, proj, proj, proj, cos, sin, cos, sin, sink_row)


def _swa_bwd(proj, cos, sin, sink_row, dycat, name):
    L = proj.shape[0]
    W = WINDOW
    nb = L // W
    rev = lambda i: nb - 1 - i

    def body(q_ref, kc_ref, kp_ref, vc_ref, vp_ref, cc_ref, sc_ref, cp_ref, sp_ref, sk_ref, dy_ref,
             dq_ref, dk_ref, dv_ref, dsk_ref, ck, cv):
        i = pl.program_id(0)
        blk = nb - 1 - i

        @pl.when(i == 0)
        def _():
            ck[...] = jnp.zeros_like(ck)
            cv[...] = jnp.zeros_like(cv)
            dsk_ref[...] = jnp.zeros_like(dsk_ref)

        qs, ks, vs, mask, (cos_q, sin_q, cos_k, sin_k) = _swa_prepare(
            blk, q_ref, kc_ref, kp_ref, vc_ref, vp_ref, cc_ref, sc_ref, cp_ref, sp_ref)
        _, vjp = jax.vjp(functools.partial(_swa_core_fn, mask=mask), qs, ks, vs, sk_ref[...])
        dqs, dks, dvs, dsk = vjp(tuple(dy_ref[:, 128 * p:128 * p + 128] for p in range(4)))
        dq_ref[...] = _rope_t(jnp.concatenate(dqs, axis=1), cos_q, sin_q).astype(BF16)
        dkb = _rope_t(jnp.concatenate(dks, axis=1), cos_k, sin_k)
        dvb = jnp.concatenate(dvs, axis=1)
        dk_ref[...] = (dkb[W:2 * W] + ck[...]).astype(BF16)
        dv_ref[...] = (dvb[W:2 * W] + cv[...]).astype(BF16)
        ck[...] = dkb[0:W]
        cv[...] = dvb[0:W]
        dsk_ref[...] += dsk

    return pl.pallas_call(
        body, name=name, grid=(nb,),
        in_specs=_swa_specs(nb, rev) + [pl.BlockSpec((W, 512), lambda i: (rev(i), 2))],
        out_specs=(pl.BlockSpec((W, 512), lambda i: (rev(i), 0)), pl.BlockSpec((W, 256), lambda i: (rev(i), 0)),
                   pl.BlockSpec((W, 256), lambda i: (rev(i), 0)), _vec_spec(128)),
        out_shape=(jax.ShapeDtypeStruct((L, 512), BF16), jax.ShapeDtypeStruct((L, 256), BF16),
                   jax.ShapeDtypeStruct((L, 256), BF16), jax.ShapeDtypeStruct((1, 128), F32)),
        scratch_shapes=[pltpu.VMEM((W, 256), F32), pltpu.VMEM((W, 256), F32)],
        compiler_params=_cparams(("arbitrary",), VMEM_BIG))(
            proj, proj, proj, proj, proj, cos, sin, cos, sin, sink_row, dycat)


def _adamw(w, g, m, v):
    m = ADAM_B1 * m + (1.0 - ADAM_B1) * g
    v = ADAM_B2 * v + (1.0 - ADAM_B2) * (g * g)
    m_hat = m / (1.0 - ADAM_B1 ** ADAM_STEP)
    v_hat = v / (1.0 - ADAM_B2 ** ADAM_STEP)
    delta = -ADAM_LR * (m_hat / (jnp.sqrt(v_hat) + ADAM_EPS) + ADAM_WD * w)
    return delta, m, v


def _ada_fwd(c_all, ada_w, ada_b_sh, name):
    S = ada_w.shape[2]

    def body(c_ref, w_ref, b_ref, o_ref):
        o_ref[0] = _bdot(_silu(c_ref[...]), w_ref[0]) + b_ref[0]

    return pl.pallas_call(
        body, name=name, grid=(2,),
        in_specs=[pl.BlockSpec((N_DEV, D), lambda i: (0, 0)), pl.BlockSpec((1, D, S), lambda i: (i, 0, 0)),
                  pl.BlockSpec((1, 1, S), lambda i: (i, 0, 0))],
        out_specs=pl.BlockSpec((1, N_DEV, S), lambda i: (i, 0, 0)), out_shape=jax.ShapeDtypeStruct((2, N_DEV, S), F32),
        compiler_params=_cparams(("parallel",), VMEM_BIG))(c_all, ada_w, ada_b_sh.reshape(2, 1, S))


def _ada_bwd_adamw(c_all, dmod_sh, w, m, v, name):
    S = w.shape[2]
    T = 256

    def body(c_ref, d_ref, w_ref, m_ref, v_ref, g_ref, dl_ref, nm_ref, nv_ref):
        g = _bdot(_silu(c_ref[...]), d_ref[0], TN)
        dl, nm, nv = _adamw(w_ref[0], g, m_ref[0], v_ref[0])
        g_ref[0], dl_ref[0], nm_ref[0], nv_ref[0] = g, dl, nm, nv

    blk = pl.BlockSpec((1, T, S), lambda i, j: (i, j, 0))
    sds = jax.ShapeDtypeStruct(w.shape, F32)
    return pl.pallas_call(
        body, name=name, grid=(2, D // T),
        in_specs=[pl.BlockSpec((N_DEV, T), lambda i, j: (0, j)), pl.BlockSpec((1, N_DEV, S), lambda i, j: (i, 0, 0)),
                  blk, blk, blk],
        out_specs=(blk, blk, blk, blk), out_shape=(sds, sds, sds, sds),
        compiler_params=_cparams(("parallel", "parallel"), VMEM_BIG))(c_all, dmod_sh, w, m, v)


def _sum8(r, name):
    _, R, W = r.shape
    T = R
    for cand in (2496, 2048, 1024, 512, 256, 128, 64, 32, 16, 8):
        if R % cand == 0:
            T = cand
            break

    def body(r_ref, o_ref):
        acc = r_ref[0].astype(F32)
        for d in range(1, N_DEV):
            acc = acc + r_ref[d].astype(F32)
        o_ref[...] = acc

    return pl.pallas_call(
        body, name=name, grid=(R // T,), in_specs=[pl.BlockSpec((N_DEV, T, W), lambda i: (0, i, 0))],
        out_specs=pl.BlockSpec((T, W), lambda i: (i, 0)), out_shape=jax.ShapeDtypeStruct((R, W), F32),
        compiler_params=_cparams(("parallel",), VMEM_BIG))(r)


def _adamw_call(w, g, m, v, name):
    R, W = w.shape
    T = R
    for cand in (1024, 512, 256, 128, 64, 32, 16, 8):
        if R % cand == 0 and R > cand and cand * W * 4 <= 2 * 1024 * 1024:
            T = cand
            break

    def body(w_ref, g_ref, m_ref, v_ref, dl_ref, nm_ref, nv_ref):
        dl_ref[...], nm_ref[...], nv_ref[...] = _adamw(w_ref[...], g_ref[...], m_ref[...], v_ref[...])

    blk = pl.BlockSpec((T, W), lambda i: (i, 0))
    sds = jax.ShapeDtypeStruct((R, W), F32)
    return pl.pallas_call(
        body, name=name, grid=(R // T,), in_specs=[blk, blk, blk, blk], out_specs=(blk, blk, blk),
        out_shape=(sds, sds, sds), compiler_params=_cparams(("parallel",), VMEM_BIG))(w, g, m, v)


def _sum8_adamw(r0, r1, w, m, v, name):
    _, _, R, W = r0.shape
    T = (256 * 1024) // W
    assert R % T == 0, (R, W, T)
    nj = R // T

    def body(r0_ref, r1_ref, w_ref, m_ref, v_ref, g_ref, dl_ref, nm_ref, nv_ref):
        def run(r_ref):
            g = r_ref[0].astype(F32)
            for d in range(1, N_DEV):
                g = g + r_ref[d].astype(F32)
            g_ref[...] = g
            dl_ref[...], nm_ref[...], nv_ref[...] = _adamw(w_ref[...], g, m_ref[...], v_ref[...])

        pl.when(pl.program_id(0) == 0)(lambda: run(r0_ref))
        pl.when(pl.program_id(0) == 1)(lambda: run(r1_ref))

    r0_spec = pl.BlockSpec((N_DEV, None, T, W), lambda i, j: (0, 0, jnp.where(i == 0, j, nj - 1), 0))
    r1_spec = pl.BlockSpec((N_DEV, None, T, W), lambda i, j: (0, 0, jnp.where(i == 1, j, 0), 0))
    blk = pl.BlockSpec((None, T, W), lambda i, j: (i, j, 0))
    sds = jax.ShapeDtypeStruct((2, R, W), F32)
    return pl.pallas_call(
        body, name=name, grid=(2, nj), in_specs=[r0_spec, r1_spec, blk, blk, blk],
        out_specs=(blk, blk, blk, blk), out_shape=(sds, sds, sds, sds),
        compiler_params=_cparams(("arbitrary", "arbitrary"), VMEM_BIG))(r0, r1, w, m, v)


def _permute_rows(wt):
    z = lambda n: jnp.zeros((n, wt.shape[1]), wt.dtype)
    s = lambda o, n: wt[o:o + n]
    kd = [s(O_AK, 64), s(O_AK, 64), s(O_AK + 64, 64), s(O_AK + 64, 64)]
    vd = [s(O_AV, 64), s(O_AV, 64), s(O_AV + 64, 64), s(O_AV + 64, 64)]
    return jnp.concatenate(
        [s(O_GQ, 1536), s(O_GZ, 512), s(O_SXBC, 1024), s(O_SZ, 512), s(O_AQ, 512), s(O_CB, 1536)] + kd + vd
        + [s(O_GB, 4), s(O_GA, 4), s(O_SDT, 8), z(112)], axis=0)


def _unpermute_rows(g):
    s = lambda o, n: g[o:o + n]
    add = lambda o: (s(o, 64).astype(F32) + s(o + 64, 64).astype(F32)).astype(g.dtype)
    dk = [add(P_AK), add(P_AK + 128)]
    dv = [add(P_AV), add(P_AV + 128)]
    return jnp.concatenate(
        [s(P_QKV, 1536), s(P_GZ, 512), s(P_GATE, 8), s(P_SZ, 512), s(P_SXBC, 1024), s(P_GATE + 8, 8), s(P_AQ, 512)]
        + dk + dv + [s(P_CB, 1536)], axis=0)


def _row128(vals, lane0):
    n = vals.shape[0]
    return jnp.concatenate([jnp.zeros((lane0,), F32), vals, jnp.zeros((128 - lane0 - n,), F32)]).reshape(1, 128)


def _pad_rows128(flat):
    n = flat.shape[0]
    pad = (-n) % 1024
    return jnp.concatenate([flat, jnp.zeros((pad,), flat.dtype)]).reshape(-1, 128)


def kernel(x, c, positions, ada_w, ada_b, norm_pre_mix, norm_post_mix, norm_pre_mlp, norm_post_mlp, w_in, w_out, gdn_conv_w, gdn_a_log, gdn_dt_bias, gdn_norm_w, ssm_conv_w, ssm_conv_b, ssm_a_log, ssm_dt_bias, ssm_d, ssm_norm_w, attn_sinks, sc_conv_w, w_up, w_down, loss_target, m_ada_w, m_ada_b, m_norm_pre_mix, m_norm_post_mix, m_norm_pre_mlp, m_norm_post_mlp, m_w_in, m_w_out, m_gdn_conv_w, m_gdn_a_log, m_gdn_dt_bias, m_gdn_norm_w, m_ssm_conv_w, m_ssm_conv_b, m_ssm_a_log, m_ssm_dt_bias, m_ssm_d, m_ssm_norm_w, m_attn_sinks, m_sc_conv_w, m_w_up, m_w_down, v_ada_w, v_ada_b, v_norm_pre_mix, v_norm_post_mix, v_norm_pre_mlp, v_norm_post_mlp, v_w_in, v_w_out, v_gdn_conv_w, v_gdn_a_log, v_gdn_dt_bias, v_gdn_norm_w, v_ssm_conv_w, v_ssm_conv_b, v_ssm_a_log, v_ssm_dt_bias, v_ssm_d, v_ssm_norm_w, v_attn_sinks, v_sc_conv_w, v_w_up, v_w_down):
    L = x.shape[1]
    me = 4 * lax.axis_index("x") + 2 * lax.axis_index("y") + lax.axis_index("c")
    x0 = x[0]
    target = loss_target[0]
    S_ADA = ada_w.shape[2]

    small = jnp.concatenate([c.reshape(-1), gdn_conv_w.reshape(-1), ssm_conv_w.reshape(-1), sc_conv_w.reshape(-1)])
    n_small = small.shape[0]
    g_small = _ag_small(_pad_rows128(small), "ag_c_conv").reshape(N_DEV, -1)[:, :n_small]
    c_all = g_small[:, :D]
    o = D
    gdn_cw = g_small[:, o:o + 2 * 4 * 192].reshape(N_DEV, 2, 4, 192).transpose(1, 2, 0, 3).reshape(2, 4, 1536)
    o += 2 * 4 * 192
    ssm_cw = g_small[:, o:o + 2 * 4 * 128].reshape(N_DEV, 2, 4, 128).transpose(1, 2, 0, 3).reshape(2, 4, 1024)
    o += 2 * 4 * 128
    sc_cw = g_small[:, o:o + 2 * 3 * 64].reshape(N_DEV, 2, 3, 64).transpose(1, 2, 0, 3).reshape(2, 3, 512)

    ada_b_sh = lax.dynamic_slice(ada_b, (0, me * S_ADA), (2, S_ADA))
    mod_sh = _ada_fwd(c_all, ada_w, ada_b_sh, "ada_fwd")
    mod_all = _ag_small(mod_sh.reshape(-1, 128), "ag_mod").reshape(N_DEV, 2, N_DEV, S_ADA)
    mod_me = lax.dynamic_index_in_dim(mod_all, me, axis=2, keepdims=False)
    mod_me = mod_me.transpose(1, 0, 2).reshape(2, 6 * D)

    wt_sh = [jnp.pad(w_in[i].T.astype(BF16), ((0, SHARD_PAD - SHARD_IN), (0, 0))) for i in range(2)]
    wo_sh, wu_sh, wd_sh = ([w[i].astype(BF16) for i in range(2)] for w in (w_out, w_up, w_down))
    full_w_in = lambda g: _permute_rows(g[:, :SHARD_IN].reshape(IN_W, D))
    win_t = [full_w_in(_ag_multi([wt_sh[0]], "ag_w_in0")[0]), None]
    wout_f, wup_g, wdown_f = [None, None], [None, None], [None, None]

    inv_freq = ROPE_THETA ** (-jnp.arange(0, 64, 2, dtype=F32) / 64)
    ang = positions[0].astype(F32)[:, None] * inv_freq
    cos = jnp.tile(jnp.cos(ang), (1, 4))
    sin = jnp.tile(jnp.sin(ang), (1, 4))

    vec = lambda a: a.reshape(1, -1)

    saved = []
    xc = x0
    for i in range(2):
        sh_a, sc_a, gt_a, sh_m, sc_m, gt_m = [vec(t) for t in jnp.split(mod_me[i], 6)]
        gal, gdb = _row128(gdn_a_log[i], LANE_GA), _row128(gdn_dt_bias[i], LANE_GA)
        sal, sdb, sdd = (_row128(ssm_a_log[i], LANE_SDT), _row128(ssm_dt_bias[i], LANE_SDT), _row128(ssm_d[i], LANE_SDT))
        gnw, snw, skr = vec(gdn_norm_w[i]), vec(ssm_norm_w[i]), _row128(attn_sinks[i], 0)
        zero_b = jnp.zeros((1, 1536), F32)

        h = _pre_fwd(xc, vec(norm_pre_mix[i]), sc_a, sh_a, f"pre_mix_fwd{i}")
        riders = [wo_sh[0], wu_sh[0]] if i == 0 else [wd_sh[1]]
        proj, *got = _mm(h, win_t[i], name=f"in_proj{i}", tb=True, tn=896, tk=2048, cargo=_AgCargo(riders))
        if i == 0:
            wout_f[0], wup_g[0] = got[0].reshape(D, D), got[1]
        else:
            wdown_f[1] = got[0].reshape(MLP, D)
        qkvc = _conv_fwd(proj, 0, 1536, gdn_cw[i], zero_b, f"gdn_conv_fwd{i}")
        if i == 0:
            ya, gdn_s, got_i = _gdn_fwd(qkvc, proj, gal, gdb, gnw, "gdn_fwd0", cargo=_AgCargo([wt_sh[1]]))
            win_t[1] = full_w_in(got_i)
        else:
            ya, gdn_s = _gdn_fwd(qkvc, proj, gal, gdb, gnw, "gdn_fwd1")
        xbcc = _conv_fwd(proj, P_SXBC // 1024, 1024, ssm_cw[i], vec(ssm_conv_b[i]), f"ssm_conv_fwd{i}")
        yb, ssd_s = _ssd_fwd(xbcc, proj, sal, sdb, sdd, snw, f"ssd_fwd{i}")
        yc = _swa_fwd(proj, cos, sin, skr, f"swa_fwd{i}")
        yd = _sc_fwd(proj, sc_cw[i], f"sc_fwd{i}")
        ycat = jnp.concatenate([ya, yb, yc, yd], axis=1)
        out = _mm(ycat, wout_f[i], name=f"out_proj{i}", tk=2048)
        x1, h2 = _post_pre_fwd(xc, out, vec(norm_post_mix[i]), gt_a, vec(norm_pre_mlp[i]), sc_m, sh_m, f"post_mix_pre_mlp_fwd{i}")
        if i == 0:
            act, u, got_d, got_o = _mm(h2, wup_g[0], name="up_proj0", tk=2048, mode="relu2", shards="b",
                                       cargo=_AgCargo([wd_sh[0], wo_sh[1]]))
            wdown_f[0], wout_f[1] = got_d.reshape(MLP, D), got_o.reshape(D, D)
            y2, wup_g[1] = _mm(act, wdown_f[0], name="down_proj0", tk=2048, cargo=_AgCargo([wu_sh[1]]))
        else:
            act, u = _mm(h2, wup_g[1], name="up_proj1", tk=2048, mode="relu2", shards="b")
            y2 = _mm(act, wdown_f[1], name="down_proj1", tk=2048)
        x2 = _post_fwd(x1, y2, vec(norm_post_mlp[i]), gt_m, f"post_mlp_fwd{i}")
        saved.append(dict(x0=xc, h=h, proj=proj, qkvc=qkvc, gdn_s=gdn_s, xbcc=xbcc, ssd_s=ssd_s, ycat=ycat, out=out,
                          x1=x1, h2=h2, act=act, u=u, y2=y2, mods=(sh_a, sc_a, gt_a, sh_m, sc_m, gt_m),
                          rows=(gal, gdb, sal, sdb, sdd, gnw, snw, skr)))
        xc = x2

    sq, dx = _loss_head(xc, target, "loss_head")
    loss = lax.psum(0.5 * sq[0, 0] / float(D), ("x", "y", "c"))

    gw = [dict() for _ in range(2)]
    r_in, r_out, r_up, r_down = [None, None], [None, None], [None, None], [None, None]
    for i in (1, 0):
        sv = saved[i]
        sh_a, sc_a, gt_a, sh_m, sc_m, gt_m = sv["mods"]
        gal, gdb, sal, sdb, sdd, gnw, snw, skr = sv["rows"]
        proj = sv["proj"]
        dy2, g_npostmlp, d_gt_m = _post_bwd(sv["y2"], vec(norm_post_mlp[i]), gt_m, dx, f"post_mlp_bwd{i}")
        du = _mm(dy2, wdown_f[i], name=f"down_proj_dx{i}", tb=True, tk=2048, out_dtype=BF16, mode="drelu2", u=sv["u"])
        g_wdown = _mm(sv["act"], dy2, name=f"down_proj_dw{i}", ta=True, out_dtype=BF16).reshape(N_DEV, MLP // N_DEV, D)
        dh2, r_down[i] = _mm(du, wup_g[i], name=f"up_proj_dx{i}", tb=True, tk=1024, out_dtype=BF16, shards="b",
                             cargo=_ExCargo([[g_wdown]]))
        g_wup = _mm(sv["h2"], du, name=f"up_proj_dw{i}", ta=True, out_dtype=BF16, shards="o")
        dx1, dout, g_npremlp, d_sc_m, d_sh_m, g_npostmix, d_gt_a = _pre_post_bwd(
            sv["x1"], vec(norm_pre_mlp[i]), sc_m, sh_m, dh2, dx, sv["out"], vec(norm_post_mix[i]), gt_a,
            f"pre_mlp_post_mix_bwd{i}")
        dycat = _mm(dout, wout_f[i], name=f"out_proj_dx{i}", tb=True, tk=2048)
        g_wout = _mm(sv["ycat"], dout, name=f"out_proj_dw{i}", ta=True, out_dtype=BF16).reshape(N_DEV, D // N_DEV, D)

        dqkvc, dgz, dgate_g, d_gal, d_gdb, d_gnw, r_up[i] = _gdn_bwd(
            sv["qkvc"], proj, gal, gdb, gnw, sv["gdn_s"], dycat, f"gdn_bwd{i}", cargo=_ExCargo([[g_wup]]))
        dqkv, g_gdn_cw, _ = _conv_bwd(dqkvc, proj, 0, 1536, gdn_cw[i], f"gdn_conv_bwd{i}")
        dxbcc, dsz, dgate_s, d_sal, d_sdb, d_sdd, d_snw = _ssd_bwd(
            sv["xbcc"], proj, sal, sdb, sdd, snw, sv["ssd_s"], dycat, f"ssd_bwd{i}")
        dsxbc, g_ssm_cw, g_ssm_cb = _conv_bwd(dxbcc, proj, P_SXBC // 1024, 1024, ssm_cw[i], f"ssm_conv_bwd{i}")
        daq, dak, dav, d_skr = _swa_bwd(proj, cos, sin, skr, dycat, f"swa_bwd{i}")
        dsc, g_sc_cw = _sc_bwd(dycat, proj, sc_cw[i], f"sc_bwd{i}")
        dgate = (dgate_g + dgate_s).astype(BF16)
        dproj = jnp.concatenate([dqkv, dgz, dsxbc, dsz, daq, dsc, dak, dav, dgate], axis=1)
        g_wint, r_out[i] = _mm(dproj, sv["h"], name=f"in_proj_dw{i}", ta=True, tm=896, out_dtype=BF16,
                               cargo=_ExCargo([[g_wout]]))
        g_wint = jnp.pad(_unpermute_rows(g_wint).reshape(N_DEV, SHARD_IN, D), ((0, 0), (0, SHARD_PAD - SHARD_IN), (0, 0)))
        dh, r_in[i] = _mm(dproj, win_t[i], name=f"in_proj_dx{i}", tk=896, out_dtype=BF16, cargo=_ExCargo([[g_wint]]))
        dx, g_npremix, d_sc_a, d_sh_a = _pre_bwd(sv["x0"], vec(norm_pre_mix[i]), sc_a, sh_a, dh, dx1, f"pre_mix_bwd{i}")

        gw[i] = dict(
            dmod=jnp.concatenate([d_sh_a, d_sc_a, d_gt_a, d_sh_m, d_sc_m, d_gt_m], axis=1).reshape(-1),
            norm_pre_mix=g_npremix.reshape(-1), norm_post_mix=g_npostmix.reshape(-1),
            norm_pre_mlp=g_npremlp.reshape(-1), norm_post_mlp=g_npostmlp.reshape(-1),
            gdn_conv_w=g_gdn_cw.reshape(-1), gdn_a_log=d_gal[0, LANE_GA:LANE_GA + 4], gdn_dt_bias=d_gdb[0, LANE_GA:LANE_GA + 4],
            gdn_norm_w=d_gnw.reshape(-1), ssm_conv_w=g_ssm_cw.reshape(-1), ssm_conv_b=g_ssm_cb.reshape(-1),
            ssm_a_log=d_sal[0, LANE_SDT:LANE_SDT + 8], ssm_dt_bias=d_sdb[0, LANE_SDT:LANE_SDT + 8],
            ssm_d=d_sdd[0, LANE_SDT:LANE_SDT + 8], ssm_norm_w=d_snw.reshape(-1), attn_sinks=d_skr[0, 0:8],
            sc_conv_w=g_sc_cw.reshape(-1))
    grad_x = dx[None]

    grads, deltas, new_m, new_v = {}, {}, {}, {}
    for n_, r, w, m, v in (("w_out", r_out, w_out, m_w_out, v_w_out), ("w_up", r_up, w_up, m_w_up, v_w_up),
                           ("w_down", r_down, w_down, m_w_down, v_w_down)):
        grads[n_], deltas[n_], new_m[n_], new_v[n_] = _sum8_adamw(r[0], r[1], w, m, v, f"sum_adamw_{n_}")
    g_int = jnp.stack([_sum8(r_in[i].reshape(N_DEV, SHARD_PAD, D), f"grad_sum_w_in{i}") for i in range(2)])
    g_in = g_int[:, :SHARD_IN].transpose(0, 2, 1)
    dl, nm, nv = _adamw_call(w_in.reshape(-1, SHARD_IN), g_in.reshape(-1, SHARD_IN), m_w_in.reshape(-1, SHARD_IN),
                             v_w_in.reshape(-1, SHARD_IN), "adamw_w_in")
    grads["w_in"], deltas["w_in"] = g_in, dl.reshape(w_in.shape)
    new_m["w_in"], new_v["w_in"] = nm.reshape(w_in.shape), nv.reshape(w_in.shape)

    small_names = ["dmod", "norm_pre_mix", "norm_post_mix", "norm_pre_mlp", "norm_post_mlp", "gdn_conv_w", "gdn_a_log",
                   "gdn_dt_bias", "gdn_norm_w", "ssm_conv_w", "ssm_conv_b", "ssm_a_log", "ssm_dt_bias", "ssm_d",
                   "ssm_norm_w", "attn_sinks", "sc_conv_w"]
    flat = jnp.concatenate([gw[i][nm_] for nm_ in small_names for i in range(2)])
    n_flat = flat.shape[0]
    g_all = _ag_small(_pad_rows128(flat), "ag_small_grads")
    tot = _sum8(g_all, "small_grad_sum").reshape(-1)[:n_flat]
    dmod_all = g_all.reshape(N_DEV, -1)[:, :2 * 6 * D].reshape(N_DEV, 2, 6 * D)
    sm = {}
    o = 0
    for nm_ in small_names:
        n = gw[0][nm_].shape[0]
        sm[nm_] = jnp.stack([tot[o:o + n], tot[o + n:o + 2 * n]])
        o += 2 * n

    dmod_sh = lax.dynamic_slice(dmod_all.transpose(1, 0, 2), (0, 0, me * S_ADA), (2, N_DEV, S_ADA))
    grads["ada_w"], deltas["ada_w"], new_m["ada_w"], new_v["ada_w"] = _ada_bwd_adamw(
        c_all, dmod_sh, ada_w, m_ada_w, v_ada_w, "ada_bwd_adamw")

    def shard_cols(full, K, C):
        return lax.dynamic_slice(full.reshape(2, K, C), (0, 0, me * (C // N_DEV)), (2, K, C // N_DEV))

    small_g = dict(
        ada_b=sm["dmod"], norm_pre_mix=sm["norm_pre_mix"], norm_post_mix=sm["norm_post_mix"],
        norm_pre_mlp=sm["norm_pre_mlp"], norm_post_mlp=sm["norm_post_mlp"],
        gdn_conv_w=shard_cols(sm["gdn_conv_w"], 4, 1536), gdn_a_log=sm["gdn_a_log"], gdn_dt_bias=sm["gdn_dt_bias"],
        gdn_norm_w=sm["gdn_norm_w"], ssm_conv_w=shard_cols(sm["ssm_conv_w"], 4, 1024), ssm_conv_b=sm["ssm_conv_b"],
        ssm_a_log=sm["ssm_a_log"], ssm_dt_bias=sm["ssm_dt_bias"], ssm_d=sm["ssm_d"], ssm_norm_w=sm["ssm_norm_w"],
        attn_sinks=sm["attn_sinks"], sc_conv_w=shard_cols(sm["sc_conv_w"], 3, 512))
    small_w = dict(
        ada_b=(ada_b, m_ada_b, v_ada_b), norm_pre_mix=(norm_pre_mix, m_norm_pre_mix, v_norm_pre_mix),
        norm_post_mix=(norm_post_mix, m_norm_post_mix, v_norm_post_mix),
        norm_pre_mlp=(norm_pre_mlp, m_norm_pre_mlp, v_norm_pre_mlp),
        norm_post_mlp=(norm_post_mlp, m_norm_post_mlp, v_norm_post_mlp),
        gdn_conv_w=(gdn_conv_w, m_gdn_conv_w, v_gdn_conv_w), gdn_a_log=(gdn_a_log, m_gdn_a_log, v_gdn_a_log),
        gdn_dt_bias=(gdn_dt_bias, m_gdn_dt_bias, v_gdn_dt_bias), gdn_norm_w=(gdn_norm_w, m_gdn_norm_w, v_gdn_norm_w),
        ssm_conv_w=(ssm_conv_w, m_ssm_conv_w, v_ssm_conv_w), ssm_conv_b=(ssm_conv_b, m_ssm_conv_b, v_ssm_conv_b),
        ssm_a_log=(ssm_a_log, m_ssm_a_log, v_ssm_a_log), ssm_dt_bias=(ssm_dt_bias, m_ssm_dt_bias, v_ssm_dt_bias),
        ssm_d=(ssm_d, m_ssm_d, v_ssm_d), ssm_norm_w=(ssm_norm_w, m_ssm_norm_w, v_ssm_norm_w),
        attn_sinks=(attn_sinks, m_attn_sinks, v_attn_sinks), sc_conv_w=(sc_conv_w, m_sc_conv_w, v_sc_conv_w))
    names = list(small_w)
    sizes = [int(np.prod(small_w[n_][0].shape)) for n_ in names]
    pk = lambda j: _pad_rows128(jnp.concatenate([small_w[n_][j].reshape(-1) for n_ in names]))
    gk = _pad_rows128(jnp.concatenate([small_g[n_].reshape(-1) for n_ in names]))
    dl, nm, nv = _adamw_call(pk(0), gk, pk(1), pk(2), "adamw_small")
    dl, nm, nv = dl.reshape(-1), nm.reshape(-1), nv.reshape(-1)
    o = 0
    for n_, sz in zip(names, sizes):
        shp = small_w[n_][0].shape
        grads[n_] = small_g[n_].reshape(shp)
        deltas[n_], new_m[n_], new_v[n_] = dl[o:o + sz].reshape(shp), nm[o:o + sz].reshape(shp), nv[o:o + sz].reshape(shp)
        o += sz

    order = ["ada_w", "ada_b", "norm_pre_mix", "norm_post_mix", "norm_pre_mlp", "norm_post_mlp", "w_in", "w_out",
             "gdn_conv_w", "gdn_a_log", "gdn_dt_bias", "gdn_norm_w", "ssm_conv_w", "ssm_conv_b", "ssm_a_log",
             "ssm_dt_bias", "ssm_d", "ssm_norm_w", "attn_sinks", "sc_conv_w", "w_up", "w_down"]
    return (loss, grad_x, *[grads[n_] for n_ in order], *[deltas[n_] for n_ in order],
            *[new_m[n_] for n_ in order], *[new_v[n_] for n_ in order])
```

```python
import functools
import math

import jax
import jax.numpy as jnp
import numpy as np
from jax import lax
from jax.experimental import pallas as pl
from jax.experimental.pallas import tpu as pltpu

F32 = jnp.float32
BF16 = jnp.bfloat16
MESH = pl.DeviceIdType.MESH

N_DEV = 8
D = 2048
GW = 512
MLP = 8192
EPS = 1e-6
IN_W = 5904
SHARD_IN = IN_W // N_DEV
SHARD_PAD = 752
PW = 6272
ROPE_THETA = 10000.0
WINDOW = 128
GDN_CHUNK = 64
SSM_CHUNK = 128
NEG = -1e30

ADAM_LR, ADAM_B1, ADAM_B2, ADAM_EPS, ADAM_WD, ADAM_STEP = 0.001, 0.9, 0.999, 1e-08, 0.01, 10

O_GQ, O_GK, O_GV, O_GZ, O_GB, O_GA = 0, 512, 1024, 1536, 2048, 2052
O_SZ, O_SXBC, O_SDT = 2056, 2568, 3592
O_AQ, O_AK, O_AV = 3600, 4112, 4240
O_CB, O_CC, O_CH = 4368, 4880, 5392
P_QKV, P_GZ, P_SXBC, P_SZ, P_AQ, P_CB, P_AK, P_AV, P_GATE = 0, 1536, 2048, 3072, 3584, 4096, 5632, 5888, 6144
LANE_GB, LANE_GA, LANE_SDT = 0, 4, 8

VMEM_BIG = 56 * 1024 * 1024


def _cparams(sem=None, vmem=None):
    kw = {}
    if sem is not None:
        kw["dimension_semantics"] = sem
    if vmem is not None:
        kw["vmem_limit_bytes"] = vmem
    return pltpu.CompilerParams(**kw)


def _me():
    x, y, c = lax.axis_index("x"), lax.axis_index("y"), lax.axis_index("c")
    return x, y, c, 4 * x + 2 * y + c


def _peer(x, y, c, k):
    px = 1 - x if (k >> 2) & 1 else x
    py = 1 - y if (k >> 1) & 1 else y
    pc = 1 - c if k & 1 else c
    return (px, py, pc), 4 * px + 2 * py + pc


def _ag_small(v, name):
    R, W = v.shape

    def body(x_ref, out_ref, send_sems, recv_sems):
        x, y, c, me = _me()
        out_ref[me] = x_ref[...]
        copies = []
        for k in range(1, N_DEV):
            peer, _ = _peer(x, y, c, k)
            cp = pltpu.make_async_remote_copy(
                src_ref=x_ref, dst_ref=out_ref.at[me], send_sem=send_sems.at[k - 1], recv_sem=recv_sems.at[k - 1],
                device_id=peer, device_id_type=MESH)
            cp.start()
            copies.append(cp)
        for cp in copies:
            cp.wait()

    return pl.pallas_call(
        body, name=name,
        out_shape=jax.ShapeDtypeStruct((N_DEV, R, W), v.dtype),
        in_specs=[pl.BlockSpec(memory_space=pltpu.VMEM)],
        out_specs=pl.BlockSpec(memory_space=pltpu.VMEM),
        scratch_shapes=[pltpu.SemaphoreType.DMA((N_DEV - 1,)), pltpu.SemaphoreType.DMA((N_DEV - 1,))],
    )(v)


def _ag_multi(arrs, name):
    cargo = _AgCargo(arrs)
    A = cargo.n_in

    def body(*refs):
        start, mid, end = cargo.phases(refs[:A], refs[A:A + cargo.n_out], refs[A + cargo.n_out:])
        start()
        mid()
        end()

    hbm = pl.BlockSpec(memory_space=pl.ANY)
    return pl.pallas_call(
        body, name=name, out_shape=tuple(cargo.out_shapes), in_specs=[hbm] * A, out_specs=tuple([hbm] * cargo.n_out),
        scratch_shapes=cargo.scratch)(*cargo.arrays)


class _AgCargo:
    def __init__(self, arrs):
        A = len(arrs)
        self.arrays = list(arrs)
        self.n_in = self.n_out = A
        self.out_shapes = [jax.ShapeDtypeStruct((N_DEV,) + v.shape, v.dtype) for v in arrs]
        self.scratch = [pltpu.SemaphoreType.DMA((7 * A,)), pltpu.SemaphoreType.DMA((7 * A,)),
                        pltpu.SemaphoreType.DMA((A,))]

    def phases(self, x_refs, out_refs, sems):
        A = self.n_in
        send_sems, recv_sems, local_sems = sems

        def ctx():
            x, y, c, me = _me()
            return x, y, c, me, (x, y, 1 - c), [(1 - x, y), (x, 1 - y), (1 - x, 1 - y)]

        def copy(a, k, block, to, src=None):
            slot = out_refs[a].at[4 * block[0] + 2 * block[1] + block[2]]
            return pltpu.make_async_remote_copy(
                src_ref=slot if src is None else src, dst_ref=slot, send_sem=send_sems.at[7 * a + k],
                recv_sem=recv_sems.at[7 * a + k], device_id=to, device_id_type=MESH)

        def start():
            x, y, c, me, sibling, chips = ctx()
            for a in range(A):
                pltpu.make_async_copy(x_refs[a], out_refs[a].at[me], local_sems.at[a]).start()
            for a in range(A):
                copy(a, 0, (x, y, c), sibling, src=x_refs[a]).start()
                for j, chip in enumerate(chips):
                    copy(a, 1 + j, (x, y, c), (*chip, c), src=x_refs[a]).start()

        def mid():
            x, y, c, me, sibling, chips = ctx()
            for j, chip in enumerate(chips):
                for a in range(A):
                    copy(a, 1 + j, (*chip, c), (x, y, c)).wait_recv()
                    copy(a, 4 + j, (*chip, c), sibling).start()

        def end():
            x, y, c, me, sibling, chips = ctx()
            for a in range(A):
                copy(a, 0, sibling, (x, y, c)).wait_recv()
                for j, chip in enumerate(chips):
                    copy(a, 4 + j, (*chip, 1 - c), (x, y, c)).wait_recv()
            for a in range(A):
                copy(a, 0, (x, y, c), sibling, src=x_refs[a]).wait_send()
                for j, chip in enumerate(chips):
                    copy(a, 1 + j, (x, y, c), (*chip, c), src=x_refs[a]).wait_send()
                    copy(a, 4 + j, (*chip, c), sibling).wait_send()
                pltpu.make_async_copy(x_refs[a], out_refs[a].at[me], local_sems.at[a]).wait()

        return start, mid, end


class _ExCargo:
    def __init__(self, groups):
        self.flat = [(w, i) for w, layers in enumerate(groups) for i, _ in enumerate(layers)]
        self.arrays = [g for layers in groups for g in layers]
        A = len(self.arrays)
        self.n_in, self.n_out = A, len(groups)
        self.out_shapes = [jax.ShapeDtypeStruct((N_DEV, len(layers)) + layers[0].shape[1:], layers[0].dtype)
                           for layers in groups]
        self.scratch = [pltpu.SemaphoreType.DMA((7 * A,)), pltpu.SemaphoreType.DMA((7 * A,)),
                        pltpu.SemaphoreType.DMA((A,))]

    def phases(self, g_refs, out_refs, sems):
        send_sems, recv_sems, local_sems = sems

        def copies():
            x, y, c, me = _me()
            local = [pltpu.make_async_copy(g_refs[a].at[me], out_refs[w].at[me, i], local_sems.at[a])
                     for a, (w, i) in enumerate(self.flat)]
            remote = []
            for k in range(1, N_DEV):
                peer, pid = _peer(x, y, c, k)
                for a, (w, i) in enumerate(self.flat):
                    remote.append(pltpu.make_async_remote_copy(
                        src_ref=g_refs[a].at[pid], dst_ref=out_refs[w].at[me, i], send_sem=send_sems.at[7 * a + k - 1],
                        recv_sem=recv_sems.at[7 * a + k - 1], device_id=peer, device_id_type=MESH))
            return local, remote

        def start():
            local, remote = copies()
            for cp in local + remote:
                cp.start()

        def end():
            local, remote = copies()
            for cp in remote + local:
                cp.wait()

        return start, (lambda: None), end


def _mm(a, b, *, name, ta=False, tb=False, tm=1024, tn=1024, tk=1024, out_dtype=F32, mode="plain", u=None,
        shards=None, cargo=None):
    K, M = a.shape if ta else a.shape[::-1]
    if shards == "b":
        N, K2 = (b.shape[1], b.shape[0] * b.shape[2]) if tb else (b.shape[0] * b.shape[2], b.shape[1])
    else:
        N, K2 = b.shape if tb else b.shape[::-1]
    assert K == K2, (a.shape, b.shape)
    tm, tn, tk = min(tm, M), min(tn, N), min(tk, K)
    assert M % tm == 0 and N % tn == 0 and K % tk == 0, (M, N, K, tm, tn, tk)
    nk = K // tk
    a_spec = pl.BlockSpec((tk, tm), lambda i, j, k: (k, i)) if ta else pl.BlockSpec((tm, tk), lambda i, j, k: (i, k))
    if shards == "b" and tb:
        assert tk == b.shape[2]
        b_spec = pl.BlockSpec((None, tn, tk), lambda i, j, k: (k, j, 0))
    elif shards == "b":
        assert tn == b.shape[2]
        b_spec = pl.BlockSpec((None, tk, tn), lambda i, j, k: (j, k, 0))
    else:
        b_spec = pl.BlockSpec((tn, tk), lambda i, j, k: (j, k)) if tb else pl.BlockSpec((tk, tn), lambda i, j, k: (k, j))
    if shards == "o":
        o_spec = pl.BlockSpec((None, tm, tn), lambda i, j, k: (j, i, 0))
        o_shape = (N // tn, M, tn)
    else:
        o_spec = pl.BlockSpec((tm, tn), lambda i, j, k: (i, j))
        o_shape = (M, N)
    dn = (((0 if ta else 1,), (1 if tb else 0,)), ((), ()))
    ni_, nj = M // tm, N // tn
    nsteps = ni_ * nj * nk
    n_in = 3 if mode == "drelu2" else 2
    n_out = 2 if mode == "relu2" else 1
    c_arr, c_ispec, c_oshape, c_ospec = _cargo_io(cargo)
    ni, no = len(c_arr), len(c_oshape)

    def body(*refs):
        a_ref, b_ref = refs[0], refs[1]
        u_ref = refs[2] if mode == "drelu2" else None
        outs = refs[n_in + ni:n_in + ni + n_out]
        acc = refs[n_in + ni + n_out + no]
        k = pl.program_id(2)
        step = (pl.program_id(0) * nj + pl.program_id(1)) * nk + k
        drain = _carry(cargo, (refs[n_in:n_in + ni], refs[n_in + ni + n_out:n_in + ni + n_out + no],
                               refs[n_in + ni + n_out + no + 1:]), step, nsteps)

        @pl.when(k == 0)
        def _():
            acc[...] = jnp.zeros_like(acc)

        acc[...] += lax.dot_general(a_ref[...].astype(BF16), b_ref[...].astype(BF16), dn, preferred_element_type=F32)

        @pl.when(k == nk - 1)
        def _():
            r = acc[...]
            if mode == "plain":
                outs[0][...] = r.astype(out_dtype)
            elif mode == "relu2":
                rr = jnp.maximum(r, 0.0)
                outs[0][...] = (rr * rr).astype(BF16)
                outs[1][...] = r.astype(BF16)
            else:
                outs[0][...] = (r * (2.0 * jnp.maximum(u_ref[...].astype(F32), 0.0))).astype(out_dtype)

        drain()

    in_specs, args = [a_spec, b_spec], [a, b]
    if mode == "drelu2":
        in_specs.append(o_spec)
        args.append(u)
    if mode == "relu2":
        out_shape = [jax.ShapeDtypeStruct(o_shape, BF16), jax.ShapeDtypeStruct(o_shape, BF16)]
        out_specs = [o_spec, o_spec]
    else:
        out_shape = [jax.ShapeDtypeStruct(o_shape, out_dtype)]
        out_specs = [o_spec]
    res = pl.pallas_call(
        body, name=name, grid=(ni_, nj, nk), in_specs=in_specs + c_ispec, out_specs=tuple(out_specs + c_ospec),
        out_shape=tuple(out_shape + c_oshape),
        scratch_shapes=[pltpu.VMEM((tm, tn), F32)] + (cargo.scratch if cargo else []),
        compiler_params=_cparams(("arbitrary",) * 3 if cargo else ("parallel", "parallel", "arbitrary"), VMEM_BIG),
    )(*args, *c_arr)
    return res[0] if len(res) == 1 else res


ROW_T = 256


def _rms(x, w):
    return x * lax.rsqrt(jnp.mean(x * x, axis=-1, keepdims=True) + EPS) * w


def _pre_fn(x, w, scale, shift):
    return _rms(x, w) * (1.0 + scale) + shift


def _post_fn(y, w, gate):
    return gate * _rms(y, w)


def _row_spec(T, W):
    return pl.BlockSpec((T, W), lambda i: (i, 0))


def _vec_spec(W):
    return pl.BlockSpec((1, W), lambda i: (0, 0))


def _pre_fwd(x, w, scale, shift, name):
    L = x.shape[0]
    T = min(ROW_T, L)

    def body(x_ref, w_ref, sc_ref, sh_ref, h_ref):
        h_ref[...] = _pre_fn(x_ref[...], w_ref[...], sc_ref[...], sh_ref[...]).astype(BF16)

    return pl.pallas_call(
        body, name=name, grid=(L // T,), in_specs=[_row_spec(T, D), _vec_spec(D), _vec_spec(D), _vec_spec(D)],
        out_specs=_row_spec(T, D), out_shape=jax.ShapeDtypeStruct((L, D), BF16),
        compiler_params=_cparams(("parallel",), VMEM_BIG))(x, w, scale, shift)


def _pre_bwd(x, w, scale, shift, dh, dres, name):
    L = x.shape[0]
    T = min(ROW_T, L)

    def body(x_ref, w_ref, sc_ref, sh_ref, dh_ref, dres_ref, dx_ref, dw_ref, dsc_ref, dsh_ref):
        @pl.when(pl.program_id(0) == 0)
        def _():
            dw_ref[...] = jnp.zeros_like(dw_ref)
            dsc_ref[...] = jnp.zeros_like(dsc_ref)
            dsh_ref[...] = jnp.zeros_like(dsh_ref)

        _, vjp = jax.vjp(_pre_fn, x_ref[...], w_ref[...], sc_ref[...], sh_ref[...])
        dx, dw, dsc, dsh = vjp(dh_ref[...].astype(F32))
        dx_ref[...] = dres_ref[...] + dx
        dw_ref[...] += dw
        dsc_ref[...] += dsc
        dsh_ref[...] += dsh

    vec = jax.ShapeDtypeStruct((1, D), F32)
    return pl.pallas_call(
        body, name=name, grid=(L // T,),
        in_specs=[_row_spec(T, D), _vec_spec(D), _vec_spec(D), _vec_spec(D), _row_spec(T, D), _row_spec(T, D)],
        out_specs=(_row_spec(T, D), _vec_spec(D), _vec_spec(D), _vec_spec(D)),
        out_shape=(jax.ShapeDtypeStruct((L, D), F32), vec, vec, vec),
        compiler_params=_cparams(("arbitrary",), VMEM_BIG))(x, w, scale, shift, dh, dres)


def _post_pre_fwd(x, y, w_post, gate, w_pre, scale, shift, name):
    L = x.shape[0]
    T = min(ROW_T, L)

    def body(x_ref, y_ref, wp_ref, g_ref, w_ref, sc_ref, sh_ref, o_ref, h_ref):
        xn = x_ref[...] + _post_fn(y_ref[...], wp_ref[...], g_ref[...])
        o_ref[...] = xn
        h_ref[...] = _pre_fn(xn, w_ref[...], sc_ref[...], sh_ref[...]).astype(BF16)

    return pl.pallas_call(
        body, name=name, grid=(L // T,),
        in_specs=[_row_spec(T, D), _row_spec(T, D)] + [_vec_spec(D)] * 5,
        out_specs=(_row_spec(T, D), _row_spec(T, D)),
        out_shape=(jax.ShapeDtypeStruct((L, D), F32), jax.ShapeDtypeStruct((L, D), BF16)),
        compiler_params=_cparams(("parallel",), VMEM_BIG))(x, y, w_post, gate, w_pre, scale, shift)


def _pre_post_bwd(x, w_pre, scale, shift, dh, dres, y, w_post, gate, name):
    L = x.shape[0]
    T = min(ROW_T, L)

    def body(x_ref, w_ref, sc_ref, sh_ref, dh_ref, dres_ref, y_ref, wp_ref, g_ref,
             dx_ref, dy_ref, dw_ref, dsc_ref, dsh_ref, dwp_ref, dg_ref):
        @pl.when(pl.program_id(0) == 0)
        def _():
            for r in (dw_ref, dsc_ref, dsh_ref, dwp_ref, dg_ref):
                r[...] = jnp.zeros_like(r)

        _, vjp = jax.vjp(_pre_fn, x_ref[...], w_ref[...], sc_ref[...], sh_ref[...])
        dx, dw, dsc, dsh = vjp(dh_ref[...].astype(F32))
        dx = dres_ref[...] + dx
        dx_ref[...] = dx
        _, vjp2 = jax.vjp(_post_fn, y_ref[...], wp_ref[...], g_ref[...])
        dy, dwp, dg = vjp2(dx)
        dy_ref[...] = dy.astype(BF16)
        dw_ref[...] += dw
        dsc_ref[...] += dsc
        dsh_ref[...] += dsh
        dwp_ref[...] += dwp
        dg_ref[...] += dg

    vec = jax.ShapeDtypeStruct((1, D), F32)
    v = _vec_spec(D)
    r = _row_spec(T, D)
    return pl.pallas_call(
        body, name=name, grid=(L // T,), in_specs=[r, v, v, v, r, r, r, v, v],
        out_specs=(r, r, v, v, v, v, v),
        out_shape=(jax.ShapeDtypeStruct((L, D), F32), jax.ShapeDtypeStruct((L, D), BF16), vec, vec, vec, vec, vec),
        compiler_params=_cparams(("arbitrary",), VMEM_BIG))(x, w_pre, scale, shift, dh, dres, y, w_post, gate)


def _last_residual_loss(x, y, w, gate, target, name):
    L = x.shape[0]
    T = min(ROW_T, L)

    def body(x_ref, y_ref, w_ref, g_ref, t_ref, s_ref, d_ref, dy_ref, dw_ref, dg_ref):
        @pl.when(pl.program_id(0) == 0)
        def _():
            for r in (s_ref, dw_ref, dg_ref):
                r[...] = jnp.zeros_like(r)

        res, vjp = jax.vjp(_post_fn, y_ref[...], w_ref[...], g_ref[...])
        e = (x_ref[...] + res) - t_ref[...]
        d = e / float(D)
        d_ref[...] = d
        s_ref[...] += jnp.sum(e * e)
        dy, dw, dg = vjp(d)
        dy_ref[...] = dy.astype(BF16)
        dw_ref[...] += dw
        dg_ref[...] += dg

    vec = jax.ShapeDtypeStruct((1, D), F32)
    r, v = _row_spec(T, D), _vec_spec(D)
    return pl.pallas_call(
        body, name=name, grid=(L // T,), in_specs=[r, r, v, v, r],
        out_specs=(pl.BlockSpec((8, 128), lambda i: (0, 0)), r, r, v, v),
        out_shape=(jax.ShapeDtypeStruct((8, 128), F32), jax.ShapeDtypeStruct((L, D), F32),
                   jax.ShapeDtypeStruct((L, D), BF16), vec, vec),
        compiler_params=_cparams(("arbitrary",), VMEM_BIG))(x, y, w, gate, target)


CONV_T = 512


def _conv_fwd(x, ci, C, w, b, name):
    L = x.shape[0]
    K = w.shape[0]
    T = min(CONV_T, L)

    def body(xc_ref, xp_ref, w_ref, b_ref, y_ref, buf):
        i = pl.program_id(0)
        buf[0:8, :] = jnp.where(i > 0, xp_ref[...], 0.0)
        buf[8:T + 8, :] = xc_ref[...]
        acc = jnp.zeros((T, C), F32) + b_ref[...]
        for k in range(K):
            acc = acc + w_ref[k:k + 1, :] * buf[pl.ds(8 - (K - 1) + k, T), :]
        y_ref[...] = acc

    return pl.pallas_call(
        body, name=name, grid=(L // T,),
        in_specs=[pl.BlockSpec((T, C), lambda i: (i, ci)),
                  pl.BlockSpec((8, C), lambda i: (jnp.maximum(i * (T // 8) - 1, 0), ci)),
                  pl.BlockSpec((K, C), lambda i: (0, 0)), _vec_spec(C)],
        out_specs=_row_spec(T, C), out_shape=jax.ShapeDtypeStruct((L, C), F32),
        scratch_shapes=[pltpu.VMEM((T + 8, C), F32)],
        compiler_params=_cparams(("parallel",), VMEM_BIG))(x, x, w, b)


def _conv_bwd(dy, x, ci, C, w, name):
    L = x.shape[0]
    K = w.shape[0]
    T = min(CONV_T, L)
    nt = L // T

    def body(dc_ref, dn_ref, xc_ref, xp_ref, w_ref, dx_ref, dw_ref, db_ref, dbuf, xbuf):
        i = pl.program_id(0)

        @pl.when(i == 0)
        def _():
            dw_ref[...] = jnp.zeros_like(dw_ref)
            db_ref[...] = jnp.zeros_like(db_ref)

        dyc = dc_ref[...]
        dbuf[0:T, :] = dyc
        dbuf[T:T + 8, :] = jnp.where(i < nt - 1, dn_ref[...], 0.0)
        xbuf[0:8, :] = jnp.where(i > 0, xp_ref[...], 0.0)
        xbuf[8:T + 8, :] = xc_ref[...]
        dx = jnp.zeros((T, C), F32)
        for k in range(K):
            dx = dx + w_ref[k:k + 1, :] * dbuf[pl.ds(K - 1 - k, T), :]
            dw_ref[k:k + 1, :] += jnp.sum(dyc * xbuf[pl.ds(8 - (K - 1) + k, T), :], axis=0, keepdims=True)
        dx_ref[...] = dx.astype(BF16)
        db_ref[...] += jnp.sum(dyc, axis=0, keepdims=True)

    return pl.pallas_call(
        body, name=name, grid=(nt,),
        in_specs=[_row_spec(T, C),
                  pl.BlockSpec((8, C), lambda i: (jnp.minimum((i + 1) * (T // 8), L // 8 - 1), 0)),
                  pl.BlockSpec((T, C), lambda i: (i, ci)),
                  pl.BlockSpec((8, C), lambda i: (jnp.maximum(i * (T // 8) - 1, 0), ci)),
                  pl.BlockSpec((K, C), lambda i: (0, 0))],
        out_specs=(_row_spec(T, C), pl.BlockSpec((K, C), lambda i: (0, 0)), _vec_spec(C)),
        out_shape=(jax.ShapeDtypeStruct((L, C), BF16), jax.ShapeDtypeStruct((K, C), F32),
                   jax.ShapeDtypeStruct((1, C), F32)),
        scratch_shapes=[pltpu.VMEM((T + 8, C), F32), pltpu.VMEM((T + 8, C), F32)],
        compiler_params=_cparams(("arbitrary",), VMEM_BIG))(dy, dy, x, x, w)


def _sc_fwd(proj, w, name):
    L = proj.shape[0]
    K = w.shape[0]
    C = GW
    T = min(CONV_T, L)
    cb0 = P_CB // C

    def body(b_ref, cc_ref, ch_ref, cp_ref, hp_ref, w_ref, y_ref, buf):
        i = pl.program_id(0)
        buf[0:8, :] = jnp.where(i > 0, cp_ref[...] * hp_ref[...], 0.0)
        buf[8:T + 8, :] = cc_ref[...] * ch_ref[...]
        acc = jnp.zeros((T, C), F32)
        for k in range(K):
            acc = acc + w_ref[k:k + 1, :] * buf[pl.ds(8 - (K - 1) + k, T), :]
        y_ref[...] = (b_ref[...] * acc).astype(BF16)

    def cur(j):
        return pl.BlockSpec((T, C), lambda i: (i, cb0 + j))

    def prev(j):
        return pl.BlockSpec((8, C), lambda i: (jnp.maximum(i * (T // 8) - 1, 0), cb0 + j))

    return pl.pallas_call(
        body, name=name, grid=(L // T,),
        in_specs=[cur(0), cur(1), cur(2), prev(1), prev(2), pl.BlockSpec((K, C), lambda i: (0, 0))],
        out_specs=_row_spec(T, C), out_shape=jax.ShapeDtypeStruct((L, C), BF16),
        scratch_shapes=[pltpu.VMEM((T + 8, C), F32)],
        compiler_params=_cparams(("parallel",), VMEM_BIG))(proj, proj, proj, proj, proj, w)


def _sc_bwd(dycat, proj, w, name):
    L = proj.shape[0]
    K = w.shape[0]
    C = GW
    T = min(CONV_T, L)
    nt = L // T
    cb0 = P_CB // C

    def body(dy_ref, dyn_ref, b_ref, bn_ref, cc_ref, ch_ref, cp_ref, hp_ref, w_ref, d_ref, dw_ref, gbuf, ubuf):
        i = pl.program_id(0)

        @pl.when(i == 0)
        def _():
            dw_ref[...] = jnp.zeros_like(dw_ref)

        dy = dy_ref[...]
        cc, ch = cc_ref[...], ch_ref[...]
        g = dy * b_ref[...]
        gbuf[0:T, :] = g
        gbuf[T:T + 8, :] = jnp.where(i < nt - 1, dyn_ref[...] * bn_ref[...], 0.0)
        ubuf[0:8, :] = jnp.where(i > 0, cp_ref[...] * hp_ref[...], 0.0)
        ubuf[8:T + 8, :] = cc * ch
        conv = jnp.zeros((T, C), F32)
        du = jnp.zeros((T, C), F32)
        for k in range(K):
            us = ubuf[pl.ds(8 - (K - 1) + k, T), :]
            conv = conv + w_ref[k:k + 1, :] * us
            du = du + w_ref[k:k + 1, :] * gbuf[pl.ds(K - 1 - k, T), :]
            dw_ref[k:k + 1, :] += jnp.sum(g * us, axis=0, keepdims=True)
        d_ref[:, 0:C] = (dy * conv).astype(BF16)
        d_ref[:, C:2 * C] = (du * ch).astype(BF16)
        d_ref[:, 2 * C:3 * C] = (du * cc).astype(BF16)

    def cur(j):
        return pl.BlockSpec((T, C), lambda i: (i, cb0 + j))

    def prev(j):
        return pl.BlockSpec((8, C), lambda i: (jnp.maximum(i * (T // 8) - 1, 0), cb0 + j))

    def nxt(col):
        return pl.BlockSpec((8, C), lambda i: (jnp.minimum((i + 1) * (T // 8), L // 8 - 1), col))

    return pl.pallas_call(
        body, name=name, grid=(nt,),
        in_specs=[pl.BlockSpec((T, C), lambda i: (i, 3)), nxt(3), cur(0), nxt(cb0), cur(1), cur(2), prev(1), prev(2),
                  pl.BlockSpec((K, C), lambda i: (0, 0))],
        out_specs=(_row_spec(T, 3 * C), pl.BlockSpec((K, C), lambda i: (0, 0))),
        out_shape=(jax.ShapeDtypeStruct((L, 3 * C), BF16), jax.ShapeDtypeStruct((K, C), F32)),
        scratch_shapes=[pltpu.VMEM((T + 8, C), F32), pltpu.VMEM((T + 8, C), F32)],
        compiler_params=_cparams(("arbitrary",), VMEM_BIG))(dycat, dycat, proj, proj, proj, proj, proj, proj, w)


def _silu(x):
    return x * jax.nn.sigmoid(x)


def _softplus(x):
    return jnp.maximum(x, 0.0) + jnp.log1p(jnp.exp(-jnp.abs(x)))


def _lane_pick(arr, idx):
    lane = lax.broadcasted_iota(jnp.int32, (1, 128), 1)
    return jnp.sum(jnp.where(lane == idx, arr, 0.0), axis=1, keepdims=True)


def _tri(n, strict=False):
    r = lax.broadcasted_iota(jnp.int32, (n, n), 0)
    c = lax.broadcasted_iota(jnp.int32, (n, n), 1)
    return (r > c) if strict else (r >= c)


def _bdot(a, b, dn=(((1,), (0,)), ((), ()))):
    return lax.dot_general(a.astype(BF16), b.astype(BF16), dn, preferred_element_type=F32)


NT = (((1,), (1,)), ((), ()))
TN = (((0,), (0,)), ((), ()))


def _split3(x):
    hi = x.astype(BF16)
    r = x - hi.astype(F32)
    mid = r.astype(BF16)
    lo = (r - mid.astype(F32)).astype(BF16)
    return hi, mid, lo


def _mask_dot(pieces, mask, dn, mask_first):
    def one(p):
        ops = (mask, p) if mask_first else (p, mask)
        return lax.dot_general(ops[0], ops[1], dn, preferred_element_type=F32)
    hi, mid, lo = pieces
    return (one(lo) + one(mid)) + one(hi)


def _tri_apply(x, upper):
    n = x.shape[0]
    r = lax.broadcasted_iota(jnp.int32, (n, n), 0)
    c = lax.broadcasted_iota(jnp.int32, (n, n), 1)
    mask = ((r <= c) if upper else (r >= c)).astype(BF16)
    return _mask_dot(_split3(x), mask, (((1,), (0,)), ((), ())), True)


@jax.custom_vjp
def _cumsum_col(cb):
    return _tri_apply(cb, False)


_cumsum_col.defvjp(lambda cb: (_tri_apply(cb, False), None), lambda _, d: (_tri_apply(d, True),))


def _cumsum_row_fwd(cb):
    tri = _tri(cb.shape[0]).astype(BF16)
    return _mask_dot(_split3(cb), tri, (((0,), (1,)), ((), ())), False)


def _cumsum_row_bwd(_, d):
    tri = _tri(d.shape[1]).astype(BF16)
    return (_mask_dot(_split3(d), tri, (((0,), (1,)), ((), ())), True),)


@jax.custom_vjp
def _cumsum_row(cb):
    return _cumsum_row_fwd(cb)


_cumsum_row.defvjp(lambda cb: (_cumsum_row_fwd(cb), None), _cumsum_row_bwd)


def _dot3(a, b):
    ah, bh = a.astype(BF16), b.astype(BF16)
    al, bl = (a - ah.astype(F32)).astype(BF16), (b - bh.astype(F32)).astype(BF16)
    d = lambda p, q: jnp.dot(p, q, preferred_element_type=F32)
    return (d(al, bh) + d(ah, bl)) + d(ah, bh)


def _unit_lower_inv_impl(ms):
    n = ms[0].shape[0]
    eye = (lax.broadcasted_iota(jnp.int32, (n, n), 0) == lax.broadcasted_iota(jnp.int32, (n, n), 1)).astype(F32)
    ps = [-m for m in ms]
    ts = [eye + p for p in ps]
    for _ in range(int(math.log2(n)) - 1):
        ps = [_dot3(p, p) for p in ps]
        ts = [t + _dot3(t, p) for t, p in zip(ts, ps)]
    return tuple(ts)


@jax.custom_vjp
def _unit_lower_inv(ms):
    return _unit_lower_inv_impl(ms)


def _unit_lower_inv_fwd(ms):
    ts = _unit_lower_inv_impl(ms)
    return ts, ts


def _unit_lower_inv_bwd(ts, ds):
    xs = [_bdot(t, d, TN) for t, d in zip(ts, ds)]
    return (tuple(-_bdot(x, t, NT) for x, t in zip(xs, ts)),)


_unit_lower_inv.defvjp(_unit_lower_inv_fwd, _unit_lower_inv_bwd)


GDN_SUB = 4
GDN_H = 4


def _gdn_step_fn(qs, ks, vs, zs, gs, S, alog_row, dtb_row, nw):
    n = GDN_CHUNK
    lanes = [(h, c) for h in range(GDN_H) for c in range(GDN_SUB)]
    z3 = lambda f, a, b, c_: [f(x, y, z) for x, y, z in zip(a, b, c_)]
    q = [_silu(qs[h][c]) for h, c in lanes]
    k = [_silu(ks[h][c]) for h, c in lanes]
    v = [_silu(vs[h][c]) for h, c in lanes]
    q = [t * lax.rsqrt(jnp.sum(t * t, axis=-1, keepdims=True) + EPS) * (128.0 ** -0.5) for t in q]
    k = [t * lax.rsqrt(jnp.sum(t * t, axis=-1, keepdims=True) + EPS) for t in k]
    beta = [jax.nn.sigmoid(_lane_pick(gs[c], LANE_GB + h)) for h, c in lanes]
    rate = [-jnp.exp(_lane_pick(alog_row, LANE_GA + h)) for h in range(GDN_H)]
    bias = [_lane_pick(dtb_row, LANE_GA + h) for h in range(GDN_H)]
    g = [rate[h] * _softplus(_lane_pick(gs[c], LANE_GA + h) + bias[h]) for h, c in lanes]
    cb = [jnp.broadcast_to(t, (n, n)) for t in g]
    gc_col = [_cumsum_col(t) for t in cb]
    gc_row = [_cumsum_row(t) for t in cb]
    tri_incl, tri_strict = _tri(n), _tri(n, strict=True)
    decay = [jnp.exp(jnp.where(tri_incl, a - b, NEG)) for a, b in zip(gc_col, gc_row)]
    gc = [t[:, 0:1] for t in gc_col]
    g_last = [jnp.sum(t, axis=0, keepdims=True) for t in g]
    kb = [a * b for a, b in zip(k, beta)]
    m = z3(lambda a, b, d: jnp.where(tri_strict, _bdot(a, b, NT) * d, 0.0), kb, k, decay)
    t_inv = _unit_lower_inv(tuple(m))
    egc = [jnp.exp(t) for t in gc]
    u = z3(lambda t, a, b: _bdot(t, a * b), t_inv, v, beta)
    w = z3(lambda t, a, e: _bdot(t, a * e), t_inv, kb, egc)
    attn = z3(lambda a, b, d: jnp.where(tri_incl, _bdot(a, b, NT) * d, 0.0), q, k, decay)
    qd = [a * e for a, e in zip(q, egc)]
    kd = z3(lambda a, gl, c_: a * jnp.exp(gl - c_), k, g_last, gc)
    egl = [jnp.exp(t) for t in g_last]
    zg = [_silu(zs[h][c]) for h, c in lanes]
    S = list(S)
    ys = [[None] * GDN_SUB for _ in range(GDN_H)]
    for c in range(GDN_SUB):
        ii = [h * GDN_SUB + c for h in range(GDN_H)]
        v_new = [u[i] - _bdot(w[i], S[h]) for h, i in enumerate(ii)]
        o = [_bdot(qd[i], S[h]) + _bdot(attn[i], v_new[h]) for h, i in enumerate(ii)]
        S = [S[h] * egl[i] + _bdot(kd[i], v_new[h], TN) for h, i in enumerate(ii)]
        for h, i in enumerate(ii):
            ys[h][c] = _rms(o[h], nw) * zg[i]
    return tuple(tuple(y) for y in ys), tuple(S)


def _gdn_step_args(qkv_ref, z_ref, g_ref):
    n = GDN_CHUNK
    rows = [slice(n * c, n * c + n) for c in range(GDN_SUB)]
    col = lambda ref, o: tuple(tuple(ref[r, o + 128 * h:o + 128 * h + 128] for r in rows) for h in range(GDN_H))
    return rows, (col(qkv_ref, 0), col(qkv_ref, 512), col(qkv_ref, 1024), col(z_ref, 0), tuple(g_ref[r, :] for r in rows))


def _carry(cargo, refs, step, nsteps):
    if cargo is None:
        return lambda: None
    start, mid, end = cargo.phases(*refs)
    pl.when(step == 0)(start)
    pl.when(step == (3 * nsteps) // 4)(mid)
    return lambda: pl.when(step == nsteps - 1)(end)


def _cargo_io(cargo):
    if cargo is None:
        return [], [], [], []
    hbm = pl.BlockSpec(memory_space=pl.ANY)
    return list(cargo.arrays), [hbm] * cargo.n_in, list(cargo.out_shapes), [hbm] * cargo.n_out


def _gdn_fwd(qkvc, proj, alog_row, dtb_row, nw, name, cargo=None):
    L = qkvc.shape[0]
    n = GDN_CHUNK * GDN_SUB
    Nc = L // n
    c_arr, c_ispec, c_oshape, c_ospec = _cargo_io(cargo)
    ni, no = len(c_arr), len(c_oshape)

    def body(*refs):
        qkv_ref, z_ref, g_ref, al_ref, db_ref, nw_ref = refs[:6]
        y_ref, sin_ref = refs[6 + ni:8 + ni]
        S = refs[8 + ni + no]
        step = pl.program_id(0)
        drain = _carry(cargo, (refs[6:6 + ni], refs[8 + ni:8 + ni + no], refs[9 + ni + no:]), step, Nc)

        @pl.when(step == 0)
        def _():
            S[...] = jnp.zeros_like(S)

        rows, args = _gdn_step_args(qkv_ref, z_ref, g_ref)
        states = tuple(S[h] for h in range(GDN_H))
        ys, sn = _gdn_step_fn(*args, states, al_ref[...], db_ref[...], nw_ref[...])
        for h in range(GDN_H):
            sin_ref[0, h] = states[h]
            for c in range(GDN_SUB):
                y_ref[rows[c], 128 * h:128 * h + 128] = ys[h][c].astype(BF16)
            S[h] = sn[h]
        drain()

    return pl.pallas_call(
        body, name=name, grid=(Nc,),
        in_specs=[pl.BlockSpec((n, 1536), lambda i: (i, 0)), pl.BlockSpec((n, 512), lambda i: (i, P_GZ // 512)),
                  pl.BlockSpec((n, 128), lambda i: (i, P_GATE // 128)), _vec_spec(128), _vec_spec(128), _vec_spec(128)]
        + c_ispec,
        out_specs=tuple([pl.BlockSpec((n, 512), lambda i: (i, 0)), pl.BlockSpec((1, 4, 128, 128), lambda i: (i, 0, 0, 0))]
                        + c_ospec),
        out_shape=tuple([jax.ShapeDtypeStruct((L, 512), BF16), jax.ShapeDtypeStruct((Nc, 4, 128, 128), F32)] + c_oshape),
        scratch_shapes=[pltpu.VMEM((4, 128, 128), F32)] + (cargo.scratch if cargo else []),
        compiler_params=_cparams(("arbitrary",), VMEM_BIG))(qkvc, proj, proj, alog_row, dtb_row, nw, *c_arr)


def _gdn_bwd(qkvc, proj, alog_row, dtb_row, nw, s_in, dycat, name, cargo=None):
    L = qkvc.shape[0]
    n = GDN_CHUNK * GDN_SUB
    Nc = L // n
    c_arr, c_ispec, c_oshape, c_ospec = _cargo_io(cargo)
    ni, no = len(c_arr), len(c_oshape)

    def body(*refs):
        qkv_ref, z_ref, g_ref, al_ref, db_ref, nw_ref, sin_ref, dy_ref = refs[:8]
        dqkv_ref, dz_ref, dg_ref, dal_ref, ddb_ref, dnw_ref = refs[8 + ni:14 + ni]
        dS = refs[14 + ni + no]
        step = pl.program_id(0)
        drain = _carry(cargo, (refs[8:8 + ni], refs[14 + ni:14 + ni + no], refs[15 + ni + no:]), step, Nc)

        @pl.when(step == 0)
        def _():
            dS[...] = jnp.zeros_like(dS)
            dal_ref[...] = jnp.zeros_like(dal_ref)
            ddb_ref[...] = jnp.zeros_like(ddb_ref)
            dnw_ref[...] = jnp.zeros_like(dnw_ref)

        rows, args = _gdn_step_args(qkv_ref, z_ref, g_ref)
        s_in = tuple(sin_ref[0, h] for h in range(GDN_H))
        _, vjp = jax.vjp(_gdn_step_fn, *args, s_in, al_ref[...], db_ref[...], nw_ref[...])
        dys = tuple(tuple(dy_ref[r, 128 * h:128 * h + 128] for r in rows) for h in range(GDN_H))
        dq, dk, dv, dz, dgt, ds, dal, ddb, dnw = vjp((dys, tuple(dS[h] for h in range(GDN_H))))
        for h in range(GDN_H):
            for c in range(GDN_SUB):
                dqkv_ref[rows[c], 128 * h:128 * h + 128] = dq[h][c]
                dqkv_ref[rows[c], 512 + 128 * h:640 + 128 * h] = dk[h][c]
                dqkv_ref[rows[c], 1024 + 128 * h:1152 + 128 * h] = dv[h][c]
                dz_ref[rows[c], 128 * h:128 * h + 128] = dz[h][c].astype(BF16)
            dS[h] = ds[h]
        for c in range(GDN_SUB):
            dg_ref[rows[c], :] = dgt[c]
        dal_ref[...] += dal
        ddb_ref[...] += ddb
        dnw_ref[...] += dnw
        drain()

    rev = lambda i: Nc - 1 - i
    vec = jax.ShapeDtypeStruct((1, 128), F32)
    return pl.pallas_call(
        body, name=name, grid=(Nc,),
        in_specs=[pl.BlockSpec((n, 1536), lambda i: (rev(i), 0)), pl.BlockSpec((n, 512), lambda i: (rev(i), P_GZ // 512)),
                  pl.BlockSpec((n, 128), lambda i: (rev(i), P_GATE // 128)), _vec_spec(128), _vec_spec(128), _vec_spec(128),
                  pl.BlockSpec((1, 4, 128, 128), lambda i: (rev(i), 0, 0, 0)), pl.BlockSpec((n, 512), lambda i: (rev(i), 0))]
        + c_ispec,
        out_specs=tuple([pl.BlockSpec((n, 1536), lambda i: (rev(i), 0)), pl.BlockSpec((n, 512), lambda i: (rev(i), 0)),
                         pl.BlockSpec((n, 128), lambda i: (rev(i), 0)), _vec_spec(128), _vec_spec(128), _vec_spec(128)]
                        + c_ospec),
        out_shape=tuple([jax.ShapeDtypeStruct((L, 1536), F32), jax.ShapeDtypeStruct((L, 512), BF16),
                         jax.ShapeDtypeStruct((L, 128), F32), vec, vec, vec] + c_oshape),
        scratch_shapes=[pltpu.VMEM((4, 128, 128), F32)] + (cargo.scratch if cargo else []),
        compiler_params=_cparams(("arbitrary",), VMEM_BIG))(qkvc, proj, proj, alog_row, dtb_row, nw, s_in, dycat, *c_arr)


def _ssd_chunk_fn(xs, bs, cs, zs, gates, st, alog_row, dtb_row, d_row, nws):
    n = SSM_CHUNK
    tri = _tri(n)
    lane = lax.broadcasted_iota(jnp.int32, (1, 128), 1)
    lo = lane < 64
    xs = [_silu(t) for t in xs]
    bs = [_silu(t) for t in bs]
    cs = [_silu(t) for t in cs]
    cb = [_bdot(cs[g], bs[g], NT) for g in range(2)]
    H = range(8)
    P = range(4)
    dt = [_softplus(_lane_pick(gates, LANE_SDT + h) + _lane_pick(dtb_row, LANE_SDT + h)) for h in H]
    da = [dt[h] * (-jnp.exp(_lane_pick(alog_row, LANE_SDT + h))) for h in H]
    dab = [jnp.broadcast_to(t, (n, n)) for t in da]
    cs_col = [_cumsum_col(t) for t in dab]
    cs_row = [_cumsum_row(t) for t in dab]
    lmat = [jnp.exp(jnp.where(tri, a - b, NEG)) for a, b in zip(cs_col, cs_row)]
    xdt_h = [jnp.where(lo if h % 2 == 0 else jnp.logical_not(lo), xs[h // 2] * dt[h], 0.0) for h in H]
    yd = [_bdot(cb[h // 4] * lmat[h], xdt_h[h]) for h in H]
    tot = [jnp.sum(t, axis=0, keepdims=True) for t in da]
    dsk = [_lane_pick(d_row, LANE_SDT + h) for h in H]
    cs_lane = [jnp.where(lo, cs_col[2 * p], cs_col[2 * p + 1]) for p in P]
    dt_lane = [jnp.where(lo, dt[2 * p], dt[2 * p + 1]) for p in P]
    tot_lane = [jnp.where(lo, tot[2 * p], tot[2 * p + 1]) for p in P]
    dsk_lane = [jnp.where(lo, dsk[2 * p], dsk[2 * p + 1]) for p in P]
    xdt = [xs[p] * dt_lane[p] for p in P]
    upd = [_bdot(bs[p // 2], xdt[p] * jnp.exp(tot_lane[p] - cs_lane[p]), TN) for p in P]
    st_new = [st[p] * jnp.exp(tot_lane[p]) + upd[p] for p in P]
    yoff = [_bdot(cs[p // 2], st[p]) * jnp.exp(cs_lane[p]) for p in P]
    zg = [_silu(zs[p]) for p in P]
    ys = [(yd[2 * p] + yd[2 * p + 1] + yoff[p] + xs[p] * dsk_lane[p]) * zg[p] for p in P]
    out = []
    for g in range(2):
        ms = (jnp.sum(ys[2 * g] ** 2, axis=-1, keepdims=True) + jnp.sum(ys[2 * g + 1] ** 2, axis=-1, keepdims=True)) / 256.0
        rstd = lax.rsqrt(ms + EPS)
        out += [ys[2 * g] * rstd * nws[2 * g], ys[2 * g + 1] * rstd * nws[2 * g + 1]]
    return tuple(out), tuple(st_new)


def _ssd_args(xbc_ref, z_ref, nw_ref):
    xs = [xbc_ref[:, 128 * p:128 * p + 128] for p in range(4)]
    bs = [xbc_ref[:, 512 + 128 * g:640 + 128 * g] for g in range(2)]
    cs = [xbc_ref[:, 768 + 128 * g:896 + 128 * g] for g in range(2)]
    zs = [z_ref[:, 128 * p:128 * p + 128] for p in range(4)]
    nws = [nw_ref[:, 128 * p:128 * p + 128] for p in range(4)]
    return xs, bs, cs, zs, nws


def _ssd_fwd(xbcc, proj, alog_row, dtb_row, d_row, nw, name):
    L = xbcc.shape[0]
    n = SSM_CHUNK
    Nc = L // n

    def body(xbc_ref, z_ref, g_ref, al_ref, db_ref, d_ref, nw_ref, y_ref, sin_ref, S):
        @pl.when(pl.program_id(0) == 0)
        def _():
            S[...] = jnp.zeros_like(S)

        xs, bs, cs, zs, nws = _ssd_args(xbc_ref, z_ref, nw_ref)
        st = [S[p] for p in range(4)]
        sin_ref[0] = S[...]
        ys, sn = _ssd_chunk_fn(xs, bs, cs, zs, g_ref[...], st, al_ref[...], db_ref[...], d_ref[...], nws)
        for p in range(4):
            y_ref[:, 128 * p:128 * p + 128] = ys[p].astype(BF16)
            S[p] = sn[p]

    return pl.pallas_call(
        body, name=name, grid=(Nc,),
        in_specs=[pl.BlockSpec((n, 1024), lambda i: (i, 0)), pl.BlockSpec((n, 512), lambda i: (i, P_SZ // 512)),
                  pl.BlockSpec((n, 128), lambda i: (i, P_GATE // 128)), _vec_spec(128), _vec_spec(128), _vec_spec(128),
                  _vec_spec(512)],
        out_specs=(pl.BlockSpec((n, 512), lambda i: (i, 0)), pl.BlockSpec((1, 4, 128, 128), lambda i: (i, 0, 0, 0))),
        out_shape=(jax.ShapeDtypeStruct((L, 512), BF16), jax.ShapeDtypeStruct((Nc, 4, 128, 128), F32)),
        scratch_shapes=[pltpu.VMEM((4, 128, 128), F32)],
        compiler_params=_cparams(("arbitrary",), VMEM_BIG))(xbcc, proj, proj, alog_row, dtb_row, d_row, nw)


def _ssd_bwd(xbcc, proj, alog_row, dtb_row, d_row, nw, s_in, dycat, name):
    L = xbcc.shape[0]
    n = SSM_CHUNK
    Nc = L // n

    def body(xbc_ref, z_ref, g_ref, al_ref, db_ref, d_ref, nw_ref, sin_ref, dy_ref,
             dxbc_ref, dz_ref, dg_ref, dal_ref, ddb_ref, dd_ref, dnw_ref, dS):
        @pl.when(pl.program_id(0) == 0)
        def _():
            dS[...] = jnp.zeros_like(dS)
            dal_ref[...] = jnp.zeros_like(dal_ref)
            ddb_ref[...] = jnp.zeros_like(ddb_ref)
            dd_ref[...] = jnp.zeros_like(dd_ref)
            dnw_ref[...] = jnp.zeros_like(dnw_ref)

        xs, bs, cs, zs, nws = _ssd_args(xbc_ref, z_ref, nw_ref)
        st = [sin_ref[0, p] for p in range(4)]
        _, vjp = jax.vjp(_ssd_chunk_fn, xs, bs, cs, zs, g_ref[...], st, al_ref[...], db_ref[...], d_ref[...], nws)
        dys = tuple(dy_ref[:, 128 * p:128 * p + 128] for p in range(4))
        dxs, dbs, dcs, dzs, dgt, dst, dal, ddb, dd, dnws = vjp((dys, tuple(dS[p] for p in range(4))))
        for p in range(4):
            dxbc_ref[:, 128 * p:128 * p + 128] = dxs[p]
            dz_ref[:, 128 * p:128 * p + 128] = dzs[p].astype(BF16)
            dS[p] = dst[p]
            dnw_ref[:, 128 * p:128 * p + 128] += dnws[p]
        for g in range(2):
            dxbc_ref[:, 512 + 128 * g:640 + 128 * g] = dbs[g]
            dxbc_ref[:, 768 + 128 * g:896 + 128 * g] = dcs[g]
        dg_ref[...] = dgt
        dal_ref[...] += dal
        ddb_ref[...] += ddb
        dd_ref[...] += dd

    rev = lambda i: Nc - 1 - i
    vec = jax.ShapeDtypeStruct((1, 128), F32)
    return pl.pallas_call(
        body, name=name, grid=(Nc,),
        in_specs=[pl.BlockSpec((n, 1024), lambda i: (rev(i), 0)), pl.BlockSpec((n, 512), lambda i: (rev(i), P_SZ // 512)),
                  pl.BlockSpec((n, 128), lambda i: (rev(i), P_GATE // 128)), _vec_spec(128), _vec_spec(128), _vec_spec(128),
                  _vec_spec(512), pl.BlockSpec((1, 4, 128, 128), lambda i: (rev(i), 0, 0, 0)),
                  pl.BlockSpec((n, 512), lambda i: (rev(i), 1))],
        out_specs=(pl.BlockSpec((n, 1024), lambda i: (rev(i), 0)), pl.BlockSpec((n, 512), lambda i: (rev(i), 0)),
                   pl.BlockSpec((n, 128), lambda i: (rev(i), 0)), _vec_spec(128), _vec_spec(128), _vec_spec(128),
                   _vec_spec(512)),
        out_shape=(jax.ShapeDtypeStruct((L, 1024), F32), jax.ShapeDtypeStruct((L, 512), BF16),
                   jax.ShapeDtypeStruct((L, 128), F32), vec, vec, vec, jax.ShapeDtypeStruct((1, 512), F32)),
        scratch_shapes=[pltpu.VMEM((4, 128, 128), F32)],
        compiler_params=_cparams(("arbitrary",), VMEM_BIG))(xbcc, proj, proj, alog_row, dtb_row, d_row, nw, s_in, dycat)


def _rot(x):
    W = x.shape[1]
    lane = lax.broadcasted_iota(jnp.int32, (1, W), 1)
    first = (lane % 64) < 32
    return jnp.where(first, -pltpu.roll(x, W - 32, 1), pltpu.roll(x, 32, 1))


def _rope(x, cos, sin):
    return x * cos + _rot(x) * sin


def _rope_t(d, cos, sin):
    return d * cos - _rot(d * sin)


def _swa_core_fn(qs, ks, vs, sink_row, mask):
    lane = lax.broadcasted_iota(jnp.int32, (1, 128), 1)
    lo = lane < 64
    H = range(8)
    qm = [jnp.where(lo if h % 2 == 0 else jnp.logical_not(lo), qs[h // 2], 0.0) for h in H]
    s = [jnp.where(mask, _bdot(qm[h], ks[h // 4], NT) * 0.125, NEG) for h in H]
    sk = [_lane_pick(sink_row, h) for h in H]
    mx = [lax.stop_gradient(jnp.maximum(jnp.max(s[h], axis=-1, keepdims=True), sk[h])) for h in H]
    pr = [jnp.exp(s[h] - mx[h]) for h in H]
    den = [jnp.sum(pr[h], axis=-1, keepdims=True) + jnp.exp(sk[h] - mx[h]) for h in H]
    pr = [pr[h] / den[h] for h in H]
    ov = [_bdot(pr[h], vs[h // 4]) for h in H]
    return tuple(jnp.where(lo, ov[2 * p], ov[2 * p + 1]) for p in range(4))


def _swa_prepare(blk, q_ref, kc_ref, kp_ref, vc_ref, vp_ref, cc_ref, sc_ref, cp_ref, sp_ref):
    W = WINDOW
    cos_c, sin_c = cc_ref[...], sc_ref[...]
    cos_b = jnp.concatenate([cp_ref[...], cos_c], axis=0)
    sin_b = jnp.concatenate([sp_ref[...], sin_c], axis=0)
    cos_q, sin_q = jnp.concatenate([cos_c] * 4, axis=1), jnp.concatenate([sin_c] * 4, axis=1)
    cos_k, sin_k = jnp.concatenate([cos_b] * 2, axis=1), jnp.concatenate([sin_b] * 2, axis=1)
    qr = _rope(q_ref[...], cos_q, sin_q)
    kr = _rope(jnp.concatenate([kp_ref[...], kc_ref[...]], axis=0), cos_k, sin_k)
    vb = jnp.concatenate([vp_ref[...], vc_ref[...]], axis=0)
    qi = lax.broadcasted_iota(jnp.int32, (W, 2 * W), 0)
    kj = lax.broadcasted_iota(jnp.int32, (W, 2 * W), 1)
    rel = qi + W - kj
    mask = (rel >= 0) & (rel < W) & ((blk > 0) | (kj >= W))
    qs = [qr[:, 128 * p:128 * p + 128] for p in range(4)]
    ks = [kr[:, 128 * g:128 * g + 128] for g in range(2)]
    vs = [vb[:, 128 * g:128 * g + 128] for g in range(2)]
    return qs, ks, vs, mask, (cos_q, sin_q, cos_k, sin_k)


def _swa_specs(nb, order):
    W = WINDOW
    cur = lambda i: order(i)
    prv = lambda i: jnp.maximum(order(i) - 1, 0)
    return [pl.BlockSpec((W, 512), lambda i: (cur(i), P_AQ // 512)),
            pl.BlockSpec((W, 256), lambda i: (cur(i), P_AK // 256)), pl.BlockSpec((W, 256), lambda i: (prv(i), P_AK // 256)),
            pl.BlockSpec((W, 256), lambda i: (cur(i), P_AV // 256)), pl.BlockSpec((W, 256), lambda i: (prv(i), P_AV // 256)),
            pl.BlockSpec((W, 128), lambda i: (cur(i), 0)), pl.BlockSpec((W, 128), lambda i: (cur(i), 0)),
            pl.BlockSpec((W, 128), lambda i: (prv(i), 0)), pl.BlockSpec((W, 128), lambda i: (prv(i), 0)),
            _vec_spec(128)]


def _swa_fwd(proj, cos, sin, sink_row, name):
    L = proj.shape[0]
    W = WINDOW
    nb = L // W

    def body(q_ref, kc_ref, kp_ref, vc_ref, vp_ref, cc_ref, sc_ref, cp_ref, sp_ref, sk_ref, y_ref):
        blk = pl.program_id(0)
        qs, ks, vs, mask, _ = _swa_prepare(blk, q_ref, kc_ref, kp_ref, vc_ref, vp_ref, cc_ref, sc_ref, cp_ref, sp_ref)
        outs = _swa_core_fn(qs, ks, vs, sk_ref[...], mask)
        for p in range(4):
            y_ref[:, 128 * p:128 * p + 128] = outs[p].astype(BF16)

    return pl.pallas_call(
        body, name=name, grid=(nb,), in_specs=_swa_specs(nb, lambda i: i),
        out_specs=pl.BlockSpec((W, 512), lambda i: (i, 0)), out_shape=jax.ShapeDtypeStruct((L, 512), BF16),
        compiler_params=_cparams(("parallel",), VMEM_BIG))(proj, proj, proj, proj, proj, cos, sin, cos, sin, sink_row)


def _swa_bwd(proj, cos, sin, sink_row, dycat, name):
    L = proj.shape[0]
    W = WINDOW
    nb = L // W
    rev = lambda i: nb - 1 - i

    def body(q_ref, kc_ref, kp_ref, vc_ref, vp_ref, cc_ref, sc_ref, cp_ref, sp_ref, sk_ref, dy_ref,
             dq_ref, dk_ref, dv_ref, dsk_ref, ck, cv):
        i = pl.program_id(0)
        blk = nb - 1 - i

        @pl.when(i == 0)
        def _():
            ck[...] = jnp.zeros_like(ck)
            cv[...] = jnp.zeros_like(cv)
            dsk_ref[...] = jnp.zeros_like(dsk_ref)

        qs, ks, vs, mask, (cos_q, sin_q, cos_k, sin_k) = _swa_prepare(
            blk, q_ref, kc_ref, kp_ref, vc_ref, vp_ref, cc_ref, sc_ref, cp_ref, sp_ref)
        _, vjp = jax.vjp(functools.partial(_swa_core_fn, mask=mask), qs, ks, vs, sk_ref[...])
        dqs, dks, dvs, dsk = vjp(tuple(dy_ref[:, 128 * p:128 * p + 128] for p in range(4)))
        dq_ref[...] = _rope_t(jnp.concatenate(dqs, axis=1), cos_q, sin_q).astype(BF16)
        dkb = _rope_t(jnp.concatenate(dks, axis=1), cos_k, sin_k)
        dvb = jnp.concatenate(dvs, axis=1)
        dk_ref[...] = (dkb[W:2 * W] + ck[...]).astype(BF16)
        dv_ref[...] = (dvb[W:2 * W] + cv[...]).astype(BF16)
        ck[...] = dkb[0:W]
        cv[...] = dvb[0:W]
        dsk_ref[...] += dsk

    return pl.pallas_call(
        body, name=name, grid=(nb,),
        in_specs=_swa_specs(nb, rev) + [pl.BlockSpec((W, 512), lambda i: (rev(i), 2))],
        out_specs=(pl.BlockSpec((W, 512), lambda i: (rev(i), 0)), pl.BlockSpec((W, 256), lambda i: (rev(i), 0)),
                   pl.BlockSpec((W, 256), lambda i: (rev(i), 0)), _vec_spec(128)),
        out_shape=(jax.ShapeDtypeStruct((L, 512), BF16), jax.ShapeDtypeStruct((L, 256), BF16),
                   jax.ShapeDtypeStruct((L, 256), BF16), jax.ShapeDtypeStruct((1, 128), F32)),
        scratch_shapes=[pltpu.VMEM((W, 256), F32), pltpu.VMEM((W, 256), F32)],
        compiler_params=_cparams(("arbitrary",), VMEM_BIG))(
            proj, proj, proj, proj, proj, cos, sin, cos, sin, sink_row, dycat)


def _adamw(w, g, m, v):
    m = ADAM_B1 * m + (1.0 - ADAM_B1) * g
    v = ADAM_B2 * v + (1.0 - ADAM_B2) * (g * g)
    m_hat = m / (1.0 - ADAM_B1 ** ADAM_STEP)
    v_hat = v / (1.0 - ADAM_B2 ** ADAM_STEP)
    delta = -ADAM_LR * (m_hat / (jnp.sqrt(v_hat) + ADAM_EPS) + ADAM_WD * w)
    return delta, m, v


def _ada_fwd(c_all, ada_w, ada_b_sh, name):
    S = ada_w.shape[2]

    def body(c_ref, w_ref, b_ref, o_ref):
        o_ref[0] = _bdot(_silu(c_ref[...]), w_ref[0]) + b_ref[0]

    return pl.pallas_call(
        body, name=name, grid=(2,),
        in_specs=[pl.BlockSpec((N_DEV, D), lambda i: (0, 0)), pl.BlockSpec((1, D, S), lambda i: (i, 0, 0)),
                  pl.BlockSpec((1, 1, S), lambda i: (i, 0, 0))],
        out_specs=pl.BlockSpec((1, N_DEV, S), lambda i: (i, 0, 0)), out_shape=jax.ShapeDtypeStruct((2, N_DEV, S), F32),
        compiler_params=_cparams(("parallel",), VMEM_BIG))(c_all, ada_w, ada_b_sh.reshape(2, 1, S))


def _ada_bwd_adamw(c_all, dmod_sh, w, m, v, name):
    S = w.shape[2]
    T = 256

    def body(c_ref, d_ref, w_ref, m_ref, v_ref, g_ref, dl_ref, nm_ref, nv_ref):
        g = _bdot(_silu(c_ref[...]), d_ref[0], TN)
        dl, nm, nv = _adamw(w_ref[0], g, m_ref[0], v_ref[0])
        g_ref[0], dl_ref[0], nm_ref[0], nv_ref[0] = g, dl, nm, nv

    blk = pl.BlockSpec((1, T, S), lambda i, j: (i, j, 0))
    sds = jax.ShapeDtypeStruct(w.shape, F32)
    return pl.pallas_call(
        body, name=name, grid=(2, D // T),
        in_specs=[pl.BlockSpec((N_DEV, T), lambda i, j: (0, j)), pl.BlockSpec((1, N_DEV, S), lambda i, j: (i, 0, 0)),
                  blk, blk, blk],
        out_specs=(blk, blk, blk, blk), out_shape=(sds, sds, sds, sds),
        compiler_params=_cparams(("parallel", "parallel"), VMEM_BIG))(c_all, dmod_sh, w, m, v)


def _sum8(r, name):
    _, R, W = r.shape
    T = R
    for cand in (2496, 2048, 1024, 512, 256, 128, 64, 32, 16, 8):
        if R % cand == 0:
            T = cand
            break

    def body(r_ref, o_ref):
        acc = r_ref[0].astype(F32)
        for d in range(1, N_DEV):
            acc = acc + r_ref[d].astype(F32)
        o_ref[...] = acc

    return pl.pallas_call(
        body, name=name, grid=(R // T,), in_specs=[pl.BlockSpec((N_DEV, T, W), lambda i: (0, i, 0))],
        out_specs=pl.BlockSpec((T, W), lambda i: (i, 0)), out_shape=jax.ShapeDtypeStruct((R, W), F32),
        compiler_params=_cparams(("parallel",), VMEM_BIG))(r)


def _adamw_call(w, g, m, v, name):
    R, W = w.shape
    T = R
    for cand in (1024, 512, 256, 128, 64, 32, 16, 8):
        if R % cand == 0 and R > cand and cand * W * 4 <= 2 * 1024 * 1024:
            T = cand
            break

    def body(w_ref, g_ref, m_ref, v_ref, dl_ref, nm_ref, nv_ref):
        dl_ref[...], nm_ref[...], nv_ref[...] = _adamw(w_ref[...], g_ref[...], m_ref[...], v_ref[...])

    blk = pl.BlockSpec((T, W), lambda i: (i, 0))
    sds = jax.ShapeDtypeStruct((R, W), F32)
    return pl.pallas_call(
        body, name=name, grid=(R // T,), in_specs=[blk, blk, blk, blk], out_specs=(blk, blk, blk),
        out_shape=(sds, sds, sds), compiler_params=_cparams(("parallel",), VMEM_BIG))(w, g, m, v)


def _sum8_adamw(r0, r1, w, m, v, name):
    _, _, R, W = r0.shape
    T = (256 * 1024) // W
    assert R % T == 0, (R, W, T)
    nj = R // T

    def body(r0_ref, r1_ref, w_ref, m_ref, v_ref, g_ref, dl_ref, nm_ref, nv_ref):
        def run(r_ref):
            g = r_ref[0].astype(F32)
            for d in range(1, N_DEV):
                g = g + r_ref[d].astype(F32)
            g_ref[...] = g
            dl_ref[...], nm_ref[...], nv_ref[...] = _adamw(w_ref[...], g, m_ref[...], v_ref[...])

        pl.when(pl.program_id(0) == 0)(lambda: run(r0_ref))
        pl.when(pl.program_id(0) == 1)(lambda: run(r1_ref))

    r0_spec = pl.BlockSpec((N_DEV, None, T, W), lambda i, j: (0, 0, jnp.where(i == 0, j, nj - 1), 0))
    r1_spec = pl.BlockSpec((N_DEV, None, T, W), lambda i, j: (0, 0, jnp.where(i == 1, j, 0), 0))
    blk = pl.BlockSpec((None, T, W), lambda i, j: (i, j, 0))
    sds = jax.ShapeDtypeStruct((2, R, W), F32)
    return pl.pallas_call(
        body, name=name, grid=(2, nj), in_specs=[r0_spec, r1_spec, blk, blk, blk],
        out_specs=(blk, blk, blk, blk), out_shape=(sds, sds, sds, sds),
        compiler_params=_cparams(("arbitrary", "arbitrary"), VMEM_BIG))(r0, r1, w, m, v)


def _permute_rows(wt):
    z = lambda n: jnp.zeros((n, wt.shape[1]), wt.dtype)
    s = lambda o, n: wt[o:o + n]
    kd = [s(O_AK, 64), s(O_AK, 64), s(O_AK + 64, 64), s(O_AK + 64, 64)]
    vd = [s(O_AV, 64), s(O_AV, 64), s(O_AV + 64, 64), s(O_AV + 64, 64)]
    return jnp.concatenate(
        [s(O_GQ, 1536), s(O_GZ, 512), s(O_SXBC, 1024), s(O_SZ, 512), s(O_AQ, 512), s(O_CB, 1536)] + kd + vd
        + [s(O_GB, 4), s(O_GA, 4), s(O_SDT, 8), z(112)], axis=0)


def _unpermute_rows(g):
    s = lambda o, n: g[o:o + n]
    add = lambda o: (s(o, 64).astype(F32) + s(o + 64, 64).astype(F32)).astype(g.dtype)
    dk = [add(P_AK), add(P_AK + 128)]
    dv = [add(P_AV), add(P_AV + 128)]
    return jnp.concatenate(
        [s(P_QKV, 1536), s(P_GZ, 512), s(P_GATE, 8), s(P_SZ, 512), s(P_SXBC, 1024), s(P_GATE + 8, 8), s(P_AQ, 512)]
        + dk + dv + [s(P_CB, 1536)], axis=0)


def _row128(vals, lane0):
    n = vals.shape[0]
    return jnp.concatenate([jnp.zeros((lane0,), F32), vals, jnp.zeros((128 - lane0 - n,), F32)]).reshape(1, 128)


def _pad_rows128(flat):
    n = flat.shape[0]
    pad = (-n) % 1024
    return jnp.concatenate([flat, jnp.zeros((pad,), flat.dtype)]).reshape(-1, 128)


def kernel(x, c, positions, ada_w, ada_b, norm_pre_mix, norm_post_mix, norm_pre_mlp, norm_post_mlp, w_in, w_out, gdn_conv_w, gdn_a_log, gdn_dt_bias, gdn_norm_w, ssm_conv_w, ssm_conv_b, ssm_a_log, ssm_dt_bias, ssm_d, ssm_norm_w, attn_sinks, sc_conv_w, w_up, w_down, loss_target, m_ada_w, m_ada_b, m_norm_pre_mix, m_norm_post_mix, m_norm_pre_mlp, m_norm_post_mlp, m_w_in, m_w_out, m_gdn_conv_w, m_gdn_a_log, m_gdn_dt_bias, m_gdn_norm_w, m_ssm_conv_w, m_ssm_conv_b, m_ssm_a_log, m_ssm_dt_bias, m_ssm_d, m_ssm_norm_w, m_attn_sinks, m_sc_conv_w, m_w_up, m_w_down, v_ada_w, v_ada_b, v_norm_pre_mix, v_norm_post_mix, v_norm_pre_mlp, v_norm_post_mlp, v_w_in, v_w_out, v_gdn_conv_w, v_gdn_a_log, v_gdn_dt_bias, v_gdn_norm_w, v_ssm_conv_w, v_ssm_conv_b, v_ssm_a_log, v_ssm_dt_bias, v_ssm_d, v_ssm_norm_w, v_attn_sinks, v_sc_conv_w, v_w_up, v_w_down):
    L = x.shape[1]
    me = 4 * lax.axis_index("x") + 2 * lax.axis_index("y") + lax.axis_index("c")
    x0 = x[0]
    target = loss_target[0]
    S_ADA = ada_w.shape[2]

    small = jnp.concatenate([c.reshape(-1), gdn_conv_w.reshape(-1), ssm_conv_w.reshape(-1), sc_conv_w.reshape(-1)])
    n_small = small.shape[0]
    g_small = _ag_small(_pad_rows128(small), "ag_c_conv").reshape(N_DEV, -1)[:, :n_small]
    c_all = g_small[:, :D]
    o = D
    gdn_cw = g_small[:, o:o + 2 * 4 * 192].reshape(N_DEV, 2, 4, 192).transpose(1, 2, 0, 3).reshape(2, 4, 1536)
    o += 2 * 4 * 192
    ssm_cw = g_small[:, o:o + 2 * 4 * 128].reshape(N_DEV, 2, 4, 128).transpose(1, 2, 0, 3).reshape(2, 4, 1024)
    o += 2 * 4 * 128
    sc_cw = g_small[:, o:o + 2 * 3 * 64].reshape(N_DEV, 2, 3, 64).transpose(1, 2, 0, 3).reshape(2, 3, 512)

    ada_b_sh = lax.dynamic_slice(ada_b, (0, me * S_ADA), (2, S_ADA))
    mod_sh = _ada_fwd(c_all, ada_w, ada_b_sh, "ada_fwd")
    mod_all = _ag_small(mod_sh.reshape(-1, 128), "ag_mod").reshape(N_DEV, 2, N_DEV, S_ADA)
    mod_me = lax.dynamic_index_in_dim(mod_all, me, axis=2, keepdims=False)
    mod_me = mod_me.transpose(1, 0, 2).reshape(2, 6 * D)

    wt_sh = [jnp.pad(w_in[i].T.astype(BF16), ((0, SHARD_PAD - SHARD_IN), (0, 0))) for i in range(2)]
    wo_sh, wu_sh, wd_sh = ([w[i].astype(BF16) for i in range(2)] for w in (w_out, w_up, w_down))
    full_w_in = lambda g: _permute_rows(g[:, :SHARD_IN].reshape(IN_W, D))
    win_t = [full_w_in(_ag_multi([wt_sh[0]], "ag_w_in0")[0]), None]
    wout_f, wup_g, wdown_f = [None, None], [None, None], [None, None]

    inv_freq = ROPE_THETA ** (-jnp.arange(0, 64, 2, dtype=F32) / 64)
    ang = positions[0].astype(F32)[:, None] * inv_freq
    cos = jnp.tile(jnp.cos(ang), (1, 4))
    sin = jnp.tile(jnp.sin(ang), (1, 4))

    vec = lambda a: a.reshape(1, -1)

    mods = [[vec(t) for t in jnp.split(mod_me[i], 6)] for i in range(2)]
    saved = []
    xc = x0
    h = _pre_fwd(x0, vec(norm_pre_mix[0]), mods[0][1], mods[0][0], "pre_mix_fwd0")
    for i in range(2):
        sh_a, sc_a, gt_a, sh_m, sc_m, gt_m = mods[i]
        gal, gdb = _row128(gdn_a_log[i], LANE_GA), _row128(gdn_dt_bias[i], LANE_GA)
        sal, sdb, sdd = (_row128(ssm_a_log[i], LANE_SDT), _row128(ssm_dt_bias[i], LANE_SDT), _row128(ssm_d[i], LANE_SDT))
        gnw, snw, skr = vec(gdn_norm_w[i]), vec(ssm_norm_w[i]), _row128(attn_sinks[i], 0)
        zero_b = jnp.zeros((1, 1536), F32)

        riders = [wo_sh[0], wu_sh[0]] if i == 0 else [wd_sh[1]]
        proj, *got = _mm(h, win_t[i], name=f"in_proj{i}", tb=True, tn=896, tk=2048, cargo=_AgCargo(riders))
        if i == 0:
            wout_f[0], wup_g[0] = got[0].reshape(D, D), got[1]
        else:
            wdown_f[1] = got[0].reshape(MLP, D)
        qkvc = _conv_fwd(proj, 0, 1536, gdn_cw[i], zero_b, f"gdn_conv_fwd{i}")
        if i == 0:
            ya, gdn_s, got_i = _gdn_fwd(qkvc, proj, gal, gdb, gnw, "gdn_fwd0", cargo=_AgCargo([wt_sh[1]]))
            win_t[1] = full_w_in(got_i)
        else:
            ya, gdn_s = _gdn_fwd(qkvc, proj, gal, gdb, gnw, "gdn_fwd1")
        xbcc = _conv_fwd(proj, P_SXBC // 1024, 1024, ssm_cw[i], vec(ssm_conv_b[i]), f"ssm_conv_fwd{i}")
        yb, ssd_s = _ssd_fwd(xbcc, proj, sal, sdb, sdd, snw, f"ssd_fwd{i}")
        yc = _swa_fwd(proj, cos, sin, skr, f"swa_fwd{i}")
        yd = _sc_fwd(proj, sc_cw[i], f"sc_fwd{i}")
        ycat = jnp.concatenate([ya, yb, yc, yd], axis=1)
        out = _mm(ycat, wout_f[i], name=f"out_proj{i}", tk=2048)
        x1, h2 = _post_pre_fwd(xc, out, vec(norm_post_mix[i]), gt_a, vec(norm_pre_mlp[i]), sc_m, sh_m, f"post_mix_pre_mlp_fwd{i}")
        if i == 0:
            act, u, got_d, got_o = _mm(h2, wup_g[0], name="up_proj0", tk=2048, mode="relu2", shards="b",
                                       cargo=_AgCargo([wd_sh[0], wo_sh[1]]))
            wdown_f[0], wout_f[1] = got_d.reshape(MLP, D), got_o.reshape(D, D)
            y2, wup_g[1] = _mm(act, wdown_f[0], name="down_proj0", tk=2048, cargo=_AgCargo([wu_sh[1]]))
        else:
            act, u = _mm(h2, wup_g[1], name="up_proj1", tk=2048, mode="relu2", shards="b")
            y2 = _mm(act, wdown_f[1], name="down_proj1", tk=2048)
        saved.append(dict(x0=xc, h=h, proj=proj, qkvc=qkvc, gdn_s=gdn_s, xbcc=xbcc, ssd_s=ssd_s, ycat=ycat, out=out,
                          x1=x1, h2=h2, act=act, u=u, y2=y2, rows=(gal, gdb, sal, sdb, sdd, gnw, snw, skr)))
        if i == 0:
            xc, h = _post_pre_fwd(x1, y2, vec(norm_post_mlp[0]), gt_m, vec(norm_pre_mix[1]), mods[1][1], mods[1][0],
                                  "post_mlp_pre_mix_fwd")

    sq, dx, *pend = _last_residual_loss(saved[1]["x1"], saved[1]["y2"], vec(norm_post_mlp[1]), mods[1][5], target,
                                        "post_mlp_loss")
    loss = lax.psum(0.5 * sq[0, 0] / float(D), ("x", "y", "c"))

    gw = [dict() for _ in range(2)]
    r_in, r_out, r_up, r_down = [None, None], [None, None], [None, None], [None, None]
    for i in (1, 0):
        sv = saved[i]
        sh_a, sc_a, gt_a, sh_m, sc_m, gt_m = mods[i]
        gal, gdb, sal, sdb, sdd, gnw, snw, skr = sv["rows"]
        proj = sv["proj"]
        dy2, g_npostmlp, d_gt_m = pend
        du = _mm(dy2, wdown_f[i], name=f"down_proj_dx{i}", tb=True, tk=2048, out_dtype=BF16, mode="drelu2", u=sv["u"])
        g_wdown = _mm(sv["act"], dy2, name=f"down_proj_dw{i}", ta=True, out_dtype=BF16).reshape(N_DEV, MLP // N_DEV, D)
        dh2, r_down[i] = _mm(du, wup_g[i], name=f"up_proj_dx{i}", tb=True, tk=1024, out_dtype=BF16, shards="b",
                             cargo=_ExCargo([[g_wdown]]))
        g_wup = _mm(sv["h2"], du, name=f"up_proj_dw{i}", ta=True, out_dtype=BF16, shards="o")
        dx1, dout, g_npremlp, d_sc_m, d_sh_m, g_npostmix, d_gt_a = _pre_post_bwd(
            sv["x1"], vec(norm_pre_mlp[i]), sc_m, sh_m, dh2, dx, sv["out"], vec(norm_post_mix[i]), gt_a,
            f"pre_mlp_post_mix_bwd{i}")
        dycat = _mm(dout, wout_f[i], name=f"out_proj_dx{i}", tb=True, tk=2048)
        g_wout = _mm(sv["ycat"], dout, name=f"out_proj_dw{i}", ta=True, out_dtype=BF16).reshape(N_DEV, D // N_DEV, D)

        dqkvc, dgz, dgate_g, d_gal, d_gdb, d_gnw, r_up[i] = _gdn_bwd(
            sv["qkvc"], proj, gal, gdb, gnw, sv["gdn_s"], dycat, f"gdn_bwd{i}", cargo=_ExCargo([[g_wup]]))
        dqkv, g_gdn_cw, _ = _conv_bwd(dqkvc, proj, 0, 1536, gdn_cw[i], f"gdn_conv_bwd{i}")
        dxbcc, dsz, dgate_s, d_sal, d_sdb, d_sdd, d_snw = _ssd_bwd(
            sv["xbcc"], proj, sal, sdb, sdd, snw, sv["ssd_s"], dycat, f"ssd_bwd{i}")
        dsxbc, g_ssm_cw, g_ssm_cb = _conv_bwd(dxbcc, proj, P_SXBC // 1024, 1024, ssm_cw[i], f"ssm_conv_bwd{i}")
        daq, dak, dav, d_skr = _swa_bwd(proj, cos, sin, skr, dycat, f"swa_bwd{i}")
        dsc, g_sc_cw = _sc_bwd(dycat, proj, sc_cw[i], f"sc_bwd{i}")
        dgate = (dgate_g + dgate_s).astype(BF16)
        dproj = jnp.concatenate([dqkv, dgz, dsxbc, dsz, daq, dsc, dak, dav, dgate], axis=1)
        g_wint, r_out[i] = _mm(dproj, sv["h"], name=f"in_proj_dw{i}", ta=True, tm=896, out_dtype=BF16,
                               cargo=_ExCargo([[g_wout]]))
        g_wint = jnp.pad(_unpermute_rows(g_wint).reshape(N_DEV, SHARD_IN, D), ((0, 0), (0, SHARD_PAD - SHARD_IN), (0, 0)))
        dh, r_in[i] = _mm(dproj, win_t[i], name=f"in_proj_dx{i}", tk=896, out_dtype=BF16, cargo=_ExCargo([[g_wint]]))
        if i == 1:
            dx, dy2_0, g_npremix, d_sc_a, d_sh_a, *post0 = _pre_post_bwd(
                sv["x0"], vec(norm_pre_mix[1]), sc_a, sh_a, dh, dx1, saved[0]["y2"], vec(norm_post_mlp[0]), mods[0][5],
                "pre_mix_post_mlp_bwd")
            pend = (dy2_0, *post0)
        else:
            dx, g_npremix, d_sc_a, d_sh_a = _pre_bwd(sv["x0"], vec(norm_pre_mix[0]), sc_a, sh_a, dh, dx1, "pre_mix_bwd0")

        gw[i] = dict(
            dmod=jnp.concatenate([d_sh_a, d_sc_a, d_gt_a, d_sh_m, d_sc_m, d_gt_m], axis=1).reshape(-1),
            norm_pre_mix=g_npremix.reshape(-1), norm_post_mix=g_npostmix.reshape(-1),
            norm_pre_mlp=g_npremlp.reshape(-1), norm_post_mlp=g_npostmlp.reshape(-1),
            gdn_conv_w=g_gdn_cw.reshape(-1), gdn_a_log=d_gal[0, LANE_GA:LANE_GA + 4], gdn_dt_bias=d_gdb[0, LANE_GA:LANE_GA + 4],
            gdn_norm_w=d_gnw.reshape(-1), ssm_conv_w=g_ssm_cw.reshape(-1), ssm_conv_b=g_ssm_cb.reshape(-1),
            ssm_a_log=d_sal[0, LANE_SDT:LANE_SDT + 8], ssm_dt_bias=d_sdb[0, LANE_SDT:LANE_SDT + 8],
            ssm_d=d_sdd[0, LANE_SDT:LANE_SDT + 8], ssm_norm_w=d_snw.reshape(-1), attn_sinks=d_skr[0, 0:8],
            sc_conv_w=g_sc_cw.reshape(-1))
    grad_x = dx[None]

    grads, deltas, new_m, new_v = {}, {}, {}, {}
    for n_, r, w, m, v in (("w_out", r_out, w_out, m_w_out, v_w_out), ("w_up", r_up, w_up, m_w_up, v_w_up),
                           ("w_down", r_down, w_down, m_w_down, v_w_down)):
        grads[n_], deltas[n_], new_m[n_], new_v[n_] = _sum8_adamw(r[0], r[1], w, m, v, f"sum_adamw_{n_}")
    g_int = jnp.stack([_sum8(r_in[i].reshape(N_DEV, SHARD_PAD, D), f"grad_sum_w_in{i}") for i in range(2)])
    g_in = g_int[:, :SHARD_IN].transpose(0, 2, 1)
    dl, nm, nv = _adamw_call(w_in.reshape(-1, SHARD_IN), g_in.reshape(-1, SHARD_IN), m_w_in.reshape(-1, SHARD_IN),
                             v_w_in.reshape(-1, SHARD_IN), "adamw_w_in")
    grads["w_in"], deltas["w_in"] = g_in, dl.reshape(w_in.shape)
    new_m["w_in"], new_v["w_in"] = nm.reshape(w_in.shape), nv.reshape(w_in.shape)

    small_names = ["dmod", "norm_pre_mix", "norm_post_mix", "norm_pre_mlp", "norm_post_mlp", "gdn_conv_w", "gdn_a_log",
                   "gdn_dt_bias", "gdn_norm_w", "ssm_conv_w", "ssm_conv_b", "ssm_a_log", "ssm_dt_bias", "ssm_d",
                   "ssm_norm_w", "attn_sinks", "sc_conv_w"]
    flat = jnp.concatenate([gw[i][nm_] for nm_ in small_names for i in range(2)])
    n_flat = flat.shape[0]
    g_all = _ag_small(_pad_rows128(flat), "ag_small_grads")
    tot = _sum8(g_all, "small_grad_sum").reshape(-1)[:n_flat]
    dmod_all = g_all.reshape(N_DEV, -1)[:, :2 * 6 * D].reshape(N_DEV, 2, 6 * D)
    sm = {}
    o = 0
    for nm_ in small_names:
        n = gw[0][nm_].shape[0]
        sm[nm_] = jnp.stack([tot[o:o + n], tot[o + n:o + 2 * n]])
        o += 2 * n

    dmod_sh = lax.dynamic_slice(dmod_all.transpose(1, 0, 2), (0, 0, me * S_ADA), (2, N_DEV, S_ADA))
    grads["ada_w"], deltas["ada_w"], new_m["ada_w"], new_v["ada_w"] = _ada_bwd_adamw(
        c_all, dmod_sh, ada_w, m_ada_w, v_ada_w, "ada_bwd_adamw")

    def shard_cols(full, K, C):
        return lax.dynamic_slice(full.reshape(2, K, C), (0, 0, me * (C // N_DEV)), (2, K, C // N_DEV))

    small_g = dict(
        ada_b=sm["dmod"], norm_pre_mix=sm["norm_pre_mix"], norm_post_mix=sm["norm_post_mix"],
        norm_pre_mlp=sm["norm_pre_mlp"], norm_post_mlp=sm["norm_post_mlp"],
        gdn_conv_w=shard_cols(sm["gdn_conv_w"], 4, 1536), gdn_a_log=sm["gdn_a_log"], gdn_dt_bias=sm["gdn_dt_bias"],
        gdn_norm_w=sm["gdn_norm_w"], ssm_conv_w=shard_cols(sm["ssm_conv_w"], 4, 1024), ssm_conv_b=sm["ssm_conv_b"],
        ssm_a_log=sm["ssm_a_log"], ssm_dt_bias=sm["ssm_dt_bias"], ssm_d=sm["ssm_d"], ssm_norm_w=sm["ssm_norm_w"],
        attn_sinks=sm["attn_sinks"], sc_conv_w=shard_cols(sm["sc_conv_w"], 3, 512))
    small_w = dict(
        ada_b=(ada_b, m_ada_b, v_ada_b), norm_pre_mix=(norm_pre_mix, m_norm_pre_mix, v_norm_pre_mix),
        norm_post_mix=(norm_post_mix, m_norm_post_mix, v_norm_post_mix),
        norm_pre_mlp=(norm_pre_mlp, m_norm_pre_mlp, v_norm_pre_mlp),
        norm_post_mlp=(norm_post_mlp, m_norm_post_mlp, v_norm_post_mlp),
        gdn_conv_w=(gdn_conv_w, m_gdn_conv_w, v_gdn_conv_w), gdn_a_log=(gdn_a_log, m_gdn_a_log, v_gdn_a_log),
        gdn_dt_bias=(gdn_dt_bias, m_gdn_dt_bias, v_gdn_dt_bias), gdn_norm_w=(gdn_norm_w, m_gdn_norm_w, v_gdn_norm_w),
        ssm_conv_w=(ssm_conv_w, m_ssm_conv_w, v_ssm_conv_w), ssm_conv_b=(ssm_conv_b, m_ssm_conv_b, v_ssm_conv_b),
        ssm_a_log=(ssm_a_log, m_ssm_a_log, v_ssm_a_log), ssm_dt_bias=(ssm_dt_bias, m_ssm_dt_bias, v_ssm_dt_bias),
        ssm_d=(ssm_d, m_ssm_d, v_ssm_d), ssm_norm_w=(ssm_norm_w, m_ssm_norm_w, v_ssm_norm_w),
        attn_sinks=(attn_sinks, m_attn_sinks, v_attn_sinks), sc_conv_w=(sc_conv_w, m_sc_conv_w, v_sc_conv_w))
    names = list(small_w)
    sizes = [int(np.prod(small_w[n_][0].shape)) for n_ in names]
    pk = lambda j: _pad_rows128(jnp.concatenate([small_w[n_][j].reshape(-1) for n_ in names]))
    gk = _pad_rows128(jnp.concatenate([small_g[n_].reshape(-1) for n_ in names]))
    dl, nm, nv = _adamw_call(pk(0), gk, pk(1), pk(2), "adamw_small")
    dl, nm, nv = dl.reshape(-1), nm.reshape(-1), nv.reshape(-1)
    o = 0
    for n_, sz in zip(names, sizes):
        shp = small_w[n_][0].shape
        grads[n_] = small_g[n_].reshape(shp)
        deltas[n_], new_m[n_], new_v[n_] = dl[o:o + sz].reshape(shp), nm[o:o + sz].reshape(shp), nv[o:o + sz].reshape(shp)
        o += sz

    order = ["ada_w", "ada_b", "norm_pre_mix", "norm_post_mix", "norm_pre_mlp", "norm_post_mlp", "w_in", "w_out",
             "gdn_conv_w", "gdn_a_log", "gdn_dt_bias", "gdn_norm_w", "ssm_conv_w", "ssm_conv_b", "ssm_a_log",
             "ssm_dt_bias", "ssm_d", "ssm_norm_w", "attn_sinks", "sc_conv_w", "w_up", "w_down"]
    return (loss, grad_x, *[grads[n_] for n_ in order], *[deltas[n_] for n_ in order],
            *[new_m[n_] for n_ in order], *[new_v[n_] for n_ in order])
```

```python
import functools
import math

import jax
import jax.numpy as jnp
import numpy as np
from jax import lax
from jax.experimental import pallas as pl
from jax.experimental.pallas import tpu as pltpu

F32 = jnp.float32
BF16 = jnp.bfloat16
MESH = pl.DeviceIdType.MESH

N_DEV = 8
D = 2048
GW = 512
MLP = 8192
EPS = 1e-6
IN_W = 5904
SHARD_IN = IN_W // N_DEV
SHARD_PAD = 752
PW = 6272
ROPE_THETA = 10000.0
WINDOW = 128
GDN_CHUNK = 64
SSM_CHUNK = 128
NEG = -1e30

ADAM_LR, ADAM_B1, ADAM_B2, ADAM_EPS, ADAM_WD, ADAM_STEP = 0.001, 0.9, 0.999, 1e-08, 0.01, 10

O_GQ, O_GK, O_GV, O_GZ, O_GB, O_GA = 0, 512, 1024, 1536, 2048, 2052
O_SZ, O_SXBC, O_SDT = 2056, 2568, 3592
O_AQ, O_AK, O_AV = 3600, 4112, 4240
O_CB, O_CC, O_CH = 4368, 4880, 5392
P_QKV, P_GZ, P_SXBC, P_SZ, P_AQ, P_CB, P_AK, P_AV, P_GATE = 0, 1536, 2048, 3072, 3584, 4096, 5632, 5888, 6144
LANE_GB, LANE_GA, LANE_SDT = 0, 4, 8

VMEM_BIG = 56 * 1024 * 1024


def _cparams(sem=None, vmem=None):
    kw = {}
    if sem is not None:
        kw["dimension_semantics"] = sem
    if vmem is not None:
        kw["vmem_limit_bytes"] = vmem
    return pltpu.CompilerParams(**kw)


def _me():
    x, y, c = lax.axis_index("x"), lax.axis_index("y"), lax.axis_index("c")
    return x, y, c, 4 * x + 2 * y + c


def _peer(x, y, c, k):
    px = 1 - x if (k >> 2) & 1 else x
    py = 1 - y if (k >> 1) & 1 else y
    pc = 1 - c if k & 1 else c
    return (px, py, pc), 4 * px + 2 * py + pc


def _ag_small(v, name):
    R, W = v.shape

    def body(x_ref, out_ref, send_sems, recv_sems):
        x, y, c, me = _me()
        out_ref[me] = x_ref[...]
        copies = []
        for k in range(1, N_DEV):
            peer, _ = _peer(x, y, c, k)
            cp = pltpu.make_async_remote_copy(
                src_ref=x_ref, dst_ref=out_ref.at[me], send_sem=send_sems.at[k - 1], recv_sem=recv_sems.at[k - 1],
                device_id=peer, device_id_type=MESH)
            cp.start()
            copies.append(cp)
        for cp in copies:
            cp.wait()

    return pl.pallas_call(
        body, name=name,
        out_shape=jax.ShapeDtypeStruct((N_DEV, R, W), v.dtype),
        in_specs=[pl.BlockSpec(memory_space=pltpu.VMEM)],
        out_specs=pl.BlockSpec(memory_space=pltpu.VMEM),
        scratch_shapes=[pltpu.SemaphoreType.DMA((N_DEV - 1,)), pltpu.SemaphoreType.DMA((N_DEV - 1,))],
    )(v)


def _ag_multi(arrs, name):
    cargo = _AgCargo(arrs)
    A = cargo.n_in

    def body(*refs):
        start, mid, end = cargo.phases(refs[:A], refs[A:A + cargo.n_out], refs[A + cargo.n_out:])
        start()
        mid()
        end()

    hbm = pl.BlockSpec(memory_space=pl.ANY)
    return pl.pallas_call(
        body, name=name, out_shape=tuple(cargo.out_shapes), in_specs=[hbm] * A, out_specs=tuple([hbm] * cargo.n_out),
        scratch_shapes=cargo.scratch)(*cargo.arrays)


class _AgCargo:
    def __init__(self, arrs):
        A = len(arrs)
        self.arrays = list(arrs)
        self.n_in = self.n_out = A
        self.out_shapes = [jax.ShapeDtypeStruct((N_DEV,) + v.shape, v.dtype) for v in arrs]
        self.scratch = [pltpu.SemaphoreType.DMA((7 * A,)), pltpu.SemaphoreType.DMA((7 * A,)),
                        pltpu.SemaphoreType.DMA((A,))]

    def phases(self, x_refs, out_refs, sems):
        A = self.n_in
        send_sems, recv_sems, local_sems = sems

        def ctx():
            x, y, c, me = _me()
            return x, y, c, me, (x, y, 1 - c), [(1 - x, y), (x, 1 - y), (1 - x, 1 - y)]

        def copy(a, k, block, to, src=None):
            slot = out_refs[a].at[4 * block[0] + 2 * block[1] + block[2]]
            return pltpu.make_async_remote_copy(
                src_ref=slot if src is None else src, dst_ref=slot, send_sem=send_sems.at[7 * a + k],
                recv_sem=recv_sems.at[7 * a + k], device_id=to, device_id_type=MESH)

        def start():
            x, y, c, me, sibling, chips = ctx()
            for a in range(A):
                pltpu.make_async_copy(x_refs[a], out_refs[a].at[me], local_sems.at[a]).start()
            for a in range(A):
                copy(a, 0, (x, y, c), sibling, src=x_refs[a]).start()
                for j, chip in enumerate(chips):
                    copy(a, 1 + j, (x, y, c), (*chip, c), src=x_refs[a]).start()

        def mid():
            x, y, c, me, sibling, chips = ctx()
            for j, chip in enumerate(chips):
                for a in range(A):
                    copy(a, 1 + j, (*chip, c), (x, y, c)).wait_recv()
                    copy(a, 4 + j, (*chip, c), sibling).start()

        def end():
            x, y, c, me, sibling, chips = ctx()
            for a in range(A):
                copy(a, 0, sibling, (x, y, c)).wait_recv()
                for j, chip in enumerate(chips):
                    copy(a, 4 + j, (*chip, 1 - c), (x, y, c)).wait_recv()
            for a in range(A):
                copy(a, 0, (x, y, c), sibling, src=x_refs[a]).wait_send()
                for j, chip in enumerate(chips):
                    copy(a, 1 + j, (x, y, c), (*chip, c), src=x_refs[a]).wait_send()
                    copy(a, 4 + j, (*chip, c), sibling).wait_send()
                pltpu.make_async_copy(x_refs[a], out_refs[a].at[me], local_sems.at[a]).wait()

        return start, mid, end


class _ExCargo:
    def __init__(self, groups):
        self.flat = [(w, i) for w, layers in enumerate(groups) for i, _ in enumerate(layers)]
        self.arrays = [g for layers in groups for g in layers]
        A = len(self.arrays)
        self.n_in, self.n_out = A, len(groups)
        self.out_shapes = [jax.ShapeDtypeStruct((N_DEV, len(layers)) + layers[0].shape[1:], layers[0].dtype)
                           for layers in groups]
        self.scratch = [pltpu.SemaphoreType.DMA((7 * A,)), pltpu.SemaphoreType.DMA((7 * A,)),
                        pltpu.SemaphoreType.DMA((A,))]

    def phases(self, g_refs, out_refs, sems):
        send_sems, recv_sems, local_sems = sems

        def copies():
            x, y, c, me = _me()
            local = [pltpu.make_async_copy(g_refs[a].at[me], out_refs[w].at[me, i], local_sems.at[a])
                     for a, (w, i) in enumerate(self.flat)]
            remote = []
            for k in range(1, N_DEV):
                peer, pid = _peer(x, y, c, k)
                for a, (w, i) in enumerate(self.flat):
                    remote.append(pltpu.make_async_remote_copy(
                        src_ref=g_refs[a].at[pid], dst_ref=out_refs[w].at[me, i], send_sem=send_sems.at[7 * a + k - 1],
                        recv_sem=recv_sems.at[7 * a + k - 1], device_id=peer, device_id_type=MESH))
            return local, remote

        def start():
            local, remote = copies()
            for cp in local + remote:
                cp.start()

        def end():
            local, remote = copies()
            for cp in remote + local:
                cp.wait()

        return start, (lambda: None), end


def _mm(a, b, *, name, ta=False, tb=False, tm=1024, tn=1024, tk=1024, out_dtype=F32, mode="plain", u=None,
        shards=None, cargo=None):
    K, M = a.shape if ta else a.shape[::-1]
    if shards == "b":
        N, K2 = (b.shape[1], b.shape[0] * b.shape[2]) if tb else (b.shape[0] * b.shape[2], b.shape[1])
    else:
        N, K2 = b.shape if tb else b.shape[::-1]
    assert K == K2, (a.shape, b.shape)
    tm, tn, tk = min(tm, M), min(tn, N), min(tk, K)
    assert M % tm == 0 and N % tn == 0 and K % tk == 0, (M, N, K, tm, tn, tk)
    nk = K // tk
    a_spec = pl.BlockSpec((tk, tm), lambda i, j, k: (k, i)) if ta else pl.BlockSpec((tm, tk), lambda i, j, k: (i, k))
    if shards == "b" and tb:
        assert tk == b.shape[2]
        b_spec = pl.BlockSpec((None, tn, tk), lambda i, j, k: (k, j, 0))
    elif shards == "b":
        assert tn == b.shape[2]
        b_spec = pl.BlockSpec((None, tk, tn), lambda i, j, k: (j, k, 0))
    else:
        b_spec = pl.BlockSpec((tn, tk), lambda i, j, k: (j, k)) if tb else pl.BlockSpec((tk, tn), lambda i, j, k: (k, j))
    if shards == "o":
        o_spec = pl.BlockSpec((None, tm, tn), lambda i, j, k: (j, i, 0))
        o_shape = (N // tn, M, tn)
    else:
        o_spec = pl.BlockSpec((tm, tn), lambda i, j, k: (i, j))
        o_shape = (M, N)
    dn = (((0 if ta else 1,), (1 if tb else 0,)), ((), ()))
    ni_, nj = M // tm, N // tn
    nsteps = ni_ * nj * nk
    n_in = 3 if mode == "drelu2" else 2
    n_out = 2 if mode == "relu2" else 1
    c_arr, c_ispec, c_oshape, c_ospec = _cargo_io(cargo)
    ni, no = len(c_arr), len(c_oshape)

    def body(*refs):
        a_ref, b_ref = refs[0], refs[1]
        u_ref = refs[2] if mode == "drelu2" else None
        outs = refs[n_in + ni:n_in + ni + n_out]
        acc = refs[n_in + ni + n_out + no]
        k = pl.program_id(2)
        step = (pl.program_id(0) * nj + pl.program_id(1)) * nk + k
        drain = _carry(cargo, (refs[n_in:n_in + ni], refs[n_in + ni + n_out:n_in + ni + n_out + no],
                               refs[n_in + ni + n_out + no + 1:]), step, nsteps)

        @pl.when(k == 0)
        def _():
            acc[...] = jnp.zeros_like(acc)

        acc[...] += lax.dot_general(a_ref[...].astype(BF16), b_ref[...].astype(BF16), dn, preferred_element_type=F32)

        @pl.when(k == nk - 1)
        def _():
            r = acc[...]
            if mode == "plain":
                outs[0][...] = r.astype(out_dtype)
            elif mode == "relu2":
                rr = jnp.maximum(r, 0.0)
                outs[0][...] = (rr * rr).astype(BF16)
                outs[1][...] = r.astype(BF16)
            else:
                outs[0][...] = (r * (2.0 * jnp.maximum(u_ref[...].astype(F32), 0.0))).astype(out_dtype)

        drain()

    in_specs, args = [a_spec, b_spec], [a, b]
    if mode == "drelu2":
        in_specs.append(o_spec)
        args.append(u)
    if mode == "relu2":
        out_shape = [jax.ShapeDtypeStruct(o_shape, BF16), jax.ShapeDtypeStruct(o_shape, BF16)]
        out_specs = [o_spec, o_spec]
    else:
        out_shape = [jax.ShapeDtypeStruct(o_shape, out_dtype)]
        out_specs = [o_spec]
    res = pl.pallas_call(
        body, name=name, grid=(ni_, nj, nk), in_specs=in_specs + c_ispec, out_specs=tuple(out_specs + c_ospec),
        out_shape=tuple(out_shape + c_oshape),
        scratch_shapes=[pltpu.VMEM((tm, tn), F32)] + (cargo.scratch if cargo else []),
        compiler_params=_cparams(("arbitrary",) * 3 if cargo else ("parallel", "parallel", "arbitrary"), VMEM_BIG),
    )(*args, *c_arr)
    return res[0] if len(res) == 1 else res


ROW_T = 256


def _rms(x, w):
    return x * lax.rsqrt(jnp.mean(x * x, axis=-1, keepdims=True) + EPS) * w


def _pre_fn(x, w, scale, shift):
    return _rms(x, w) * (1.0 + scale) + shift


def _post_fn(y, w, gate):
    return gate * _rms(y, w)


def _row_spec(T, W):
    return pl.BlockSpec((T, W), lambda i: (i, 0))


def _vec_spec(W):
    return pl.BlockSpec((1, W), lambda i: (0, 0))


def _pre_fwd(x, w, scale, shift, name):
    L = x.shape[0]
    T = min(ROW_T, L)

    def body(x_ref, w_ref, sc_ref, sh_ref, h_ref):
        h_ref[...] = _pre_fn(x_ref[...], w_ref[...], sc_ref[...], sh_ref[...]).astype(BF16)

    return pl.pallas_call(
        body, name=name, grid=(L // T,), in_specs=[_row_spec(T, D), _vec_spec(D), _vec_spec(D), _vec_spec(D)],
        out_specs=_row_spec(T, D), out_shape=jax.ShapeDtypeStruct((L, D), BF16),
        compiler_params=_cparams(("parallel",), VMEM_BIG))(x, w, scale, shift)


def _pre_bwd(x, w, scale, shift, dh, dres, name):
    L = x.shape[0]
    T = min(ROW_T, L)

    def body(x_ref, w_ref, sc_ref, sh_ref, dh_ref, dres_ref, dx_ref, dw_ref, dsc_ref, dsh_ref):
        @pl.when(pl.program_id(0) == 0)
        def _():
            dw_ref[...] = jnp.zeros_like(dw_ref)
            dsc_ref[...] = jnp.zeros_like(dsc_ref)
            dsh_ref[...] = jnp.zeros_like(dsh_ref)

        _, vjp = jax.vjp(_pre_fn, x_ref[...], w_ref[...], sc_ref[...], sh_ref[...])
        dx, dw, dsc, dsh = vjp(dh_ref[...].astype(F32))
        dx_ref[...] = dres_ref[...] + dx
        dw_ref[...] += dw
        dsc_ref[...] += dsc
        dsh_ref[...] += dsh

    vec = jax.ShapeDtypeStruct((1, D), F32)
    return pl.pallas_call(
        body, name=name, grid=(L // T,),
        in_specs=[_row_spec(T, D), _vec_spec(D), _vec_spec(D), _vec_spec(D), _row_spec(T, D), _row_spec(T, D)],
        out_specs=(_row_spec(T, D), _vec_spec(D), _vec_spec(D), _vec_spec(D)),
        out_shape=(jax.ShapeDtypeStruct((L, D), F32), vec, vec, vec),
        compiler_params=_cparams(("arbitrary",), VMEM_BIG))(x, w, scale, shift, dh, dres)


def _post_pre_fwd(x, y, w_post, gate, w_pre, scale, shift, name):
    L = x.shape[0]
    T = min(ROW_T, L)

    def body(x_ref, y_ref, wp_ref, g_ref, w_ref, sc_ref, sh_ref, o_ref, h_ref):
        xn = x_ref[...] + _post_fn(y_ref[...], wp_ref[...], g_ref[...])
        o_ref[...] = xn
        h_ref[...] = _pre_fn(xn, w_ref[...], sc_ref[...], sh_ref[...]).astype(BF16)

    return pl.pallas_call(
        body, name=name, grid=(L // T,),
        in_specs=[_row_spec(T, D), _row_spec(T, D)] + [_vec_spec(D)] * 5,
        out_specs=(_row_spec(T, D), _row_spec(T, D)),
        out_shape=(jax.ShapeDtypeStruct((L, D), F32), jax.ShapeDtypeStruct((L, D), BF16)),
        compiler_params=_cparams(("parallel",), VMEM_BIG))(x, y, w_post, gate, w_pre, scale, shift)


def _pre_post_bwd(x, w_pre, scale, shift, dh, dres, y, w_post, gate, name):
    L = x.shape[0]
    T = min(ROW_T, L)

    def body(x_ref, w_ref, sc_ref, sh_ref, dh_ref, dres_ref, y_ref, wp_ref, g_ref,
             dx_ref, dy_ref, dw_ref, dsc_ref, dsh_ref, dwp_ref, dg_ref):
        @pl.when(pl.program_id(0) == 0)
        def _():
            for r in (dw_ref, dsc_ref, dsh_ref, dwp_ref, dg_ref):
                r[...] = jnp.zeros_like(r)

        _, vjp = jax.vjp(_pre_fn, x_ref[...], w_ref[...], sc_ref[...], sh_ref[...])
        dx, dw, dsc, dsh = vjp(dh_ref[...].astype(F32))
        dx = dres_ref[...] + dx
        dx_ref[...] = dx
        _, vjp2 = jax.vjp(_post_fn, y_ref[...], wp_ref[...], g_ref[...])
        dy, dwp, dg = vjp2(dx)
        dy_ref[...] = dy.astype(BF16)
        dw_ref[...] += dw
        dsc_ref[...] += dsc
        dsh_ref[...] += dsh
        dwp_ref[...] += dwp
        dg_ref[...] += dg

    vec = jax.ShapeDtypeStruct((1, D), F32)
    v = _vec_spec(D)
    r = _row_spec(T, D)
    return pl.pallas_call(
        body, name=name, grid=(L // T,), in_specs=[r, v, v, v, r, r, r, v, v],
        out_specs=(r, r, v, v, v, v, v),
        out_shape=(jax.ShapeDtypeStruct((L, D), F32), jax.ShapeDtypeStruct((L, D), BF16), vec, vec, vec, vec, vec),
        compiler_params=_cparams(("arbitrary",), VMEM_BIG))(x, w_pre, scale, shift, dh, dres, y, w_post, gate)


def _last_residual_loss(x, y, w, gate, target, name):
    L = x.shape[0]
    T = min(ROW_T, L)

    def body(x_ref, y_ref, w_ref, g_ref, t_ref, s_ref, d_ref, dy_ref, dw_ref, dg_ref):
        @pl.when(pl.program_id(0) == 0)
        def _():
            for r in (s_ref, dw_ref, dg_ref):
                r[...] = jnp.zeros_like(r)

        res, vjp = jax.vjp(_post_fn, y_ref[...], w_ref[...], g_ref[...])
        e = (x_ref[...] + res) - t_ref[...]
        d = e / float(D)
        d_ref[...] = d
        s_ref[...] += jnp.sum(e * e)
        dy, dw, dg = vjp(d)
        dy_ref[...] = dy.astype(BF16)
        dw_ref[...] += dw
        dg_ref[...] += dg

    vec = jax.ShapeDtypeStruct((1, D), F32)
    r, v = _row_spec(T, D), _vec_spec(D)
    return pl.pallas_call(
        body, name=name, grid=(L // T,), in_specs=[r, r, v, v, r],
        out_specs=(pl.BlockSpec((8, 128), lambda i: (0, 0)), r, r, v, v),
        out_shape=(jax.ShapeDtypeStruct((8, 128), F32), jax.ShapeDtypeStruct((L, D), F32),
                   jax.ShapeDtypeStruct((L, D), BF16), vec, vec),
        compiler_params=_cparams(("arbitrary",), VMEM_BIG))(x, y, w, gate, target)


CONV_T = 512


def _conv_fwd(x, ci, C, w, b, name):
    L = x.shape[0]
    K = w.shape[0]
    T = min(CONV_T, L)

    def body(xc_ref, xp_ref, w_ref, b_ref, y_ref, buf):
        i = pl.program_id(0)
        buf[0:8, :] = jnp.where(i > 0, xp_ref[...], 0.0)
        buf[8:T + 8, :] = xc_ref[...]
        acc = jnp.zeros((T, C), F32) + b_ref[...]
        for k in range(K):
            acc = acc + w_ref[k:k + 1, :] * buf[pl.ds(8 - (K - 1) + k, T), :]
        y_ref[...] = acc

    return pl.pallas_call(
        body, name=name, grid=(L // T,),
        in_specs=[pl.BlockSpec((T, C), lambda i: (i, ci)),
                  pl.BlockSpec((8, C), lambda i: (jnp.maximum(i * (T // 8) - 1, 0), ci)),
                  pl.BlockSpec((K, C), lambda i: (0, 0)), _vec_spec(C)],
        out_specs=_row_spec(T, C), out_shape=jax.ShapeDtypeStruct((L, C), F32),
        scratch_shapes=[pltpu.VMEM((T + 8, C), F32)],
        compiler_params=_cparams(("parallel",), VMEM_BIG))(x, x, w, b)


def _conv_bwd(dy, x, ci, C, w, name):
    L = x.shape[0]
    K = w.shape[0]
    T = min(CONV_T, L)
    nt = L // T

    def body(dc_ref, dn_ref, xc_ref, xp_ref, w_ref, dx_ref, dw_ref, db_ref, dbuf, xbuf):
        i = pl.program_id(0)

        @pl.when(i == 0)
        def _():
            dw_ref[...] = jnp.zeros_like(dw_ref)
            db_ref[...] = jnp.zeros_like(db_ref)

        dyc = dc_ref[...]
        dbuf[0:T, :] = dyc
        dbuf[T:T + 8, :] = jnp.where(i < nt - 1, dn_ref[...], 0.0)
        xbuf[0:8, :] = jnp.where(i > 0, xp_ref[...], 0.0)
        xbuf[8:T + 8, :] = xc_ref[...]
        dx = jnp.zeros((T, C), F32)
        for k in range(K):
            dx = dx + w_ref[k:k + 1, :] * dbuf[pl.ds(K - 1 - k, T), :]
            dw_ref[k:k + 1, :] += jnp.sum(dyc * xbuf[pl.ds(8 - (K - 1) + k, T), :], axis=0, keepdims=True)
        dx_ref[...] = dx.astype(BF16)
        db_ref[...] += jnp.sum(dyc, axis=0, keepdims=True)

    return pl.pallas_call(
        body, name=name, grid=(nt,),
        in_specs=[_row_spec(T, C),
                  pl.BlockSpec((8, C), lambda i: (jnp.minimum((i + 1) * (T // 8), L // 8 - 1), 0)),
                  pl.BlockSpec((T, C), lambda i: (i, ci)),
                  pl.BlockSpec((8, C), lambda i: (jnp.maximum(i * (T // 8) - 1, 0), ci)),
                  pl.BlockSpec((K, C), lambda i: (0, 0))],
        out_specs=(_row_spec(T, C), pl.BlockSpec((K, C), lambda i: (0, 0)), _vec_spec(C)),
        out_shape=(jax.ShapeDtypeStruct((L, C), BF16), jax.ShapeDtypeStruct((K, C), F32),
                   jax.ShapeDtypeStruct((1, C), F32)),
        scratch_shapes=[pltpu.VMEM((T + 8, C), F32), pltpu.VMEM((T + 8, C), F32)],
        compiler_params=_cparams(("arbitrary",), VMEM_BIG))(dy, dy, x, x, w)


def _sc_fwd(proj, w, name):
    L = proj.shape[0]
    K = w.shape[0]
    C = GW
    T = min(CONV_T, L)
    cb0 = P_CB // C

    def body(b_ref, cc_ref, ch_ref, cp_ref, hp_ref, w_ref, y_ref, buf):
        i = pl.program_id(0)
        buf[0:8, :] = jnp.where(i > 0, cp_ref[...] * hp_ref[...], 0.0)
        buf[8:T + 8, :] = cc_ref[...] * ch_ref[...]
        acc = jnp.zeros((T, C), F32)
        for k in range(K):
            acc = acc + w_ref[k:k + 1, :] * buf[pl.ds(8 - (K - 1) + k, T), :]
        y_ref[...] = (b_ref[...] * acc).astype(BF16)

    def cur(j):
        return pl.BlockSpec((T, C), lambda i: (i, cb0 + j))

    def prev(j):
        return pl.BlockSpec((8, C), lambda i: (jnp.maximum(i * (T // 8) - 1, 0), cb0 + j))

    return pl.pallas_call(
        body, name=name, grid=(L // T,),
        in_specs=[cur(0), cur(1), cur(2), prev(1), prev(2), pl.BlockSpec((K, C), lambda i: (0, 0))],
        out_specs=_row_spec(T, C), out_shape=jax.ShapeDtypeStruct((L, C), BF16),
        scratch_shapes=[pltpu.VMEM((T + 8, C), F32)],
        compiler_params=_cparams(("parallel",), VMEM_BIG))(proj, proj, proj, proj, proj, w)


def _sc_bwd(dycat, proj, w, name):
    L = proj.shape[0]
    K = w.shape[0]
    C = GW
    T = min(CONV_T, L)
    nt = L // T
    cb0 = P_CB // C

    def body(dy_ref, dyn_ref, b_ref, bn_ref, cc_ref, ch_ref, cp_ref, hp_ref, w_ref, d_ref, dw_ref, gbuf, ubuf):
        i = pl.program_id(0)

        @pl.when(i == 0)
        def _():
            dw_ref[...] = jnp.zeros_like(dw_ref)

        dy = dy_ref[...]
        cc, ch = cc_ref[...], ch_ref[...]
        g = dy * b_ref[...]
        gbuf[0:T, :] = g
        gbuf[T:T + 8, :] = jnp.where(i < nt - 1, dyn_ref[...] * bn_ref[...], 0.0)
        ubuf[0:8, :] = jnp.where(i > 0, cp_ref[...] * hp_ref[...], 0.0)
        ubuf[8:T + 8, :] = cc * ch
        conv = jnp.zeros((T, C), F32)
        du = jnp.zeros((T, C), F32)
        for k in range(K):
            us = ubuf[pl.ds(8 - (K - 1) + k, T), :]
            conv = conv + w_ref[k:k + 1, :] * us
            du = du + w_ref[k:k + 1, :] * gbuf[pl.ds(K - 1 - k, T), :]
            dw_ref[k:k + 1, :] += jnp.sum(g * us, axis=0, keepdims=True)
        d_ref[:, 0:C] = (dy * conv).astype(BF16)
        d_ref[:, C:2 * C] = (du * ch).astype(BF16)
        d_ref[:, 2 * C:3 * C] = (du * cc).astype(BF16)

    def cur(j):
        return pl.BlockSpec((T, C), lambda i: (i, cb0 + j))

    def prev(j):
        return pl.BlockSpec((8, C), lambda i: (jnp.maximum(i * (T // 8) - 1, 0), cb0 + j))

    def nxt(col):
        return pl.BlockSpec((8, C), lambda i: (jnp.minimum((i + 1) * (T // 8), L // 8 - 1), col))

    return pl.pallas_call(
        body, name=name, grid=(nt,),
        in_specs=[pl.BlockSpec((T, C), lambda i: (i, 3)), nxt(3), cur(0), nxt(cb0), cur(1), cur(2), prev(1), prev(2),
                  pl.BlockSpec((K, C), lambda i: (0, 0))],
        out_specs=(_row_spec(T, 3 * C), pl.BlockSpec((K, C), lambda i: (0, 0))),
        out_shape=(jax.ShapeDtypeStruct((L, 3 * C), BF16), jax.ShapeDtypeStruct((K, C), F32)),
        scratch_shapes=[pltpu.VMEM((T + 8, C), F32), pltpu.VMEM((T + 8, C), F32)],
        compiler_params=_cparams(("arbitrary",), VMEM_BIG))(dycat, dycat, proj, proj, proj, proj, proj, proj, w)


def _silu(x):
    return x * jax.nn.sigmoid(x)


def _softplus(x):
    return jnp.maximum(x, 0.0) + jnp.log1p(jnp.exp(-jnp.abs(x)))


def _lane_pick(arr, idx):
    lane = lax.broadcasted_iota(jnp.int32, (1, 128), 1)
    return jnp.sum(jnp.where(lane == idx, arr, 0.0), axis=1, keepdims=True)


def _tri(n, strict=False):
    r = lax.broadcasted_iota(jnp.int32, (n, n), 0)
    c = lax.broadcasted_iota(jnp.int32, (n, n), 1)
    return (r > c) if strict else (r >= c)


def _bdot(a, b, dn=(((1,), (0,)), ((), ()))):
    return lax.dot_general(a.astype(BF16), b.astype(BF16), dn, preferred_element_type=F32)


NT = (((1,), (1,)), ((), ()))
TN = (((0,), (0,)), ((), ()))


def _split3(x):
    hi = x.astype(BF16)
    r = x - hi.astype(F32)
    mid = r.astype(BF16)
    lo = (r - mid.astype(F32)).astype(BF16)
    return hi, mid, lo


def _mask_dot(pieces, mask, dn, mask_first):
    def one(p):
        ops = (mask, p) if mask_first else (p, mask)
        return lax.dot_general(ops[0], ops[1], dn, preferred_element_type=F32)
    hi, mid, lo = pieces
    return (one(lo) + one(mid)) + one(hi)


def _tri_apply(x, upper):
    n = x.shape[0]
    r = lax.broadcasted_iota(jnp.int32, (n, n), 0)
    c = lax.broadcasted_iota(jnp.int32, (n, n), 1)
    mask = ((r <= c) if upper else (r >= c)).astype(BF16)
    return _mask_dot(_split3(x), mask, (((1,), (0,)), ((), ())), True)


@jax.custom_vjp
def _cumsum_col(cb):
    return _tri_apply(cb, False)


_cumsum_col.defvjp(lambda cb: (_tri_apply(cb, False), None), lambda _, d: (_tri_apply(d, True),))


def _cumsum_row_fwd(cb):
    tri = _tri(cb.shape[0]).astype(BF16)
    return _mask_dot(_split3(cb), tri, (((0,), (1,)), ((), ())), False)


def _cumsum_row_bwd(_, d):
    tri = _tri(d.shape[1]).astype(BF16)
    return (_mask_dot(_split3(d), tri, (((0,), (1,)), ((), ())), True),)


@jax.custom_vjp
def _cumsum_row(cb):
    return _cumsum_row_fwd(cb)


_cumsum_row.defvjp(lambda cb: (_cumsum_row_fwd(cb), None), _cumsum_row_bwd)


def _dot3(a, b):
    ah, bh = a.astype(BF16), b.astype(BF16)
    al, bl = (a - ah.astype(F32)).astype(BF16), (b - bh.astype(F32)).astype(BF16)
    d = lambda p, q: jnp.dot(p, q, preferred_element_type=F32)
    return (d(al, bh) + d(ah, bl)) + d(ah, bh)


def _unit_lower_inv_impl(ms):
    n = ms[0].shape[0]
    eye = (lax.broadcasted_iota(jnp.int32, (n, n), 0) == lax.broadcasted_iota(jnp.int32, (n, n), 1)).astype(F32)
    ps = [-m for m in ms]
    ts = [eye + p for p in ps]
    for _ in range(int(math.log2(n)) - 1):
        ps = [_dot3(p, p) for p in ps]
        ts = [t + _dot3(t, p) for t, p in zip(ts, ps)]
    return tuple(ts)


@jax.custom_vjp
def _unit_lower_inv(ms):
    return _unit_lower_inv_impl(ms)


def _unit_lower_inv_fwd(ms):
    ts = _unit_lower_inv_impl(ms)
    return ts, ts


def _unit_lower_inv_bwd(ts, ds):
    xs = [_bdot(t, d, TN) for t, d in zip(ts, ds)]
    return (tuple(-_bdot(x, t, NT) for x, t in zip(xs, ts)),)


_unit_lower_inv.defvjp(_unit_lower_inv_fwd, _unit_lower_inv_bwd)


GDN_SUB = 4
GDN_H = 4


def _gdn_step_fn(qs, ks, vs, zs, gs, S, alog_row, dtb_row, nw):
    n = GDN_CHUNK
    lanes = [(h, c) for h in range(GDN_H) for c in range(GDN_SUB)]
    z3 = lambda f, a, b, c_: [f(x, y, z) for x, y, z in zip(a, b, c_)]
    q = [_silu(qs[h][c]) for h, c in lanes]
    k = [_silu(ks[h][c]) for h, c in lanes]
    v = [_silu(vs[h][c]) for h, c in lanes]
    q = [t * lax.rsqrt(jnp.sum(t * t, axis=-1, keepdims=True) + EPS) * (128.0 ** -0.5) for t in q]
    k = [t * lax.rsqrt(jnp.sum(t * t, axis=-1, keepdims=True) + EPS) for t in k]
    beta = [jax.nn.sigmoid(_lane_pick(gs[c], LANE_GB + h)) for h, c in lanes]
    rate = [-jnp.exp(_lane_pick(alog_row, LANE_GA + h)) for h in range(GDN_H)]
    bias = [_lane_pick(dtb_row, LANE_GA + h) for h in range(GDN_H)]
    g = [rate[h] * _softplus(_lane_pick(gs[c], LANE_GA + h) + bias[h]) for h, c in lanes]
    cb = [jnp.broadcast_to(t, (n, n)) for t in g]
    gc_col = [_cumsum_col(t) for t in cb]
    gc_row = [_cumsum_row(t) for t in cb]
    tri_incl, tri_strict = _tri(n), _tri(n, strict=True)
    decay = [jnp.exp(jnp.where(tri_incl, a - b, NEG)) for a, b in zip(gc_col, gc_row)]
    gc = [t[:, 0:1] for t in gc_col]
    g_last = [jnp.sum(t, axis=0, keepdims=True) for t in g]
    kb = [a * b for a, b in zip(k, beta)]
    m = z3(lambda a, b, d: jnp.where(tri_strict, _bdot(a, b, NT) * d, 0.0), kb, k, decay)
    t_inv = _unit_lower_inv(tuple(m))
    egc = [jnp.exp(t) for t in gc]
    u = z3(lambda t, a, b: _bdot(t, a * b), t_inv, v, beta)
    w = z3(lambda t, a, e: _bdot(t, a * e), t_inv, kb, egc)
    attn = z3(lambda a, b, d: jnp.where(tri_incl, _bdot(a, b, NT) * d, 0.0), q, k, decay)
    qd = [a * e for a, e in zip(q, egc)]
    kd = z3(lambda a, gl, c_: a * jnp.exp(gl - c_), k, g_last, gc)
    egl = [jnp.exp(t) for t in g_last]
    zg = [_silu(zs[h][c]) for h, c in lanes]
    S = list(S)
    ys = [[None] * GDN_SUB for _ in range(GDN_H)]
    for c in range(GDN_SUB):
        ii = [h * GDN_SUB + c for h in range(GDN_H)]
        v_new = [u[i] - _bdot(w[i], S[h]) for h, i in enumerate(ii)]
        o = [_bdot(qd[i], S[h]) + _bdot(attn[i], v_new[h]) for h, i in enumerate(ii)]
        S = [S[h] * egl[i] + _bdot(kd[i], v_new[h], TN) for h, i in enumerate(ii)]
        for h, i in enumerate(ii):
            ys[h][c] = _rms(o[h], nw) * zg[i]
    return tuple(tuple(y) for y in ys), tuple(S)


def _gdn_step_args(qkv_ref, z_ref, g_ref):
    n = GDN_CHUNK
    rows = [slice(n * c, n * c + n) for c in range(GDN_SUB)]
    col = lambda ref, o: tuple(tuple(ref[r, o + 128 * h:o + 128 * h + 128] for r in rows) for h in range(GDN_H))
    return rows, (col(qkv_ref, 0), col(qkv_ref, 512), col(qkv_ref, 1024), col(z_ref, 0), tuple(g_ref[r, :] for r in rows))


def _carry(cargo, refs, step, nsteps):
    if cargo is None:
        return lambda: None
    start, mid, end = cargo.phases(*refs)
    pl.when(step == 0)(start)
    pl.when(step == (3 * nsteps) // 4)(mid)
    return lambda: pl.when(step == nsteps - 1)(end)


def _cargo_io(cargo):
    if cargo is None:
        return [], [], [], []
    hbm = pl.BlockSpec(memory_space=pl.ANY)
    return list(cargo.arrays), [hbm] * cargo.n_in, list(cargo.out_shapes), [hbm] * cargo.n_out


def _gdn_fwd(qkvc, proj, alog_row, dtb_row, nw, name, cargo=None):
    L = qkvc.shape[0]
    n = GDN_CHUNK * GDN_SUB
    Nc = L // n
    c_arr, c_ispec, c_oshape, c_ospec = _cargo_io(cargo)
    ni, no = len(c_arr), len(c_oshape)

    def body(*refs):
        qkv_ref, z_ref, g_ref, al_ref, db_ref, nw_ref = refs[:6]
        y_ref, sin_ref = refs[6 + ni:8 + ni]
        S = refs[8 + ni + no]
        step = pl.program_id(0)
        drain = _carry(cargo, (refs[6:6 + ni], refs[8 + ni:8 + ni + no], refs[9 + ni + no:]), step, Nc)

        @pl.when(step == 0)
        def _():
            S[...] = jnp.zeros_like(S)

        rows, args = _gdn_step_args(qkv_ref, z_ref, g_ref)
        states = tuple(S[h] for h in range(GDN_H))
        ys, sn = _gdn_step_fn(*args, states, al_ref[...], db_ref[...], nw_ref[...])
        for h in range(GDN_H):
            sin_ref[0, h] = states[h]
            for c in range(GDN_SUB):
                y_ref[rows[c], 128 * h:128 * h + 128] = ys[h][c].astype(BF16)
            S[h] = sn[h]
        drain()

    return pl.pallas_call(
        body, name=name, grid=(Nc,),
        in_specs=[pl.BlockSpec((n, 1536), lambda i: (i, 0)), pl.BlockSpec((n, 512), lambda i: (i, P_GZ // 512)),
                  pl.BlockSpec((n, 128), lambda i: (i, P_GATE // 128)), _vec_spec(128), _vec_spec(128), _vec_spec(128)]
        + c_ispec,
        out_specs=tuple([pl.BlockSpec((n, 512), lambda i: (i, 0)), pl.BlockSpec((1, 4, 128, 128), lambda i: (i, 0, 0, 0))]
                        + c_ospec),
        out_shape=tuple([jax.ShapeDtypeStruct((L, 512), BF16), jax.ShapeDtypeStruct((Nc, 4, 128, 128), F32)] + c_oshape),
        scratch_shapes=[pltpu.VMEM((4, 128, 128), F32)] + (cargo.scratch if cargo else []),
        compiler_params=_cparams(("arbitrary",), VMEM_BIG))(qkvc, proj, proj, alog_row, dtb_row, nw, *c_arr)


def _gdn_bwd(qkvc, proj, alog_row, dtb_row, nw, s_in, dycat, name, cargo=None):
    L = qkvc.shape[0]
    n = GDN_CHUNK * GDN_SUB
    Nc = L // n
    c_arr, c_ispec, c_oshape, c_ospec = _cargo_io(cargo)
    ni, no = len(c_arr), len(c_oshape)

    def body(*refs):
        qkv_ref, z_ref, g_ref, al_ref, db_ref, nw_ref, sin_ref, dy_ref = refs[:8]
        dqkv_ref, dz_ref, dg_ref, dal_ref, ddb_ref, dnw_ref = refs[8 + ni:14 + ni]
        dS = refs[14 + ni + no]
        step = pl.program_id(0)
        drain = _carry(cargo, (refs[8:8 + ni], refs[14 + ni:14 + ni + no], refs[15 + ni + no:]), step, Nc)

        @pl.when(step == 0)
        def _():
            dS[...] = jnp.zeros_like(dS)
            dal_ref[...] = jnp.zeros_like(dal_ref)
            ddb_ref[...] = jnp.zeros_like(ddb_ref)
            dnw_ref[...] = jnp.zeros_like(dnw_ref)

        rows, args = _gdn_step_args(qkv_ref, z_ref, g_ref)
        s_in = tuple(sin_ref[0, h] for h in range(GDN_H))
        _, vjp = jax.vjp(_gdn_step_fn, *args, s_in, al_ref[...], db_ref[...], nw_ref[...])
        dys = tuple(tuple(dy_ref[r, 128 * h:128 * h + 128] for r in rows) for h in range(GDN_H))
        dq, dk, dv, dz, dgt, ds, dal, ddb, dnw = vjp((dys, tuple(dS[h] for h in range(GDN_H))))
        for h in range(GDN_H):
            for c in range(GDN_SUB):
                dqkv_ref[rows[c], 128 * h:128 * h + 128] = dq[h][c]
                dqkv_ref[rows[c], 512 + 128 * h:640 + 128 * h] = dk[h][c]
                dqkv_ref[rows[c], 1024 + 128 * h:1152 + 128 * h] = dv[h][c]
                dz_ref[rows[c], 128 * h:128 * h + 128] = dz[h][c].astype(BF16)
            dS[h] = ds[h]
        for c in range(GDN_SUB):
            dg_ref[rows[c], :] = dgt[c]
        dal_ref[...] += dal
        ddb_ref[...] += ddb
        dnw_ref[...] += dnw
        drain()

    rev = lambda i: Nc - 1 - i
    vec = jax.ShapeDtypeStruct((1, 128), F32)
    return pl.pallas_call(
        body, name=name, grid=(Nc,),
        in_specs=[pl.BlockSpec((n, 1536), lambda i: (rev(i), 0)), pl.BlockSpec((n, 512), lambda i: (rev(i), P_GZ // 512)),
                  pl.BlockSpec((n, 128), lambda i: (rev(i), P_GATE // 128)), _vec_spec(128), _vec_spec(128), _vec_spec(128),
                  pl.BlockSpec((1, 4, 128, 128), lambda i: (rev(i), 0, 0, 0)), pl.BlockSpec((n, 512), lambda i: (rev(i), 0))]
        + c_ispec,
        out_specs=tuple([pl.BlockSpec((n, 1536), lambda i: (rev(i), 0)), pl.BlockSpec((n, 512), lambda i: (rev(i), 0)),
                         pl.BlockSpec((n, 128), lambda i: (rev(i), 0)), _vec_spec(128), _vec_spec(128), _vec_spec(128)]
                        + c_ospec),
        out_shape=tuple([jax.ShapeDtypeStruct((L, 1536), F32), jax.ShapeDtypeStruct((L, 512), BF16),
                         jax.ShapeDtypeStruct((L, 128), F32), vec, vec, vec] + c_oshape),
        scratch_shapes=[pltpu.VMEM((4, 128, 128), F32)] + (cargo.scratch if cargo else []),
        compiler_params=_cparams(("arbitrary",), VMEM_BIG))(qkvc, proj, proj, alog_row, dtb_row, nw, s_in, dycat, *c_arr)


def _ssd_chunk_fn(xs, bs, cs, zs, gates, st, alog_row, dtb_row, d_row, nws):
    n = SSM_CHUNK
    tri = _tri(n)
    lane = lax.broadcasted_iota(jnp.int32, (1, 128), 1)
    lo = lane < 64
    xs = [_silu(t) for t in xs]
    bs = [_silu(t) for t in bs]
    cs = [_silu(t) for t in cs]
    cb = [_bdot(cs[g], bs[g], NT) for g in range(2)]
    H = range(8)
    P = range(4)
    dt = [_softplus(_lane_pick(gates, LANE_SDT + h) + _lane_pick(dtb_row, LANE_SDT + h)) for h in H]
    da = [dt[h] * (-jnp.exp(_lane_pick(alog_row, LANE_SDT + h))) for h in H]
    dab = [jnp.broadcast_to(t, (n, n)) for t in da]
    cs_col = [_cumsum_col(t) for t in dab]
    cs_row = [_cumsum_row(t) for t in dab]
    lmat = [jnp.exp(jnp.where(tri, a - b, NEG)) for a, b in zip(cs_col, cs_row)]
    xdt_h = [jnp.where(lo if h % 2 == 0 else jnp.logical_not(lo), xs[h // 2] * dt[h], 0.0) for h in H]
    yd = [_bdot(cb[h // 4] * lmat[h], xdt_h[h]) for h in H]
    tot = [jnp.sum(t, axis=0, keepdims=True) for t in da]
    dsk = [_lane_pick(d_row, LANE_SDT + h) for h in H]
    cs_lane = [jnp.where(lo, cs_col[2 * p], cs_col[2 * p + 1]) for p in P]
    dt_lane = [jnp.where(lo, dt[2 * p], dt[2 * p + 1]) for p in P]
    tot_lane = [jnp.where(lo, tot[2 * p], tot[2 * p + 1]) for p in P]
    dsk_lane = [jnp.where(lo, dsk[2 * p], dsk[2 * p + 1]) for p in P]
    xdt = [xs[p] * dt_lane[p] for p in P]
    upd = [_bdot(bs[p // 2], xdt[p] * jnp.exp(tot_lane[p] - cs_lane[p]), TN) for p in P]
    st_new = [st[p] * jnp.exp(tot_lane[p]) + upd[p] for p in P]
    yoff = [_bdot(cs[p // 2], st[p]) * jnp.exp(cs_lane[p]) for p in P]
    zg = [_silu(zs[p]) for p in P]
    ys = [(yd[2 * p] + yd[2 * p + 1] + yoff[p] + xs[p] * dsk_lane[p]) * zg[p] for p in P]
    out = []
    for g in range(2):
        ms = (jnp.sum(ys[2 * g] ** 2, axis=-1, keepdims=True) + jnp.sum(ys[2 * g + 1] ** 2, axis=-1, keepdims=True)) / 256.0
        rstd = lax.rsqrt(ms + EPS)
        out += [ys[2 * g] * rstd * nws[2 * g], ys[2 * g + 1] * rstd * nws[2 * g + 1]]
    return tuple(out), tuple(st_new)


def _ssd_args(xbc_ref, z_ref, nw_ref):
    xs = [xbc_ref[:, 128 * p:128 * p + 128] for p in range(4)]
    bs = [xbc_ref[:, 512 + 128 * g:640 + 128 * g] for g in range(2)]
    cs = [xbc_ref[:, 768 + 128 * g:896 + 128 * g] for g in range(2)]
    zs = [z_ref[:, 128 * p:128 * p + 128] for p in range(4)]
    nws = [nw_ref[:, 128 * p:128 * p + 128] for p in range(4)]
    return xs, bs, cs, zs, nws


def _ssd_fwd(xbcc, proj, alog_row, dtb_row, d_row, nw, name):
    L = xbcc.shape[0]
    n = SSM_CHUNK
    Nc = L // n

    def body(xbc_ref, z_ref, g_ref, al_ref, db_ref, d_ref, nw_ref, y_ref, sin_ref, S):
        @pl.when(pl.program_id(0) == 0)
        def _():
            S[...] = jnp.zeros_like(S)

        xs, bs, cs, zs, nws = _ssd_args(xbc_ref, z_ref, nw_ref)
        st = [S[p] for p in range(4)]
        sin_ref[0] = S[...]
        ys, sn = _ssd_chunk_fn(xs, bs, cs, zs, g_ref[...], st, al_ref[...], db_ref[...], d_ref[...], nws)
        for p in range(4):
            y_ref[:, 128 * p:128 * p + 128] = ys[p].astype(BF16)
            S[p] = sn[p]

    return pl.pallas_call(
        body, name=name, grid=(Nc,),
        in_specs=[pl.BlockSpec((n, 1024), lambda i: (i, 0)), pl.BlockSpec((n, 512), lambda i: (i, P_SZ // 512)),
                  pl.BlockSpec((n, 128), lambda i: (i, P_GATE // 128)), _vec_spec(128), _vec_spec(128), _vec_spec(128),
                  _vec_spec(512)],
        out_specs=(pl.BlockSpec((n, 512), lambda i: (i, 0)), pl.BlockSpec((1, 4, 128, 128), lambda i: (i, 0, 0, 0))),
        out_shape=(jax.ShapeDtypeStruct((L, 512), BF16), jax.ShapeDtypeStruct((Nc, 4, 128, 128), F32)),
        scratch_shapes=[pltpu.VMEM((4, 128, 128), F32)],
        compiler_params=_cparams(("arbitrary",), VMEM_BIG))(xbcc, proj, proj, alog_row, dtb_row, d_row, nw)


def _ssd_bwd(xbcc, proj, alog_row, dtb_row, d_row, nw, s_in, dycat, name):
    L = xbcc.shape[0]
    n = SSM_CHUNK
    Nc = L // n

    def body(xbc_ref, z_ref, g_ref, al_ref, db_ref, d_ref, nw_ref, sin_ref, dy_ref,
             dxbc_ref, dz_ref, dg_ref, dal_ref, ddb_ref, dd_ref, dnw_ref, dS):
        @pl.when(pl.program_id(0) == 0)
        def _():
            dS[...] = jnp.zeros_like(dS)
            dal_ref[...] = jnp.zeros_like(dal_ref)
            ddb_ref[...] = jnp.zeros_like(ddb_ref)
            dd_ref[...] = jnp.zeros_like(dd_ref)
            dnw_ref[...] = jnp.zeros_like(dnw_ref)

        xs, bs, cs, zs, nws = _ssd_args(xbc_ref, z_ref, nw_ref)
        st = [sin_ref[0, p] for p in range(4)]
        _, vjp = jax.vjp(_ssd_chunk_fn, xs, bs, cs, zs, g_ref[...], st, al_ref[...], db_ref[...], d_ref[...], nws)
        dys = tuple(dy_ref[:, 128 * p:128 * p + 128] for p in range(4))
        dxs, dbs, dcs, dzs, dgt, dst, dal, ddb, dd, dnws = vjp((dys, tuple(dS[p] for p in range(4))))
        for p in range(4):
            dxbc_ref[:, 128 * p:128 * p + 128] = dxs[p]
            dz_ref[:, 128 * p:128 * p + 128] = dzs[p].astype(BF16)
            dS[p] = dst[p]
            dnw_ref[:, 128 * p:128 * p + 128] += dnws[p]
        for g in range(2):
            dxbc_ref[:, 512 + 128 * g:640 + 128 * g] = dbs[g]
            dxbc_ref[:, 768 + 128 * g:896 + 128 * g] = dcs[g]
        dg_ref[...] = dgt
        dal_ref[...] += dal
        ddb_ref[...] += ddb
        dd_ref[...] += dd

    rev = lambda i: Nc - 1 - i
    vec = jax.ShapeDtypeStruct((1, 128), F32)
    return pl.pallas_call(
        body, name=name, grid=(Nc,),
        in_specs=[pl.BlockSpec((n, 1024), lambda i: (rev(i), 0)), pl.BlockSpec((n, 512), lambda i: (rev(i), P_SZ // 512)),
                  pl.BlockSpec((n, 128), lambda i: (rev(i), P_GATE // 128)), _vec_spec(128), _vec_spec(128), _vec_spec(128),
                  _vec_spec(512), pl.BlockSpec((1, 4, 128, 128), lambda i: (rev(i), 0, 0, 0)),
                  pl.BlockSpec((n, 512), lambda i: (rev(i), 1))],
        out_specs=(pl.BlockSpec((n, 1024), lambda i: (rev(i), 0)), pl.BlockSpec((n, 512), lambda i: (rev(i), 0)),
                   pl.BlockSpec((n, 128), lambda i: (rev(i), 0)), _vec_spec(128), _vec_spec(128), _vec_spec(128),
                   _vec_spec(512)),
        out_shape=(jax.ShapeDtypeStruct((L, 1024), F32), jax.ShapeDtypeStruct((L, 512), BF16),
                   jax.ShapeDtypeStruct((L, 128), F32), vec, vec, vec, jax.ShapeDtypeStruct((1, 512), F32)),
        scratch_shapes=[pltpu.VMEM((4, 128, 128), F32)],
        compiler_params=_cparams(("arbitrary",), VMEM_BIG))(xbcc, proj, proj, alog_row, dtb_row, d_row, nw, s_in, dycat)


def _rot(x):
    W = x.shape[1]
    lane = lax.broadcasted_iota(jnp.int32, (1, W), 1)
    first = (lane % 64) < 32
    return jnp.where(first, -pltpu.roll(x, W - 32, 1), pltpu.roll(x, 32, 1))


def _rope(x, cos, sin):
    return x * cos + _rot(x) * sin


def _rope_t(d, cos, sin):
    return d * cos - _rot(d * sin)


def _swa_core_fn(qs, ks, vs, sink_row, mask):
    lane = lax.broadcasted_iota(jnp.int32, (1, 128), 1)
    lo = lane < 64
    H = range(8)
    qm = [jnp.where(lo if h % 2 == 0 else jnp.logical_not(lo), qs[h // 2], 0.0) for h in H]
    s = [jnp.where(mask, _bdot(qm[h], ks[h // 4], NT) * 0.125, NEG) for h in H]
    sk = [_lane_pick(sink_row, h) for h in H]
    mx = [lax.stop_gradient(jnp.maximum(jnp.max(s[h], axis=-1, keepdims=True), sk[h])) for h in H]
    pr = [jnp.exp(s[h] - mx[h]) for h in H]
    den = [jnp.sum(pr[h], axis=-1, keepdims=True) + jnp.exp(sk[h] - mx[h]) for h in H]
    pr = [pr[h] / den[h] for h in H]
    ov = [_bdot(pr[h], vs[h // 4]) for h in H]
    return tuple(jnp.where(lo, ov[2 * p], ov[2 * p + 1]) for p in range(4))


def _swa_prepare(blk, q_ref, kc_ref, kp_ref, vc_ref, vp_ref, cc_ref, sc_ref, cp_ref, sp_ref):
    W = WINDOW
    cos_c, sin_c = cc_ref[...], sc_ref[...]
    cos_b = jnp.concatenate([cp_ref[...], cos_c], axis=0)
    sin_b = jnp.concatenate([sp_ref[...], sin_c], axis=0)
    cos_q, sin_q = jnp.concatenate([cos_c] * 4, axis=1), jnp.concatenate([sin_c] * 4, axis=1)
    cos_k, sin_k = jnp.concatenate([cos_b] * 2, axis=1), jnp.concatenate([sin_b] * 2, axis=1)
    qr = _rope(q_ref[...], cos_q, sin_q)
    kr = _rope(jnp.concatenate([kp_ref[...], kc_ref[...]], axis=0), cos_k, sin_k)
    vb = jnp.concatenate([vp_ref[...], vc_ref[...]], axis=0)
    qi = lax.broadcasted_iota(jnp.int32, (W, 2 * W), 0)
    kj = lax.broadcasted_iota(jnp.int32, (W, 2 * W), 1)
    rel = qi + W - kj
    mask = (rel >= 0) & (rel < W) & ((blk > 0) | (kj >= W))
    qs = [qr[:, 128 * p:128 * p + 128] for p in range(4)]
    ks = [kr[:, 128 * g:128 * g + 128] for g in range(2)]
    vs = [vb[:, 128 * g:128 * g + 128] for g in range(2)]
    return qs, ks, vs, mask, (cos_q, sin_q, cos_k, sin_k)


def _swa_specs(nb, order):
    W = WINDOW
    cur = lambda i: order(i)
    prv = lambda i: jnp.maximum(order(i) - 1, 0)
    return [pl.BlockSpec((W, 512), lambda i: (cur(i), P_AQ // 512)),
            pl.BlockSpec((W, 256), lambda i: (cur(i), P_AK // 256)), pl.BlockSpec((W, 256), lambda i: (prv(i), P_AK // 256)),
            pl.BlockSpec((W, 256), lambda i: (cur(i), P_AV // 256)), pl.BlockSpec((W, 256), lambda i: (prv(i), P_AV // 256)),
            pl.BlockSpec((W, 128), lambda i: (cur(i), 0)), pl.BlockSpec((W, 128), lambda i: (cur(i), 0)),
            pl.BlockSpec((W, 128), lambda i: (prv(i), 0)), pl.BlockSpec((W, 128), lambda i: (prv(i), 0)),
            _vec_spec(128)]


def _swa_fwd(proj, cos, sin, sink_row, name):
    L = proj.shape[0]
    W = WINDOW
    nb = L // W

    def body(q_ref, kc_ref, kp_ref, vc_ref, vp_ref, cc_ref, sc_ref, cp_ref, sp_ref, sk_ref, y_ref):
        blk = pl.program_id(0)
        qs, ks, vs, mask, _ = _swa_prepare(blk, q_ref, kc_ref, kp_ref, vc_ref, vp_ref, cc_ref, sc_ref, cp_ref, sp_ref)
        outs = _swa_core_fn(qs, ks, vs, sk_ref[...], mask)
        for p in range(4):
            y_ref[:, 128 * p:128 * p + 128] = outs[p].astype(BF16)

    return pl.pallas_call(
        body, name=name, grid=(nb,), in_specs=_swa_specs(nb, lambda i: i),
        out_specs=pl.BlockSpec((W, 512), lambda i: (i, 0)), out_shape=jax.ShapeDtypeStruct((L, 512), BF16),
        compiler_params=_cparams(("parallel",), VMEM_BIG))(proj, proj, proj, proj, proj, cos, sin, cos, sin, sink_row)


def _swa_bwd(proj, cos, sin, sink_row, dycat, name):
    L = proj.shape[0]
    W = WINDOW
    nb = L // W
    rev = lambda i: nb - 1 - i

    def body(q_ref, kc_ref, kp_ref, vc_ref, vp_ref, cc_ref, sc_ref, cp_ref, sp_ref, sk_ref, dy_ref,
             dq_ref, dk_ref, dv_ref, dsk_ref, ck, cv):
        i = pl.program_id(0)
        blk = nb - 1 - i

        @pl.when(i == 0)
        def _():
            ck[...] = jnp.zeros_like(ck)
            cv[...] = jnp.zeros_like(cv)
            dsk_ref[...] = jnp.zeros_like(dsk_ref)

        qs, ks, vs, mask, (cos_q, sin_q, cos_k, sin_k) = _swa_prepare(
            blk, q_ref, kc_ref, kp_ref, vc_ref, vp_ref, cc_ref, sc_ref, cp_ref, sp_ref)
        _, vjp = jax.vjp(functools.partial(_swa_core_fn, mask=mask), qs, ks, vs, sk_ref[...])
        dqs, dks, dvs, dsk = vjp(tuple(dy_ref[:, 128 * p:128 * p + 128] for p in range(4)))
        dq_ref[...] = _rope_t(jnp.concatenate(dqs, axis=1), cos_q, sin_q).astype(BF16)
        dkb = _rope_t(jnp.concatenate(dks, axis=1), cos_k, sin_k)
        dvb = jnp.concatenate(dvs, axis=1)
        dk_ref[...] = (dkb[W:2 * W] + ck[...]).astype(BF16)
        dv_ref[...] = (dvb[W:2 * W] + cv[...]).astype(BF16)
        ck[...] = dkb[0:W]
        cv[...] = dvb[0:W]
        dsk_ref[...] += dsk

    return pl.pallas_call(
        body, name=name, grid=(nb,),
        in_specs=_swa_specs(nb, rev) + [pl.BlockSpec((W, 512), lambda i: (rev(i), 2))],
        out_specs=(pl.BlockSpec((W, 512), lambda i: (rev(i), 0)), pl.BlockSpec((W, 256), lambda i: (rev(i), 0)),
                   pl.BlockSpec((W, 256), lambda i: (rev(i), 0)), _vec_spec(128)),
        out_shape=(jax.ShapeDtypeStruct((L, 512), BF16), jax.ShapeDtypeStruct((L, 256), BF16),
                   jax.ShapeDtypeStruct((L, 256), BF16), jax.ShapeDtypeStruct((1, 128), F32)),
        scratch_shapes=[pltpu.VMEM((W, 256), F32), pltpu.VMEM((W, 256), F32)],
        compiler_params=_cparams(("arbitrary",), VMEM_BIG))(
            proj, proj, proj, proj, proj, cos, sin, cos, sin, sink_row, dycat)


def _adamw(w, g, m, v):
    m = ADAM_B1 * m + (1.0 - ADAM_B1) * g
    v = ADAM_B2 * v + (1.0 - ADAM_B2) * (g * g)
    m_hat = m / (1.0 - ADAM_B1 ** ADAM_STEP)
    v_hat = v / (1.0 - ADAM_B2 ** ADAM_STEP)
    delta = -ADAM_LR * (m_hat / (jnp.sqrt(v_hat) + ADAM_EPS) + ADAM_WD * w)
    return delta, m, v


def _ada_fwd(c_all, ada_w, ada_b_sh, name):
    S = ada_w.shape[2]

    def body(c_ref, w_ref, b_ref, o_ref):
        o_ref[0] = _bdot(_silu(c_ref[...]), w_ref[0]) + b_ref[0]

    return pl.pallas_call(
        body, name=name, grid=(2,),
        in_specs=[pl.BlockSpec((N_DEV, D), lambda i: (0, 0)), pl.BlockSpec((1, D, S), lambda i: (i, 0, 0)),
                  pl.BlockSpec((1, 1, S), lambda i: (i, 0, 0))],
        out_specs=pl.BlockSpec((1, N_DEV, S), lambda i: (i, 0, 0)), out_shape=jax.ShapeDtypeStruct((2, N_DEV, S), F32),
        compiler_params=_cparams(("parallel",), VMEM_BIG))(c_all, ada_w, ada_b_sh.reshape(2, 1, S))


def _ada_bwd_adamw(c_all, dmod_sh, w, m, v, name):
    S = w.shape[2]
    T = 256

    def body(c_ref, d_ref, w_ref, m_ref, v_ref, g_ref, dl_ref, nm_ref, nv_ref):
        g = _bdot(_silu(c_ref[...]), d_ref[0], TN)
        dl, nm, nv = _adamw(w_ref[0], g, m_ref[0], v_ref[0])
        g_ref[0], dl_ref[0], nm_ref[0], nv_ref[0] = g, dl, nm, nv

    blk = pl.BlockSpec((1, T, S), lambda i, j: (i, j, 0))
    sds = jax.ShapeDtypeStruct(w.shape, F32)
    return pl.pallas_call(
        body, name=name, grid=(2, D // T),
        in_specs=[pl.BlockSpec((N_DEV, T), lambda i, j: (0, j)), pl.BlockSpec((1, N_DEV, S), lambda i, j: (i, 0, 0)),
                  blk, blk, blk],
        out_specs=(blk, blk, blk, blk), out_shape=(sds, sds, sds, sds),
        compiler_params=_cparams(("parallel", "parallel"), VMEM_BIG))(c_all, dmod_sh, w, m, v)


def _sum8(r, name):
    _, R, W = r.shape
    T = R
    for cand in (2496, 2048, 1024, 512, 256, 128, 64, 32, 16, 8):
        if R % cand == 0:
            T = cand
            break

    def body(r_ref, o_ref):
        acc = r_ref[0].astype(F32)
        for d in range(1, N_DEV):
            acc = acc + r_ref[d].astype(F32)
        o_ref[...] = acc

    return pl.pallas_call(
        body, name=name, grid=(R // T,), in_specs=[pl.BlockSpec((N_DEV, T, W), lambda i: (0, i, 0))],
        out_specs=pl.BlockSpec((T, W), lambda i: (i, 0)), out_shape=jax.ShapeDtypeStruct((R, W), F32),
        compiler_params=_cparams(("parallel",), VMEM_BIG))(r)


def _adamw_call(w, g, m, v, name):
    R, W = w.shape
    T = R
    for cand in (1024, 512, 256, 128, 64, 32, 16, 8):
        if R % cand == 0 and R > cand and cand * W * 4 <= 2 * 1024 * 1024:
            T = cand
            break

    def body(w_ref, g_ref, m_ref, v_ref, dl_ref, nm_ref, nv_ref):
        dl_ref[...], nm_ref[...], nv_ref[...] = _adamw(w_ref[...], g_ref[...], m_ref[...], v_ref[...])

    blk = pl.BlockSpec((T, W), lambda i: (i, 0))
    sds = jax.ShapeDtypeStruct((R, W), F32)
    return pl.pallas_call(
        body, name=name, grid=(R // T,), in_specs=[blk, blk, blk, blk], out_specs=(blk, blk, blk),
        out_shape=(sds, sds, sds), compiler_params=_cparams(("parallel",), VMEM_BIG))(w, g, m, v)


def _sum8_adamw(r0, r1, w, m, v, name):
    _, _, R, W = r0.shape
    T = (256 * 1024) // W
    assert R % T == 0, (R, W, T)
    nj = R // T

    def body(r0_ref, r1_ref, w_ref, m_ref, v_ref, g_ref, dl_ref, nm_ref, nv_ref):
        def run(r_ref):
            g = r_ref[0].astype(F32)
            for d in range(1, N_DEV):
                g = g + r_ref[d].astype(F32)
            g_ref[...] = g
            dl_ref[...], nm_ref[...], nv_ref[...] = _adamw(w_ref[...], g, m_ref[...], v_ref[...])

        pl.when(pl.program_id(0) == 0)(lambda: run(r0_ref))
        pl.when(pl.program_id(0) == 1)(lambda: run(r1_ref))

    r0_spec = pl.BlockSpec((N_DEV, None, T, W), lambda i, j: (0, 0, jnp.where(i == 0, j, nj - 1), 0))
    r1_spec = pl.BlockSpec((N_DEV, None, T, W), lambda i, j: (0, 0, jnp.where(i == 1, j, 0), 0))
    blk = pl.BlockSpec((None, T, W), lambda i, j: (i, j, 0))
    sds = jax.ShapeDtypeStruct((2, R, W), F32)
    return pl.pallas_call(
        body, name=name, grid=(2, nj), in_specs=[r0_spec, r1_spec, blk, blk, blk],
        out_specs=(blk, blk, blk, blk), out_shape=(sds, sds, sds, sds),
        compiler_params=_cparams(("arbitrary", "arbitrary"), VMEM_BIG))(r0, r1, w, m, v)


def _permute_rows(wt):
    z = lambda n: jnp.zeros((n, wt.shape[1]), wt.dtype)
    s = lambda o, n: wt[o:o + n]
    kd = [s(O_AK, 64), s(O_AK, 64), s(O_AK + 64, 64), s(O_AK + 64, 64)]
    vd = [s(O_AV, 64), s(O_AV, 64), s(O_AV + 64, 64), s(O_AV + 64, 64)]
    return jnp.concatenate(
        [s(O_GQ, 1536), s(O_GZ, 512), s(O_SXBC, 1024), s(O_SZ, 512), s(O_AQ, 512), s(O_CB, 1536)] + kd + vd
        + [s(O_GB, 4), s(O_GA, 4), s(O_SDT, 8), z(112)], axis=0)


def _unpermute_rows(g):
    s = lambda o, n: g[o:o + n]
    add = lambda o: (s(o, 64).astype(F32) + s(o + 64, 64).astype(F32)).astype(g.dtype)
    dk = [add(P_AK), add(P_AK + 128)]
    dv = [add(P_AV), add(P_AV + 128)]
    return jnp.concatenate(
        [s(P_QKV, 1536), s(P_GZ, 512), s(P_GATE, 8), s(P_SZ, 512), s(P_SXBC, 1024), s(P_GATE + 8, 8), s(P_AQ, 512)]
        + dk + dv + [s(P_CB, 1536)], axis=0)


def _row128(vals, lane0):
    n = vals.shape[0]
    return jnp.concatenate([jnp.zeros((lane0,), F32), vals, jnp.zeros((128 - lane0 - n,), F32)]).reshape(1, 128)


def _pad_rows128(flat):
    n = flat.shape[0]
    pad = (-n) % 1024
    return jnp.concatenate([flat, jnp.zeros((pad,), flat.dtype)]).reshape(-1, 128)


def kernel(x, c, positions, ada_w, ada_b, norm_pre_mix, norm_post_mix, norm_pre_mlp, norm_post_mlp, w_in, w_out, gdn_conv_w, gdn_a_log, gdn_dt_bias, gdn_norm_w, ssm_conv_w, ssm_conv_b, ssm_a_log, ssm_dt_bias, ssm_d, ssm_norm_w, attn_sinks, sc_conv_w, w_up, w_down, loss_target, m_ada_w, m_ada_b, m_norm_pre_mix, m_norm_post_mix, m_norm_pre_mlp, m_norm_post_mlp, m_w_in, m_w_out, m_gdn_conv_w, m_gdn_a_log, m_gdn_dt_bias, m_gdn_norm_w, m_ssm_conv_w, m_ssm_conv_b, m_ssm_a_log, m_ssm_dt_bias, m_ssm_d, m_ssm_norm_w, m_attn_sinks, m_sc_conv_w, m_w_up, m_w_down, v_ada_w, v_ada_b, v_norm_pre_mix, v_norm_post_mix, v_norm_pre_mlp, v_norm_post_mlp, v_w_in, v_w_out, v_gdn_conv_w, v_gdn_a_log, v_gdn_dt_bias, v_gdn_norm_w, v_ssm_conv_w, v_ssm_conv_b, v_ssm_a_log, v_ssm_dt_bias, v_ssm_d, v_ssm_norm_w, v_attn_sinks, v_sc_conv_w, v_w_up, v_w_down):
    L = x.shape[1]
    me = 4 * lax.axis_index("x") + 2 * lax.axis_index("y") + lax.axis_index("c")
    x0 = x[0]
    target = loss_target[0]
    S_ADA = ada_w.shape[2]

    small = jnp.concatenate([c.reshape(-1), gdn_conv_w.reshape(-1), ssm_conv_w.reshape(-1), sc_conv_w.reshape(-1)])
    n_small = small.shape[0]
    g_small = _ag_small(_pad_rows128(small), "ag_c_conv").reshape(N_DEV, -1)[:, :n_small]
    c_all = g_small[:, :D]
    o = D
    gdn_cw = g_small[:, o:o + 2 * 4 * 192].reshape(N_DEV, 2, 4, 192).transpose(1, 2, 0, 3).reshape(2, 4, 1536)
    o += 2 * 4 * 192
    ssm_cw = g_small[:, o:o + 2 * 4 * 128].reshape(N_DEV, 2, 4, 128).transpose(1, 2, 0, 3).reshape(2, 4, 1024)
    o += 2 * 4 * 128
    sc_cw = g_small[:, o:o + 2 * 3 * 64].reshape(N_DEV, 2, 3, 64).transpose(1, 2, 0, 3).reshape(2, 3, 512)

    ada_b_sh = lax.dynamic_slice(ada_b, (0, me * S_ADA), (2, S_ADA))
    mod_sh = _ada_fwd(c_all, ada_w, ada_b_sh, "ada_fwd")
    mod_all = _ag_small(mod_sh.reshape(-1, 128), "ag_mod").reshape(N_DEV, 2, N_DEV, S_ADA)
    mod_me = lax.dynamic_index_in_dim(mod_all, me, axis=2, keepdims=False)
    mod_me = mod_me.transpose(1, 0, 2).reshape(2, 6 * D)

    wt_sh = [jnp.pad(w_in[i].T.astype(BF16), ((0, SHARD_PAD - SHARD_IN), (0, 0))) for i in range(2)]
    wo_sh, wu_sh, wd_sh = ([w[i].astype(BF16) for i in range(2)] for w in (w_out, w_up, w_down))
    full_w_in = lambda g: _permute_rows(g[:, :SHARD_IN].reshape(IN_W, D))
    win_t = [full_w_in(_ag_multi([wt_sh[0]], "ag_w_in0")[0]), None]
    wout_f, wup_g, wdown_f = [None, None], [None, None], [None, None]

    inv_freq = ROPE_THETA ** (-jnp.arange(0, 64, 2, dtype=F32) / 64)
    ang = positions[0].astype(F32)[:, None] * inv_freq
    cos = jnp.tile(jnp.cos(ang), (1, 4))
    sin = jnp.tile(jnp.sin(ang), (1, 4))

    vec = lambda a: a.reshape(1, -1)

    mods = [[vec(t) for t in jnp.split(mod_me[i], 6)] for i in range(2)]
    saved = []
    xc = x0
    h = _pre_fwd(x0, vec(norm_pre_mix[0]), mods[0][1], mods[0][0], "pre_mix_fwd0")
    for i in range(2):
        sh_a, sc_a, gt_a, sh_m, sc_m, gt_m = mods[i]
        gal, gdb = _row128(gdn_a_log[i], LANE_GA), _row128(gdn_dt_bias[i], LANE_GA)
        sal, sdb, sdd = (_row128(ssm_a_log[i], LANE_SDT), _row128(ssm_dt_bias[i], LANE_SDT), _row128(ssm_d[i], LANE_SDT))
        gnw, snw, skr = vec(gdn_norm_w[i]), vec(ssm_norm_w[i]), _row128(attn_sinks[i], 0)
        zero_b = jnp.zeros((1, 1536), F32)

        riders = [wo_sh[0], wu_sh[0]] if i == 0 else [wd_sh[1]]
        proj, *got = _mm(h, win_t[i], name=f"in_proj{i}", tb=True, tn=896, tk=2048, cargo=_AgCargo(riders))
        if i == 0:
            wout_f[0], wup_g[0] = got[0].reshape(D, D), got[1]
        else:
            wdown_f[1] = got[0].reshape(MLP, D)
        qkvc = _conv_fwd(proj, 0, 1536, gdn_cw[i], zero_b, f"gdn_conv_fwd{i}")
        if i == 0:
            ya, gdn_s, got_i = _gdn_fwd(qkvc, proj, gal, gdb, gnw, "gdn_fwd0", cargo=_AgCargo([wt_sh[1]]))
            win_t[1] = full_w_in(got_i)
        else:
            ya, gdn_s = _gdn_fwd(qkvc, proj, gal, gdb, gnw, "gdn_fwd1")
        xbcc = _conv_fwd(proj, P_SXBC // 1024, 1024, ssm_cw[i], vec(ssm_conv_b[i]), f"ssm_conv_fwd{i}")
        yb, ssd_s = _ssd_fwd(xbcc, proj, sal, sdb, sdd, snw, f"ssd_fwd{i}")
        yc = _swa_fwd(proj, cos, sin, skr, f"swa_fwd{i}")
        yd = _sc_fwd(proj, sc_cw[i], f"sc_fwd{i}")
        ycat = jnp.concatenate([ya, yb, yc, yd], axis=1)
        out = _mm(ycat, wout_f[i], name=f"out_proj{i}", tk=2048)
        x1, h2 = _post_pre_fwd(xc, out, vec(norm_post_mix[i]), gt_a, vec(norm_pre_mlp[i]), sc_m, sh_m, f"post_mix_pre_mlp_fwd{i}")
        if i == 0:
            act, u, got_d, got_o = _mm(h2, wup_g[0], name="up_proj0", tk=2048, mode="relu2", shards="b",
                                       cargo=_AgCargo([wd_sh[0], wo_sh[1]]))
            wdown_f[0], wout_f[1] = got_d.reshape(MLP, D), got_o.reshape(D, D)
            y2, wup_g[1] = _mm(act, wdown_f[0], name="down_proj0", tk=2048, cargo=_AgCargo([wu_sh[1]]))
        else:
            act, u = _mm(h2, wup_g[1], name="up_proj1", tk=2048, mode="relu2", shards="b")
            y2 = _mm(act, wdown_f[1], name="down_proj1", tk=2048)
        saved.append(dict(x0=xc, h=h, proj=proj, qkvc=qkvc, gdn_s=gdn_s, xbcc=xbcc, ssd_s=ssd_s, ycat=ycat, out=out,
                          x1=x1, h2=h2, act=act, u=u, y2=y2, rows=(gal, gdb, sal, sdb, sdd, gnw, snw, skr)))
        if i == 0:
            xc, h = _post_pre_fwd(x1, y2, vec(norm_post_mlp[0]), gt_m, vec(norm_pre_mix[1]), mods[1][1], mods[1][0],
                                  "post_mlp_pre_mix_fwd")

    sq, dx, *pend = _last_residual_loss(saved[1]["x1"], saved[1]["y2"], vec(norm_post_mlp[1]), mods[1][5], target,
                                        "post_mlp_loss")
    loss = lax.psum(0.5 * sq[0, 0] / float(D), ("x", "y", "c"))

    gw = [dict() for _ in range(2)]
    r_in, r_out, r_up, r_down = [None, None], [None, None], [None, None], [None, None]
    for i in (1, 0):
        sv = saved[i]
        sh_a, sc_a, gt_a, sh_m, sc_m, gt_m = mods[i]
        gal, gdb, sal, sdb, sdd, gnw, snw, skr = sv["rows"]
        proj = sv["proj"]
        dy2, g_npostmlp, d_gt_m = pend
        du = _mm(dy2, wdown_f[i], name=f"down_proj_dx{i}", tb=True, tk=2048, out_dtype=BF16, mode="drelu2", u=sv["u"])
        g_wdown = _mm(sv["act"], dy2, name=f"down_proj_dw{i}", ta=True, tk=2048,
                      out_dtype=BF16).reshape(N_DEV, MLP // N_DEV, D)
        dh2, r_down[i] = _mm(du, wup_g[i], name=f"up_proj_dx{i}", tb=True, tk=1024, out_dtype=BF16, shards="b",
                             cargo=_ExCargo([[g_wdown]]))
        g_wup = _mm(sv["h2"], du, name=f"up_proj_dw{i}", ta=True, tk=2048, out_dtype=BF16, shards="o")
        dx1, dout, g_npremlp, d_sc_m, d_sh_m, g_npostmix, d_gt_a = _pre_post_bwd(
            sv["x1"], vec(norm_pre_mlp[i]), sc_m, sh_m, dh2, dx, sv["out"], vec(norm_post_mix[i]), gt_a,
            f"pre_mlp_post_mix_bwd{i}")
        dycat = _mm(dout, wout_f[i], name=f"out_proj_dx{i}", tb=True, tk=2048)
        g_wout = _mm(sv["ycat"], dout, name=f"out_proj_dw{i}", ta=True, tk=2048,
                     out_dtype=BF16).reshape(N_DEV, D // N_DEV, D)

        dqkvc, dgz, dgate_g, d_gal, d_gdb, d_gnw, r_up[i] = _gdn_bwd(
            sv["qkvc"], proj, gal, gdb, gnw, sv["gdn_s"], dycat, f"gdn_bwd{i}", cargo=_ExCargo([[g_wup]]))
        dqkv, g_gdn_cw, _ = _conv_bwd(dqkvc, proj, 0, 1536, gdn_cw[i], f"gdn_conv_bwd{i}")
        dxbcc, dsz, dgate_s, d_sal, d_sdb, d_sdd, d_snw = _ssd_bwd(
            sv["xbcc"], proj, sal, sdb, sdd, snw, sv["ssd_s"], dycat, f"ssd_bwd{i}")
        dsxbc, g_ssm_cw, g_ssm_cb = _conv_bwd(dxbcc, proj, P_SXBC // 1024, 1024, ssm_cw[i], f"ssm_conv_bwd{i}")
        daq, dak, dav, d_skr = _swa_bwd(proj, cos, sin, skr, dycat, f"swa_bwd{i}")
        dsc, g_sc_cw = _sc_bwd(dycat, proj, sc_cw[i], f"sc_bwd{i}")
        dgate = (dgate_g + dgate_s).astype(BF16)
        dproj = jnp.concatenate([dqkv, dgz, dsxbc, dsz, daq, dsc, dak, dav, dgate], axis=1)
        g_wint, r_out[i] = _mm(dproj, sv["h"], name=f"in_proj_dw{i}", ta=True, tm=896, tk=2048, out_dtype=BF16,
                               cargo=_ExCargo([[g_wout]]))
        g_wint = jnp.pad(_unpermute_rows(g_wint).reshape(N_DEV, SHARD_IN, D), ((0, 0), (0, SHARD_PAD - SHARD_IN), (0, 0)))
        dh, r_in[i] = _mm(dproj, win_t[i], name=f"in_proj_dx{i}", tk=896, out_dtype=BF16, cargo=_ExCargo([[g_wint]]))
        if i == 1:
            dx, dy2_0, g_npremix, d_sc_a, d_sh_a, *post0 = _pre_post_bwd(
                sv["x0"], vec(norm_pre_mix[1]), sc_a, sh_a, dh, dx1, saved[0]["y2"], vec(norm_post_mlp[0]), mods[0][5],
                "pre_mix_post_mlp_bwd")
            pend = (dy2_0, *post0)
        else:
            dx, g_npremix, d_sc_a, d_sh_a = _pre_bwd(sv["x0"], vec(norm_pre_mix[0]), sc_a, sh_a, dh, dx1, "pre_mix_bwd0")

        gw[i] = dict(
            dmod=jnp.concatenate([d_sh_a, d_sc_a, d_gt_a, d_sh_m, d_sc_m, d_gt_m], axis=1).reshape(-1),
            norm_pre_mix=g_npremix.reshape(-1), norm_post_mix=g_npostmix.reshape(-1),
            norm_pre_mlp=g_npremlp.reshape(-1), norm_post_mlp=g_npostmlp.reshape(-1),
            gdn_conv_w=g_gdn_cw.reshape(-1), gdn_a_log=d_gal[0, LANE_GA:LANE_GA + 4], gdn_dt_bias=d_gdb[0, LANE_GA:LANE_GA + 4],
            gdn_norm_w=d_gnw.reshape(-1), ssm_conv_w=g_ssm_cw.reshape(-1), ssm_conv_b=g_ssm_cb.reshape(-1),
            ssm_a_log=d_sal[0, LANE_SDT:LANE_SDT + 8], ssm_dt_bias=d_sdb[0, LANE_SDT:LANE_SDT + 8],
            ssm_d=d_sdd[0, LANE_SDT:LANE_SDT + 8], ssm_norm_w=d_snw.reshape(-1), attn_sinks=d_skr[0, 0:8],
            sc_conv_w=g_sc_cw.reshape(-1))
    grad_x = dx[None]

    grads, deltas, new_m, new_v = {}, {}, {}, {}
    for n_, r, w, m, v in (("w_out", r_out, w_out, m_w_out, v_w_out), ("w_up", r_up, w_up, m_w_up, v_w_up),
                           ("w_down", r_down, w_down, m_w_down, v_w_down)):
        grads[n_], deltas[n_], new_m[n_], new_v[n_] = _sum8_adamw(r[0], r[1], w, m, v, f"sum_adamw_{n_}")
    g_int = jnp.stack([_sum8(r_in[i].reshape(N_DEV, SHARD_PAD, D), f"grad_sum_w_in{i}") for i in range(2)])
    g_in = g_int[:, :SHARD_IN].transpose(0, 2, 1)
    dl, nm, nv = _adamw_call(w_in.reshape(-1, SHARD_IN), g_in.reshape(-1, SHARD_IN), m_w_in.reshape(-1, SHARD_IN),
                             v_w_in.reshape(-1, SHARD_IN), "adamw_w_in")
    grads["w_in"], deltas["w_in"] = g_in, dl.reshape(w_in.shape)
    new_m["w_in"], new_v["w_in"] = nm.reshape(w_in.shape), nv.reshape(w_in.shape)

    small_names = ["dmod", "norm_pre_mix", "norm_post_mix", "norm_pre_mlp", "norm_post_mlp", "gdn_conv_w", "gdn_a_log",
                   "gdn_dt_bias", "gdn_norm_w", "ssm_conv_w", "ssm_conv_b", "ssm_a_log", "ssm_dt_bias", "ssm_d",
                   "ssm_norm_w", "attn_sinks", "sc_conv_w"]
    flat = jnp.concatenate([gw[i][nm_] for nm_ in small_names for i in range(2)])
    n_flat = flat.shape[0]
    g_all = _ag_small(_pad_rows128(flat), "ag_small_grads")
    tot = _sum8(g_all, "small_grad_sum").reshape(-1)[:n_flat]
    dmod_all = g_all.reshape(N_DEV, -1)[:, :2 * 6 * D].reshape(N_DEV, 2, 6 * D)
    sm = {}
    o = 0
    for nm_ in small_names:
        n = gw[0][nm_].shape[0]
        sm[nm_] = jnp.stack([tot[o:o + n], tot[o + n:o + 2 * n]])
        o += 2 * n

    dmod_sh = lax.dynamic_slice(dmod_all.transpose(1, 0, 2), (0, 0, me * S_ADA), (2, N_DEV, S_ADA))
    grads["ada_w"], deltas["ada_w"], new_m["ada_w"], new_v["ada_w"] = _ada_bwd_adamw(
        c_all, dmod_sh, ada_w, m_ada_w, v_ada_w, "ada_bwd_adamw")

    def shard_cols(full, K, C):
        return lax.dynamic_slice(full.reshape(2, K, C), (0, 0, me * (C // N_DEV)), (2, K, C // N_DEV))

    small_g = dict(
        ada_b=sm["dmod"], norm_pre_mix=sm["norm_pre_mix"], norm_post_mix=sm["norm_post_mix"],
        norm_pre_mlp=sm["norm_pre_mlp"], norm_post_mlp=sm["norm_post_mlp"],
        gdn_conv_w=shard_cols(sm["gdn_conv_w"], 4, 1536), gdn_a_log=sm["gdn_a_log"], gdn_dt_bias=sm["gdn_dt_bias"],
        gdn_norm_w=sm["gdn_norm_w"], ssm_conv_w=shard_cols(sm["ssm_conv_w"], 4, 1024), ssm_conv_b=sm["ssm_conv_b"],
        ssm_a_log=sm["ssm_a_log"], ssm_dt_bias=sm["ssm_dt_bias"], ssm_d=sm["ssm_d"], ssm_norm_w=sm["ssm_norm_w"],
        attn_sinks=sm["attn_sinks"], sc_conv_w=shard_cols(sm["sc_conv_w"], 3, 512))
    small_w = dict(
        ada_b=(ada_b, m_ada_b, v_ada_b), norm_pre_mix=(norm_pre_mix, m_norm_pre_mix, v_norm_pre_mix),
        norm_post_mix=(norm_post_mix, m_norm_post_mix, v_norm_post_mix),
        norm_pre_mlp=(norm_pre_mlp, m_norm_pre_mlp, v_norm_pre_mlp),
        norm_post_mlp=(norm_post_mlp, m_norm_post_mlp, v_norm_post_mlp),
        gdn_conv_w=(gdn_conv_w, m_gdn_conv_w, v_gdn_conv_w), gdn_a_log=(gdn_a_log, m_gdn_a_log, v_gdn_a_log),
        gdn_dt_bias=(gdn_dt_bias, m_gdn_dt_bias, v_gdn_dt_bias), gdn_norm_w=(gdn_norm_w, m_gdn_norm_w, v_gdn_norm_w),
        ssm_conv_w=(ssm_conv_w, m_ssm_conv_w, v_ssm_conv_w), ssm_conv_b=(ssm_conv_b, m_ssm_conv_b, v_ssm_conv_b),
        ssm_a_log=(ssm_a_log, m_ssm_a_log, v_ssm_a_log), ssm_dt_bias=(ssm_dt_bias, m_ssm_dt_bias, v_ssm_dt_bias),
        ssm_d=(ssm_d, m_ssm_d, v_ssm_d), ssm_norm_w=(ssm_norm_w, m_ssm_norm_w, v_ssm_norm_w),
        attn_sinks=(attn_sinks, m_attn_sinks, v_attn_sinks), sc_conv_w=(sc_conv_w, m_sc_conv_w, v_sc_conv_w))
    names = list(small_w)
    sizes = [int(np.prod(small_w[n_][0].shape)) for n_ in names]
    pk = lambda j: _pad_rows128(jnp.concatenate([small_w[n_][j].reshape(-1) for n_ in names]))
    gk = _pad_rows128(jnp.concatenate([small_g[n_].reshape(-1) for n_ in names]))
    dl, nm, nv = _adamw_call(pk(0), gk, pk(1), pk(2), "adamw_small")
    dl, nm, nv = dl.reshape(-1), nm.reshape(-1), nv.reshape(-1)
    o = 0
    for n_, sz in zip(names, sizes):
        shp = small_w[n_][0].shape
        grads[n_] = small_g[n_].reshape(shp)
        deltas[n_], new_m[n_], new_v[n_] = dl[o:o + sz].reshape(shp), nm[o:o + sz].reshape(shp), nv[o:o + sz].reshape(shp)
        o += sz

    order = ["ada_w", "ada_b", "norm_pre_mix", "norm_post_mix", "norm_pre_mlp", "norm_post_mlp", "w_in", "w_out",
             "gdn_conv_w", "gdn_a_log", "gdn_dt_bias", "gdn_norm_w", "ssm_conv_w", "ssm_conv_b", "ssm_a_log",
             "ssm_dt_bias", "ssm_d", "ssm_norm_w", "attn_sinks", "sc_conv_w", "w_up", "w_down"]
    return (loss, grad_x, *[grads[n_] for n_ in order], *[deltas[n_] for n_ in order],
            *[new_m[n_] for n_ in order], *[new_v[n_] for n_ in order])
```

```python
import functools
import math

import jax
import jax.numpy as jnp
import numpy as np
from jax import lax
from jax.experimental import pallas as pl
from jax.experimental.pallas import tpu as pltpu

F32 = jnp.float32
BF16 = jnp.bfloat16
MESH = pl.DeviceIdType.MESH

N_DEV = 8
D = 2048
GW = 512
MLP = 8192
EPS = 1e-6
IN_W = 5904
SHARD_IN = IN_W // N_DEV
SHARD_PAD = 752
PW = 6272
ROPE_THETA = 10000.0
WINDOW = 128
GDN_CHUNK = 64
SSM_CHUNK = 128
NEG = -1e30

ADAM_LR, ADAM_B1, ADAM_B2, ADAM_EPS, ADAM_WD, ADAM_STEP = 0.001, 0.9, 0.999, 1e-08, 0.01, 10

O_GQ, O_GK, O_GV, O_GZ, O_GB, O_GA = 0, 512, 1024, 1536, 2048, 2052
O_SZ, O_SXBC, O_SDT = 2056, 2568, 3592
O_AQ, O_AK, O_AV = 3600, 4112, 4240
O_CB, O_CC, O_CH = 4368, 4880, 5392
P_QKV, P_GZ, P_SXBC, P_SZ, P_AQ, P_CB, P_AK, P_AV, P_GATE = 0, 1536, 2048, 3072, 3584, 4096, 5632, 5888, 6144
LANE_GB, LANE_GA, LANE_SDT = 0, 4, 8

VMEM_BIG = 56 * 1024 * 1024


def _cparams(sem=None, vmem=None):
    kw = {}
    if sem is not None:
        kw["dimension_semantics"] = sem
    if vmem is not None:
        kw["vmem_limit_bytes"] = vmem
    return pltpu.CompilerParams(**kw)


def _me():
    x, y, c = lax.axis_index("x"), lax.axis_index("y"), lax.axis_index("c")
    return x, y, c, 4 * x + 2 * y + c


def _peer(x, y, c, k):
    px = 1 - x if (k >> 2) & 1 else x
    py = 1 - y if (k >> 1) & 1 else y
    pc = 1 - c if k & 1 else c
    return (px, py, pc), 4 * px + 2 * py + pc


def _ag_small(v, name):
    R, W = v.shape

    def body(x_ref, out_ref, send_sems, recv_sems):
        x, y, c, me = _me()
        out_ref[me] = x_ref[...]
        copies = []
        for k in range(1, N_DEV):
            peer, _ = _peer(x, y, c, k)
            cp = pltpu.make_async_remote_copy(
                src_ref=x_ref, dst_ref=out_ref.at[me], send_sem=send_sems.at[k - 1], recv_sem=recv_sems.at[k - 1],
                device_id=peer, device_id_type=MESH)
            cp.start()
            copies.append(cp)
        for cp in copies:
            cp.wait()

    return pl.pallas_call(
        body, name=name,
        out_shape=jax.ShapeDtypeStruct((N_DEV, R, W), v.dtype),
        in_specs=[pl.BlockSpec(memory_space=pltpu.VMEM)],
        out_specs=pl.BlockSpec(memory_space=pltpu.VMEM),
        scratch_shapes=[pltpu.SemaphoreType.DMA((N_DEV - 1,)), pltpu.SemaphoreType.DMA((N_DEV - 1,))],
    )(v)


def _ag_multi(arrs, name):
    cargo = _AgCargo(arrs)
    A = cargo.n_in

    def body(*refs):
        start, mid, end = cargo.phases(refs[:A], refs[A:A + cargo.n_out], refs[A + cargo.n_out:])
        start()
        mid()
        end()

    hbm = pl.BlockSpec(memory_space=pl.ANY)
    return pl.pallas_call(
        body, name=name, out_shape=tuple(cargo.out_shapes), in_specs=[hbm] * A, out_specs=tuple([hbm] * cargo.n_out),
        scratch_shapes=cargo.scratch)(*cargo.arrays)


class _AgCargo:
    def __init__(self, arrs):
        A = len(arrs)
        self.arrays = list(arrs)
        self.n_in = self.n_out = A
        self.out_shapes = [jax.ShapeDtypeStruct((N_DEV,) + v.shape, v.dtype) for v in arrs]
        self.scratch = [pltpu.SemaphoreType.DMA((7 * A,)), pltpu.SemaphoreType.DMA((7 * A,)),
                        pltpu.SemaphoreType.DMA((A,))]

    def phases(self, x_refs, out_refs, sems):
        A = self.n_in
        send_sems, recv_sems, local_sems = sems

        def ctx():
            x, y, c, me = _me()
            return x, y, c, me, (x, y, 1 - c), [(1 - x, y), (x, 1 - y), (1 - x, 1 - y)]

        def copy(a, k, block, to, src=None):
            slot = out_refs[a].at[4 * block[0] + 2 * block[1] + block[2]]
            return pltpu.make_async_remote_copy(
                src_ref=slot if src is None else src, dst_ref=slot, send_sem=send_sems.at[7 * a + k],
                recv_sem=recv_sems.at[7 * a + k], device_id=to, device_id_type=MESH)

        def start():
            x, y, c, me, sibling, chips = ctx()
            for a in range(A):
                pltpu.make_async_copy(x_refs[a], out_refs[a].at[me], local_sems.at[a]).start()
            for a in range(A):
                copy(a, 0, (x, y, c), sibling, src=x_refs[a]).start()
                for j, chip in enumerate(chips):
                    copy(a, 1 + j, (x, y, c), (*chip, c), src=x_refs[a]).start()

        def mid():
            x, y, c, me, sibling, chips = ctx()
            for j, chip in enumerate(chips):
                for a in range(A):
                    copy(a, 1 + j, (*chip, c), (x, y, c)).wait_recv()
                    copy(a, 4 + j, (*chip, c), sibling).start()

        def end():
            x, y, c, me, sibling, chips = ctx()
            for a in range(A):
                copy(a, 0, sibling, (x, y, c)).wait_recv()
                for j, chip in enumerate(chips):
                    copy(a, 4 + j, (*chip, 1 - c), (x, y, c)).wait_recv()
            for a in range(A):
                copy(a, 0, (x, y, c), sibling, src=x_refs[a]).wait_send()
                for j, chip in enumerate(chips):
                    copy(a, 1 + j, (x, y, c), (*chip, c), src=x_refs[a]).wait_send()
                    copy(a, 4 + j, (*chip, c), sibling).wait_send()
                pltpu.make_async_copy(x_refs[a], out_refs[a].at[me], local_sems.at[a]).wait()

        return start, mid, end


class _ExCargo:
    def __init__(self, groups):
        self.flat = [(w, i) for w, layers in enumerate(groups) for i, _ in enumerate(layers)]
        self.arrays = [g for layers in groups for g in layers]
        A = len(self.arrays)
        self.n_in, self.n_out = A, len(groups)
        self.out_shapes = [jax.ShapeDtypeStruct((N_DEV, len(layers)) + layers[0].shape[1:], layers[0].dtype)
                           for layers in groups]
        self.scratch = [pltpu.SemaphoreType.DMA((7 * A,)), pltpu.SemaphoreType.DMA((7 * A,)),
                        pltpu.SemaphoreType.DMA((A,))]

    def phases(self, g_refs, out_refs, sems):
        send_sems, recv_sems, local_sems = sems

        def copies():
            x, y, c, me = _me()
            local = [pltpu.make_async_copy(g_refs[a].at[me], out_refs[w].at[me, i], local_sems.at[a])
                     for a, (w, i) in enumerate(self.flat)]
            remote = []
            for k in range(1, N_DEV):
                peer, pid = _peer(x, y, c, k)
                for a, (w, i) in enumerate(self.flat):
                    remote.append(pltpu.make_async_remote_copy(
                        src_ref=g_refs[a].at[pid], dst_ref=out_refs[w].at[me, i], send_sem=send_sems.at[7 * a + k - 1],
                        recv_sem=recv_sems.at[7 * a + k - 1], device_id=peer, device_id_type=MESH))
            return local, remote

        def start():
            local, remote = copies()
            for cp in local + remote:
                cp.start()

        def end():
            local, remote = copies()
            for cp in remote + local:
                cp.wait()

        return start, (lambda: None), end


def _mm(a, b, *, name, ta=False, tb=False, tm=1024, tn=1024, tk=1024, out_dtype=F32, mode="plain", u=None,
        shards=None, cargo=None):
    K, M = a.shape if ta else a.shape[::-1]
    if shards == "b":
        N, K2 = (b.shape[1], b.shape[0] * b.shape[2]) if tb else (b.shape[0] * b.shape[2], b.shape[1])
    else:
        N, K2 = b.shape if tb else b.shape[::-1]
    assert K == K2, (a.shape, b.shape)
    tm, tn, tk = min(tm, M), min(tn, N), min(tk, K)
    assert M % tm == 0 and N % tn == 0 and K % tk == 0, (M, N, K, tm, tn, tk)
    nk = K // tk
    a_spec = pl.BlockSpec((tk, tm), lambda i, j, k: (k, i)) if ta else pl.BlockSpec((tm, tk), lambda i, j, k: (i, k))
    if shards == "b" and tb:
        assert tk == b.shape[2]
        b_spec = pl.BlockSpec((None, tn, tk), lambda i, j, k: (k, j, 0))
    elif shards == "b":
        assert tn == b.shape[2]
        b_spec = pl.BlockSpec((None, tk, tn), lambda i, j, k: (j, k, 0))
    else:
        b_spec = pl.BlockSpec((tn, tk), lambda i, j, k: (j, k)) if tb else pl.BlockSpec((tk, tn), lambda i, j, k: (k, j))
    if shards == "o":
        o_spec = pl.BlockSpec((None, tm, tn), lambda i, j, k: (j, i, 0))
        o_shape = (N // tn, M, tn)
    else:
        o_spec = pl.BlockSpec((tm, tn), lambda i, j, k: (i, j))
        o_shape = (M, N)
    dn = (((0 if ta else 1,), (1 if tb else 0,)), ((), ()))
    ni_, nj = M // tm, N // tn
    nsteps = ni_ * nj * nk
    n_in = 3 if mode == "drelu2" else 2
    n_out = 2 if mode == "relu2" else 1
    c_arr, c_ispec, c_oshape, c_ospec = _cargo_io(cargo)
    ni, no = len(c_arr), len(c_oshape)

    def body(*refs):
        a_ref, b_ref = refs[0], refs[1]
        u_ref = refs[2] if mode == "drelu2" else None
        outs = refs[n_in + ni:n_in + ni + n_out]
        acc = refs[n_in + ni + n_out + no]
        k = pl.program_id(2)
        step = (pl.program_id(0) * nj + pl.program_id(1)) * nk + k
        drain = _carry(cargo, (refs[n_in:n_in + ni], refs[n_in + ni + n_out:n_in + ni + n_out + no],
                               refs[n_in + ni + n_out + no + 1:]), step, nsteps)

        def finish(r):
            if mode == "plain":
                outs[0][...] = r.astype(out_dtype)
            elif mode == "relu2":
                rr = jnp.maximum(r, 0.0)
                outs[0][...] = (rr * rr).astype(BF16)
                outs[1][...] = r.astype(BF16)
            else:
                outs[0][...] = (r * (2.0 * jnp.maximum(u_ref[...].astype(F32), 0.0))).astype(out_dtype)

        part = lax.dot_general(a_ref[...].astype(BF16), b_ref[...].astype(BF16), dn, preferred_element_type=F32)
        if nk == 1:
            finish(part)
        else:
            @pl.when(k == 0)
            def _():
                acc[...] = jnp.zeros_like(acc)

            acc[...] += part
            pl.when(k == nk - 1)(lambda: finish(acc[...]))

        drain()

    in_specs, args = [a_spec, b_spec], [a, b]
    if mode == "drelu2":
        in_specs.append(o_spec)
        args.append(u)
    if mode == "relu2":
        out_shape = [jax.ShapeDtypeStruct(o_shape, BF16), jax.ShapeDtypeStruct(o_shape, BF16)]
        out_specs = [o_spec, o_spec]
    else:
        out_shape = [jax.ShapeDtypeStruct(o_shape, out_dtype)]
        out_specs = [o_spec]
    res = pl.pallas_call(
        body, name=name, grid=(ni_, nj, nk), in_specs=in_specs + c_ispec, out_specs=tuple(out_specs + c_ospec),
        out_shape=tuple(out_shape + c_oshape),
        scratch_shapes=[pltpu.VMEM((tm, tn) if nk > 1 else (8, 128), F32)] + (cargo.scratch if cargo else []),
        compiler_params=_cparams(("arbitrary",) * 3 if cargo else ("parallel", "parallel", "arbitrary"), VMEM_BIG),
    )(*args, *c_arr)
    return res[0] if len(res) == 1 else res


ROW_T = 256


def _rms(x, w):
    return x * lax.rsqrt(jnp.mean(x * x, axis=-1, keepdims=True) + EPS) * w


def _pre_fn(x, w, scale, shift):
    return _rms(x, w) * (1.0 + scale) + shift


def _post_fn(y, w, gate):
    return gate * _rms(y, w)


def _row_spec(T, W):
    return pl.BlockSpec((T, W), lambda i: (i, 0))


def _vec_spec(W):
    return pl.BlockSpec((1, W), lambda i: (0, 0))


def _pre_fwd(x, w, scale, shift, name):
    L = x.shape[0]
    T = min(ROW_T, L)

    def body(x_ref, w_ref, sc_ref, sh_ref, h_ref):
        h_ref[...] = _pre_fn(x_ref[...], w_ref[...], sc_ref[...], sh_ref[...]).astype(BF16)

    return pl.pallas_call(
        body, name=name, grid=(L // T,), in_specs=[_row_spec(T, D), _vec_spec(D), _vec_spec(D), _vec_spec(D)],
        out_specs=_row_spec(T, D), out_shape=jax.ShapeDtypeStruct((L, D), BF16),
        compiler_params=_cparams(("parallel",), VMEM_BIG))(x, w, scale, shift)


def _pre_bwd(x, w, scale, shift, dh, dres, name):
    L = x.shape[0]
    T = min(ROW_T, L)

    def body(x_ref, w_ref, sc_ref, sh_ref, dh_ref, dres_ref, dx_ref, dw_ref, dsc_ref, dsh_ref):
        @pl.when(pl.program_id(0) == 0)
        def _():
            dw_ref[...] = jnp.zeros_like(dw_ref)
            dsc_ref[...] = jnp.zeros_like(dsc_ref)
            dsh_ref[...] = jnp.zeros_like(dsh_ref)

        _, vjp = jax.vjp(_pre_fn, x_ref[...], w_ref[...], sc_ref[...], sh_ref[...])
        dx, dw, dsc, dsh = vjp(dh_ref[...].astype(F32))
        dx_ref[...] = dres_ref[...] + dx
        dw_ref[...] += dw
        dsc_ref[...] += dsc
        dsh_ref[...] += dsh

    vec = jax.ShapeDtypeStruct((1, D), F32)
    return pl.pallas_call(
        body, name=name, grid=(L // T,),
        in_specs=[_row_spec(T, D), _vec_spec(D), _vec_spec(D), _vec_spec(D), _row_spec(T, D), _row_spec(T, D)],
        out_specs=(_row_spec(T, D), _vec_spec(D), _vec_spec(D), _vec_spec(D)),
        out_shape=(jax.ShapeDtypeStruct((L, D), F32), vec, vec, vec),
        compiler_params=_cparams(("arbitrary",), VMEM_BIG))(x, w, scale, shift, dh, dres)


def _post_pre_fwd(x, y, w_post, gate, w_pre, scale, shift, name):
    L = x.shape[0]
    T = min(ROW_T, L)

    def body(x_ref, y_ref, wp_ref, g_ref, w_ref, sc_ref, sh_ref, o_ref, h_ref):
        xn = x_ref[...] + _post_fn(y_ref[...], wp_ref[...], g_ref[...])
        o_ref[...] = xn
        h_ref[...] = _pre_fn(xn, w_ref[...], sc_ref[...], sh_ref[...]).astype(BF16)

    return pl.pallas_call(
        body, name=name, grid=(L // T,),
        in_specs=[_row_spec(T, D), _row_spec(T, D)] + [_vec_spec(D)] * 5,
        out_specs=(_row_spec(T, D), _row_spec(T, D)),
        out_shape=(jax.ShapeDtypeStruct((L, D), F32), jax.ShapeDtypeStruct((L, D), BF16)),
        compiler_params=_cparams(("parallel",), VMEM_BIG))(x, y, w_post, gate, w_pre, scale, shift)


def _pre_post_bwd(x, w_pre, scale, shift, dh, dres, y, w_post, gate, name):
    L = x.shape[0]
    T = min(ROW_T, L)

    def body(x_ref, w_ref, sc_ref, sh_ref, dh_ref, dres_ref, y_ref, wp_ref, g_ref,
             dx_ref, dy_ref, dw_ref, dsc_ref, dsh_ref, dwp_ref, dg_ref):
        @pl.when(pl.program_id(0) == 0)
        def _():
            for r in (dw_ref, dsc_ref, dsh_ref, dwp_ref, dg_ref):
                r[...] = jnp.zeros_like(r)

        _, vjp = jax.vjp(_pre_fn, x_ref[...], w_ref[...], sc_ref[...], sh_ref[...])
        dx, dw, dsc, dsh = vjp(dh_ref[...].astype(F32))
        dx = dres_ref[...] + dx
        dx_ref[...] = dx
        _, vjp2 = jax.vjp(_post_fn, y_ref[...], wp_ref[...], g_ref[...])
        dy, dwp, dg = vjp2(dx)
        dy_ref[...] = dy.astype(BF16)
        dw_ref[...] += dw
        dsc_ref[...] += dsc
        dsh_ref[...] += dsh
        dwp_ref[...] += dwp
        dg_ref[...] += dg

    vec = jax.ShapeDtypeStruct((1, D), F32)
    v = _vec_spec(D)
    r = _row_spec(T, D)
    return pl.pallas_call(
        body, name=name, grid=(L // T,), in_specs=[r, v, v, v, r, r, r, v, v],
        out_specs=(r, r, v, v, v, v, v),
        out_shape=(jax.ShapeDtypeStruct((L, D), F32), jax.ShapeDtypeStruct((L, D), BF16), vec, vec, vec, vec, vec),
        compiler_params=_cparams(("arbitrary",), VMEM_BIG))(x, w_pre, scale, shift, dh, dres, y, w_post, gate)


def _last_residual_loss(x, y, w, gate, target, name):
    L = x.shape[0]
    T = min(ROW_T, L)

    def body(x_ref, y_ref, w_ref, g_ref, t_ref, s_ref, d_ref, dy_ref, dw_ref, dg_ref):
        @pl.when(pl.program_id(0) == 0)
        def _():
            for r in (s_ref, dw_ref, dg_ref):
                r[...] = jnp.zeros_like(r)

        res, vjp = jax.vjp(_post_fn, y_ref[...], w_ref[...], g_ref[...])
        e = (x_ref[...] + res) - t_ref[...]
        d = e / float(D)
        d_ref[...] = d
        s_ref[...] += jnp.sum(e * e)
        dy, dw, dg = vjp(d)
        dy_ref[...] = dy.astype(BF16)
        dw_ref[...] += dw
        dg_ref[...] += dg

    vec = jax.ShapeDtypeStruct((1, D), F32)
    r, v = _row_spec(T, D), _vec_spec(D)
    return pl.pallas_call(
        body, name=name, grid=(L // T,), in_specs=[r, r, v, v, r],
        out_specs=(pl.BlockSpec((8, 128), lambda i: (0, 0)), r, r, v, v),
        out_shape=(jax.ShapeDtypeStruct((8, 128), F32), jax.ShapeDtypeStruct((L, D), F32),
                   jax.ShapeDtypeStruct((L, D), BF16), vec, vec),
        compiler_params=_cparams(("arbitrary",), VMEM_BIG))(x, y, w, gate, target)


CONV_T = 512


def _conv_fwd(x, ci, C, w, b, name):
    L = x.shape[0]
    K = w.shape[0]
    T = min(CONV_T, L)

    def body(xc_ref, xp_ref, w_ref, b_ref, y_ref, buf):
        i = pl.program_id(0)
        buf[0:8, :] = jnp.where(i > 0, xp_ref[...], 0.0)
        buf[8:T + 8, :] = xc_ref[...]
        acc = jnp.zeros((T, C), F32) + b_ref[...]
        for k in range(K):
            acc = acc + w_ref[k:k + 1, :] * buf[pl.ds(8 - (K - 1) + k, T), :]
        y_ref[...] = acc

    return pl.pallas_call(
        body, name=name, grid=(L // T,),
        in_specs=[pl.BlockSpec((T, C), lambda i: (i, ci)),
                  pl.BlockSpec((8, C), lambda i: (jnp.maximum(i * (T // 8) - 1, 0), ci)),
                  pl.BlockSpec((K, C), lambda i: (0, 0)), _vec_spec(C)],
        out_specs=_row_spec(T, C), out_shape=jax.ShapeDtypeStruct((L, C), F32),
        scratch_shapes=[pltpu.VMEM((T + 8, C), F32)],
        compiler_params=_cparams(("parallel",), VMEM_BIG))(x, x, w, b)


def _conv_bwd(dy, x, ci, C, w, name):
    L = x.shape[0]
    K = w.shape[0]
    T = min(CONV_T, L)
    nt = L // T

    def body(dc_ref, dn_ref, xc_ref, xp_ref, w_ref, dx_ref, dw_ref, db_ref, dbuf, xbuf):
        i = pl.program_id(0)

        @pl.when(i == 0)
        def _():
            dw_ref[...] = jnp.zeros_like(dw_ref)
            db_ref[...] = jnp.zeros_like(db_ref)

        dyc = dc_ref[...]
        dbuf[0:T, :] = dyc
        dbuf[T:T + 8, :] = jnp.where(i < nt - 1, dn_ref[...], 0.0)
        xbuf[0:8, :] = jnp.where(i > 0, xp_ref[...], 0.0)
        xbuf[8:T + 8, :] = xc_ref[...]
        dx = jnp.zeros((T, C), F32)
        for k in range(K):
            dx = dx + w_ref[k:k + 1, :] * dbuf[pl.ds(K - 1 - k, T), :]
            dw_ref[k:k + 1, :] += jnp.sum(dyc * xbuf[pl.ds(8 - (K - 1) + k, T), :], axis=0, keepdims=True)
        dx_ref[...] = dx.astype(BF16)
        db_ref[...] += jnp.sum(dyc, axis=0, keepdims=True)

    return pl.pallas_call(
        body, name=name, grid=(nt,),
        in_specs=[_row_spec(T, C),
                  pl.BlockSpec((8, C), lambda i: (jnp.minimum((i + 1) * (T // 8), L // 8 - 1), 0)),
                  pl.BlockSpec((T, C), lambda i: (i, ci)),
                  pl.BlockSpec((8, C), lambda i: (jnp.maximum(i * (T // 8) - 1, 0), ci)),
                  pl.BlockSpec((K, C), lambda i: (0, 0))],
        out_specs=(_row_spec(T, C), pl.BlockSpec((K, C), lambda i: (0, 0)), _vec_spec(C)),
        out_shape=(jax.ShapeDtypeStruct((L, C), BF16), jax.ShapeDtypeStruct((K, C), F32),
                   jax.ShapeDtypeStruct((1, C), F32)),
        scratch_shapes=[pltpu.VMEM((T + 8, C), F32), pltpu.VMEM((T + 8, C), F32)],
        compiler_params=_cparams(("arbitrary",), VMEM_BIG))(dy, dy, x, x, w)


def _sc_fwd(proj, w, name):
    L = proj.shape[0]
    K = w.shape[0]
    C = GW
    T = min(CONV_T, L)
    cb0 = P_CB // C

    def body(b_ref, cc_ref, ch_ref, cp_ref, hp_ref, w_ref, y_ref, buf):
        i = pl.program_id(0)
        buf[0:8, :] = jnp.where(i > 0, cp_ref[...] * hp_ref[...], 0.0)
        buf[8:T + 8, :] = cc_ref[...] * ch_ref[...]
        acc = jnp.zeros((T, C), F32)
        for k in range(K):
            acc = acc + w_ref[k:k + 1, :] * buf[pl.ds(8 - (K - 1) + k, T), :]
        y_ref[...] = (b_ref[...] * acc).astype(BF16)

    def cur(j):
        return pl.BlockSpec((T, C), lambda i: (i, cb0 + j))

    def prev(j):
        return pl.BlockSpec((8, C), lambda i: (jnp.maximum(i * (T // 8) - 1, 0), cb0 + j))

    return pl.pallas_call(
        body, name=name, grid=(L // T,),
        in_specs=[cur(0), cur(1), cur(2), prev(1), prev(2), pl.BlockSpec((K, C), lambda i: (0, 0))],
        out_specs=_row_spec(T, C), out_shape=jax.ShapeDtypeStruct((L, C), BF16),
        scratch_shapes=[pltpu.VMEM((T + 8, C), F32)],
        compiler_params=_cparams(("parallel",), VMEM_BIG))(proj, proj, proj, proj, proj, w)


def _sc_bwd(dycat, proj, w, name):
    L = proj.shape[0]
    K = w.shape[0]
    C = GW
    T = min(CONV_T, L)
    nt = L // T
    cb0 = P_CB // C

    def body(dy_ref, dyn_ref, b_ref, bn_ref, cc_ref, ch_ref, cp_ref, hp_ref, w_ref, d_ref, dw_ref, gbuf, ubuf):
        i = pl.program_id(0)

        @pl.when(i == 0)
        def _():
            dw_ref[...] = jnp.zeros_like(dw_ref)

        dy = dy_ref[...]
        cc, ch = cc_ref[...], ch_ref[...]
        g = dy * b_ref[...]
        gbuf[0:T, :] = g
        gbuf[T:T + 8, :] = jnp.where(i < nt - 1, dyn_ref[...] * bn_ref[...], 0.0)
        ubuf[0:8, :] = jnp.where(i > 0, cp_ref[...] * hp_ref[...], 0.0)
        ubuf[8:T + 8, :] = cc * ch
        conv = jnp.zeros((T, C), F32)
        du = jnp.zeros((T, C), F32)
        for k in range(K):
            us = ubuf[pl.ds(8 - (K - 1) + k, T), :]
            conv = conv + w_ref[k:k + 1, :] * us
            du = du + w_ref[k:k + 1, :] * gbuf[pl.ds(K - 1 - k, T), :]
            dw_ref[k:k + 1, :] += jnp.sum(g * us, axis=0, keepdims=True)
        d_ref[:, 0:C] = (dy * conv).astype(BF16)
        d_ref[:, C:2 * C] = (du * ch).astype(BF16)
        d_ref[:, 2 * C:3 * C] = (du * cc).astype(BF16)

    def cur(j):
        return pl.BlockSpec((T, C), lambda i: (i, cb0 + j))

    def prev(j):
        return pl.BlockSpec((8, C), lambda i: (jnp.maximum(i * (T // 8) - 1, 0), cb0 + j))

    def nxt(col):
        return pl.BlockSpec((8, C), lambda i: (jnp.minimum((i + 1) * (T // 8), L // 8 - 1), col))

    return pl.pallas_call(
        body, name=name, grid=(nt,),
        in_specs=[pl.BlockSpec((T, C), lambda i: (i, 3)), nxt(3), cur(0), nxt(cb0), cur(1), cur(2), prev(1), prev(2),
                  pl.BlockSpec((K, C), lambda i: (0, 0))],
        out_specs=(_row_spec(T, 3 * C), pl.BlockSpec((K, C), lambda i: (0, 0))),
        out_shape=(jax.ShapeDtypeStruct((L, 3 * C), BF16), jax.ShapeDtypeStruct((K, C), F32)),
        scratch_shapes=[pltpu.VMEM((T + 8, C), F32), pltpu.VMEM((T + 8, C), F32)],
        compiler_params=_cparams(("arbitrary",), VMEM_BIG))(dycat, dycat, proj, proj, proj, proj, proj, proj, w)


def _silu(x):
    return x * jax.nn.sigmoid(x)


def _softplus(x):
    return jnp.maximum(x, 0.0) + jnp.log1p(jnp.exp(-jnp.abs(x)))


def _lane_pick(arr, idx):
    lane = lax.broadcasted_iota(jnp.int32, (1, 128), 1)
    return jnp.sum(jnp.where(lane == idx, arr, 0.0), axis=1, keepdims=True)


def _tri(n, strict=False):
    r = lax.broadcasted_iota(jnp.int32, (n, n), 0)
    c = lax.broadcasted_iota(jnp.int32, (n, n), 1)
    return (r > c) if strict else (r >= c)


def _bdot(a, b, dn=(((1,), (0,)), ((), ()))):
    return lax.dot_general(a.astype(BF16), b.astype(BF16), dn, preferred_element_type=F32)


NT = (((1,), (1,)), ((), ()))
TN = (((0,), (0,)), ((), ()))


def _split3(x):
    hi = x.astype(BF16)
    r = x - hi.astype(F32)
    mid = r.astype(BF16)
    lo = (r - mid.astype(F32)).astype(BF16)
    return hi, mid, lo


def _mask_dot(pieces, mask, dn, mask_first):
    def one(p):
        ops = (mask, p) if mask_first else (p, mask)
        return lax.dot_general(ops[0], ops[1], dn, preferred_element_type=F32)
    hi, mid, lo = pieces
    return (one(lo) + one(mid)) + one(hi)


def _tri_apply(x, upper):
    n = x.shape[0]
    r = lax.broadcasted_iota(jnp.int32, (n, n), 0)
    c = lax.broadcasted_iota(jnp.int32, (n, n), 1)
    mask = ((r <= c) if upper else (r >= c)).astype(BF16)
    return _mask_dot(_split3(x), mask, (((1,), (0,)), ((), ())), True)


@jax.custom_vjp
def _cumsum_col(cb):
    return _tri_apply(cb, False)


_cumsum_col.defvjp(lambda cb: (_tri_apply(cb, False), None), lambda _, d: (_tri_apply(d, True),))


def _cumsum_row_fwd(cb):
    tri = _tri(cb.shape[0]).astype(BF16)
    return _mask_dot(_split3(cb), tri, (((0,), (1,)), ((), ())), False)


def _cumsum_row_bwd(_, d):
    tri = _tri(d.shape[1]).astype(BF16)
    return (_mask_dot(_split3(d), tri, (((0,), (1,)), ((), ())), True),)


@jax.custom_vjp
def _cumsum_row(cb):
    return _cumsum_row_fwd(cb)


_cumsum_row.defvjp(lambda cb: (_cumsum_row_fwd(cb), None), _cumsum_row_bwd)


def _dot3(a, b):
    ah, bh = a.astype(BF16), b.astype(BF16)
    al, bl = (a - ah.astype(F32)).astype(BF16), (b - bh.astype(F32)).astype(BF16)
    d = lambda p, q: jnp.dot(p, q, preferred_element_type=F32)
    return (d(al, bh) + d(ah, bl)) + d(ah, bh)


def _unit_lower_inv_impl(ms):
    n = ms[0].shape[0]
    eye = (lax.broadcasted_iota(jnp.int32, (n, n), 0) == lax.broadcasted_iota(jnp.int32, (n, n), 1)).astype(F32)
    ps = [-m for m in ms]
    ts = [eye + p for p in ps]
    for _ in range(int(math.log2(n)) - 1):
        ps = [_dot3(p, p) for p in ps]
        ts = [t + _dot3(t, p) for t, p in zip(ts, ps)]
    return tuple(ts)


@jax.custom_vjp
def _unit_lower_inv(ms):
    return _unit_lower_inv_impl(ms)


def _unit_lower_inv_fwd(ms):
    ts = _unit_lower_inv_impl(ms)
    return ts, ts


def _unit_lower_inv_bwd(ts, ds):
    xs = [_bdot(t, d, TN) for t, d in zip(ts, ds)]
    return (tuple(-_bdot(x, t, NT) for x, t in zip(xs, ts)),)


_unit_lower_inv.defvjp(_unit_lower_inv_fwd, _unit_lower_inv_bwd)


GDN_SUB = 4
GDN_H = 4


def _gdn_step_fn(qs, ks, vs, zs, gs, S, alog_row, dtb_row, nw):
    n = GDN_CHUNK
    lanes = [(h, c) for h in range(GDN_H) for c in range(GDN_SUB)]
    z3 = lambda f, a, b, c_: [f(x, y, z) for x, y, z in zip(a, b, c_)]
    q = [_silu(qs[h][c]) for h, c in lanes]
    k = [_silu(ks[h][c]) for h, c in lanes]
    v = [_silu(vs[h][c]) for h, c in lanes]
    q = [t * lax.rsqrt(jnp.sum(t * t, axis=-1, keepdims=True) + EPS) * (128.0 ** -0.5) for t in q]
    k = [t * lax.rsqrt(jnp.sum(t * t, axis=-1, keepdims=True) + EPS) for t in k]
    beta = [jax.nn.sigmoid(_lane_pick(gs[c], LANE_GB + h)) for h, c in lanes]
    rate = [-jnp.exp(_lane_pick(alog_row, LANE_GA + h)) for h in range(GDN_H)]
    bias = [_lane_pick(dtb_row, LANE_GA + h) for h in range(GDN_H)]
    g = [rate[h] * _softplus(_lane_pick(gs[c], LANE_GA + h) + bias[h]) for h, c in lanes]
    cb = [jnp.broadcast_to(t, (n, n)) for t in g]
    gc_col = [_cumsum_col(t) for t in cb]
    gc_row = [_cumsum_row(t) for t in cb]
    tri_incl, tri_strict = _tri(n), _tri(n, strict=True)
    decay = [jnp.exp(jnp.where(tri_incl, a - b, NEG)) for a, b in zip(gc_col, gc_row)]
    gc = [t[:, 0:1] for t in gc_col]
    g_last = [jnp.sum(t, axis=0, keepdims=True) for t in g]
    kb = [a * b for a, b in zip(k, beta)]
    m = z3(lambda a, b, d: jnp.where(tri_strict, _bdot(a, b, NT) * d, 0.0), kb, k, decay)
    t_inv = _unit_lower_inv(tuple(m))
    egc = [jnp.exp(t) for t in gc]
    u = z3(lambda t, a, b: _bdot(t, a * b), t_inv, v, beta)
    w = z3(lambda t, a, e: _bdot(t, a * e), t_inv, kb, egc)
    attn = z3(lambda a, b, d: jnp.where(tri_incl, _bdot(a, b, NT) * d, 0.0), q, k, decay)
    qd = [a * e for a, e in zip(q, egc)]
    kd = z3(lambda a, gl, c_: a * jnp.exp(gl - c_), k, g_last, gc)
    egl = [jnp.exp(t) for t in g_last]
    zg = [_silu(zs[h][c]) for h, c in lanes]
    S = list(S)
    ys = [[None] * GDN_SUB for _ in range(GDN_H)]
    for c in range(GDN_SUB):
        ii = [h * GDN_SUB + c for h in range(GDN_H)]
        v_new = [u[i] - _bdot(w[i], S[h]) for h, i in enumerate(ii)]
        o = [_bdot(qd[i], S[h]) + _bdot(attn[i], v_new[h]) for h, i in enumerate(ii)]
        S = [S[h] * egl[i] + _bdot(kd[i], v_new[h], TN) for h, i in enumerate(ii)]
        for h, i in enumerate(ii):
            ys[h][c] = _rms(o[h], nw) * zg[i]
    return tuple(tuple(y) for y in ys), tuple(S)


def _gdn_step_args(qkv_ref, z_ref, g_ref):
    n = GDN_CHUNK
    rows = [slice(n * c, n * c + n) for c in range(GDN_SUB)]
    col = lambda ref, o: tuple(tuple(ref[r, o + 128 * h:o + 128 * h + 128] for r in rows) for h in range(GDN_H))
    return rows, (col(qkv_ref, 0), col(qkv_ref, 512), col(qkv_ref, 1024), col(z_ref, 0), tuple(g_ref[r, :] for r in rows))


def _carry(cargo, refs, step, nsteps):
    if cargo is None:
        return lambda: None
    start, mid, end = cargo.phases(*refs)
    pl.when(step == 0)(start)
    pl.when(step == (3 * nsteps) // 4)(mid)
    return lambda: pl.when(step == nsteps - 1)(end)


def _cargo_io(cargo):
    if cargo is None:
        return [], [], [], []
    hbm = pl.BlockSpec(memory_space=pl.ANY)
    return list(cargo.arrays), [hbm] * cargo.n_in, list(cargo.out_shapes), [hbm] * cargo.n_out


def _gdn_fwd(qkvc, proj, alog_row, dtb_row, nw, name, cargo=None):
    L = qkvc.shape[0]
    n = GDN_CHUNK * GDN_SUB
    Nc = L // n
    c_arr, c_ispec, c_oshape, c_ospec = _cargo_io(cargo)
    ni, no = len(c_arr), len(c_oshape)

    def body(*refs):
        qkv_ref, z_ref, g_ref, al_ref, db_ref, nw_ref = refs[:6]
        y_ref, sin_ref = refs[6 + ni:8 + ni]
        S = refs[8 + ni + no]
        step = pl.program_id(0)
        drain = _carry(cargo, (refs[6:6 + ni], refs[8 + ni:8 + ni + no], refs[9 + ni + no:]), step, Nc)

        @pl.when(step == 0)
        def _():
            S[...] = jnp.zeros_like(S)

        rows, args = _gdn_step_args(qkv_ref, z_ref, g_ref)
        states = tuple(S[h] for h in range(GDN_H))
        ys, sn = _gdn_step_fn(*args, states, al_ref[...], db_ref[...], nw_ref[...])
        for h in range(GDN_H):
            sin_ref[0, h] = states[h]
            for c in range(GDN_SUB):
                y_ref[rows[c], 128 * h:128 * h + 128] = ys[h][c].astype(BF16)
            S[h] = sn[h]
        drain()

    return pl.pallas_call(
        body, name=name, grid=(Nc,),
        in_specs=[pl.BlockSpec((n, 1536), lambda i: (i, 0)), pl.BlockSpec((n, 512), lambda i: (i, P_GZ // 512)),
                  pl.BlockSpec((n, 128), lambda i: (i, P_GATE // 128)), _vec_spec(128), _vec_spec(128), _vec_spec(128)]
        + c_ispec,
        out_specs=tuple([pl.BlockSpec((n, 512), lambda i: (i, 0)), pl.BlockSpec((1, 4, 128, 128), lambda i: (i, 0, 0, 0))]
                        + c_ospec),
        out_shape=tuple([jax.ShapeDtypeStruct((L, 512), BF16), jax.ShapeDtypeStruct((Nc, 4, 128, 128), F32)] + c_oshape),
        scratch_shapes=[pltpu.VMEM((4, 128, 128), F32)] + (cargo.scratch if cargo else []),
        compiler_params=_cparams(("arbitrary",), VMEM_BIG))(qkvc, proj, proj, alog_row, dtb_row, nw, *c_arr)


def _gdn_bwd(qkvc, proj, alog_row, dtb_row, nw, s_in, dycat, name, cargo=None):
    L = qkvc.shape[0]
    n = GDN_CHUNK * GDN_SUB
    Nc = L // n
    c_arr, c_ispec, c_oshape, c_ospec = _cargo_io(cargo)
    ni, no = len(c_arr), len(c_oshape)

    def body(*refs):
        qkv_ref, z_ref, g_ref, al_ref, db_ref, nw_ref, sin_ref, dy_ref = refs[:8]
        dqkv_ref, dz_ref, dg_ref, dal_ref, ddb_ref, dnw_ref = refs[8 + ni:14 + ni]
        dS = refs[14 + ni + no]
        step = pl.program_id(0)
        drain = _carry(cargo, (refs[8:8 + ni], refs[14 + ni:14 + ni + no], refs[15 + ni + no:]), step, Nc)

        @pl.when(step == 0)
        def _():
            dS[...] = jnp.zeros_like(dS)
            dal_ref[...] = jnp.zeros_like(dal_ref)
            ddb_ref[...] = jnp.zeros_like(ddb_ref)
            dnw_ref[...] = jnp.zeros_like(dnw_ref)

        rows, args = _gdn_step_args(qkv_ref, z_ref, g_ref)
        s_in = tuple(sin_ref[0, h] for h in range(GDN_H))
        _, vjp = jax.vjp(_gdn_step_fn, *args, s_in, al_ref[...], db_ref[...], nw_ref[...])
        dys = tuple(tuple(dy_ref[r, 128 * h:128 * h + 128] for r in rows) for h in range(GDN_H))
        dq, dk, dv, dz, dgt, ds, dal, ddb, dnw = vjp((dys, tuple(dS[h] for h in range(GDN_H))))
        for h in range(GDN_H):
            for c in range(GDN_SUB):
                dqkv_ref[rows[c], 128 * h:128 * h + 128] = dq[h][c]
                dqkv_ref[rows[c], 512 + 128 * h:640 + 128 * h] = dk[h][c]
                dqkv_ref[rows[c], 1024 + 128 * h:1152 + 128 * h] = dv[h][c]
                dz_ref[rows[c], 128 * h:128 * h + 128] = dz[h][c].astype(BF16)
            dS[h] = ds[h]
        for c in range(GDN_SUB):
            dg_ref[rows[c], :] = dgt[c]
        dal_ref[...] += dal
        ddb_ref[...] += ddb
        dnw_ref[...] += dnw
        drain()

    rev = lambda i: Nc - 1 - i
    vec = jax.ShapeDtypeStruct((1, 128), F32)
    return pl.pallas_call(
        body, name=name, grid=(Nc,),
        in_specs=[pl.BlockSpec((n, 1536), lambda i: (rev(i), 0)), pl.BlockSpec((n, 512), lambda i: (rev(i), P_GZ // 512)),
                  pl.BlockSpec((n, 128), lambda i: (rev(i), P_GATE // 128)), _vec_spec(128), _vec_spec(128), _vec_spec(128),
                  pl.BlockSpec((1, 4, 128, 128), lambda i: (rev(i), 0, 0, 0)), pl.BlockSpec((n, 512), lambda i: (rev(i), 0))]
        + c_ispec,
        out_specs=tuple([pl.BlockSpec((n, 1536), lambda i: (rev(i), 0)), pl.BlockSpec((n, 512), lambda i: (rev(i), 0)),
                         pl.BlockSpec((n, 128), lambda i: (rev(i), 0)), _vec_spec(128), _vec_spec(128), _vec_spec(128)]
                        + c_ospec),
        out_shape=tuple([jax.ShapeDtypeStruct((L, 1536), F32), jax.ShapeDtypeStruct((L, 512), BF16),
                         jax.ShapeDtypeStruct((L, 128), F32), vec, vec, vec] + c_oshape),
        scratch_shapes=[pltpu.VMEM((4, 128, 128), F32)] + (cargo.scratch if cargo else []),
        compiler_params=_cparams(("arbitrary",), VMEM_BIG))(qkvc, proj, proj, alog_row, dtb_row, nw, s_in, dycat, *c_arr)


def _ssd_chunk_fn(xs, bs, cs, zs, gates, st, alog_row, dtb_row, d_row, nws):
    n = SSM_CHUNK
    tri = _tri(n)
    lane = lax.broadcasted_iota(jnp.int32, (1, 128), 1)
    lo = lane < 64
    xs = [_silu(t) for t in xs]
    bs = [_silu(t) for t in bs]
    cs = [_silu(t) for t in cs]
    cb = [_bdot(cs[g], bs[g], NT) for g in range(2)]
    H = range(8)
    P = range(4)
    dt = [_softplus(_lane_pick(gates, LANE_SDT + h) + _lane_pick(dtb_row, LANE_SDT + h)) for h in H]
    da = [dt[h] * (-jnp.exp(_lane_pick(alog_row, LANE_SDT + h))) for h in H]
    dab = [jnp.broadcast_to(t, (n, n)) for t in da]
    cs_col = [_cumsum_col(t) for t in dab]
    cs_row = [_cumsum_row(t) for t in dab]
    lmat = [jnp.exp(jnp.where(tri, a - b, NEG)) for a, b in zip(cs_col, cs_row)]
    xdt_h = [jnp.where(lo if h % 2 == 0 else jnp.logical_not(lo), xs[h // 2] * dt[h], 0.0) for h in H]
    yd = [_bdot(cb[h // 4] * lmat[h], xdt_h[h]) for h in H]
    tot = [jnp.sum(t, axis=0, keepdims=True) for t in da]
    dsk = [_lane_pick(d_row, LANE_SDT + h) for h in H]
    cs_lane = [jnp.where(lo, cs_col[2 * p], cs_col[2 * p + 1]) for p in P]
    dt_lane = [jnp.where(lo, dt[2 * p], dt[2 * p + 1]) for p in P]
    tot_lane = [jnp.where(lo, tot[2 * p], tot[2 * p + 1]) for p in P]
    dsk_lane = [jnp.where(lo, dsk[2 * p], dsk[2 * p + 1]) for p in P]
    xdt = [xs[p] * dt_lane[p] for p in P]
    upd = [_bdot(bs[p // 2], xdt[p] * jnp.exp(tot_lane[p] - cs_lane[p]), TN) for p in P]
    st_new = [st[p] * jnp.exp(tot_lane[p]) + upd[p] for p in P]
    yoff = [_bdot(cs[p // 2], st[p]) * jnp.exp(cs_lane[p]) for p in P]
    zg = [_silu(zs[p]) for p in P]
    ys = [(yd[2 * p] + yd[2 * p + 1] + yoff[p] + xs[p] * dsk_lane[p]) * zg[p] for p in P]
    out = []
    for g in range(2):
        ms = (jnp.sum(ys[2 * g] ** 2, axis=-1, keepdims=True) + jnp.sum(ys[2 * g + 1] ** 2, axis=-1, keepdims=True)) / 256.0
        rstd = lax.rsqrt(ms + EPS)
        out += [ys[2 * g] * rstd * nws[2 * g], ys[2 * g + 1] * rstd * nws[2 * g + 1]]
    return tuple(out), tuple(st_new)


def _ssd_args(xbc_ref, z_ref, nw_ref):
    xs = [xbc_ref[:, 128 * p:128 * p + 128] for p in range(4)]
    bs = [xbc_ref[:, 512 + 128 * g:640 + 128 * g] for g in range(2)]
    cs = [xbc_ref[:, 768 + 128 * g:896 + 128 * g] for g in range(2)]
    zs = [z_ref[:, 128 * p:128 * p + 128] for p in range(4)]
    nws = [nw_ref[:, 128 * p:128 * p + 128] for p in range(4)]
    return xs, bs, cs, zs, nws


def _ssd_fwd(xbcc, proj, alog_row, dtb_row, d_row, nw, name):
    L = xbcc.shape[0]
    n = SSM_CHUNK
    Nc = L // n

    def body(xbc_ref, z_ref, g_ref, al_ref, db_ref, d_ref, nw_ref, y_ref, sin_ref, S):
        @pl.when(pl.program_id(0) == 0)
        def _():
            S[...] = jnp.zeros_like(S)

        xs, bs, cs, zs, nws = _ssd_args(xbc_ref, z_ref, nw_ref)
        st = [S[p] for p in range(4)]
        sin_ref[0] = S[...]
        ys, sn = _ssd_chunk_fn(xs, bs, cs, zs, g_ref[...], st, al_ref[...], db_ref[...], d_ref[...], nws)
        for p in range(4):
            y_ref[:, 128 * p:128 * p + 128] = ys[p].astype(BF16)
            S[p] = sn[p]

    return pl.pallas_call(
        body, name=name, grid=(Nc,),
        in_specs=[pl.BlockSpec((n, 1024), lambda i: (i, 0)), pl.BlockSpec((n, 512), lambda i: (i, P_SZ // 512)),
                  pl.BlockSpec((n, 128), lambda i: (i, P_GATE // 128)), _vec_spec(128), _vec_spec(128), _vec_spec(128),
                  _vec_spec(512)],
        out_specs=(pl.BlockSpec((n, 512), lambda i: (i, 0)), pl.BlockSpec((1, 4, 128, 128), lambda i: (i, 0, 0, 0))),
        out_shape=(jax.ShapeDtypeStruct((L, 512), BF16), jax.ShapeDtypeStruct((Nc, 4, 128, 128), F32)),
        scratch_shapes=[pltpu.VMEM((4, 128, 128), F32)],
        compiler_params=_cparams(("arbitrary",), VMEM_BIG))(xbcc, proj, proj, alog_row, dtb_row, d_row, nw)


def _ssd_bwd(xbcc, proj, alog_row, dtb_row, d_row, nw, s_in, dycat, name):
    L = xbcc.shape[0]
    n = SSM_CHUNK
    Nc = L // n

    def body(xbc_ref, z_ref, g_ref, al_ref, db_ref, d_ref, nw_ref, sin_ref, dy_ref,
             dxbc_ref, dz_ref, dg_ref, dal_ref, ddb_ref, dd_ref, dnw_ref, dS):
        @pl.when(pl.program_id(0) == 0)
        def _():
            dS[...] = jnp.zeros_like(dS)
            dal_ref[...] = jnp.zeros_like(dal_ref)
            ddb_ref[...] = jnp.zeros_like(ddb_ref)
            dd_ref[...] = jnp.zeros_like(dd_ref)
            dnw_ref[...] = jnp.zeros_like(dnw_ref)

        xs, bs, cs, zs, nws = _ssd_args(xbc_ref, z_ref, nw_ref)
        st = [sin_ref[0, p] for p in range(4)]
        _, vjp = jax.vjp(_ssd_chunk_fn, xs, bs, cs, zs, g_ref[...], st, al_ref[...], db_ref[...], d_ref[...], nws)
        dys = tuple(dy_ref[:, 128 * p:128 * p + 128] for p in range(4))
        dxs, dbs, dcs, dzs, dgt, dst, dal, ddb, dd, dnws = vjp((dys, tuple(dS[p] for p in range(4))))
        for p in range(4):
            dxbc_ref[:, 128 * p:128 * p + 128] = dxs[p]
            dz_ref[:, 128 * p:128 * p + 128] = dzs[p].astype(BF16)
            dS[p] = dst[p]
            dnw_ref[:, 128 * p:128 * p + 128] += dnws[p]
        for g in range(2):
            dxbc_ref[:, 512 + 128 * g:640 + 128 * g] = dbs[g]
            dxbc_ref[:, 768 + 128 * g:896 + 128 * g] = dcs[g]
        dg_ref[...] = dgt
        dal_ref[...] += dal
        ddb_ref[...] += ddb
        dd_ref[...] += dd

    rev = lambda i: Nc - 1 - i
    vec = jax.ShapeDtypeStruct((1, 128), F32)
    return pl.pallas_call(
        body, name=name, grid=(Nc,),
        in_specs=[pl.BlockSpec((n, 1024), lambda i: (rev(i), 0)), pl.BlockSpec((n, 512), lambda i: (rev(i), P_SZ // 512)),
                  pl.BlockSpec((n, 128), lambda i: (rev(i), P_GATE // 128)), _vec_spec(128), _vec_spec(128), _vec_spec(128),
                  _vec_spec(512), pl.BlockSpec((1, 4, 128, 128), lambda i: (rev(i), 0, 0, 0)),
                  pl.BlockSpec((n, 512), lambda i: (rev(i), 1))],
        out_specs=(pl.BlockSpec((n, 1024), lambda i: (rev(i), 0)), pl.BlockSpec((n, 512), lambda i: (rev(i), 0)),
                   pl.BlockSpec((n, 128), lambda i: (rev(i), 0)), _vec_spec(128), _vec_spec(128), _vec_spec(128),
                   _vec_spec(512)),
        out_shape=(jax.ShapeDtypeStruct((L, 1024), F32), jax.ShapeDtypeStruct((L, 512), BF16),
                   jax.ShapeDtypeStruct((L, 128), F32), vec, vec, vec, jax.ShapeDtypeStruct((1, 512), F32)),
        scratch_shapes=[pltpu.VMEM((4, 128, 128), F32)],
        compiler_params=_cparams(("arbitrary",), VMEM_BIG))(xbcc, proj, proj, alog_row, dtb_row, d_row, nw, s_in, dycat)


def _rot(x):
    W = x.shape[1]
    lane = lax.broadcasted_iota(jnp.int32, (1, W), 1)
    first = (lane % 64) < 32
    return jnp.where(first, -pltpu.roll(x, W - 32, 1), pltpu.roll(x, 32, 1))


def _rope(x, cos, sin):
    return x * cos + _rot(x) * sin


def _rope_t(d, cos, sin):
    return d * cos - _rot(d * sin)


def _swa_core_fn(qs, ks, vs, sink_row, mask):
    lane = lax.broadcasted_iota(jnp.int32, (1, 128), 1)
    lo = lane < 64
    H = range(8)
    qm = [jnp.where(lo if h % 2 == 0 else jnp.logical_not(lo), qs[h // 2], 0.0) for h in H]
    s = [jnp.where(mask, _bdot(qm[h], ks[h // 4], NT) * 0.125, NEG) for h in H]
    sk = [_lane_pick(sink_row, h) for h in H]
    mx = [lax.stop_gradient(jnp.maximum(jnp.max(s[h], axis=-1, keepdims=True), sk[h])) for h in H]
    pr = [jnp.exp(s[h] - mx[h]) for h in H]
    den = [jnp.sum(pr[h], axis=-1, keepdims=True) + jnp.exp(sk[h] - mx[h]) for h in H]
    pr = [pr[h] / den[h] for h in H]
    ov = [_bdot(pr[h], vs[h // 4]) for h in H]
    return tuple(jnp.where(lo, ov[2 * p], ov[2 * p + 1]) for p in range(4))


def _swa_prepare(blk, q_ref, kc_ref, kp_ref, vc_ref, vp_ref, cc_ref, sc_ref, cp_ref, sp_ref):
    W = WINDOW
    cos_c, sin_c = cc_ref[...], sc_ref[...]
    cos_b = jnp.concatenate([cp_ref[...], cos_c], axis=0)
    sin_b = jnp.concatenate([sp_ref[...], sin_c], axis=0)
    cos_q, sin_q = jnp.concatenate([cos_c] * 4, axis=1), jnp.concatenate([sin_c] * 4, axis=1)
    cos_k, sin_k = jnp.concatenate([cos_b] * 2, axis=1), jnp.concatenate([sin_b] * 2, axis=1)
    qr = _rope(q_ref[...], cos_q, sin_q)
    kr = _rope(jnp.concatenate([kp_ref[...], kc_ref[...]], axis=0), cos_k, sin_k)
    vb = jnp.concatenate([vp_ref[...], vc_ref[...]], axis=0)
    qi = lax.broadcasted_iota(jnp.int32, (W, 2 * W), 0)
    kj = lax.broadcasted_iota(jnp.int32, (W, 2 * W), 1)
    rel = qi + W - kj
    mask = (rel >= 0) & (rel < W) & ((blk > 0) | (kj >= W))
    qs = [qr[:, 128 * p:128 * p + 128] for p in range(4)]
    ks = [kr[:, 128 * g:128 * g + 128] for g in range(2)]
    vs = [vb[:, 128 * g:128 * g + 128] for g in range(2)]
    return qs, ks, vs, mask, (cos_q, sin_q, cos_k, sin_k)


def _swa_specs(nb, order):
    W = WINDOW
    cur = lambda i: order(i)
    prv = lambda i: jnp.maximum(order(i) - 1, 0)
    return [pl.BlockSpec((W, 512), lambda i: (cur(i), P_AQ // 512)),
            pl.BlockSpec((W, 256), lambda i: (cur(i), P_AK // 256)), pl.BlockSpec((W, 256), lambda i: (prv(i), P_AK // 256)),
            pl.BlockSpec((W, 256), lambda i: (cur(i), P_AV // 256)), pl.BlockSpec((W, 256), lambda i: (prv(i), P_AV // 256)),
            pl.BlockSpec((W, 128), lambda i: (cur(i), 0)), pl.BlockSpec((W, 128), lambda i: (cur(i), 0)),
            pl.BlockSpec((W, 128), lambda i: (prv(i), 0)), pl.BlockSpec((W, 128), lambda i: (prv(i), 0)),
            _vec_spec(128)]


def _swa_fwd(proj, cos, sin, sink_row, name):
    L = proj.shape[0]
    W = WINDOW
    nb = L // W

    def body(q_ref, kc_ref, kp_ref, vc_ref, vp_ref, cc_ref, sc_ref, cp_ref, sp_ref, sk_ref, y_ref):
        blk = pl.program_id(0)
        qs, ks, vs, mask, _ = _swa_prepare(blk, q_ref, kc_ref, kp_ref, vc_ref, vp_ref, cc_ref, sc_ref, cp_ref, sp_ref)
        outs = _swa_core_fn(qs, ks, vs, sk_ref[...], mask)
        for p in range(4):
            y_ref[:, 128 * p:128 * p + 128] = outs[p].astype(BF16)

    return pl.pallas_call(
        body, name=name, grid=(nb,), in_specs=_swa_specs(nb, lambda i: i),
        out_specs=pl.BlockSpec((W, 512), lambda i: (i, 0)), out_shape=jax.ShapeDtypeStruct((L, 512), BF16),
        compiler_params=_cparams(("parallel",), VMEM_BIG))(proj, proj, proj, proj, proj, cos, sin, cos, sin, sink_row)


def _swa_bwd(proj, cos, sin, sink_row, dycat, name):
    L = proj.shape[0]
    W = WINDOW
    nb = L // W
    rev = lambda i: nb - 1 - i

    def body(q_ref, kc_ref, kp_ref, vc_ref, vp_ref, cc_ref, sc_ref, cp_ref, sp_ref, sk_ref, dy_ref,
             dq_ref, dk_ref, dv_ref, dsk_ref, ck, cv):
        i = pl.program_id(0)
        blk = nb - 1 - i

        @pl.when(i == 0)
        def _():
            ck[...] = jnp.zeros_like(ck)
            cv[...] = jnp.zeros_like(cv)
            dsk_ref[...] = jnp.zeros_like(dsk_ref)

        qs, ks, vs, mask, (cos_q, sin_q, cos_k, sin_k) = _swa_prepare(
            blk, q_ref, kc_ref, kp_ref, vc_ref, vp_ref, cc_ref, sc_ref, cp_ref, sp_ref)
        _, vjp = jax.vjp(functools.partial(_swa_core_fn, mask=mask), qs, ks, vs, sk_ref[...])
        dqs, dks, dvs, dsk = vjp(tuple(dy_ref[:, 128 * p:128 * p + 128] for p in range(4)))
        dq_ref[...] = _rope_t(jnp.concatenate(dqs, axis=1), cos_q, sin_q).astype(BF16)
        dkb = _rope_t(jnp.concatenate(dks, axis=1), cos_k, sin_k)
        dvb = jnp.concatenate(dvs, axis=1)
        dk_ref[...] = (dkb[W:2 * W] + ck[...]).astype(BF16)
        dv_ref[...] = (dvb[W:2 * W] + cv[...]).astype(BF16)
        ck[...] = dkb[0:W]
        cv[...] = dvb[0:W]
        dsk_ref[...] += dsk

    return pl.pallas_call(
        body, name=name, grid=(nb,),
        in_specs=_swa_specs(nb, rev) + [pl.BlockSpec((W, 512), lambda i: (rev(i), 2))],
        out_specs=(pl.BlockSpec((W, 512), lambda i: (rev(i), 0)), pl.BlockSpec((W, 256), lambda i: (rev(i), 0)),
                   pl.BlockSpec((W, 256), lambda i: (rev(i), 0)), _vec_spec(128)),
        out_shape=(jax.ShapeDtypeStruct((L, 512), BF16), jax.ShapeDtypeStruct((L, 256), BF16),
                   jax.ShapeDtypeStruct((L, 256), BF16), jax.ShapeDtypeStruct((1, 128), F32)),
        scratch_shapes=[pltpu.VMEM((W, 256), F32), pltpu.VMEM((W, 256), F32)],
        compiler_params=_cparams(("arbitrary",), VMEM_BIG))(
            proj, proj, proj, proj, proj, cos, sin, cos, sin, sink_row, dycat)


def _adamw(w, g, m, v):
    m = ADAM_B1 * m + (1.0 - ADAM_B1) * g
    v = ADAM_B2 * v + (1.0 - ADAM_B2) * (g * g)
    m_hat = m / (1.0 - ADAM_B1 ** ADAM_STEP)
    v_hat = v / (1.0 - ADAM_B2 ** ADAM_STEP)
    delta = -ADAM_LR * (m_hat / (jnp.sqrt(v_hat) + ADAM_EPS) + ADAM_WD * w)
    return delta, m, v


def _ada_fwd(c_all, ada_w, ada_b_sh, name):
    S = ada_w.shape[2]

    def body(c_ref, w_ref, b_ref, o_ref):
        o_ref[0] = _bdot(_silu(c_ref[...]), w_ref[0]) + b_ref[0]

    return pl.pallas_call(
        body, name=name, grid=(2,),
        in_specs=[pl.BlockSpec((N_DEV, D), lambda i: (0, 0)), pl.BlockSpec((1, D, S), lambda i: (i, 0, 0)),
                  pl.BlockSpec((1, 1, S), lambda i: (i, 0, 0))],
        out_specs=pl.BlockSpec((1, N_DEV, S), lambda i: (i, 0, 0)), out_shape=jax.ShapeDtypeStruct((2, N_DEV, S), F32),
        compiler_params=_cparams(("parallel",), VMEM_BIG))(c_all, ada_w, ada_b_sh.reshape(2, 1, S))


def _ada_bwd_adamw(c_all, dmod_sh, w, m, v, name):
    S = w.shape[2]
    T = 256

    def body(c_ref, d_ref, w_ref, m_ref, v_ref, g_ref, dl_ref, nm_ref, nv_ref):
        g = _bdot(_silu(c_ref[...]), d_ref[0], TN)
        dl, nm, nv = _adamw(w_ref[0], g, m_ref[0], v_ref[0])
        g_ref[0], dl_ref[0], nm_ref[0], nv_ref[0] = g, dl, nm, nv

    blk = pl.BlockSpec((1, T, S), lambda i, j: (i, j, 0))
    sds = jax.ShapeDtypeStruct(w.shape, F32)
    return pl.pallas_call(
        body, name=name, grid=(2, D // T),
        in_specs=[pl.BlockSpec((N_DEV, T), lambda i, j: (0, j)), pl.BlockSpec((1, N_DEV, S), lambda i, j: (i, 0, 0)),
                  blk, blk, blk],
        out_specs=(blk, blk, blk, blk), out_shape=(sds, sds, sds, sds),
        compiler_params=_cparams(("parallel", "parallel"), VMEM_BIG))(c_all, dmod_sh, w, m, v)


def _sum8(r, name):
    _, R, W = r.shape
    T = R
    for cand in (2496, 2048, 1024, 512, 256, 128, 64, 32, 16, 8):
        if R % cand == 0:
            T = cand
            break

    def body(r_ref, o_ref):
        acc = r_ref[0].astype(F32)
        for d in range(1, N_DEV):
            acc = acc + r_ref[d].astype(F32)
        o_ref[...] = acc

    return pl.pallas_call(
        body, name=name, grid=(R // T,), in_specs=[pl.BlockSpec((N_DEV, T, W), lambda i: (0, i, 0))],
        out_specs=pl.BlockSpec((T, W), lambda i: (i, 0)), out_shape=jax.ShapeDtypeStruct((R, W), F32),
        compiler_params=_cparams(("parallel",), VMEM_BIG))(r)


def _adamw_call(w, g, m, v, name):
    R, W = w.shape
    T = R
    for cand in (1024, 512, 256, 128, 64, 32, 16, 8):
        if R % cand == 0 and R > cand and cand * W * 4 <= 2 * 1024 * 1024:
            T = cand
            break

    def body(w_ref, g_ref, m_ref, v_ref, dl_ref, nm_ref, nv_ref):
        dl_ref[...], nm_ref[...], nv_ref[...] = _adamw(w_ref[...], g_ref[...], m_ref[...], v_ref[...])

    blk = pl.BlockSpec((T, W), lambda i: (i, 0))
    sds = jax.ShapeDtypeStruct((R, W), F32)
    return pl.pallas_call(
        body, name=name, grid=(R // T,), in_specs=[blk, blk, blk, blk], out_specs=(blk, blk, blk),
        out_shape=(sds, sds, sds), compiler_params=_cparams(("parallel",), VMEM_BIG))(w, g, m, v)


def _sum8_adamw(r0, r1, w, m, v, name):
    _, _, R, W = r0.shape
    T = (256 * 1024) // W
    assert R % T == 0, (R, W, T)
    nj = R // T

    def body(r0_ref, r1_ref, w_ref, m_ref, v_ref, g_ref, dl_ref, nm_ref, nv_ref):
        def run(r_ref):
            g = r_ref[0].astype(F32)
            for d in range(1, N_DEV):
                g = g + r_ref[d].astype(F32)
            g_ref[...] = g
            dl_ref[...], nm_ref[...], nv_ref[...] = _adamw(w_ref[...], g, m_ref[...], v_ref[...])

        pl.when(pl.program_id(0) == 0)(lambda: run(r0_ref))
        pl.when(pl.program_id(0) == 1)(lambda: run(r1_ref))

    r0_spec = pl.BlockSpec((N_DEV, None, T, W), lambda i, j: (0, 0, jnp.where(i == 0, j, nj - 1), 0))
    r1_spec = pl.BlockSpec((N_DEV, None, T, W), lambda i, j: (0, 0, jnp.where(i == 1, j, 0), 0))
    blk = pl.BlockSpec((None, T, W), lambda i, j: (i, j, 0))
    sds = jax.ShapeDtypeStruct((2, R, W), F32)
    return pl.pallas_call(
        body, name=name, grid=(2, nj), in_specs=[r0_spec, r1_spec, blk, blk, blk],
        out_specs=(blk, blk, blk, blk), out_shape=(sds, sds, sds, sds),
        compiler_params=_cparams(("arbitrary", "arbitrary"), VMEM_BIG))(r0, r1, w, m, v)


def _permute_rows(wt):
    z = lambda n: jnp.zeros((n, wt.shape[1]), wt.dtype)
    s = lambda o, n: wt[o:o + n]
    kd = [s(O_AK, 64), s(O_AK, 64), s(O_AK + 64, 64), s(O_AK + 64, 64)]
    vd = [s(O_AV, 64), s(O_AV, 64), s(O_AV + 64, 64), s(O_AV + 64, 64)]
    return jnp.concatenate(
        [s(O_GQ, 1536), s(O_GZ, 512), s(O_SXBC, 1024), s(O_SZ, 512), s(O_AQ, 512), s(O_CB, 1536)] + kd + vd
        + [s(O_GB, 4), s(O_GA, 4), s(O_SDT, 8), z(112)], axis=0)


def _unpermute_rows(g):
    s = lambda o, n: g[o:o + n]
    add = lambda o: (s(o, 64).astype(F32) + s(o + 64, 64).astype(F32)).astype(g.dtype)
    dk = [add(P_AK), add(P_AK + 128)]
    dv = [add(P_AV), add(P_AV + 128)]
    return jnp.concatenate(
        [s(P_QKV, 1536), s(P_GZ, 512), s(P_GATE, 8), s(P_SZ, 512), s(P_SXBC, 1024), s(P_GATE + 8, 8), s(P_AQ, 512)]
        + dk + dv + [s(P_CB, 1536)], axis=0)


def _row128(vals, lane0):
    n = vals.shape[0]
    return jnp.concatenate([jnp.zeros((lane0,), F32), vals, jnp.zeros((128 - lane0 - n,), F32)]).reshape(1, 128)


def _pad_rows128(flat):
    n = flat.shape[0]
    pad = (-n) % 1024
    return jnp.concatenate([flat, jnp.zeros((pad,), flat.dtype)]).reshape(-1, 128)


def kernel(x, c, positions, ada_w, ada_b, norm_pre_mix, norm_post_mix, norm_pre_mlp, norm_post_mlp, w_in, w_out, gdn_conv_w, gdn_a_log, gdn_dt_bias, gdn_norm_w, ssm_conv_w, ssm_conv_b, ssm_a_log, ssm_dt_bias, ssm_d, ssm_norm_w, attn_sinks, sc_conv_w, w_up, w_down, loss_target, m_ada_w, m_ada_b, m_norm_pre_mix, m_norm_post_mix, m_norm_pre_mlp, m_norm_post_mlp, m_w_in, m_w_out, m_gdn_conv_w, m_gdn_a_log, m_gdn_dt_bias, m_gdn_norm_w, m_ssm_conv_w, m_ssm_conv_b, m_ssm_a_log, m_ssm_dt_bias, m_ssm_d, m_ssm_norm_w, m_attn_sinks, m_sc_conv_w, m_w_up, m_w_down, v_ada_w, v_ada_b, v_norm_pre_mix, v_norm_post_mix, v_norm_pre_mlp, v_norm_post_mlp, v_w_in, v_w_out, v_gdn_conv_w, v_gdn_a_log, v_gdn_dt_bias, v_gdn_norm_w, v_ssm_conv_w, v_ssm_conv_b, v_ssm_a_log, v_ssm_dt_bias, v_ssm_d, v_ssm_norm_w, v_attn_sinks, v_sc_conv_w, v_w_up, v_w_down):
    L = x.shape[1]
    me = 4 * lax.axis_index("x") + 2 * lax.axis_index("y") + lax.axis_index("c")
    x0 = x[0]
    target = loss_target[0]
    S_ADA = ada_w.shape[2]

    small = jnp.concatenate([c.reshape(-1), gdn_conv_w.reshape(-1), ssm_conv_w.reshape(-1), sc_conv_w.reshape(-1)])
    n_small = small.shape[0]
    g_small = _ag_small(_pad_rows128(small), "ag_c_conv").reshape(N_DEV, -1)[:, :n_small]
    c_all = g_small[:, :D]
    o = D
    gdn_cw = g_small[:, o:o + 2 * 4 * 192].reshape(N_DEV, 2, 4, 192).transpose(1, 2, 0, 3).reshape(2, 4, 1536)
    o += 2 * 4 * 192
    ssm_cw = g_small[:, o:o + 2 * 4 * 128].reshape(N_DEV, 2, 4, 128).transpose(1, 2, 0, 3).reshape(2, 4, 1024)
    o += 2 * 4 * 128
    sc_cw = g_small[:, o:o + 2 * 3 * 64].reshape(N_DEV, 2, 3, 64).transpose(1, 2, 0, 3).reshape(2, 3, 512)

    ada_b_sh = lax.dynamic_slice(ada_b, (0, me * S_ADA), (2, S_ADA))
    mod_sh = _ada_fwd(c_all, ada_w, ada_b_sh, "ada_fwd")
    mod_all = _ag_small(mod_sh.reshape(-1, 128), "ag_mod").reshape(N_DEV, 2, N_DEV, S_ADA)
    mod_me = lax.dynamic_index_in_dim(mod_all, me, axis=2, keepdims=False)
    mod_me = mod_me.transpose(1, 0, 2).reshape(2, 6 * D)

    wt_sh = [jnp.pad(w_in[i].T.astype(BF16), ((0, SHARD_PAD - SHARD_IN), (0, 0))) for i in range(2)]
    wo_sh, wu_sh, wd_sh = ([w[i].astype(BF16) for i in range(2)] for w in (w_out, w_up, w_down))
    full_w_in = lambda g: _permute_rows(g[:, :SHARD_IN].reshape(IN_W, D))
    win_t = [full_w_in(_ag_multi([wt_sh[0]], "ag_w_in0")[0]), None]
    wout_f, wup_g, wdown_f = [None, None], [None, None], [None, None]

    inv_freq = ROPE_THETA ** (-jnp.arange(0, 64, 2, dtype=F32) / 64)
    ang = positions[0].astype(F32)[:, None] * inv_freq
    cos = jnp.tile(jnp.cos(ang), (1, 4))
    sin = jnp.tile(jnp.sin(ang), (1, 4))

    vec = lambda a: a.reshape(1, -1)

    mods = [[vec(t) for t in jnp.split(mod_me[i], 6)] for i in range(2)]
    saved = []
    xc = x0
    h = _pre_fwd(x0, vec(norm_pre_mix[0]), mods[0][1], mods[0][0], "pre_mix_fwd0")
    for i in range(2):
        sh_a, sc_a, gt_a, sh_m, sc_m, gt_m = mods[i]
        gal, gdb = _row128(gdn_a_log[i], LANE_GA), _row128(gdn_dt_bias[i], LANE_GA)
        sal, sdb, sdd = (_row128(ssm_a_log[i], LANE_SDT), _row128(ssm_dt_bias[i], LANE_SDT), _row128(ssm_d[i], LANE_SDT))
        gnw, snw, skr = vec(gdn_norm_w[i]), vec(ssm_norm_w[i]), _row128(attn_sinks[i], 0)
        zero_b = jnp.zeros((1, 1536), F32)

        riders = [wo_sh[0], wu_sh[0]] if i == 0 else [wd_sh[1]]
        proj, *got = _mm(h, win_t[i], name=f"in_proj{i}", tb=True, tn=896, tk=2048, cargo=_AgCargo(riders))
        if i == 0:
            wout_f[0], wup_g[0] = got[0].reshape(D, D), got[1]
        else:
            wdown_f[1] = got[0].reshape(MLP, D)
        qkvc = _conv_fwd(proj, 0, 1536, gdn_cw[i], zero_b, f"gdn_conv_fwd{i}")
        if i == 0:
            ya, gdn_s, got_i = _gdn_fwd(qkvc, proj, gal, gdb, gnw, "gdn_fwd0", cargo=_AgCargo([wt_sh[1]]))
            win_t[1] = full_w_in(got_i)
        else:
            ya, gdn_s = _gdn_fwd(qkvc, proj, gal, gdb, gnw, "gdn_fwd1")
        xbcc = _conv_fwd(proj, P_SXBC // 1024, 1024, ssm_cw[i], vec(ssm_conv_b[i]), f"ssm_conv_fwd{i}")
        yb, ssd_s = _ssd_fwd(xbcc, proj, sal, sdb, sdd, snw, f"ssd_fwd{i}")
        yc = _swa_fwd(proj, cos, sin, skr, f"swa_fwd{i}")
        yd = _sc_fwd(proj, sc_cw[i], f"sc_fwd{i}")
        ycat = jnp.concatenate([ya, yb, yc, yd], axis=1)
        out = _mm(ycat, wout_f[i], name=f"out_proj{i}", tk=2048)
        x1, h2 = _post_pre_fwd(xc, out, vec(norm_post_mix[i]), gt_a, vec(norm_pre_mlp[i]), sc_m, sh_m, f"post_mix_pre_mlp_fwd{i}")
        if i == 0:
            act, u, got_d, got_o = _mm(h2, wup_g[0], name="up_proj0", tk=2048, mode="relu2", shards="b",
                                       cargo=_AgCargo([wd_sh[0], wo_sh[1]]))
            wdown_f[0], wout_f[1] = got_d.reshape(MLP, D), got_o.reshape(D, D)
            y2, wup_g[1] = _mm(act, wdown_f[0], name="down_proj0", tk=2048, cargo=_AgCargo([wu_sh[1]]))
        else:
            act, u = _mm(h2, wup_g[1], name="up_proj1", tk=2048, mode="relu2", shards="b")
            y2 = _mm(act, wdown_f[1], name="down_proj1", tk=2048)
        saved.append(dict(x0=xc, h=h, proj=proj, qkvc=qkvc, gdn_s=gdn_s, xbcc=xbcc, ssd_s=ssd_s, ycat=ycat, out=out,
                          x1=x1, h2=h2, act=act, u=u, y2=y2, rows=(gal, gdb, sal, sdb, sdd, gnw, snw, skr)))
        if i == 0:
            xc, h = _post_pre_fwd(x1, y2, vec(norm_post_mlp[0]), gt_m, vec(norm_pre_mix[1]), mods[1][1], mods[1][0],
                                  "post_mlp_pre_mix_fwd")

    sq, dx, *pend = _last_residual_loss(saved[1]["x1"], saved[1]["y2"], vec(norm_post_mlp[1]), mods[1][5], target,
                                        "post_mlp_loss")
    loss = lax.psum(0.5 * sq[0, 0] / float(D), ("x", "y", "c"))

    gw = [dict() for _ in range(2)]
    r_in, r_out, r_up, r_down = [None, None], [None, None], [None, None], [None, None]
    for i in (1, 0):
        sv = saved[i]
        sh_a, sc_a, gt_a, sh_m, sc_m, gt_m = mods[i]
        gal, gdb, sal, sdb, sdd, gnw, snw, skr = sv["rows"]
        proj = sv["proj"]
        dy2, g_npostmlp, d_gt_m = pend
        du = _mm(dy2, wdown_f[i], name=f"down_proj_dx{i}", tb=True, tk=2048, out_dtype=BF16, mode="drelu2", u=sv["u"])
        g_wdown = _mm(sv["act"], dy2, name=f"down_proj_dw{i}", ta=True, tk=2048,
                      out_dtype=BF16).reshape(N_DEV, MLP // N_DEV, D)
        dh2, r_down[i] = _mm(du, wup_g[i], name=f"up_proj_dx{i}", tb=True, tk=1024, out_dtype=BF16, shards="b",
                             cargo=_ExCargo([[g_wdown]]))
        g_wup = _mm(sv["h2"], du, name=f"up_proj_dw{i}", ta=True, tk=2048, out_dtype=BF16, shards="o")
        dx1, dout, g_npremlp, d_sc_m, d_sh_m, g_npostmix, d_gt_a = _pre_post_bwd(
            sv["x1"], vec(norm_pre_mlp[i]), sc_m, sh_m, dh2, dx, sv["out"], vec(norm_post_mix[i]), gt_a,
            f"pre_mlp_post_mix_bwd{i}")
        dycat = _mm(dout, wout_f[i], name=f"out_proj_dx{i}", tb=True, tk=2048)
        g_wout = _mm(sv["ycat"], dout, name=f"out_proj_dw{i}", ta=True, tk=2048,
                     out_dtype=BF16).reshape(N_DEV, D // N_DEV, D)

        dqkvc, dgz, dgate_g, d_gal, d_gdb, d_gnw, r_up[i] = _gdn_bwd(
            sv["qkvc"], proj, gal, gdb, gnw, sv["gdn_s"], dycat, f"gdn_bwd{i}", cargo=_ExCargo([[g_wup]]))
        dqkv, g_gdn_cw, _ = _conv_bwd(dqkvc, proj, 0, 1536, gdn_cw[i], f"gdn_conv_bwd{i}")
        dxbcc, dsz, dgate_s, d_sal, d_sdb, d_sdd, d_snw = _ssd_bwd(
            sv["xbcc"], proj, sal, sdb, sdd, snw, sv["ssd_s"], dycat, f"ssd_bwd{i}")
        dsxbc, g_ssm_cw, g_ssm_cb = _conv_bwd(dxbcc, proj, P_SXBC // 1024, 1024, ssm_cw[i], f"ssm_conv_bwd{i}")
        daq, dak, dav, d_skr = _swa_bwd(proj, cos, sin, skr, dycat, f"swa_bwd{i}")
        dsc, g_sc_cw = _sc_bwd(dycat, proj, sc_cw[i], f"sc_bwd{i}")
        dgate = (dgate_g + dgate_s).astype(BF16)
        dproj = jnp.concatenate([dqkv, dgz, dsxbc, dsz, daq, dsc, dak, dav, dgate], axis=1)
        g_wint, r_out[i] = _mm(dproj, sv["h"], name=f"in_proj_dw{i}", ta=True, tm=896, tk=2048, out_dtype=BF16,
                               cargo=_ExCargo([[g_wout]]))
        g_wint = jnp.pad(_unpermute_rows(g_wint).reshape(N_DEV, SHARD_IN, D), ((0, 0), (0, SHARD_PAD - SHARD_IN), (0, 0)))
        dh, r_in[i] = _mm(dproj, win_t[i], name=f"in_proj_dx{i}", tk=896, out_dtype=BF16, cargo=_ExCargo([[g_wint]]))
        if i == 1:
            dx, dy2_0, g_npremix, d_sc_a, d_sh_a, *post0 = _pre_post_bwd(
                sv["x0"], vec(norm_pre_mix[1]), sc_a, sh_a, dh, dx1, saved[0]["y2"], vec(norm_post_mlp[0]), mods[0][5],
                "pre_mix_post_mlp_bwd")
            pend = (dy2_0, *post0)
        else:
            dx, g_npremix, d_sc_a, d_sh_a = _pre_bwd(sv["x0"], vec(norm_pre_mix[0]), sc_a, sh_a, dh, dx1, "pre_mix_bwd0")

        gw[i] = dict(
            dmod=jnp.concatenate([d_sh_a, d_sc_a, d_gt_a, d_sh_m, d_sc_m, d_gt_m], axis=1).reshape(-1),
            norm_pre_mix=g_npremix.reshape(-1), norm_post_mix=g_npostmix.reshape(-1),
            norm_pre_mlp=g_npremlp.reshape(-1), norm_post_mlp=g_npostmlp.reshape(-1),
            gdn_conv_w=g_gdn_cw.reshape(-1), gdn_a_log=d_gal[0, LANE_GA:LANE_GA + 4], gdn_dt_bias=d_gdb[0, LANE_GA:LANE_GA + 4],
            gdn_norm_w=d_gnw.reshape(-1), ssm_conv_w=g_ssm_cw.reshape(-1), ssm_conv_b=g_ssm_cb.reshape(-1),
            ssm_a_log=d_sal[0, LANE_SDT:LANE_SDT + 8], ssm_dt_bias=d_sdb[0, LANE_SDT:LANE_SDT + 8],
            ssm_d=d_sdd[0, LANE_SDT:LANE_SDT + 8], ssm_norm_w=d_snw.reshape(-1), attn_sinks=d_skr[0, 0:8],
            sc_conv_w=g_sc_cw.reshape(-1))
    grad_x = dx[None]

    grads, deltas, new_m, new_v = {}, {}, {}, {}
    for n_, r, w, m, v in (("w_out", r_out, w_out, m_w_out, v_w_out), ("w_up", r_up, w_up, m_w_up, v_w_up),
                           ("w_down", r_down, w_down, m_w_down, v_w_down)):
        grads[n_], deltas[n_], new_m[n_], new_v[n_] = _sum8_adamw(r[0], r[1], w, m, v, f"sum_adamw_{n_}")
    g_int = jnp.stack([_sum8(r_in[i].reshape(N_DEV, SHARD_PAD, D), f"grad_sum_w_in{i}") for i in range(2)])
    g_in = g_int[:, :SHARD_IN].transpose(0, 2, 1)
    dl, nm, nv = _adamw_call(w_in.reshape(-1, SHARD_IN), g_in.reshape(-1, SHARD_IN), m_w_in.reshape(-1, SHARD_IN),
                             v_w_in.reshape(-1, SHARD_IN), "adamw_w_in")
    grads["w_in"], deltas["w_in"] = g_in, dl.reshape(w_in.shape)
    new_m["w_in"], new_v["w_in"] = nm.reshape(w_in.shape), nv.reshape(w_in.shape)

    small_names = ["dmod", "norm_pre_mix", "norm_post_mix", "norm_pre_mlp", "norm_post_mlp", "gdn_conv_w", "gdn_a_log",
                   "gdn_dt_bias", "gdn_norm_w", "ssm_conv_w", "ssm_conv_b", "ssm_a_log", "ssm_dt_bias", "ssm_d",
                   "ssm_norm_w", "attn_sinks", "sc_conv_w"]
    flat = jnp.concatenate([gw[i][nm_] for nm_ in small_names for i in range(2)])
    n_flat = flat.shape[0]
    g_all = _ag_small(_pad_rows128(flat), "ag_small_grads")
    tot = _sum8(g_all, "small_grad_sum").reshape(-1)[:n_flat]
    dmod_all = g_all.reshape(N_DEV, -1)[:, :2 * 6 * D].reshape(N_DEV, 2, 6 * D)
    sm = {}
    o = 0
    for nm_ in small_names:
        n = gw[0][nm_].shape[0]
        sm[nm_] = jnp.stack([tot[o:o + n], tot[o + n:o + 2 * n]])
        o += 2 * n

    dmod_sh = lax.dynamic_slice(dmod_all.transpose(1, 0, 2), (0, 0, me * S_ADA), (2, N_DEV, S_ADA))
    grads["ada_w"], deltas["ada_w"], new_m["ada_w"], new_v["ada_w"] = _ada_bwd_adamw(
        c_all, dmod_sh, ada_w, m_ada_w, v_ada_w, "ada_bwd_adamw")

    def shard_cols(full, K, C):
        return lax.dynamic_slice(full.reshape(2, K, C), (0, 0, me * (C // N_DEV)), (2, K, C // N_DEV))

    small_g = dict(
        ada_b=sm["dmod"], norm_pre_mix=sm["norm_pre_mix"], norm_post_mix=sm["norm_post_mix"],
        norm_pre_mlp=sm["norm_pre_mlp"], norm_post_mlp=sm["norm_post_mlp"],
        gdn_conv_w=shard_cols(sm["gdn_conv_w"], 4, 1536), gdn_a_log=sm["gdn_a_log"], gdn_dt_bias=sm["gdn_dt_bias"],
        gdn_norm_w=sm["gdn_norm_w"], ssm_conv_w=shard_cols(sm["ssm_conv_w"], 4, 1024), ssm_conv_b=sm["ssm_conv_b"],
        ssm_a_log=sm["ssm_a_log"], ssm_dt_bias=sm["ssm_dt_bias"], ssm_d=sm["ssm_d"], ssm_norm_w=sm["ssm_norm_w"],
        attn_sinks=sm["attn_sinks"], sc_conv_w=shard_cols(sm["sc_conv_w"], 3, 512))
    small_w = dict(
        ada_b=(ada_b, m_ada_b, v_ada_b), norm_pre_mix=(norm_pre_mix, m_norm_pre_mix, v_norm_pre_mix),
        norm_post_mix=(norm_post_mix, m_norm_post_mix, v_norm_post_mix),
        norm_pre_mlp=(norm_pre_mlp, m_norm_pre_mlp, v_norm_pre_mlp),
        norm_post_mlp=(norm_post_mlp, m_norm_post_mlp, v_norm_post_mlp),
        gdn_conv_w=(gdn_conv_w, m_gdn_conv_w, v_gdn_conv_w), gdn_a_log=(gdn_a_log, m_gdn_a_log, v_gdn_a_log),
        gdn_dt_bias=(gdn_dt_bias, m_gdn_dt_bias, v_gdn_dt_bias), gdn_norm_w=(gdn_norm_w, m_gdn_norm_w, v_gdn_norm_w),
        ssm_conv_w=(ssm_conv_w, m_ssm_conv_w, v_ssm_conv_w), ssm_conv_b=(ssm_conv_b, m_ssm_conv_b, v_ssm_conv_b),
        ssm_a_log=(ssm_a_log, m_ssm_a_log, v_ssm_a_log), ssm_dt_bias=(ssm_dt_bias, m_ssm_dt_bias, v_ssm_dt_bias),
        ssm_d=(ssm_d, m_ssm_d, v_ssm_d), ssm_norm_w=(ssm_norm_w, m_ssm_norm_w, v_ssm_norm_w),
        attn_sinks=(attn_sinks, m_attn_sinks, v_attn_sinks), sc_conv_w=(sc_conv_w, m_sc_conv_w, v_sc_conv_w))
    names = list(small_w)
    sizes = [int(np.prod(small_w[n_][0].shape)) for n_ in names]
    pk = lambda j: _pad_rows128(jnp.concatenate([small_w[n_][j].reshape(-1) for n_ in names]))
    gk = _pad_rows128(jnp.concatenate([small_g[n_].reshape(-1) for n_ in names]))
    dl, nm, nv = _adamw_call(pk(0), gk, pk(1), pk(2), "adamw_small")
    dl, nm, nv = dl.reshape(-1), nm.reshape(-1), nv.reshape(-1)
    o = 0
    for n_, sz in zip(names, sizes):
        shp = small_w[n_][0].shape
        grads[n_] = small_g[n_].reshape(shp)
        deltas[n_], new_m[n_], new_v[n_] = dl[o:o + sz].reshape(shp), nm[o:o + sz].reshape(shp), nv[o:o + sz].reshape(shp)
        o += sz

    order = ["ada_w", "ada_b", "norm_pre_mix", "norm_post_mix", "norm_pre_mlp", "norm_post_mlp", "w_in", "w_out",
             "gdn_conv_w", "gdn_a_log", "gdn_dt_bias", "gdn_norm_w", "ssm_conv_w", "ssm_conv_b", "ssm_a_log",
             "ssm_dt_bias", "ssm_d", "ssm_norm_w", "attn_sinks", "sc_conv_w", "w_up", "w_down"]
    return (loss, grad_x, *[grads[n_] for n_ in order], *[deltas[n_] for n_ in order],
            *[new_m[n_] for n_ in order], *[new_v[n_] for n_ in order])
```

```python
import functools
import math

import jax
import jax.numpy as jnp
import numpy as np
from jax import lax
from jax.experimental import pallas as pl
from jax.experimental.pallas import tpu as pltpu

F32 = jnp.float32
BF16 = jnp.bfloat16
MESH = pl.DeviceIdType.MESH

N_DEV = 8
D = 2048
GW = 512
MLP = 8192
EPS = 1e-6
IN_W = 5904
SHARD_IN = IN_W // N_DEV
SHARD_PAD = 752
PW = 6272
ROPE_THETA = 10000.0
WINDOW = 128
GDN_CHUNK = 64
SSM_CHUNK = 128
NEG = -1e30

ADAM_LR, ADAM_B1, ADAM_B2, ADAM_EPS, ADAM_WD, ADAM_STEP = 0.001, 0.9, 0.999, 1e-08, 0.01, 10

O_GQ, O_GK, O_GV, O_GZ, O_GB, O_GA = 0, 512, 1024, 1536, 2048, 2052
O_SZ, O_SXBC, O_SDT = 2056, 2568, 3592
O_AQ, O_AK, O_AV = 3600, 4112, 4240
O_CB, O_CC, O_CH = 4368, 4880, 5392
P_QKV, P_GZ, P_SXBC, P_SZ, P_AQ, P_CB, P_AK, P_AV, P_GATE = 0, 1536, 2048, 3072, 3584, 4096, 5632, 5888, 6144
LANE_GB, LANE_GA, LANE_SDT = 0, 4, 8

VMEM_BIG = 56 * 1024 * 1024


def _cparams(sem=None, vmem=None):
    kw = {}
    if sem is not None:
        kw["dimension_semantics"] = sem
    if vmem is not None:
        kw["vmem_limit_bytes"] = vmem
    return pltpu.CompilerParams(**kw)


def _me():
    x, y, c = lax.axis_index("x"), lax.axis_index("y"), lax.axis_index("c")
    return x, y, c, 4 * x + 2 * y + c


def _peer(x, y, c, k):
    px = 1 - x if (k >> 2) & 1 else x
    py = 1 - y if (k >> 1) & 1 else y
    pc = 1 - c if k & 1 else c
    return (px, py, pc), 4 * px + 2 * py + pc


def _ag_small(v, name):
    R, W = v.shape

    def body(x_ref, out_ref, send_sems, recv_sems):
        x, y, c, me = _me()
        out_ref[me] = x_ref[...]
        copies = []
        for k in range(1, N_DEV):
            peer, _ = _peer(x, y, c, k)
            cp = pltpu.make_async_remote_copy(
                src_ref=x_ref, dst_ref=out_ref.at[me], send_sem=send_sems.at[k - 1], recv_sem=recv_sems.at[k - 1],
                device_id=peer, device_id_type=MESH)
            cp.start()
            copies.append(cp)
        for cp in copies:
            cp.wait()

    return pl.pallas_call(
        body, name=name,
        out_shape=jax.ShapeDtypeStruct((N_DEV, R, W), v.dtype),
        in_specs=[pl.BlockSpec(memory_space=pltpu.VMEM)],
        out_specs=pl.BlockSpec(memory_space=pltpu.VMEM),
        scratch_shapes=[pltpu.SemaphoreType.DMA((N_DEV - 1,)), pltpu.SemaphoreType.DMA((N_DEV - 1,))],
    )(v)


def _ag_multi(arrs, name):
    cargo = _AgCargo(arrs)
    A = cargo.n_in

    def body(*refs):
        start, mid, end = cargo.phases(refs[:A], refs[A:A + cargo.n_out], refs[A + cargo.n_out:])
        start()
        mid()
        end()

    hbm = pl.BlockSpec(memory_space=pl.ANY)
    return pl.pallas_call(
        body, name=name, out_shape=tuple(cargo.out_shapes), in_specs=[hbm] * A, out_specs=tuple([hbm] * cargo.n_out),
        scratch_shapes=cargo.scratch)(*cargo.arrays)


class _AgCargo:
    def __init__(self, arrs):
        A = len(arrs)
        self.arrays = list(arrs)
        self.n_in = self.n_out = A
        self.out_shapes = [jax.ShapeDtypeStruct((N_DEV,) + v.shape, v.dtype) for v in arrs]
        self.scratch = [pltpu.SemaphoreType.DMA((7 * A,)), pltpu.SemaphoreType.DMA((7 * A,)),
                        pltpu.SemaphoreType.DMA((A,))]

    def phases(self, x_refs, out_refs, sems):
        A = self.n_in
        send_sems, recv_sems, local_sems = sems

        def ctx():
            x, y, c, me = _me()
            return x, y, c, me, (x, y, 1 - c), [(1 - x, y), (x, 1 - y), (1 - x, 1 - y)]

        def copy(a, k, block, to, src=None):
            slot = out_refs[a].at[4 * block[0] + 2 * block[1] + block[2]]
            return pltpu.make_async_remote_copy(
                src_ref=slot if src is None else src, dst_ref=slot, send_sem=send_sems.at[7 * a + k],
                recv_sem=recv_sems.at[7 * a + k], device_id=to, device_id_type=MESH)

        def start():
            x, y, c, me, sibling, chips = ctx()
            for a in range(A):
                pltpu.make_async_copy(x_refs[a], out_refs[a].at[me], local_sems.at[a]).start()
            for a in range(A):
                copy(a, 0, (x, y, c), sibling, src=x_refs[a]).start()
                for j, chip in enumerate(chips):
                    copy(a, 1 + j, (x, y, c), (*chip, c), src=x_refs[a]).start()

        def mid():
            x, y, c, me, sibling, chips = ctx()
            for j, chip in enumerate(chips):
                for a in range(A):
                    copy(a, 1 + j, (*chip, c), (x, y, c)).wait_recv()
                    copy(a, 4 + j, (*chip, c), sibling).start()

        def end():
            x, y, c, me, sibling, chips = ctx()
            for a in range(A):
                copy(a, 0, sibling, (x, y, c)).wait_recv()
                for j, chip in enumerate(chips):
                    copy(a, 4 + j, (*chip, 1 - c), (x, y, c)).wait_recv()
            for a in range(A):
                copy(a, 0, (x, y, c), sibling, src=x_refs[a]).wait_send()
                for j, chip in enumerate(chips):
                    copy(a, 1 + j, (x, y, c), (*chip, c), src=x_refs[a]).wait_send()
                    copy(a, 4 + j, (*chip, c), sibling).wait_send()
                pltpu.make_async_copy(x_refs[a], out_refs[a].at[me], local_sems.at[a]).wait()

        return start, mid, end


class _ExCargo:
    def __init__(self, groups):
        self.flat = [(w, i) for w, layers in enumerate(groups) for i, _ in enumerate(layers)]
        self.arrays = [g for layers in groups for g in layers]
        A = len(self.arrays)
        self.n_in, self.n_out = A, len(groups)
        self.out_shapes = [jax.ShapeDtypeStruct((N_DEV, len(layers)) + layers[0].shape[1:], layers[0].dtype)
                           for layers in groups]
        self.scratch = [pltpu.SemaphoreType.DMA((7 * A,)), pltpu.SemaphoreType.DMA((7 * A,)),
                        pltpu.SemaphoreType.DMA((A,))]

    def phases(self, g_refs, out_refs, sems):
        send_sems, recv_sems, local_sems = sems

        def copies():
            x, y, c, me = _me()
            local = [pltpu.make_async_copy(g_refs[a].at[me], out_refs[w].at[me, i], local_sems.at[a])
                     for a, (w, i) in enumerate(self.flat)]
            remote = []
            for k in range(1, N_DEV):
                peer, pid = _peer(x, y, c, k)
                for a, (w, i) in enumerate(self.flat):
                    remote.append(pltpu.make_async_remote_copy(
                        src_ref=g_refs[a].at[pid], dst_ref=out_refs[w].at[me, i], send_sem=send_sems.at[7 * a + k - 1],
                        recv_sem=recv_sems.at[7 * a + k - 1], device_id=peer, device_id_type=MESH))
            return local, remote

        def start():
            local, remote = copies()
            for cp in local + remote:
                cp.start()

        def end():
            local, remote = copies()
            for cp in remote + local:
                cp.wait()

        return start, (lambda: None), end


def _mm(a, b, *, name, ta=False, tb=False, tm=1024, tn=1024, tk=1024, out_dtype=F32, mode="plain", u=None,
        shards=None, cargo=None):
    K, M = a.shape if ta else a.shape[::-1]
    if shards == "b":
        N, K2 = (b.shape[1], b.shape[0] * b.shape[2]) if tb else (b.shape[0] * b.shape[2], b.shape[1])
    else:
        N, K2 = b.shape if tb else b.shape[::-1]
    assert K == K2, (a.shape, b.shape)
    tm, tn, tk = min(tm, M), min(tn, N), min(tk, K)
    assert M % tm == 0 and N % tn == 0 and K % tk == 0, (M, N, K, tm, tn, tk)
    nk = K // tk
    a_spec = pl.BlockSpec((tk, tm), lambda i, j, k: (k, i)) if ta else pl.BlockSpec((tm, tk), lambda i, j, k: (i, k))
    if shards == "b" and tb:
        assert tk == b.shape[2]
        b_spec = pl.BlockSpec((None, tn, tk), lambda i, j, k: (k, j, 0))
    elif shards == "b":
        assert tn == b.shape[2]
        b_spec = pl.BlockSpec((None, tk, tn), lambda i, j, k: (j, k, 0))
    else:
        b_spec = pl.BlockSpec((tn, tk), lambda i, j, k: (j, k)) if tb else pl.BlockSpec((tk, tn), lambda i, j, k: (k, j))
    if shards == "o":
        o_spec = pl.BlockSpec((None, tm, tn), lambda i, j, k: (j, i, 0))
        o_shape = (N // tn, M, tn)
    else:
        o_spec = pl.BlockSpec((tm, tn), lambda i, j, k: (i, j))
        o_shape = (M, N)
    dn = (((0 if ta else 1,), (1 if tb else 0,)), ((), ()))
    ni_, nj = M // tm, N // tn
    nsteps = ni_ * nj * nk
    n_in = 3 if mode == "drelu2" else 2
    n_out = 2 if mode == "relu2" else 1
    c_arr, c_ispec, c_oshape, c_ospec = _cargo_io(cargo)
    ni, no = len(c_arr), len(c_oshape)

    def body(*refs):
        a_ref, b_ref = refs[0], refs[1]
        u_ref = refs[2] if mode == "drelu2" else None
        outs = refs[n_in + ni:n_in + ni + n_out]
        acc = refs[n_in + ni + n_out + no]
        k = pl.program_id(2)
        step = (pl.program_id(0) * nj + pl.program_id(1)) * nk + k
        drain = _carry(cargo, (refs[n_in:n_in + ni], refs[n_in + ni + n_out:n_in + ni + n_out + no],
                               refs[n_in + ni + n_out + no + 1:]), step, nsteps)

        @pl.when(k == 0)
        def _():
            acc[...] = jnp.zeros_like(acc)

        acc[...] += lax.dot_general(a_ref[...].astype(BF16), b_ref[...].astype(BF16), dn, preferred_element_type=F32)

        @pl.when(k == nk - 1)
        def _():
            r = acc[...]
            if mode == "plain":
                outs[0][...] = r.astype(out_dtype)
            elif mode == "relu2":
                rr = jnp.maximum(r, 0.0)
                outs[0][...] = (rr * rr).astype(BF16)
                outs[1][...] = r.astype(BF16)
            else:
                outs[0][...] = (r * (2.0 * jnp.maximum(u_ref[...].astype(F32), 0.0))).astype(out_dtype)

        drain()

    in_specs, args = [a_spec, b_spec], [a, b]
    if mode == "drelu2":
        in_specs.append(o_spec)
        args.append(u)
    if mode == "relu2":
        out_shape = [jax.ShapeDtypeStruct(o_shape, BF16), jax.ShapeDtypeStruct(o_shape, BF16)]
        out_specs = [o_spec, o_spec]
    else:
        out_shape = [jax.ShapeDtypeStruct(o_shape, out_dtype)]
        out_specs = [o_spec]
    res = pl.pallas_call(
        body, name=name, grid=(ni_, nj, nk), in_specs=in_specs + c_ispec, out_specs=tuple(out_specs + c_ospec),
        out_shape=tuple(out_shape + c_oshape),
        scratch_shapes=[pltpu.VMEM((tm, tn), F32)] + (cargo.scratch if cargo else []),
        compiler_params=_cparams(("arbitrary",) * 3 if cargo else ("parallel", "parallel", "arbitrary"), VMEM_BIG),
    )(*args, *c_arr)
    return res[0] if len(res) == 1 else res


ROW_T = 256


def _rms(x, w):
    return x * lax.rsqrt(jnp.mean(x * x, axis=-1, keepdims=True) + EPS) * w


def _pre_fn(x, w, scale, shift):
    return _rms(x, w) * (1.0 + scale) + shift


def _post_fn(y, w, gate):
    return gate * _rms(y, w)


def _row_spec(T, W):
    return pl.BlockSpec((T, W), lambda i: (i, 0))


def _vec_spec(W):
    return pl.BlockSpec((1, W), lambda i: (0, 0))


def _pre_fwd(x, w, scale, shift, name, cargo):
    L = x.shape[0]
    T = min(ROW_T, L)
    nt = L // T
    c_arr, c_ispec, c_oshape, c_ospec = _cargo_io(cargo)
    ni, no = len(c_arr), len(c_oshape)

    def body(*refs):
        x_ref, w_ref, sc_ref, sh_ref = refs[:4]
        h_ref = refs[4 + ni]
        drain = _carry(cargo, (refs[4:4 + ni], refs[5 + ni:5 + ni + no], refs[5 + ni + no:]), pl.program_id(0), nt)
        h_ref[...] = _pre_fn(x_ref[...], w_ref[...], sc_ref[...], sh_ref[...]).astype(BF16)
        drain()

    return pl.pallas_call(
        body, name=name, grid=(nt,),
        in_specs=[_row_spec(T, D), _vec_spec(D), _vec_spec(D), _vec_spec(D)] + c_ispec,
        out_specs=tuple([_row_spec(T, D)] + c_ospec),
        out_shape=tuple([jax.ShapeDtypeStruct((L, D), BF16)] + c_oshape), scratch_shapes=cargo.scratch,
        compiler_params=_cparams(("arbitrary",), VMEM_BIG))(x, w, scale, shift, *c_arr)


def _pre_bwd(x, w, scale, shift, dh, dres, name):
    L = x.shape[0]
    T = min(ROW_T, L)

    def body(x_ref, w_ref, sc_ref, sh_ref, dh_ref, dres_ref, dx_ref, dw_ref, dsc_ref, dsh_ref):
        @pl.when(pl.program_id(0) == 0)
        def _():
            dw_ref[...] = jnp.zeros_like(dw_ref)
            dsc_ref[...] = jnp.zeros_like(dsc_ref)
            dsh_ref[...] = jnp.zeros_like(dsh_ref)

        _, vjp = jax.vjp(_pre_fn, x_ref[...], w_ref[...], sc_ref[...], sh_ref[...])
        dx, dw, dsc, dsh = vjp(dh_ref[...].astype(F32))
        dx_ref[...] = dres_ref[...] + dx
        dw_ref[...] += dw
        dsc_ref[...] += dsc
        dsh_ref[...] += dsh

    vec = jax.ShapeDtypeStruct((1, D), F32)
    return pl.pallas_call(
        body, name=name, grid=(L // T,),
        in_specs=[_row_spec(T, D), _vec_spec(D), _vec_spec(D), _vec_spec(D), _row_spec(T, D), _row_spec(T, D)],
        out_specs=(_row_spec(T, D), _vec_spec(D), _vec_spec(D), _vec_spec(D)),
        out_shape=(jax.ShapeDtypeStruct((L, D), F32), vec, vec, vec),
        compiler_params=_cparams(("arbitrary",), VMEM_BIG))(x, w, scale, shift, dh, dres)


def _post_pre_fwd(x, y, w_post, gate, w_pre, scale, shift, name):
    L = x.shape[0]
    T = min(ROW_T, L)

    def body(x_ref, y_ref, wp_ref, g_ref, w_ref, sc_ref, sh_ref, o_ref, h_ref):
        xn = x_ref[...] + _post_fn(y_ref[...], wp_ref[...], g_ref[...])
        o_ref[...] = xn
        h_ref[...] = _pre_fn(xn, w_ref[...], sc_ref[...], sh_ref[...]).astype(BF16)

    return pl.pallas_call(
        body, name=name, grid=(L // T,),
        in_specs=[_row_spec(T, D), _row_spec(T, D)] + [_vec_spec(D)] * 5,
        out_specs=(_row_spec(T, D), _row_spec(T, D)),
        out_shape=(jax.ShapeDtypeStruct((L, D), F32), jax.ShapeDtypeStruct((L, D), BF16)),
        compiler_params=_cparams(("parallel",), VMEM_BIG))(x, y, w_post, gate, w_pre, scale, shift)


def _pre_post_bwd(x, w_pre, scale, shift, dh, dres, y, w_post, gate, name):
    L = x.shape[0]
    T = min(ROW_T, L)

    def body(x_ref, w_ref, sc_ref, sh_ref, dh_ref, dres_ref, y_ref, wp_ref, g_ref,
             dx_ref, dy_ref, dw_ref, dsc_ref, dsh_ref, dwp_ref, dg_ref):
        @pl.when(pl.program_id(0) == 0)
        def _():
            for r in (dw_ref, dsc_ref, dsh_ref, dwp_ref, dg_ref):
                r[...] = jnp.zeros_like(r)

        _, vjp = jax.vjp(_pre_fn, x_ref[...], w_ref[...], sc_ref[...], sh_ref[...])
        dx, dw, dsc, dsh = vjp(dh_ref[...].astype(F32))
        dx = dres_ref[...] + dx
        dx_ref[...] = dx
        _, vjp2 = jax.vjp(_post_fn, y_ref[...], wp_ref[...], g_ref[...])
        dy, dwp, dg = vjp2(dx)
        dy_ref[...] = dy.astype(BF16)
        dw_ref[...] += dw
        dsc_ref[...] += dsc
        dsh_ref[...] += dsh
        dwp_ref[...] += dwp
        dg_ref[...] += dg

    vec = jax.ShapeDtypeStruct((1, D), F32)
    v = _vec_spec(D)
    r = _row_spec(T, D)
    return pl.pallas_call(
        body, name=name, grid=(L // T,), in_specs=[r, v, v, v, r, r, r, v, v],
        out_specs=(r, r, v, v, v, v, v),
        out_shape=(jax.ShapeDtypeStruct((L, D), F32), jax.ShapeDtypeStruct((L, D), BF16), vec, vec, vec, vec, vec),
        compiler_params=_cparams(("arbitrary",), VMEM_BIG))(x, w_pre, scale, shift, dh, dres, y, w_post, gate)


def _last_residual_loss(x, y, w, gate, target, name):
    L = x.shape[0]
    T = min(ROW_T, L)

    def body(x_ref, y_ref, w_ref, g_ref, t_ref, s_ref, d_ref, dy_ref, dw_ref, dg_ref):
        @pl.when(pl.program_id(0) == 0)
        def _():
            for r in (s_ref, dw_ref, dg_ref):
                r[...] = jnp.zeros_like(r)

        res, vjp = jax.vjp(_post_fn, y_ref[...], w_ref[...], g_ref[...])
        e = (x_ref[...] + res) - t_ref[...]
        d = e / float(D)
        d_ref[...] = d
        s_ref[...] += jnp.sum(e * e)
        dy, dw, dg = vjp(d)
        dy_ref[...] = dy.astype(BF16)
        dw_ref[...] += dw
        dg_ref[...] += dg

    vec = jax.ShapeDtypeStruct((1, D), F32)
    r, v = _row_spec(T, D), _vec_spec(D)
    return pl.pallas_call(
        body, name=name, grid=(L // T,), in_specs=[r, r, v, v, r],
        out_specs=(pl.BlockSpec((8, 128), lambda i: (0, 0)), r, r, v, v),
        out_shape=(jax.ShapeDtypeStruct((8, 128), F32), jax.ShapeDtypeStruct((L, D), F32),
                   jax.ShapeDtypeStruct((L, D), BF16), vec, vec),
        compiler_params=_cparams(("arbitrary",), VMEM_BIG))(x, y, w, gate, target)


CONV_T = 512


def _conv_fwd(x, ci, C, w, b, name):
    L = x.shape[0]
    K = w.shape[0]
    T = min(CONV_T, L)

    def body(xc_ref, xp_ref, w_ref, b_ref, y_ref, buf):
        i = pl.program_id(0)
        buf[0:8, :] = jnp.where(i > 0, xp_ref[...], 0.0)
        buf[8:T + 8, :] = xc_ref[...]
        acc = jnp.zeros((T, C), F32) + b_ref[...]
        for k in range(K):
            acc = acc + w_ref[k:k + 1, :] * buf[pl.ds(8 - (K - 1) + k, T), :]
        y_ref[...] = acc

    return pl.pallas_call(
        body, name=name, grid=(L // T,),
        in_specs=[pl.BlockSpec((T, C), lambda i: (i, ci)),
                  pl.BlockSpec((8, C), lambda i: (jnp.maximum(i * (T // 8) - 1, 0), ci)),
                  pl.BlockSpec((K, C), lambda i: (0, 0)), _vec_spec(C)],
        out_specs=_row_spec(T, C), out_shape=jax.ShapeDtypeStruct((L, C), F32),
        scratch_shapes=[pltpu.VMEM((T + 8, C), F32)],
        compiler_params=_cparams(("parallel",), VMEM_BIG))(x, x, w, b)


def _conv_bwd(dy, x, ci, C, w, name):
    L = x.shape[0]
    K = w.shape[0]
    T = min(CONV_T, L)
    nt = L // T

    def body(dc_ref, dn_ref, xc_ref, xp_ref, w_ref, dx_ref, dw_ref, db_ref, dbuf, xbuf):
        i = pl.program_id(0)

        @pl.when(i == 0)
        def _():
            dw_ref[...] = jnp.zeros_like(dw_ref)
            db_ref[...] = jnp.zeros_like(db_ref)

        dyc = dc_ref[...]
        dbuf[0:T, :] = dyc
        dbuf[T:T + 8, :] = jnp.where(i < nt - 1, dn_ref[...], 0.0)
        xbuf[0:8, :] = jnp.where(i > 0, xp_ref[...], 0.0)
        xbuf[8:T + 8, :] = xc_ref[...]
        dx = jnp.zeros((T, C), F32)
        for k in range(K):
            dx = dx + w_ref[k:k + 1, :] * dbuf[pl.ds(K - 1 - k, T), :]
            dw_ref[k:k + 1, :] += jnp.sum(dyc * xbuf[pl.ds(8 - (K - 1) + k, T), :], axis=0, keepdims=True)
        dx_ref[...] = dx.astype(BF16)
        db_ref[...] += jnp.sum(dyc, axis=0, keepdims=True)

    return pl.pallas_call(
        body, name=name, grid=(nt,),
        in_specs=[_row_spec(T, C),
                  pl.BlockSpec((8, C), lambda i: (jnp.minimum((i + 1) * (T // 8), L // 8 - 1), 0)),
                  pl.BlockSpec((T, C), lambda i: (i, ci)),
                  pl.BlockSpec((8, C), lambda i: (jnp.maximum(i * (T // 8) - 1, 0), ci)),
                  pl.BlockSpec((K, C), lambda i: (0, 0))],
        out_specs=(_row_spec(T, C), pl.BlockSpec((K, C), lambda i: (0, 0)), _vec_spec(C)),
        out_shape=(jax.ShapeDtypeStruct((L, C), BF16), jax.ShapeDtypeStruct((K, C), F32),
                   jax.ShapeDtypeStruct((1, C), F32)),
        scratch_shapes=[pltpu.VMEM((T + 8, C), F32), pltpu.VMEM((T + 8, C), F32)],
        compiler_params=_cparams(("arbitrary",), VMEM_BIG))(dy, dy, x, x, w)


def _sc_fwd(proj, w, name):
    L = proj.shape[0]
    K = w.shape[0]
    C = GW
    T = min(CONV_T, L)
    cb0 = P_CB // C

    def body(b_ref, cc_ref, ch_ref, cp_ref, hp_ref, w_ref, y_ref, buf):
        i = pl.program_id(0)
        buf[0:8, :] = jnp.where(i > 0, cp_ref[...] * hp_ref[...], 0.0)
        buf[8:T + 8, :] = cc_ref[...] * ch_ref[...]
        acc = jnp.zeros((T, C), F32)
        for k in range(K):
            acc = acc + w_ref[k:k + 1, :] * buf[pl.ds(8 - (K - 1) + k, T), :]
        y_ref[...] = (b_ref[...] * acc).astype(BF16)

    def cur(j):
        return pl.BlockSpec((T, C), lambda i: (i, cb0 + j))

    def prev(j):
        return pl.BlockSpec((8, C), lambda i: (jnp.maximum(i * (T // 8) - 1, 0), cb0 + j))

    return pl.pallas_call(
        body, name=name, grid=(L // T,),
        in_specs=[cur(0), cur(1), cur(2), prev(1), prev(2), pl.BlockSpec((K, C), lambda i: (0, 0))],
        out_specs=_row_spec(T, C), out_shape=jax.ShapeDtypeStruct((L, C), BF16),
        scratch_shapes=[pltpu.VMEM((T + 8, C), F32)],
        compiler_params=_cparams(("parallel",), VMEM_BIG))(proj, proj, proj, proj, proj, w)


def _sc_bwd(dycat, proj, w, name):
    L = proj.shape[0]
    K = w.shape[0]
    C = GW
    T = min(CONV_T, L)
    nt = L // T
    cb0 = P_CB // C

    def body(dy_ref, dyn_ref, b_ref, bn_ref, cc_ref, ch_ref, cp_ref, hp_ref, w_ref, d_ref, dw_ref, gbuf, ubuf):
        i = pl.program_id(0)

        @pl.when(i == 0)
        def _():
            dw_ref[...] = jnp.zeros_like(dw_ref)

        dy = dy_ref[...]
        cc, ch = cc_ref[...], ch_ref[...]
        g = dy * b_ref[...]
        gbuf[0:T, :] = g
        gbuf[T:T + 8, :] = jnp.where(i < nt - 1, dyn_ref[...] * bn_ref[...], 0.0)
        ubuf[0:8, :] = jnp.where(i > 0, cp_ref[...] * hp_ref[...], 0.0)
        ubuf[8:T + 8, :] = cc * ch
        conv = jnp.zeros((T, C), F32)
        du = jnp.zeros((T, C), F32)
        for k in range(K):
            us = ubuf[pl.ds(8 - (K - 1) + k, T), :]
            conv = conv + w_ref[k:k + 1, :] * us
            du = du + w_ref[k:k + 1, :] * gbuf[pl.ds(K - 1 - k, T), :]
            dw_ref[k:k + 1, :] += jnp.sum(g * us, axis=0, keepdims=True)
        d_ref[:, 0:C] = (dy * conv).astype(BF16)
        d_ref[:, C:2 * C] = (du * ch).astype(BF16)
        d_ref[:, 2 * C:3 * C] = (du * cc).astype(BF16)

    def cur(j):
        return pl.BlockSpec((T, C), lambda i: (i, cb0 + j))

    def prev(j):
        return pl.BlockSpec((8, C), lambda i: (jnp.maximum(i * (T // 8) - 1, 0), cb0 + j))

    def nxt(col):
        return pl.BlockSpec((8, C), lambda i: (jnp.minimum((i + 1) * (T // 8), L // 8 - 1), col))

    return pl.pallas_call(
        body, name=name, grid=(nt,),
        in_specs=[pl.BlockSpec((T, C), lambda i: (i, 3)), nxt(3), cur(0), nxt(cb0), cur(1), cur(2), prev(1), prev(2),
                  pl.BlockSpec((K, C), lambda i: (0, 0))],
        out_specs=(_row_spec(T, 3 * C), pl.BlockSpec((K, C), lambda i: (0, 0))),
        out_shape=(jax.ShapeDtypeStruct((L, 3 * C), BF16), jax.ShapeDtypeStruct((K, C), F32)),
        scratch_shapes=[pltpu.VMEM((T + 8, C), F32), pltpu.VMEM((T + 8, C), F32)],
        compiler_params=_cparams(("arbitrary",), VMEM_BIG))(dycat, dycat, proj, proj, proj, proj, proj, proj, w)


def _silu(x):
    return x * jax.nn.sigmoid(x)


def _softplus(x):
    return jnp.maximum(x, 0.0) + jnp.log1p(jnp.exp(-jnp.abs(x)))


def _lane_pick(arr, idx):
    lane = lax.broadcasted_iota(jnp.int32, (1, 128), 1)
    return jnp.sum(jnp.where(lane == idx, arr, 0.0), axis=1, keepdims=True)


def _tri(n, strict=False):
    r = lax.broadcasted_iota(jnp.int32, (n, n), 0)
    c = lax.broadcasted_iota(jnp.int32, (n, n), 1)
    return (r > c) if strict else (r >= c)


def _bdot(a, b, dn=(((1,), (0,)), ((), ()))):
    return lax.dot_general(a.astype(BF16), b.astype(BF16), dn, preferred_element_type=F32)


NT = (((1,), (1,)), ((), ()))
TN = (((0,), (0,)), ((), ()))


def _split3(x):
    hi = x.astype(BF16)
    r = x - hi.astype(F32)
    mid = r.astype(BF16)
    lo = (r - mid.astype(F32)).astype(BF16)
    return hi, mid, lo


def _mask_dot(pieces, mask, dn, mask_first):
    def one(p):
        ops = (mask, p) if mask_first else (p, mask)
        return lax.dot_general(ops[0], ops[1], dn, preferred_element_type=F32)
    hi, mid, lo = pieces
    return (one(lo) + one(mid)) + one(hi)


def _tri_apply(x, upper):
    n = x.shape[0]
    r = lax.broadcasted_iota(jnp.int32, (n, n), 0)
    c = lax.broadcasted_iota(jnp.int32, (n, n), 1)
    mask = ((r <= c) if upper else (r >= c)).astype(BF16)
    return _mask_dot(_split3(x), mask, (((1,), (0,)), ((), ())), True)


@jax.custom_vjp
def _cumsum_col(cb):
    return _tri_apply(cb, False)


_cumsum_col.defvjp(lambda cb: (_tri_apply(cb, False), None), lambda _, d: (_tri_apply(d, True),))


def _cumsum_row_fwd(cb):
    tri = _tri(cb.shape[0]).astype(BF16)
    return _mask_dot(_split3(cb), tri, (((0,), (1,)), ((), ())), False)


def _cumsum_row_bwd(_, d):
    tri = _tri(d.shape[1]).astype(BF16)
    return (_mask_dot(_split3(d), tri, (((0,), (1,)), ((), ())), True),)


@jax.custom_vjp
def _cumsum_row(cb):
    return _cumsum_row_fwd(cb)


_cumsum_row.defvjp(lambda cb: (_cumsum_row_fwd(cb), None), _cumsum_row_bwd)


def _dot3(a, b):
    ah, bh = a.astype(BF16), b.astype(BF16)
    al, bl = (a - ah.astype(F32)).astype(BF16), (b - bh.astype(F32)).astype(BF16)
    d = lambda p, q: jnp.dot(p, q, preferred_element_type=F32)
    return (d(al, bh) + d(ah, bl)) + d(ah, bh)


def _unit_lower_inv_impl(ms):
    n = ms[0].shape[0]
    eye = (lax.broadcasted_iota(jnp.int32, (n, n), 0) == lax.broadcasted_iota(jnp.int32, (n, n), 1)).astype(F32)
    ps = [-m for m in ms]
    ts = [eye + p for p in ps]
    for _ in range(int(math.log2(n)) - 1):
        ps = [_dot3(p, p) for p in ps]
        ts = [t + _dot3(t, p) for t, p in zip(ts, ps)]
    return tuple(ts)


@jax.custom_vjp
def _unit_lower_inv(ms):
    return _unit_lower_inv_impl(ms)


def _unit_lower_inv_fwd(ms):
    ts = _unit_lower_inv_impl(ms)
    return ts, ts


def _unit_lower_inv_bwd(ts, ds):
    xs = [_bdot(t, d, TN) for t, d in zip(ts, ds)]
    return (tuple(-_bdot(x, t, NT) for x, t in zip(xs, ts)),)


_unit_lower_inv.defvjp(_unit_lower_inv_fwd, _unit_lower_inv_bwd)


GDN_SUB = 4
GDN_H = 4


def _gdn_step_fn(qs, ks, vs, zs, gs, S, alog_row, dtb_row, nw):
    n = GDN_CHUNK
    lanes = [(h, c) for h in range(GDN_H) for c in range(GDN_SUB)]
    z3 = lambda f, a, b, c_: [f(x, y, z) for x, y, z in zip(a, b, c_)]
    q = [_silu(qs[h][c]) for h, c in lanes]
    k = [_silu(ks[h][c]) for h, c in lanes]
    v = [_silu(vs[h][c]) for h, c in lanes]
    q = [t * lax.rsqrt(jnp.sum(t * t, axis=-1, keepdims=True) + EPS) * (128.0 ** -0.5) for t in q]
    k = [t * lax.rsqrt(jnp.sum(t * t, axis=-1, keepdims=True) + EPS) for t in k]
    beta = [jax.nn.sigmoid(_lane_pick(gs[c], LANE_GB + h)) for h, c in lanes]
    rate = [-jnp.exp(_lane_pick(alog_row, LANE_GA + h)) for h in range(GDN_H)]
    bias = [_lane_pick(dtb_row, LANE_GA + h) for h in range(GDN_H)]
    g = [rate[h] * _softplus(_lane_pick(gs[c], LANE_GA + h) + bias[h]) for h, c in lanes]
    cb = [jnp.broadcast_to(t, (n, n)) for t in g]
    gc_col = [_cumsum_col(t) for t in cb]
    gc_row = [_cumsum_row(t) for t in cb]
    tri_incl, tri_strict = _tri(n), _tri(n, strict=True)
    decay = [jnp.exp(jnp.where(tri_incl, a - b, NEG)) for a, b in zip(gc_col, gc_row)]
    gc = [t[:, 0:1] for t in gc_col]
    g_last = [jnp.sum(t, axis=0, keepdims=True) for t in g]
    kb = [a * b for a, b in zip(k, beta)]
    m = z3(lambda a, b, d: jnp.where(tri_strict, _bdot(a, b, NT) * d, 0.0), kb, k, decay)
    t_inv = _unit_lower_inv(tuple(m))
    egc = [jnp.exp(t) for t in gc]
    u = z3(lambda t, a, b: _bdot(t, a * b), t_inv, v, beta)
    w = z3(lambda t, a, e: _bdot(t, a * e), t_inv, kb, egc)
    attn = z3(lambda a, b, d: jnp.where(tri_incl, _bdot(a, b, NT) * d, 0.0), q, k, decay)
    qd = [a * e for a, e in zip(q, egc)]
    kd = z3(lambda a, gl, c_: a * jnp.exp(gl - c_), k, g_last, gc)
    egl = [jnp.exp(t) for t in g_last]
    zg = [_silu(zs[h][c]) for h, c in lanes]
    S = list(S)
    ys = [[None] * GDN_SUB for _ in range(GDN_H)]
    for c in range(GDN_SUB):
        ii = [h * GDN_SUB + c for h in range(GDN_H)]
        v_new = [u[i] - _bdot(w[i], S[h]) for h, i in enumerate(ii)]
        o = [_bdot(qd[i], S[h]) + _bdot(attn[i], v_new[h]) for h, i in enumerate(ii)]
        S = [S[h] * egl[i] + _bdot(kd[i], v_new[h], TN) for h, i in enumerate(ii)]
        for h, i in enumerate(ii):
            ys[h][c] = _rms(o[h], nw) * zg[i]
    return tuple(tuple(y) for y in ys), tuple(S)


def _gdn_step_args(qkv_ref, z_ref, g_ref):
    n = GDN_CHUNK
    rows = [slice(n * c, n * c + n) for c in range(GDN_SUB)]
    col = lambda ref, o: tuple(tuple(ref[r, o + 128 * h:o + 128 * h + 128] for r in rows) for h in range(GDN_H))
    return rows, (col(qkv_ref, 0), col(qkv_ref, 512), col(qkv_ref, 1024), col(z_ref, 0), tuple(g_ref[r, :] for r in rows))


def _carry(cargo, refs, step, nsteps):
    if cargo is None:
        return lambda: None
    start, mid, end = cargo.phases(*refs)
    pl.when(step == 0)(start)
    pl.when(step == (3 * nsteps) // 4)(mid)
    return lambda: pl.when(step == nsteps - 1)(end)


def _cargo_io(cargo):
    if cargo is None:
        return [], [], [], []
    hbm = pl.BlockSpec(memory_space=pl.ANY)
    return list(cargo.arrays), [hbm] * cargo.n_in, list(cargo.out_shapes), [hbm] * cargo.n_out


def _gdn_fwd(qkvc, proj, alog_row, dtb_row, nw, name, cargo=None):
    L = qkvc.shape[0]
    n = GDN_CHUNK * GDN_SUB
    Nc = L // n
    c_arr, c_ispec, c_oshape, c_ospec = _cargo_io(cargo)
    ni, no = len(c_arr), len(c_oshape)

    def body(*refs):
        qkv_ref, z_ref, g_ref, al_ref, db_ref, nw_ref = refs[:6]
        y_ref, sin_ref = refs[6 + ni:8 + ni]
        S = refs[8 + ni + no]
        step = pl.program_id(0)
        drain = _carry(cargo, (refs[6:6 + ni], refs[8 + ni:8 + ni + no], refs[9 + ni + no:]), step, Nc)

        @pl.when(step == 0)
        def _():
            S[...] = jnp.zeros_like(S)

        rows, args = _gdn_step_args(qkv_ref, z_ref, g_ref)
        states = tuple(S[h] for h in range(GDN_H))
        ys, sn = _gdn_step_fn(*args, states, al_ref[...], db_ref[...], nw_ref[...])
        for h in range(GDN_H):
            sin_ref[0, h] = states[h]
            for c in range(GDN_SUB):
                y_ref[rows[c], 128 * h:128 * h + 128] = ys[h][c].astype(BF16)
            S[h] = sn[h]
        drain()

    return pl.pallas_call(
        body, name=name, grid=(Nc,),
        in_specs=[pl.BlockSpec((n, 1536), lambda i: (i, 0)), pl.BlockSpec((n, 512), lambda i: (i, P_GZ // 512)),
                  pl.BlockSpec((n, 128), lambda i: (i, P_GATE // 128)), _vec_spec(128), _vec_spec(128), _vec_spec(128)]
        + c_ispec,
        out_specs=tuple([pl.BlockSpec((n, 512), lambda i: (i, 0)), pl.BlockSpec((1, 4, 128, 128), lambda i: (i, 0, 0, 0))]
                        + c_ospec),
        out_shape=tuple([jax.ShapeDtypeStruct((L, 512), BF16), jax.ShapeDtypeStruct((Nc, 4, 128, 128), F32)] + c_oshape),
        scratch_shapes=[pltpu.VMEM((4, 128, 128), F32)] + (cargo.scratch if cargo else []),
        compiler_params=_cparams(("arbitrary",), VMEM_BIG))(qkvc, proj, proj, alog_row, dtb_row, nw, *c_arr)


def _gdn_bwd(qkvc, proj, alog_row, dtb_row, nw, s_in, dycat, name, cargo=None):
    L = qkvc.shape[0]
    n = GDN_CHUNK * GDN_SUB
    Nc = L // n
    c_arr, c_ispec, c_oshape, c_ospec = _cargo_io(cargo)
    ni, no = len(c_arr), len(c_oshape)

    def body(*refs):
        qkv_ref, z_ref, g_ref, al_ref, db_ref, nw_ref, sin_ref, dy_ref = refs[:8]
        dqkv_ref, dz_ref, dg_ref, dal_ref, ddb_ref, dnw_ref = refs[8 + ni:14 + ni]
        dS = refs[14 + ni + no]
        step = pl.program_id(0)
        drain = _carry(cargo, (refs[8:8 + ni], refs[14 + ni:14 + ni + no], refs[15 + ni + no:]), step, Nc)

        @pl.when(step == 0)
        def _():
            dS[...] = jnp.zeros_like(dS)
            dal_ref[...] = jnp.zeros_like(dal_ref)
            ddb_ref[...] = jnp.zeros_like(ddb_ref)
            dnw_ref[...] = jnp.zeros_like(dnw_ref)

        rows, args = _gdn_step_args(qkv_ref, z_ref, g_ref)
        s_in = tuple(sin_ref[0, h] for h in range(GDN_H))
        _, vjp = jax.vjp(_gdn_step_fn, *args, s_in, al_ref[...], db_ref[...], nw_ref[...])
        dys = tuple(tuple(dy_ref[r, 128 * h:128 * h + 128] for r in rows) for h in range(GDN_H))
        dq, dk, dv, dz, dgt, ds, dal, ddb, dnw = vjp((dys, tuple(dS[h] for h in range(GDN_H))))
        for h in range(GDN_H):
            for c in range(GDN_SUB):
                dqkv_ref[rows[c], 128 * h:128 * h + 128] = dq[h][c]
                dqkv_ref[rows[c], 512 + 128 * h:640 + 128 * h] = dk[h][c]
                dqkv_ref[rows[c], 1024 + 128 * h:1152 + 128 * h] = dv[h][c]
                dz_ref[rows[c], 128 * h:128 * h + 128] = dz[h][c].astype(BF16)
            dS[h] = ds[h]
        for c in range(GDN_SUB):
            dg_ref[rows[c], :] = dgt[c]
        dal_ref[...] += dal
        ddb_ref[...] += ddb
        dnw_ref[...] += dnw
        drain()

    rev = lambda i: Nc - 1 - i
    vec = jax.ShapeDtypeStruct((1, 128), F32)
    return pl.pallas_call(
        body, name=name, grid=(Nc,),
        in_specs=[pl.BlockSpec((n, 1536), lambda i: (rev(i), 0)), pl.BlockSpec((n, 512), lambda i: (rev(i), P_GZ // 512)),
                  pl.BlockSpec((n, 128), lambda i: (rev(i), P_GATE // 128)), _vec_spec(128), _vec_spec(128), _vec_spec(128),
                  pl.BlockSpec((1, 4, 128, 128), lambda i: (rev(i), 0, 0, 0)), pl.BlockSpec((n, 512), lambda i: (rev(i), 0))]
        + c_ispec,
        out_specs=tuple([pl.BlockSpec((n, 1536), lambda i: (rev(i), 0)), pl.BlockSpec((n, 512), lambda i: (rev(i), 0)),
                         pl.BlockSpec((n, 128), lambda i: (rev(i), 0)), _vec_spec(128), _vec_spec(128), _vec_spec(128)]
                        + c_ospec),
        out_shape=tuple([jax.ShapeDtypeStruct((L, 1536), F32), jax.ShapeDtypeStruct((L, 512), BF16),
                         jax.ShapeDtypeStruct((L, 128), F32), vec, vec, vec] + c_oshape),
        scratch_shapes=[pltpu.VMEM((4, 128, 128), F32)] + (cargo.scratch if cargo else []),
        compiler_params=_cparams(("arbitrary",), VMEM_BIG))(qkvc, proj, proj, alog_row, dtb_row, nw, s_in, dycat, *c_arr)


def _ssd_chunk_fn(xs, bs, cs, zs, gates, st, alog_row, dtb_row, d_row, nws):
    n = SSM_CHUNK
    tri = _tri(n)
    lane = lax.broadcasted_iota(jnp.int32, (1, 128), 1)
    lo = lane < 64
    xs = [_silu(t) for t in xs]
    bs = [_silu(t) for t in bs]
    cs = [_silu(t) for t in cs]
    cb = [_bdot(cs[g], bs[g], NT) for g in range(2)]
    H = range(8)
    P = range(4)
    dt = [_softplus(_lane_pick(gates, LANE_SDT + h) + _lane_pick(dtb_row, LANE_SDT + h)) for h in H]
    da = [dt[h] * (-jnp.exp(_lane_pick(alog_row, LANE_SDT + h))) for h in H]
    dab = [jnp.broadcast_to(t, (n, n)) for t in da]
    cs_col = [_cumsum_col(t) for t in dab]
    cs_row = [_cumsum_row(t) for t in dab]
    lmat = [jnp.exp(jnp.where(tri, a - b, NEG)) for a, b in zip(cs_col, cs_row)]
    xdt_h = [jnp.where(lo if h % 2 == 0 else jnp.logical_not(lo), xs[h // 2] * dt[h], 0.0) for h in H]
    yd = [_bdot(cb[h // 4] * lmat[h], xdt_h[h]) for h in H]
    tot = [jnp.sum(t, axis=0, keepdims=True) for t in da]
    dsk = [_lane_pick(d_row, LANE_SDT + h) for h in H]
    cs_lane = [jnp.where(lo, cs_col[2 * p], cs_col[2 * p + 1]) for p in P]
    dt_lane = [jnp.where(lo, dt[2 * p], dt[2 * p + 1]) for p in P]
    tot_lane = [jnp.where(lo, tot[2 * p], tot[2 * p + 1]) for p in P]
    dsk_lane = [jnp.where(lo, dsk[2 * p], dsk[2 * p + 1]) for p in P]
    xdt = [xs[p] * dt_lane[p] for p in P]
    upd = [_bdot(bs[p // 2], xdt[p] * jnp.exp(tot_lane[p] - cs_lane[p]), TN) for p in P]
    st_new = [st[p] * jnp.exp(tot_lane[p]) + upd[p] for p in P]
    yoff = [_bdot(cs[p // 2], st[p]) * jnp.exp(cs_lane[p]) for p in P]
    zg = [_silu(zs[p]) for p in P]
    ys = [(yd[2 * p] + yd[2 * p + 1] + yoff[p] + xs[p] * dsk_lane[p]) * zg[p] for p in P]
    out = []
    for g in range(2):
        ms = (jnp.sum(ys[2 * g] ** 2, axis=-1, keepdims=True) + jnp.sum(ys[2 * g + 1] ** 2, axis=-1, keepdims=True)) / 256.0
        rstd = lax.rsqrt(ms + EPS)
        out += [ys[2 * g] * rstd * nws[2 * g], ys[2 * g + 1] * rstd * nws[2 * g + 1]]
    return tuple(out), tuple(st_new)


def _ssd_args(xbc_ref, z_ref, nw_ref):
    xs = [xbc_ref[:, 128 * p:128 * p + 128] for p in range(4)]
    bs = [xbc_ref[:, 512 + 128 * g:640 + 128 * g] for g in range(2)]
    cs = [xbc_ref[:, 768 + 128 * g:896 + 128 * g] for g in range(2)]
    zs = [z_ref[:, 128 * p:128 * p + 128] for p in range(4)]
    nws = [nw_ref[:, 128 * p:128 * p + 128] for p in range(4)]
    return xs, bs, cs, zs, nws


def _ssd_fwd(xbcc, proj, alog_row, dtb_row, d_row, nw, name):
    L = xbcc.shape[0]
    n = SSM_CHUNK
    Nc = L // n

    def body(xbc_ref, z_ref, g_ref, al_ref, db_ref, d_ref, nw_ref, y_ref, sin_ref, S):
        @pl.when(pl.program_id(0) == 0)
        def _():
            S[...] = jnp.zeros_like(S)

        xs, bs, cs, zs, nws = _ssd_args(xbc_ref, z_ref, nw_ref)
        st = [S[p] for p in range(4)]
        sin_ref[0] = S[...]
        ys, sn = _ssd_chunk_fn(xs, bs, cs, zs, g_ref[...], st, al_ref[...], db_ref[...], d_ref[...], nws)
        for p in range(4):
            y_ref[:, 128 * p:128 * p + 128] = ys[p].astype(BF16)
            S[p] = sn[p]

    return pl.pallas_call(
        body, name=name, grid=(Nc,),
        in_specs=[pl.BlockSpec((n, 1024), lambda i: (i, 0)), pl.BlockSpec((n, 512), lambda i: (i, P_SZ // 512)),
                  pl.BlockSpec((n, 128), lambda i: (i, P_GATE // 128)), _vec_spec(128), _vec_spec(128), _vec_spec(128),
                  _vec_spec(512)],
        out_specs=(pl.BlockSpec((n, 512), lambda i: (i, 0)), pl.BlockSpec((1, 4, 128, 128), lambda i: (i, 0, 0, 0))),
        out_shape=(jax.ShapeDtypeStruct((L, 512), BF16), jax.ShapeDtypeStruct((Nc, 4, 128, 128), F32)),
        scratch_shapes=[pltpu.VMEM((4, 128, 128), F32)],
        compiler_params=_cparams(("arbitrary",), VMEM_BIG))(xbcc, proj, proj, alog_row, dtb_row, d_row, nw)


def _ssd_bwd(xbcc, proj, alog_row, dtb_row, d_row, nw, s_in, dycat, name):
    L = xbcc.shape[0]
    n = SSM_CHUNK
    Nc = L // n

    def body(xbc_ref, z_ref, g_ref, al_ref, db_ref, d_ref, nw_ref, sin_ref, dy_ref,
             dxbc_ref, dz_ref, dg_ref, dal_ref, ddb_ref, dd_ref, dnw_ref, dS):
        @pl.when(pl.program_id(0) == 0)
        def _():
            dS[...] = jnp.zeros_like(dS)
            dal_ref[...] = jnp.zeros_like(dal_ref)
            ddb_ref[...] = jnp.zeros_like(ddb_ref)
            dd_ref[...] = jnp.zeros_like(dd_ref)
            dnw_ref[...] = jnp.zeros_like(dnw_ref)

        xs, bs, cs, zs, nws = _ssd_args(xbc_ref, z_ref, nw_ref)
        st = [sin_ref[0, p] for p in range(4)]
        _, vjp = jax.vjp(_ssd_chunk_fn, xs, bs, cs, zs, g_ref[...], st, al_ref[...], db_ref[...], d_ref[...], nws)
        dys = tuple(dy_ref[:, 128 * p:128 * p + 128] for p in range(4))
        dxs, dbs, dcs, dzs, dgt, dst, dal, ddb, dd, dnws = vjp((dys, tuple(dS[p] for p in range(4))))
        for p in range(4):
            dxbc_ref[:, 128 * p:128 * p + 128] = dxs[p]
            dz_ref[:, 128 * p:128 * p + 128] = dzs[p].astype(BF16)
            dS[p] = dst[p]
            dnw_ref[:, 128 * p:128 * p + 128] += dnws[p]
        for g in range(2):
            dxbc_ref[:, 512 + 128 * g:640 + 128 * g] = dbs[g]
            dxbc_ref[:, 768 + 128 * g:896 + 128 * g] = dcs[g]
        dg_ref[...] = dgt
        dal_ref[...] += dal
        ddb_ref[...] += ddb
        dd_ref[...] += dd

    rev = lambda i: Nc - 1 - i
    vec = jax.ShapeDtypeStruct((1, 128), F32)
    return pl.pallas_call(
        body, name=name, grid=(Nc,),
        in_specs=[pl.BlockSpec((n, 1024), lambda i: (rev(i), 0)), pl.BlockSpec((n, 512), lambda i: (rev(i), P_SZ // 512)),
                  pl.BlockSpec((n, 128), lambda i: (rev(i), P_GATE // 128)), _vec_spec(128), _vec_spec(128), _vec_spec(128),
                  _vec_spec(512), pl.BlockSpec((1, 4, 128, 128), lambda i: (rev(i), 0, 0, 0)),
                  pl.BlockSpec((n, 512), lambda i: (rev(i), 1))],
        out_specs=(pl.BlockSpec((n, 1024), lambda i: (rev(i), 0)), pl.BlockSpec((n, 512), lambda i: (rev(i), 0)),
                   pl.BlockSpec((n, 128), lambda i: (rev(i), 0)), _vec_spec(128), _vec_spec(128), _vec_spec(128),
                   _vec_spec(512)),
        out_shape=(jax.ShapeDtypeStruct((L, 1024), F32), jax.ShapeDtypeStruct((L, 512), BF16),
                   jax.ShapeDtypeStruct((L, 128), F32), vec, vec, vec, jax.ShapeDtypeStruct((1, 512), F32)),
        scratch_shapes=[pltpu.VMEM((4, 128, 128), F32)],
        compiler_params=_cparams(("arbitrary",), VMEM_BIG))(xbcc, proj, proj, alog_row, dtb_row, d_row, nw, s_in, dycat)


def _rot(x):
    W = x.shape[1]
    lane = lax.broadcasted_iota(jnp.int32, (1, W), 1)
    first = (lane % 64) < 32
    return jnp.where(first, -pltpu.roll(x, W - 32, 1), pltpu.roll(x, 32, 1))


def _rope(x, cos, sin):
    return x * cos + _rot(x) * sin


def _rope_t(d, cos, sin):
    return d * cos - _rot(d * sin)


def _swa_core_fn(qs, ks, vs, sink_row, mask):
    lane = lax.broadcasted_iota(jnp.int32, (1, 128), 1)
    lo = lane < 64
    H = range(8)
    qm = [jnp.where(lo if h % 2 == 0 else jnp.logical_not(lo), qs[h // 2], 0.0) for h in H]
    s = [jnp.where(mask, _bdot(qm[h], ks[h // 4], NT) * 0.125, NEG) for h in H]
    sk = [_lane_pick(sink_row, h) for h in H]
    mx = [lax.stop_gradient(jnp.maximum(jnp.max(s[h], axis=-1, keepdims=True), sk[h])) for h in H]
    pr = [jnp.exp(s[h] - mx[h]) for h in H]
    den = [jnp.sum(pr[h], axis=-1, keepdims=True) + jnp.exp(sk[h] - mx[h]) for h in H]
    pr = [pr[h] / den[h] for h in H]
    ov = [_bdot(pr[h], vs[h // 4]) for h in H]
    return tuple(jnp.where(lo, ov[2 * p], ov[2 * p + 1]) for p in range(4))


def _swa_prepare(blk, q_ref, kc_ref, kp_ref, vc_ref, vp_ref, cc_ref, sc_ref, cp_ref, sp_ref):
    W = WINDOW
    cos_c, sin_c = cc_ref[...], sc_ref[...]
    cos_b = jnp.concatenate([cp_ref[...], cos_c], axis=0)
    sin_b = jnp.concatenate([sp_ref[...], sin_c], axis=0)
    cos_q, sin_q = jnp.concatenate([cos_c] * 4, axis=1), jnp.concatenate([sin_c] * 4, axis=1)
    cos_k, sin_k = jnp.concatenate([cos_b] * 2, axis=1), jnp.concatenate([sin_b] * 2, axis=1)
    qr = _rope(q_ref[...], cos_q, sin_q)
    kr = _rope(jnp.concatenate([kp_ref[...], kc_ref[...]], axis=0), cos_k, sin_k)
    vb = jnp.concatenate([vp_ref[...], vc_ref[...]], axis=0)
    qi = lax.broadcasted_iota(jnp.int32, (W, 2 * W), 0)
    kj = lax.broadcasted_iota(jnp.int32, (W, 2 * W), 1)
    rel = qi + W - kj
    mask = (rel >= 0) & (rel < W) & ((blk > 0) | (kj >= W))
    qs = [qr[:, 128 * p:128 * p + 128] for p in range(4)]
    ks = [kr[:, 128 * g:128 * g + 128] for g in range(2)]
    vs = [vb[:, 128 * g:128 * g + 128] for g in range(2)]
    return qs, ks, vs, mask, (cos_q, sin_q, cos_k, sin_k)


def _swa_specs(nb, order):
    W = WINDOW
    cur = lambda i: order(i)
    prv = lambda i: jnp.maximum(order(i) - 1, 0)
    return [pl.BlockSpec((W, 512), lambda i: (cur(i), P_AQ // 512)),
            pl.BlockSpec((W, 256), lambda i: (cur(i), P_AK // 256)), pl.BlockSpec((W, 256), lambda i: (prv(i), P_AK // 256)),
            pl.BlockSpec((W, 256), lambda i: (cur(i), P_AV // 256)), pl.BlockSpec((W, 256), lambda i: (prv(i), P_AV // 256)),
            pl.BlockSpec((W, 128), lambda i: (cur(i), 0)), pl.BlockSpec((W, 128), lambda i: (cur(i), 0)),
            pl.BlockSpec((W, 128), lambda i: (prv(i), 0)), pl.BlockSpec((W, 128), lambda i: (prv(i), 0)),
            _vec_spec(128)]


def _swa_fwd(proj, cos, sin, sink_row, name):
    L = proj.shape[0]
    W = WINDOW
    nb = L // W

    def body(q_ref, kc_ref, kp_ref, vc_ref, vp_ref, cc_ref, sc_ref, cp_ref, sp_ref, sk_ref, y_ref):
        blk = pl.program_id(0)
        qs, ks, vs, mask, _ = _swa_prepare(blk, q_ref, kc_ref, kp_ref, vc_ref, vp_ref, cc_ref, sc_ref, cp_ref, sp_ref)
        outs = _swa_core_fn(qs, ks, vs, sk_ref[...], mask)
        for p in range(4):
            y_ref[:, 128 * p:128 * p + 128] = outs[p].astype(BF16)

    return pl.pallas_call(
        body, name=name, grid=(nb,), in_specs=_swa_specs(nb, lambda i: i),
        out_specs=pl.BlockSpec((W, 512), lambda i: (i, 0)), out_shape=jax.ShapeDtypeStruct((L, 512), BF16),
        compiler_params=_cparams(("parallel",), VMEM_BIG))(proj, proj, proj, proj, proj, cos, sin, cos, sin, sink_row)


def _swa_bwd(proj, cos, sin, sink_row, dycat, name):
    L = proj.shape[0]
    W = WINDOW
    nb = L // W
    rev = lambda i: nb - 1 - i

    def body(q_ref, kc_ref, kp_ref, vc_ref, vp_ref, cc_ref, sc_ref, cp_ref, sp_ref, sk_ref, dy_ref,
             dq_ref, dk_ref, dv_ref, dsk_ref, ck, cv):
        i = pl.program_id(0)
        blk = nb - 1 - i

        @pl.when(i == 0)
        def _():
            ck[...] = jnp.zeros_like(ck)
            cv[...] = jnp.zeros_like(cv)
            dsk_ref[...] = jnp.zeros_like(dsk_ref)

        qs, ks, vs, mask, (cos_q, sin_q, cos_k, sin_k) = _swa_prepare(
            blk, q_ref, kc_ref, kp_ref, vc_ref, vp_ref, cc_ref, sc_ref, cp_ref, sp_ref)
        _, vjp = jax.vjp(functools.partial(_swa_core_fn, mask=mask), qs, ks, vs, sk_ref[...])
        dqs, dks, dvs, dsk = vjp(tuple(dy_ref[:, 128 * p:128 * p + 128] for p in range(4)))
        dq_ref[...] = _rope_t(jnp.concatenate(dqs, axis=1), cos_q, sin_q).astype(BF16)
        dkb = _rope_t(jnp.concatenate(dks, axis=1), cos_k, sin_k)
        dvb = jnp.concatenate(dvs, axis=1)
        dk_ref[...] = (dkb[W:2 * W] + ck[...]).astype(BF16)
        dv_ref[...] = (dvb[W:2 * W] + cv[...]).astype(BF16)
        ck[...] = dkb[0:W]
        cv[...] = dvb[0:W]
        dsk_ref[...] += dsk

    return pl.pallas_call(
        body, name=name, grid=(nb,),
        in_specs=_swa_specs(nb, rev) + [pl.BlockSpec((W, 512), lambda i: (rev(i), 2))],
        out_specs=(pl.BlockSpec((W, 512), lambda i: (rev(i), 0)), pl.BlockSpec((W, 256), lambda i: (rev(i), 0)),
                   pl.BlockSpec((W, 256), lambda i: (rev(i), 0)), _vec_spec(128)),
        out_shape=(jax.ShapeDtypeStruct((L, 512), BF16), jax.ShapeDtypeStruct((L, 256), BF16),
                   jax.ShapeDtypeStruct((L, 256), BF16), jax.ShapeDtypeStruct((1, 128), F32)),
        scratch_shapes=[pltpu.VMEM((W, 256), F32), pltpu.VMEM((W, 256), F32)],
        compiler_params=_cparams(("arbitrary",), VMEM_BIG))(
            proj, proj, proj, proj, proj, cos, sin, cos, sin, sink_row, dycat)


def _adamw(w, g, m, v):
    m = ADAM_B1 * m + (1.0 - ADAM_B1) * g
    v = ADAM_B2 * v + (1.0 - ADAM_B2) * (g * g)
    m_hat = m / (1.0 - ADAM_B1 ** ADAM_STEP)
    v_hat = v / (1.0 - ADAM_B2 ** ADAM_STEP)
    delta = -ADAM_LR * (m_hat / (jnp.sqrt(v_hat) + ADAM_EPS) + ADAM_WD * w)
    return delta, m, v


def _ada_fwd(c_all, ada_w, ada_b_sh, name):
    S = ada_w.shape[2]

    def body(c_ref, w_ref, b_ref, o_ref):
        o_ref[0] = _bdot(_silu(c_ref[...]), w_ref[0]) + b_ref[0]

    return pl.pallas_call(
        body, name=name, grid=(2,),
        in_specs=[pl.BlockSpec((N_DEV, D), lambda i: (0, 0)), pl.BlockSpec((1, D, S), lambda i: (i, 0, 0)),
                  pl.BlockSpec((1, 1, S), lambda i: (i, 0, 0))],
        out_specs=pl.BlockSpec((1, N_DEV, S), lambda i: (i, 0, 0)), out_shape=jax.ShapeDtypeStruct((2, N_DEV, S), F32),
        compiler_params=_cparams(("parallel",), VMEM_BIG))(c_all, ada_w, ada_b_sh.reshape(2, 1, S))


def _ada_bwd_adamw(c_all, dmod_sh, w, m, v, name):
    S = w.shape[2]
    T = 256

    def body(c_ref, d_ref, w_ref, m_ref, v_ref, g_ref, dl_ref, nm_ref, nv_ref):
        g = _bdot(_silu(c_ref[...]), d_ref[0], TN)
        dl, nm, nv = _adamw(w_ref[0], g, m_ref[0], v_ref[0])
        g_ref[0], dl_ref[0], nm_ref[0], nv_ref[0] = g, dl, nm, nv

    blk = pl.BlockSpec((1, T, S), lambda i, j: (i, j, 0))
    sds = jax.ShapeDtypeStruct(w.shape, F32)
    return pl.pallas_call(
        body, name=name, grid=(2, D // T),
        in_specs=[pl.BlockSpec((N_DEV, T), lambda i, j: (0, j)), pl.BlockSpec((1, N_DEV, S), lambda i, j: (i, 0, 0)),
                  blk, blk, blk],
        out_specs=(blk, blk, blk, blk), out_shape=(sds, sds, sds, sds),
        compiler_params=_cparams(("parallel", "parallel"), VMEM_BIG))(c_all, dmod_sh, w, m, v)


def _sum8(r, name):
    _, R, W = r.shape
    T = R
    for cand in (2496, 2048, 1024, 512, 256, 128, 64, 32, 16, 8):
        if R % cand == 0:
            T = cand
            break

    def body(r_ref, o_ref):
        acc = r_ref[0].astype(F32)
        for d in range(1, N_DEV):
            acc = acc + r_ref[d].astype(F32)
        o_ref[...] = acc

    return pl.pallas_call(
        body, name=name, grid=(R // T,), in_specs=[pl.BlockSpec((N_DEV, T, W), lambda i: (0, i, 0))],
        out_specs=pl.BlockSpec((T, W), lambda i: (i, 0)), out_shape=jax.ShapeDtypeStruct((R, W), F32),
        compiler_params=_cparams(("parallel",), VMEM_BIG))(r)


def _adamw_call(w, g, m, v, name):
    R, W = w.shape
    T = R
    for cand in (1024, 512, 256, 128, 64, 32, 16, 8):
        if R % cand == 0 and R > cand and cand * W * 4 <= 2 * 1024 * 1024:
            T = cand
            break

    def body(w_ref, g_ref, m_ref, v_ref, dl_ref, nm_ref, nv_ref):
        dl_ref[...], nm_ref[...], nv_ref[...] = _adamw(w_ref[...], g_ref[...], m_ref[...], v_ref[...])

    blk = pl.BlockSpec((T, W), lambda i: (i, 0))
    sds = jax.ShapeDtypeStruct((R, W), F32)
    return pl.pallas_call(
        body, name=name, grid=(R // T,), in_specs=[blk, blk, blk, blk], out_specs=(blk, blk, blk),
        out_shape=(sds, sds, sds), compiler_params=_cparams(("parallel",), VMEM_BIG))(w, g, m, v)


def _sum8_adamw(r0, r1, w, m, v, name):
    _, _, R, W = r0.shape
    T = (256 * 1024) // W
    assert R % T == 0, (R, W, T)
    nj = R // T

    def body(r0_ref, r1_ref, w_ref, m_ref, v_ref, g_ref, dl_ref, nm_ref, nv_ref):
        def run(r_ref):
            g = r_ref[0].astype(F32)
            for d in range(1, N_DEV):
                g = g + r_ref[d].astype(F32)
            g_ref[...] = g
            dl_ref[...], nm_ref[...], nv_ref[...] = _adamw(w_ref[...], g, m_ref[...], v_ref[...])

        pl.when(pl.program_id(0) == 0)(lambda: run(r0_ref))
        pl.when(pl.program_id(0) == 1)(lambda: run(r1_ref))

    r0_spec = pl.BlockSpec((N_DEV, None, T, W), lambda i, j: (0, 0, jnp.where(i == 0, j, nj - 1), 0))
    r1_spec = pl.BlockSpec((N_DEV, None, T, W), lambda i, j: (0, 0, jnp.where(i == 1, j, 0), 0))
    blk = pl.BlockSpec((None, T, W), lambda i, j: (i, j, 0))
    sds = jax.ShapeDtypeStruct((2, R, W), F32)
    return pl.pallas_call(
        body, name=name, grid=(2, nj), in_specs=[r0_spec, r1_spec, blk, blk, blk],
        out_specs=(blk, blk, blk, blk), out_shape=(sds, sds, sds, sds),
        compiler_params=_cparams(("arbitrary", "arbitrary"), VMEM_BIG))(r0, r1, w, m, v)


def _permute_rows(wt):
    z = lambda n: jnp.zeros((n, wt.shape[1]), wt.dtype)
    s = lambda o, n: wt[o:o + n]
    kd = [s(O_AK, 64), s(O_AK, 64), s(O_AK + 64, 64), s(O_AK + 64, 64)]
    vd = [s(O_AV, 64), s(O_AV, 64), s(O_AV + 64, 64), s(O_AV + 64, 64)]
    return jnp.concatenate(
        [s(O_GQ, 1536), s(O_GZ, 512), s(O_SXBC, 1024), s(O_SZ, 512), s(O_AQ, 512), s(O_CB, 1536)] + kd + vd
        + [s(O_GB, 4), s(O_GA, 4), s(O_SDT, 8), z(112)], axis=0)


def _unpermute_rows(g):
    s = lambda o, n: g[o:o + n]
    add = lambda o: (s(o, 64).astype(F32) + s(o + 64, 64).astype(F32)).astype(g.dtype)
    dk = [add(P_AK), add(P_AK + 128)]
    dv = [add(P_AV), add(P_AV + 128)]
    return jnp.concatenate(
        [s(P_QKV, 1536), s(P_GZ, 512), s(P_GATE, 8), s(P_SZ, 512), s(P_SXBC, 1024), s(P_GATE + 8, 8), s(P_AQ, 512)]
        + dk + dv + [s(P_CB, 1536)], axis=0)


def _row128(vals, lane0):
    n = vals.shape[0]
    return jnp.concatenate([jnp.zeros((lane0,), F32), vals, jnp.zeros((128 - lane0 - n,), F32)]).reshape(1, 128)


def _pad_rows128(flat):
    n = flat.shape[0]
    pad = (-n) % 1024
    return jnp.concatenate([flat, jnp.zeros((pad,), flat.dtype)]).reshape(-1, 128)


def kernel(x, c, positions, ada_w, ada_b, norm_pre_mix, norm_post_mix, norm_pre_mlp, norm_post_mlp, w_in, w_out, gdn_conv_w, gdn_a_log, gdn_dt_bias, gdn_norm_w, ssm_conv_w, ssm_conv_b, ssm_a_log, ssm_dt_bias, ssm_d, ssm_norm_w, attn_sinks, sc_conv_w, w_up, w_down, loss_target, m_ada_w, m_ada_b, m_norm_pre_mix, m_norm_post_mix, m_norm_pre_mlp, m_norm_post_mlp, m_w_in, m_w_out, m_gdn_conv_w, m_gdn_a_log, m_gdn_dt_bias, m_gdn_norm_w, m_ssm_conv_w, m_ssm_conv_b, m_ssm_a_log, m_ssm_dt_bias, m_ssm_d, m_ssm_norm_w, m_attn_sinks, m_sc_conv_w, m_w_up, m_w_down, v_ada_w, v_ada_b, v_norm_pre_mix, v_norm_post_mix, v_norm_pre_mlp, v_norm_post_mlp, v_w_in, v_w_out, v_gdn_conv_w, v_gdn_a_log, v_gdn_dt_bias, v_gdn_norm_w, v_ssm_conv_w, v_ssm_conv_b, v_ssm_a_log, v_ssm_dt_bias, v_ssm_d, v_ssm_norm_w, v_attn_sinks, v_sc_conv_w, v_w_up, v_w_down):
    L = x.shape[1]
    me = 4 * lax.axis_index("x") + 2 * lax.axis_index("y") + lax.axis_index("c")
    x0 = x[0]
    target = loss_target[0]
    S_ADA = ada_w.shape[2]

    small = jnp.concatenate([c.reshape(-1), gdn_conv_w.reshape(-1), ssm_conv_w.reshape(-1), sc_conv_w.reshape(-1)])
    n_small = small.shape[0]
    g_small = _ag_small(_pad_rows128(small), "ag_c_conv").reshape(N_DEV, -1)[:, :n_small]
    c_all = g_small[:, :D]
    o = D
    gdn_cw = g_small[:, o:o + 2 * 4 * 192].reshape(N_DEV, 2, 4, 192).transpose(1, 2, 0, 3).reshape(2, 4, 1536)
    o += 2 * 4 * 192
    ssm_cw = g_small[:, o:o + 2 * 4 * 128].reshape(N_DEV, 2, 4, 128).transpose(1, 2, 0, 3).reshape(2, 4, 1024)
    o += 2 * 4 * 128
    sc_cw = g_small[:, o:o + 2 * 3 * 64].reshape(N_DEV, 2, 3, 64).transpose(1, 2, 0, 3).reshape(2, 3, 512)

    ada_b_sh = lax.dynamic_slice(ada_b, (0, me * S_ADA), (2, S_ADA))
    mod_sh = _ada_fwd(c_all, ada_w, ada_b_sh, "ada_fwd")
    mod_all = _ag_small(mod_sh.reshape(-1, 128), "ag_mod").reshape(N_DEV, 2, N_DEV, S_ADA)
    mod_me = lax.dynamic_index_in_dim(mod_all, me, axis=2, keepdims=False)
    mod_me = mod_me.transpose(1, 0, 2).reshape(2, 6 * D)

    wt_sh = [jnp.pad(w_in[i].T.astype(BF16), ((0, SHARD_PAD - SHARD_IN), (0, 0))) for i in range(2)]
    wo_sh, wu_sh, wd_sh = ([w[i].astype(BF16) for i in range(2)] for w in (w_out, w_up, w_down))
    full_w_in = lambda g: _permute_rows(g[:, :SHARD_IN].reshape(IN_W, D))
    win_t = [None, None]
    wout_f, wup_g, wdown_f = [None, None], [None, None], [None, None]

    inv_freq = ROPE_THETA ** (-jnp.arange(0, 64, 2, dtype=F32) / 64)
    ang = positions[0].astype(F32)[:, None] * inv_freq
    cos = jnp.tile(jnp.cos(ang), (1, 4))
    sin = jnp.tile(jnp.sin(ang), (1, 4))

    vec = lambda a: a.reshape(1, -1)

    mods = [[vec(t) for t in jnp.split(mod_me[i], 6)] for i in range(2)]
    saved = []
    xc = x0
    h, got_i0 = _pre_fwd(x0, vec(norm_pre_mix[0]), mods[0][1], mods[0][0], "pre_mix_fwd0", _AgCargo([wt_sh[0]]))
    win_t[0] = full_w_in(got_i0)
    for i in range(2):
        sh_a, sc_a, gt_a, sh_m, sc_m, gt_m = mods[i]
        gal, gdb = _row128(gdn_a_log[i], LANE_GA), _row128(gdn_dt_bias[i], LANE_GA)
        sal, sdb, sdd = (_row128(ssm_a_log[i], LANE_SDT), _row128(ssm_dt_bias[i], LANE_SDT), _row128(ssm_d[i], LANE_SDT))
        gnw, snw, skr = vec(gdn_norm_w[i]), vec(ssm_norm_w[i]), _row128(attn_sinks[i], 0)
        zero_b = jnp.zeros((1, 1536), F32)

        riders = [wo_sh[0], wu_sh[0]] if i == 0 else [wd_sh[1]]
        proj, *got = _mm(h, win_t[i], name=f"in_proj{i}", tb=True, tn=896, tk=2048, cargo=_AgCargo(riders))
        if i == 0:
            wout_f[0], wup_g[0] = got[0].reshape(D, D), got[1]
        else:
            wdown_f[1] = got[0].reshape(MLP, D)
        qkvc = _conv_fwd(proj, 0, 1536, gdn_cw[i], zero_b, f"gdn_conv_fwd{i}")
        if i == 0:
            ya, gdn_s, got_i = _gdn_fwd(qkvc, proj, gal, gdb, gnw, "gdn_fwd0", cargo=_AgCargo([wt_sh[1]]))
            win_t[1] = full_w_in(got_i)
        else:
            ya, gdn_s = _gdn_fwd(qkvc, proj, gal, gdb, gnw, "gdn_fwd1")
        xbcc = _conv_fwd(proj, P_SXBC // 1024, 1024, ssm_cw[i], vec(ssm_conv_b[i]), f"ssm_conv_fwd{i}")
        yb, ssd_s = _ssd_fwd(xbcc, proj, sal, sdb, sdd, snw, f"ssd_fwd{i}")
        yc = _swa_fwd(proj, cos, sin, skr, f"swa_fwd{i}")
        yd = _sc_fwd(proj, sc_cw[i], f"sc_fwd{i}")
        ycat = jnp.concatenate([ya, yb, yc, yd], axis=1)
        out = _mm(ycat, wout_f[i], name=f"out_proj{i}", tk=2048)
        x1, h2 = _post_pre_fwd(xc, out, vec(norm_post_mix[i]), gt_a, vec(norm_pre_mlp[i]), sc_m, sh_m, f"post_mix_pre_mlp_fwd{i}")
        if i == 0:
            act, u, got_d, got_o = _mm(h2, wup_g[0], name="up_proj0", tk=2048, mode="relu2", shards="b",
                                       cargo=_AgCargo([wd_sh[0], wo_sh[1]]))
            wdown_f[0], wout_f[1] = got_d.reshape(MLP, D), got_o.reshape(D, D)
            y2, wup_g[1] = _mm(act, wdown_f[0], name="down_proj0", tk=2048, cargo=_AgCargo([wu_sh[1]]))
        else:
            act, u = _mm(h2, wup_g[1], name="up_proj1", tk=2048, mode="relu2", shards="b")
            y2 = _mm(act, wdown_f[1], name="down_proj1", tk=2048)
        saved.append(dict(x0=xc, h=h, proj=proj, qkvc=qkvc, gdn_s=gdn_s, xbcc=xbcc, ssd_s=ssd_s, ycat=ycat, out=out,
                          x1=x1, h2=h2, act=act, u=u, y2=y2, rows=(gal, gdb, sal, sdb, sdd, gnw, snw, skr)))
        if i == 0:
            xc, h = _post_pre_fwd(x1, y2, vec(norm_post_mlp[0]), gt_m, vec(norm_pre_mix[1]), mods[1][1], mods[1][0],
                                  "post_mlp_pre_mix_fwd")

    sq, dx, *pend = _last_residual_loss(saved[1]["x1"], saved[1]["y2"], vec(norm_post_mlp[1]), mods[1][5], target,
                                        "post_mlp_loss")
    loss = lax.psum(0.5 * sq[0, 0] / float(D), ("x", "y", "c"))

    gw = [dict() for _ in range(2)]
    r_in, r_out, r_up, r_down = [None, None], [None, None], [None, None], [None, None]
    for i in (1, 0):
        sv = saved[i]
        sh_a, sc_a, gt_a, sh_m, sc_m, gt_m = mods[i]
        gal, gdb, sal, sdb, sdd, gnw, snw, skr = sv["rows"]
        proj = sv["proj"]
        dy2, g_npostmlp, d_gt_m = pend
        du = _mm(dy2, wdown_f[i], name=f"down_proj_dx{i}", tb=True, tk=2048, out_dtype=BF16, mode="drelu2", u=sv["u"])
        g_wdown = _mm(sv["act"], dy2, name=f"down_proj_dw{i}", ta=True, tk=2048,
                      out_dtype=BF16).reshape(N_DEV, MLP // N_DEV, D)
        dh2, r_down[i] = _mm(du, wup_g[i], name=f"up_proj_dx{i}", tb=True, tk=1024, out_dtype=BF16, shards="b",
                             cargo=_ExCargo([[g_wdown]]))
        g_wup = _mm(sv["h2"], du, name=f"up_proj_dw{i}", ta=True, tk=2048, out_dtype=BF16, shards="o")
        dx1, dout, g_npremlp, d_sc_m, d_sh_m, g_npostmix, d_gt_a = _pre_post_bwd(
            sv["x1"], vec(norm_pre_mlp[i]), sc_m, sh_m, dh2, dx, sv["out"], vec(norm_post_mix[i]), gt_a,
            f"pre_mlp_post_mix_bwd{i}")
        dycat = _mm(dout, wout_f[i], name=f"out_proj_dx{i}", tb=True, tk=2048)
        g_wout = _mm(sv["ycat"], dout, name=f"out_proj_dw{i}", ta=True, tk=2048,
                     out_dtype=BF16).reshape(N_DEV, D // N_DEV, D)

        dqkvc, dgz, dgate_g, d_gal, d_gdb, d_gnw, r_up[i] = _gdn_bwd(
            sv["qkvc"], proj, gal, gdb, gnw, sv["gdn_s"], dycat, f"gdn_bwd{i}", cargo=_ExCargo([[g_wup]]))
        dqkv, g_gdn_cw, _ = _conv_bwd(dqkvc, proj, 0, 1536, gdn_cw[i], f"gdn_conv_bwd{i}")
        dxbcc, dsz, dgate_s, d_sal, d_sdb, d_sdd, d_snw = _ssd_bwd(
            sv["xbcc"], proj, sal, sdb, sdd, snw, sv["ssd_s"], dycat, f"ssd_bwd{i}")
        dsxbc, g_ssm_cw, g_ssm_cb = _conv_bwd(dxbcc, proj, P_SXBC // 1024, 1024, ssm_cw[i], f"ssm_conv_bwd{i}")
        daq, dak, dav, d_skr = _swa_bwd(proj, cos, sin, skr, dycat, f"swa_bwd{i}")
        dsc, g_sc_cw = _sc_bwd(dycat, proj, sc_cw[i], f"sc_bwd{i}")
        dgate = (dgate_g + dgate_s).astype(BF16)
        dproj = jnp.concatenate([dqkv, dgz, dsxbc, dsz, daq, dsc, dak, dav, dgate], axis=1)
        g_wint, r_out[i] = _mm(dproj, sv["h"], name=f"in_proj_dw{i}", ta=True, tm=896, tk=2048, out_dtype=BF16,
                               cargo=_ExCargo([[g_wout]]))
        g_wint = jnp.pad(_unpermute_rows(g_wint).reshape(N_DEV, SHARD_IN, D), ((0, 0), (0, SHARD_PAD - SHARD_IN), (0, 0)))
        dh, r_in[i] = _mm(dproj, win_t[i], name=f"in_proj_dx{i}", tk=896, out_dtype=BF16, cargo=_ExCargo([[g_wint]]))
        if i == 1:
            dx, dy2_0, g_npremix, d_sc_a, d_sh_a, *post0 = _pre_post_bwd(
                sv["x0"], vec(norm_pre_mix[1]), sc_a, sh_a, dh, dx1, saved[0]["y2"], vec(norm_post_mlp[0]), mods[0][5],
                "pre_mix_post_mlp_bwd")
            pend = (dy2_0, *post0)
        else:
            dx, g_npremix, d_sc_a, d_sh_a = _pre_bwd(sv["x0"], vec(norm_pre_mix[0]), sc_a, sh_a, dh, dx1, "pre_mix_bwd0")

        gw[i] = dict(
            dmod=jnp.concatenate([d_sh_a, d_sc_a, d_gt_a, d_sh_m, d_sc_m, d_gt_m], axis=1).reshape(-1),
            norm_pre_mix=g_npremix.reshape(-1), norm_post_mix=g_npostmix.reshape(-1),
            norm_pre_mlp=g_npremlp.reshape(-1), norm_post_mlp=g_npostmlp.reshape(-1),
            gdn_conv_w=g_gdn_cw.reshape(-1), gdn_a_log=d_gal[0, LANE_GA:LANE_GA + 4], gdn_dt_bias=d_gdb[0, LANE_GA:LANE_GA + 4],
            gdn_norm_w=d_gnw.reshape(-1), ssm_conv_w=g_ssm_cw.reshape(-1), ssm_conv_b=g_ssm_cb.reshape(-1),
            ssm_a_log=d_sal[0, LANE_SDT:LANE_SDT + 8], ssm_dt_bias=d_sdb[0, LANE_SDT:LANE_SDT + 8],
            ssm_d=d_sdd[0, LANE_SDT:LANE_SDT + 8], ssm_norm_w=d_snw.reshape(-1), attn_sinks=d_skr[0, 0:8],
            sc_conv_w=g_sc_cw.reshape(-1))
    grad_x = dx[None]

    grads, deltas, new_m, new_v = {}, {}, {}, {}
    for n_, r, w, m, v in (("w_out", r_out, w_out, m_w_out, v_w_out), ("w_up", r_up, w_up, m_w_up, v_w_up),
                           ("w_down", r_down, w_down, m_w_down, v_w_down)):
        grads[n_], deltas[n_], new_m[n_], new_v[n_] = _sum8_adamw(r[0], r[1], w, m, v, f"sum_adamw_{n_}")
    g_int = jnp.stack([_sum8(r_in[i].reshape(N_DEV, SHARD_PAD, D), f"grad_sum_w_in{i}") for i in range(2)])
    g_in = g_int[:, :SHARD_IN].transpose(0, 2, 1)
    dl, nm, nv = _adamw_call(w_in.reshape(-1, SHARD_IN), g_in.reshape(-1, SHARD_IN), m_w_in.reshape(-1, SHARD_IN),
                             v_w_in.reshape(-1, SHARD_IN), "adamw_w_in")
    grads["w_in"], deltas["w_in"] = g_in, dl.reshape(w_in.shape)
    new_m["w_in"], new_v["w_in"] = nm.reshape(w_in.shape), nv.reshape(w_in.shape)

    small_names = ["dmod", "norm_pre_mix", "norm_post_mix", "norm_pre_mlp", "norm_post_mlp", "gdn_conv_w", "gdn_a_log",
                   "gdn_dt_bias", "gdn_norm_w", "ssm_conv_w", "ssm_conv_b", "ssm_a_log", "ssm_dt_bias", "ssm_d",
                   "ssm_norm_w", "attn_sinks", "sc_conv_w"]
    flat = jnp.concatenate([gw[i][nm_] for nm_ in small_names for i in range(2)])
    n_flat = flat.shape[0]
    g_all = _ag_small(_pad_rows128(flat), "ag_small_grads")
    tot = _sum8(g_all, "small_grad_sum").reshape(-1)[:n_flat]
    dmod_all = g_all.reshape(N_DEV, -1)[:, :2 * 6 * D].reshape(N_DEV, 2, 6 * D)
    sm = {}
    o = 0
    for nm_ in small_names:
        n = gw[0][nm_].shape[0]
        sm[nm_] = jnp.stack([tot[o:o + n], tot[o + n:o + 2 * n]])
        o += 2 * n

    dmod_sh = lax.dynamic_slice(dmod_all.transpose(1, 0, 2), (0, 0, me * S_ADA), (2, N_DEV, S_ADA))
    grads["ada_w"], deltas["ada_w"], new_m["ada_w"], new_v["ada_w"] = _ada_bwd_adamw(
        c_all, dmod_sh, ada_w, m_ada_w, v_ada_w, "ada_bwd_adamw")

    def shard_cols(full, K, C):
        return lax.dynamic_slice(full.reshape(2, K, C), (0, 0, me * (C // N_DEV)), (2, K, C // N_DEV))

    small_g = dict(
        ada_b=sm["dmod"], norm_pre_mix=sm["norm_pre_mix"], norm_post_mix=sm["norm_post_mix"],
        norm_pre_mlp=sm["norm_pre_mlp"], norm_post_mlp=sm["norm_post_mlp"],
        gdn_conv_w=shard_cols(sm["gdn_conv_w"], 4, 1536), gdn_a_log=sm["gdn_a_log"], gdn_dt_bias=sm["gdn_dt_bias"],
        gdn_norm_w=sm["gdn_norm_w"], ssm_conv_w=shard_cols(sm["ssm_conv_w"], 4, 1024), ssm_conv_b=sm["ssm_conv_b"],
        ssm_a_log=sm["ssm_a_log"], ssm_dt_bias=sm["ssm_dt_bias"], ssm_d=sm["ssm_d"], ssm_norm_w=sm["ssm_norm_w"],
        attn_sinks=sm["attn_sinks"], sc_conv_w=shard_cols(sm["sc_conv_w"], 3, 512))
    small_w = dict(
        ada_b=(ada_b, m_ada_b, v_ada_b), norm_pre_mix=(norm_pre_mix, m_norm_pre_mix, v_norm_pre_mix),
        norm_post_mix=(norm_post_mix, m_norm_post_mix, v_norm_post_mix),
        norm_pre_mlp=(norm_pre_mlp, m_norm_pre_mlp, v_norm_pre_mlp),
        norm_post_mlp=(norm_post_mlp, m_norm_post_mlp, v_norm_post_mlp),
        gdn_conv_w=(gdn_conv_w, m_gdn_conv_w, v_gdn_conv_w), gdn_a_log=(gdn_a_log, m_gdn_a_log, v_gdn_a_log),
        gdn_dt_bias=(gdn_dt_bias, m_gdn_dt_bias, v_gdn_dt_bias), gdn_norm_w=(gdn_norm_w, m_gdn_norm_w, v_gdn_norm_w),
        ssm_conv_w=(ssm_conv_w, m_ssm_conv_w, v_ssm_conv_w), ssm_conv_b=(ssm_conv_b, m_ssm_conv_b, v_ssm_conv_b),
        ssm_a_log=(ssm_a_log, m_ssm_a_log, v_ssm_a_log), ssm_dt_bias=(ssm_dt_bias, m_ssm_dt_bias, v_ssm_dt_bias),
        ssm_d=(ssm_d, m_ssm_d, v_ssm_d), ssm_norm_w=(ssm_norm_w, m_ssm_norm_w, v_ssm_norm_w),
        attn_sinks=(attn_sinks, m_attn_sinks, v_attn_sinks), sc_conv_w=(sc_conv_w, m_sc_conv_w, v_sc_conv_w))
    names = list(small_w)
    sizes = [int(np.prod(small_w[n_][0].shape)) for n_ in names]
    pk = lambda j: _pad_rows128(jnp.concatenate([small_w[n_][j].reshape(-1) for n_ in names]))
    gk = _pad_rows128(jnp.concatenate([small_g[n_].reshape(-1) for n_ in names]))
    dl, nm, nv = _adamw_call(pk(0), gk, pk(1), pk(2), "adamw_small")
    dl, nm, nv = dl.reshape(-1), nm.reshape(-1), nv.reshape(-1)
    o = 0
    for n_, sz in zip(names, sizes):
        shp = small_w[n_][0].shape
        grads[n_] = small_g[n_].reshape(shp)
        deltas[n_], new_m[n_], new_v[n_] = dl[o:o + sz].reshape(shp), nm[o:o + sz].reshape(shp), nv[o:o + sz].reshape(shp)
        o += sz

    order = ["ada_w", "ada_b", "norm_pre_mix", "norm_post_mix", "norm_pre_mlp", "norm_post_mlp", "w_in", "w_out",
             "gdn_conv_w", "gdn_a_log", "gdn_dt_bias", "gdn_norm_w", "ssm_conv_w", "ssm_conv_b", "ssm_a_log",
             "ssm_dt_bias", "ssm_d", "ssm_norm_w", "attn_sinks", "sc_conv_w", "w_up", "w_down"]
    return (loss, grad_x, *[grads[n_] for n_ in order], *[deltas[n_] for n_ in order],
            *[new_m[n_] for n_ in order], *[new_v[n_] for n_ in order])
```
